```python
import math
import jax
import jax.numpy as jnp
from jax import lax
import numpy as np

D_MODEL = 1024
BATCH = 8
SEQ = 4096
DEPTH = 2

GRID_W = 64
CTX_LEN = 256
QBLOCK = 128
ROPE_THETA = 10000.0
NORM_EPS = 1e-6
NEG_INF = -1e30

DIFF_HEADS = 4
DIFF_QK_DIM = 32
DIFF_V_DIM = 64
SWA_Q_HEADS = 4
SWA_KV_HEADS = 2
SWA_HEAD_DIM = 64
WINDOW = 128
MLA_HEADS = 4
MLA_Q_RANK = 192
MLA_KV_RANK = 128
MLA_NOPE_DIM = 64
MLA_ROPE_DIM = 32
MLA_V_DIM = 64
GQA_Q_HEADS = 4
GQA_KV_HEADS = 2
GQA_HEAD_DIM = 64
FFN_DIM = 2816
CONV_WIDTH = 3
N_MOD = 6

DIFF_COLS = 4 * DIFF_HEADS * DIFF_QK_DIM + DIFF_HEADS * DIFF_V_DIM
SWA_COLS = (SWA_Q_HEADS + 2 * SWA_KV_HEADS) * SWA_HEAD_DIM
MLA_COLS = MLA_Q_RANK + MLA_KV_RANK + MLA_ROPE_DIM
GQA_COLS = (GQA_Q_HEADS + 2 * GQA_KV_HEADS) * GQA_HEAD_DIM
IN_COLS = DIFF_COLS + SWA_COLS + MLA_COLS + GQA_COLS
MIX_WIDTH = (DIFF_HEADS * DIFF_V_DIM + SWA_Q_HEADS * SWA_HEAD_DIM
             + MLA_HEADS * MLA_V_DIM + GQA_Q_HEADS * GQA_HEAD_DIM)

kernel_name = "hybrid_head_group_diffusion_trunk"


def rms_norm(x, g):
    x32 = x.astype(jnp.float32)
    y = x32 * lax.rsqrt(jnp.mean(jnp.square(x32), axis=-1, keepdims=True) + NORM_EPS)
    return y.astype(x.dtype) * g


def modulate(h, shift, scale):
    return h * (1.0 + scale[..., None, :]) + shift[..., None, :]


def axial_rope_tables(row, col, rot_dim):
    n = rot_dim // 4
    inv_freq = ROPE_THETA ** (-jnp.arange(n, dtype=jnp.float32) / n)
    ang = jnp.stack([row.astype(jnp.float32)[:, None] * inv_freq,
                     col.astype(jnp.float32)[:, None] * inv_freq], axis=1)
    return jnp.cos(ang), jnp.sin(ang)


def apply_axial_rope(x, cos, sin):
    B, S, H, d = x.shape
    n = d // 4
    xr = x.reshape(B, S, H, 2, 2, n)
    x1, x2 = xr[..., 0, :], xr[..., 1, :]
    c = cos[None, :, None].astype(x.dtype)
    s = sin[None, :, None].astype(x.dtype)
    out = jnp.stack([x1 * c - x2 * s, x2 * c + x1 * s], axis=-2)
    return out.reshape(B, S, H, d)


def attend(q, k, v, scale, sink=None):
    s = jnp.einsum('bqgrd,bkgd->bgrqk', q, k).astype(jnp.float32) * scale
    if sink is None:
        p = jax.nn.softmax(s, axis=-1)
    else:
        G, R = q.shape[2], q.shape[3]
        sk = jnp.broadcast_to(sink.astype(jnp.float32).reshape(1, G, R, 1, 1), s.shape[:-1] + (1,))
        p = jax.nn.softmax(jnp.concatenate([s, sk], axis=-1), axis=-1)[..., :-1]
    return jnp.einsum('bgrqk,bkgd->bqgrd', p.astype(v.dtype), v)


def blocked_attend(q, k, v, scale):
    B, S, G, R, d = q.shape
    nb = S // QBLOCK
    qb = jnp.moveaxis(q.reshape(B, nb, QBLOCK, G, R, d), 1, 0)
    out = lax.map(lambda qi: attend(qi, k, v, scale), qb)
    return jnp.moveaxis(out, 0, 1).reshape(B, S, G, R, v.shape[-1])


def banded_window_attend(q, k, v, k_ctx, v_ctx, sink, scale):
    B, S, G, R, d = q.shape
    W = WINDOW
    nb = S // W
    pad = ((0, 0), (W, W), (0, 0), (0, 0))
    kp = jnp.pad(k, pad)
    vp = jnp.pad(v, pad)

    def band(t):
        return jnp.concatenate(
            [t[:, i * W:i * W + S].reshape(B, nb, W, G, t.shape[-1]) for i in range(3)], axis=2)

    kb, vb = band(kp), band(vp)
    qb = q.reshape(B, nb, W, G, R, d)
    s_loc = jnp.einsum('bnqgrd,bnkgd->bngrqk', qb, kb).astype(jnp.float32) * scale
    qi = jnp.arange(W)[:, None]
    kj = jnp.arange(3 * W)[None, :]
    rel = kj - W - qi
    j = jnp.arange(nb)[:, None, None] * W + kj[None] - W
    valid = (jnp.abs(rel)[None] <= W) & (j >= 0) & (j < S)
    s_loc = jnp.where(valid[None, :, None, None], s_loc, NEG_INF)
    s_ctx = jnp.einsum('bnqgrd,bkgd->bngrqk', qb, k_ctx).astype(jnp.float32) * scale
    s_sink = jnp.broadcast_to(sink.astype(jnp.float32).reshape(1, 1, G, R, 1, 1), s_loc.shape[:-1] + (1,))
    p = jax.nn.softmax(jnp.concatenate([s_loc, s_ctx, s_sink], axis=-1), axis=-1).astype(v.dtype)
    n_loc = 3 * W
    n_ctx = k_ctx.shape[1]
    out = (jnp.einsum('bngrqk,bnkgd->bnqgrd', p[..., :n_loc], vb)
           + jnp.einsum('bngrqk,bkgd->bnqgrd', p[..., n_loc:n_loc + n_ctx], v_ctx))
    return out.reshape(B, S, G, R, v.shape[-1])


def diff_attention_mixer(p, pc, lam_q1, lam_k1, lam_q2, lam_k2, g_sub, lambda_init, rope, need_ctx):
    H, dk, dv = DIFF_HEADS, DIFF_QK_DIM, DIFF_V_DIM
    scale = dk ** -0.5

    def qkv(t):
        B, L, _ = t.shape
        q = t[..., :2 * H * dk].reshape(B, L, 2 * H, dk)
        k = t[..., 2 * H * dk:4 * H * dk].reshape(B, L, 2 * H, dk)
        v = t[..., 4 * H * dk:].reshape(B, L, H, dv)
        return q, k, v

    q, k, v = qkv(p)
    qc, kc, vc = qkv(pc)
    q = apply_axial_rope(q, *rope)
    k = apply_axial_rope(k, *rope)
    B, S = q.shape[:2]
    Lc = qc.shape[1]
    lam = (jnp.exp(jnp.sum(lam_q1.astype(jnp.float32) * lam_k1.astype(jnp.float32)))
           - jnp.exp(jnp.sum(lam_q2.astype(jnp.float32) * lam_k2.astype(jnp.float32)))
           + lambda_init)

    def combine(qh, kh, vh, attn):
        a1 = attn(qh[:, :, :, 0:1], kh[:, :, :, 0], vh, scale)[:, :, :, 0]
        a2 = attn(qh[:, :, :, 1:2], kh[:, :, :, 1], vh, scale)[:, :, :, 0]
        o = a1 - lam.astype(a1.dtype) * a2
        o = rms_norm(o, g_sub) * (1.0 - lambda_init)
        return o.reshape(o.shape[0], o.shape[1], H * dv)

    k_all = jnp.concatenate([kc, k], axis=1).reshape(B, Lc + S, H, 2, dk)
    v_all = jnp.concatenate([vc, v], axis=1)
    out = combine(q.reshape(B, S, H, 2, dk), k_all, v_all, blocked_attend)
    out_c = None
    if need_ctx:
        out_c = combine(qc.reshape(B, Lc, H, 2, dk), kc.reshape(B, Lc, H, 2, dk), vc, attend)
    return out, out_c


def window_attention_mixer(p, pc, sink, rope, need_ctx):
    H, G, d = SWA_Q_HEADS, SWA_KV_HEADS, SWA_HEAD_DIM
    R = H // G
    scale = d ** -0.5

    def qkv(t):
        B, L, _ = t.shape
        q = t[..., :H * d].reshape(B, L, H, d)
        k = t[..., H * d:(H + G) * d].reshape(B, L, G, d)
        v = t[..., (H + G) * d:].reshape(B, L, G, d)
        return q, k, v

    q, k, v = qkv(p)
    qc, kc, vc = qkv(pc)
    q = apply_axial_rope(q, *rope)
    k = apply_axial_rope(k, *rope)
    B, S = q.shape[:2]
    out = banded_window_attend(q.reshape(B, S, G, R, d), k, v, kc, vc, sink, scale).reshape(B, S, H * d)
    out_c = None
    if need_ctx:
        Lc = qc.shape[1]
        out_c = attend(qc.reshape(B, Lc, G, R, d), kc, vc, scale, sink).reshape(B, Lc, H * d)
    return out, out_c


def latent_attention_mixer(p, pc, g_q, g_kv, w_uq, w_ukv, rope, need_ctx):
    H, dn, dr, dv = MLA_HEADS, MLA_NOPE_DIM, MLA_ROPE_DIM, MLA_V_DIM
    scale = (dn + dr) ** -0.5

    def qkv(t, rotate):
        B, L, _ = t.shape
        c_q = t[..., :MLA_Q_RANK]
        c_kv = t[..., MLA_Q_RANK:MLA_Q_RANK + MLA_KV_RANK]
        k_r = t[..., MLA_Q_RANK + MLA_KV_RANK:].reshape(B, L, 1, dr)
        q = (rms_norm(c_q, g_q) @ w_uq).reshape(B, L, H, dn + dr)
        kv = (rms_norm(c_kv, g_kv) @ w_ukv).reshape(B, L, H, dn + dv)
        q_n, q_r = q[..., :dn], q[..., dn:]
        if rotate:
            q_r = apply_axial_rope(q_r, *rope)
            k_r = apply_axial_rope(k_r, *rope)
        q = jnp.concatenate([q_n, q_r], axis=-1)
        k = jnp.concatenate([kv[..., :dn], jnp.broadcast_to(k_r, (B, L, H, dr))], axis=-1)
        v = kv[..., dn:]
        return q, k, v

    q, k, v = qkv(p, True)
    qc, kc, vc = qkv(pc, False)
    B, S = q.shape[:2]
    k_all = jnp.concatenate([kc, k], axis=1)
    v_all = jnp.concatenate([vc, v], axis=1)
    out = blocked_attend(q[:, :, :, None], k_all, v_all, scale).reshape(B, S, H * dv)
    out_c = None
    if need_ctx:
        Lc = qc.shape[1]
        out_c = attend(qc[:, :, :, None], kc, vc, scale).reshape(B, Lc, H * dv)
    return out, out_c


def axial_gqa_mixer(p, pc, g_q, g_k, rope, need_ctx):
    H, G, d = GQA_Q_HEADS, GQA_KV_HEADS, GQA_HEAD_DIM
    R = H // G
    scale = d ** -0.5

    def qkv(t):
        B, L, _ = t.shape
        q = rms_norm(t[..., :H * d].reshape(B, L, H, d), g_q)
        k = rms_norm(t[..., H * d:(H + G) * d].reshape(B, L, G, d), g_k)
        v = t[..., (H + G) * d:].reshape(B, L, G, d)
        return q, k, v

    q, k, v = qkv(p)
    qc, kc, vc = qkv(pc)
    q = apply_axial_rope(q, *rope)
    k = apply_axial_rope(k, *rope)
    B, S = q.shape[:2]
    k_all = jnp.concatenate([kc, k], axis=1)
    v_all = jnp.concatenate([vc, v], axis=1)
    out = blocked_attend(q.reshape(B, S, G, R, d), k_all, v_all, scale).reshape(B, S, H * d)
    out_c = None
    if need_ctx:
        Lc = qc.shape[1]
        out_c = attend(qc.reshape(B, Lc, G, R, d), kc, vc, scale).reshape(B, Lc, H * d)
    return out, out_c


def depthwise_conv(u, w, b):
    C = u.shape[-1]
    y = lax.conv_general_dilated(u, w[:, None, :].astype(u.dtype), window_strides=(1,),
                                 padding=((CONV_WIDTH // 2, CONV_WIDTH // 2),),
                                 dimension_numbers=('NWC', 'WIO', 'NWC'),
                                 feature_group_count=C)
    return y + b


def conv_ffn(h, w_up, conv_w, conv_b, w_down):
    u = depthwise_conv(h @ w_up, conv_w, conv_b)
    a, v = u[..., :FFN_DIM], u[..., FFN_DIM:]
    return (jax.nn.silu(a) * v) @ w_down


def setup_inputs(seed: int = 0) -> dict:
    key = jax.random.key(seed)
    keys = iter(jax.random.split(key, 32))
    L, D = DEPTH, D_MODEL

    def normal(shape, scale):
        return jax.random.normal(next(keys), shape, jnp.float32) * scale

    def gain(shape):
        return 1.0 + normal(shape, 0.05)

    return {
        "x": normal((BATCH, SEQ, D), 1.0),
        "c": normal((BATCH, D), 1.0),
        "ctx": normal((BATCH, CTX_LEN, D), 1.0),
        "c_ctx": normal((D,), 1.0),
        "w_mod": normal((L, D, N_MOD * D), 0.5 * D ** -0.5),
        "b_mod": normal((L, N_MOD * D), 0.01),
        "g_mix_pre": gain((L, D)),
        "g_mix_post": gain((L, D)),
        "g_ffn_pre": gain((L, D)),
        "g_ffn_post": gain((L, D)),
        "w_in": normal((L, D, IN_COLS), D ** -0.5),
        "diff_lam_q1": normal((L, DIFF_QK_DIM), 0.1),
        "diff_lam_k1": normal((L, DIFF_QK_DIM), 0.1),
        "diff_lam_q2": normal((L, DIFF_QK_DIM), 0.1),
        "diff_lam_k2": normal((L, DIFF_QK_DIM), 0.1),
        "diff_g_sub": gain((L, DIFF_V_DIM)),
        "swa_sink": normal((L, SWA_Q_HEADS), 1.0),
        "mla_g_q": gain((L, MLA_Q_RANK)),
        "mla_g_kv": gain((L, MLA_KV_RANK)),
        "mla_w_uq": normal((L, MLA_Q_RANK, MLA_HEADS * (MLA_NOPE_DIM + MLA_ROPE_DIM)), MLA_Q_RANK ** -0.5),
        "mla_w_ukv": normal((L, MLA_KV_RANK, MLA_HEADS * (MLA_NOPE_DIM + MLA_V_DIM)), MLA_KV_RANK ** -0.5),
        "gqa_g_q": gain((L, GQA_HEAD_DIM)),
        "gqa_g_k": gain((L, GQA_HEAD_DIM)),
        "w_out": normal((L, MIX_WIDTH, D), MIX_WIDTH ** -0.5),
        "ffn_w_up": normal((L, D, 2 * FFN_DIM), D ** -0.5),
        "ffn_conv_w": normal((L, CONV_WIDTH, 2 * FFN_DIM), CONV_WIDTH ** -0.5),
        "ffn_conv_b": normal((L, 2 * FFN_DIM), 0.01),
        "ffn_w_down": normal((L, FFN_DIM, D), FFN_DIM ** -0.5),
    }


def reference(x, c, ctx, c_ctx, w_mod, b_mod, g_mix_pre, g_mix_post, g_ffn_pre, g_ffn_post,
              w_in, diff_lam_q1, diff_lam_k1, diff_lam_q2, diff_lam_k2, diff_g_sub, swa_sink,
              mla_g_q, mla_g_kv, mla_w_uq, mla_w_ukv, gqa_g_q, gqa_g_k, w_out,
              ffn_w_up, ffn_conv_w, ffn_conv_b, ffn_w_down):
    B, S, D = x.shape
    rows = S // GRID_W
    row = jnp.repeat(jnp.arange(rows, dtype=jnp.int32), GRID_W)
    col = jnp.tile(jnp.arange(GRID_W, dtype=jnp.int32), rows)
    rope_diff = axial_rope_tables(row, col, DIFF_QK_DIM)
    rope_swa = axial_rope_tables(row, col, SWA_HEAD_DIM)
    rope_mla = axial_rope_tables(row, col, MLA_ROPE_DIM)
    rope_gqa = axial_rope_tables(row, col, GQA_HEAD_DIM)
    silu_c = jax.nn.silu(c)
    silu_cc = jax.nn.silu(c_ctx)
    o1 = DIFF_COLS
    o2 = o1 + SWA_COLS
    o3 = o2 + MLA_COLS
    xc = ctx
    for l in range(DEPTH):
        need_ctx = l < DEPTH - 1
        lambda_init = 0.8 - 0.6 * math.exp(-0.3 * l)
        mod = (silu_c @ w_mod[l] + b_mod[l]).reshape(B, N_MOD, D)
        modc = (silu_cc @ w_mod[l] + b_mod[l]).reshape(N_MOD, D)

        h = modulate(rms_norm(x, g_mix_pre[l]), mod[:, 0], mod[:, 1])
        hc = modulate(rms_norm(xc, g_mix_pre[l]), modc[0], modc[1])
        p = h @ w_in[l]
        pc = hc @ w_in[l]
        ya, yca = diff_attention_mixer(p[..., :o1], pc[..., :o1], diff_lam_q1[l], diff_lam_k1[l],
                                       diff_lam_q2[l], diff_lam_k2[l], diff_g_sub[l], lambda_init,
                                       rope_diff, need_ctx)
        yb, ycb = window_attention_mixer(p[..., o1:o2], pc[..., o1:o2], swa_sink[l], rope_swa, need_ctx)
        ym, ycm = latent_attention_mixer(p[..., o2:o3], pc[..., o2:o3], mla_g_q[l], mla_g_kv[l],
                                         mla_w_uq[l], mla_w_ukv[l], rope_mla, need_ctx)
        yd, ycd = axial_gqa_mixer(p[..., o3:], pc[..., o3:], gqa_g_q[l], gqa_g_k[l], rope_gqa, need_ctx)
        y = jnp.concatenate([ya, yb, ym, yd], axis=-1) @ w_out[l]
        x = x + mod[:, 2, None] * rms_norm(y, g_mix_post[l])
        if need_ctx:
            yc = jnp.concatenate([yca, ycb, ycm, ycd], axis=-1) @ w_out[l]
            xc = xc + modc[2] * rms_norm(yc, g_mix_post[l])

        f = conv_ffn(modulate(rms_norm(x, g_ffn_pre[l]), mod[:, 3], mod[:, 4]),
                     ffn_w_up[l], ffn_conv_w[l], ffn_conv_b[l], ffn_w_down[l])
        x = x + mod[:, 5, None] * rms_norm(f, g_ffn_post[l])
        if need_ctx:
            fc = conv_ffn(modulate(rms_norm(xc, g_ffn_pre[l]), modc[3], modc[4]),
                          ffn_w_up[l], ffn_conv_w[l], ffn_conv_b[l], ffn_w_down[l])
            xc = xc + modc[5] * rms_norm(fc, g_ffn_post[l])
    return x
```

```python
import functools
import math

import jax
import jax.numpy as jnp
import numpy as np
from jax import lax
from jax.experimental import pallas as pl
from jax.experimental.pallas import tpu as pltpu

F32 = jnp.float32
BF16 = jnp.bfloat16

D_MODEL = 1024
GRID_W = 64
ROPE_THETA = 10000.0
NORM_EPS = 1e-6
NEG_INF = -1e30
N_MOD = 6

DIFF_HEADS, DIFF_QK_DIM, DIFF_V_DIM = 4, 32, 64
SWA_HEAD_DIM, WINDOW = 64, 128
MLA_Q_RANK, MLA_KV_RANK = 192, 128
MLA_NOPE_DIM, MLA_ROPE_DIM, MLA_V_DIM, MLA_HEADS = 64, 32, 64, 4
GQA_HEAD_DIM = 64
FFN_DIM = 2816
MIX_COLS = 256

LANES = 128
BF16_SUBLANES = 16
HALF = LANES // 2
VMEM_LIMIT = 56 * 1024 * 1024

P_COLS = 17 * LANES
QKV_COLS = 28 * LANES

TOK_TILE = 256
KV_CHUNK = 512
FFN_TILE = 512
FFN_CHUNK = 256


def _rms(x, n):
    return lax.rsqrt(jnp.sum(x * x, axis=-1, keepdims=True) * (1.0 / n) + NORM_EPS)


def _lane(shape):
    return lax.broadcasted_iota(jnp.int32, shape, len(shape) - 1)


def _dot(a, b):
    return jnp.dot(a, b, preferred_element_type=F32)


def _dot_nt(a, b):
    return lax.dot_general(a, b, (((1,), (1,)), ((), ())), preferred_element_type=F32)


def mod_kernel(c_ref, w_ref, b_ref, o_ref):
    c = c_ref[...]
    s = c * (1.0 / (1.0 + jnp.exp(-c)))
    o_ref[...] = jnp.dot(s, w_ref[...], preferred_element_type=F32,
                         precision=lax.Precision.HIGHEST) + b_ref[...]


def mod_call(cc, w_mod, b_mod):
    depth, d, n = w_mod.shape
    rows = cc.shape[0]
    tn = 1024
    return pl.pallas_call(
        mod_kernel,
        grid=(depth, n // tn),
        in_specs=[
            pl.BlockSpec((rows, d), lambda l, j: (0, 0)),
            pl.BlockSpec((None, d, tn), lambda l, j: (l, 0, j)),
            pl.BlockSpec((None, 1, tn), lambda l, j: (l, 0, j)),
        ],
        out_specs=pl.BlockSpec((None, rows, tn), lambda l, j: (l, 0, j)),
        out_shape=jax.ShapeDtypeStruct((depth, rows, n), F32),
        compiler_params=pltpu.CompilerParams(
            dimension_semantics=("arbitrary", "arbitrary"), vmem_limit_bytes=VMEM_LIMIT),
    )(cc, w_mod, b_mod.reshape(depth, 1, n))


def _rope(x, tab_ref, n):
    return (x * tab_ref[0] + pltpu.roll(x, LANES - n, 1) * tab_ref[1]
            + pltpu.roll(x, n, 1) * tab_ref[2])


def _halves(lo, hi, shape):
    return jnp.where(_lane(shape) < HALF, lo, hi)


def _head_rms(x):
    low = _lane(x.shape) < HALF
    sq = x * x
    ss_lo = jnp.sum(jnp.where(low, sq, 0.0), axis=-1, keepdims=True)
    ss_hi = jnp.sum(jnp.where(low, 0.0, sq), axis=-1, keepdims=True)
    r_lo = lax.rsqrt(ss_lo * (1.0 / HALF) + NORM_EPS)
    r_hi = lax.rsqrt(ss_hi * (1.0 / HALF) + NORM_EPS)
    return jnp.where(low, r_lo, r_hi)


def proj_kernel(x_ref, mod_ref, gpre_ref, win_ref, wq_ref, wkv_ref, gq_ref, gkv_ref,
                gnq_ref, gnk_ref, t32_ref, t64_ref, tml_ref, o_ref, *, rotate):
    x = x_ref[...]
    h = x * _rms(x, D_MODEL) * gpre_ref[...]
    h = h * (1.0 + mod_ref[1:2, :]) + mod_ref[0:1, :]
    p = _dot(h.astype(BF16), win_ref[...])

    def grp(g):
        return p[:, g * LANES:(g + 1) * LANES]

    def put(g, v):
        o_ref[:, g * LANES:(g + 1) * LANES] = v.astype(BF16)

    def rope32(v):
        return _rope(v, t32_ref, DIFF_QK_DIM // 4) if rotate else v

    def rope64(v):
        return _rope(v, t64_ref, GQA_HEAD_DIM // 4) if rotate else v

    def ropeml(v):
        return _rope(v, tml_ref, MLA_ROPE_DIM // 4) if rotate else v

    for g in range(2):
        put(g, rope32(grp(g)) * (DIFF_QK_DIM ** -0.5))
        put(2 + g, rope32(grp(2 + g)))
        put(4 + g, grp(4 + g))

    for g in range(2):
        put(6 + g, rope64(grp(6 + g)) * (SWA_HEAD_DIM ** -0.5))
    ks = rope64(grp(8))
    put(8, ks)
    put(9, pltpu.roll(ks, HALF, 1))
    put(10, grp(9))
    put(11, pltpu.roll(grp(9), HALF, 1))

    cq0, cq1, ckv = grp(10), grp(11), grp(12)
    low = _lane(cq1.shape) < HALF
    ss = (jnp.sum(cq0 * cq0, axis=-1, keepdims=True)
          + jnp.sum(jnp.where(low, cq1 * cq1, 0.0), axis=-1, keepdims=True))
    rq = lax.rsqrt(ss * (1.0 / MLA_Q_RANK) + NORM_EPS)
    cqn = jnp.concatenate([cq0 * rq * gq_ref[:, :LANES], cq1 * rq * gq_ref[:, LANES:]], axis=1)
    qm = _dot(cqn.astype(BF16), wq_ref[...])
    scale_m = (MLA_NOPE_DIM + MLA_ROPE_DIM) ** -0.5
    for hd in range(MLA_HEADS):
        put(12 + hd, ropeml(qm[:, hd * LANES:(hd + 1) * LANES]) * scale_m)
    ckvn = ckv * _rms(ckv, MLA_KV_RANK) * gkv_ref[...]
    kvin = jnp.concatenate([ckvn, ropeml(cq1)], axis=1)
    kv = _dot(kvin.astype(BF16), wkv_ref[...])
    for g in range(6):
        put(16 + g, kv[:, g * LANES:(g + 1) * LANES])

    for g in range(2):
        qg = grp(13 + g)
        put(22 + g, rope64(qg * _head_rms(qg) * gnq_ref[...]) * (GQA_HEAD_DIM ** -0.5))
    kg = grp(15)
    kg = rope64(kg * _head_rms(kg) * gnk_ref[...])
    put(24, kg)
    put(25, pltpu.roll(kg, HALF, 1))
    put(26, grp(16))
    put(27, pltpu.roll(grp(16), HALF, 1))


def proj_call(x, mod, mod_row, lw, tabs, rotate):
    b, s, d = x.shape
    tq = TOK_TILE
    if mod_row is None:
        mod_map = lambda i, t: (i, 0, 0)
    else:
        mod_map = lambda i, t: (mod_row, 0, 0)
    const2 = lambda i, t: (0, 0)
    tab_spec = pl.BlockSpec((3, tq, LANES), lambda i, t: (0, t, 0))
    return pl.pallas_call(
        functools.partial(proj_kernel, rotate=rotate),
        grid=(b, s // tq),
        in_specs=[
            pl.BlockSpec((None, tq, d), lambda i, t: (i, t, 0)),
            pl.BlockSpec((None, N_MOD, d), mod_map),
            pl.BlockSpec((1, d), const2),
            pl.BlockSpec((d, P_COLS), const2),
            pl.BlockSpec((2 * LANES, 4 * LANES), const2),
            pl.BlockSpec((2 * LANES, 6 * LANES), const2),
            pl.BlockSpec((1, 2 * LANES), const2),
            pl.BlockSpec((1, LANES), const2),
            pl.BlockSpec((1, LANES), const2),
            pl.BlockSpec((1, LANES), const2),
            tab_spec, tab_spec, tab_spec,
        ],
        out_specs=pl.BlockSpec((None, tq, QKV_COLS), lambda i, t: (i, t, 0)),
        out_shape=jax.ShapeDtypeStruct((b, s, QKV_COLS), BF16),
        compiler_params=pltpu.CompilerParams(
            dimension_semantics=("arbitrary", "arbitrary"), vmem_limit_bytes=VMEM_LIMIT),
    )(x, mod, lw["g_mix_pre"], lw["w_in"], lw["wq"], lw["wkv"], lw["gq"], lw["gkv"],
      lw["gnq"], lw["gnk"], tabs[0], tabs[1], tabs[2])


def _flash(qm, kc, vc, kl_ref, vl_ref, kb, vb, n_chunks):
    s = _dot_nt(qm, kc)
    m = jnp.max(s, axis=-1, keepdims=True)
    p = jnp.exp(s - m)
    l = jnp.sum(p, axis=-1, keepdims=True)
    acc = _dot(p.astype(BF16), vc)

    def body(i, carry):
        m, l, acc = carry
        rows = pl.ds(pl.multiple_of(i * KV_CHUNK, KV_CHUNK), KV_CHUNK)
        k = kl_ref[rows, kb * LANES:(kb + 1) * LANES]
        v = vl_ref[rows, vb * LANES:(vb + 1) * LANES]
        s = _dot_nt(qm, k)
        m_new = jnp.maximum(m, jnp.max(s, axis=-1, keepdims=True))
        alpha = jnp.exp(m - m_new)
        p = jnp.exp(s - m_new)
        l = l * alpha + jnp.sum(p, axis=-1, keepdims=True)
        acc = acc * alpha + _dot(p.astype(BF16), v)
        return m_new, l, acc

    if n_chunks:
        m, l, acc = lax.fori_loop(0, n_chunks, body, (m, l, acc))
    return acc / l


def _mask_lanes(q, lo, hi):
    if lo == 0 and hi == LANES:
        return q
    lane = _lane(q.shape)
    return jnp.where((lane >= lo) & (lane < hi), q, jnp.zeros_like(q))


def attn_kernel(*refs, plan, n_chunks):
    if n_chunks:
        q_ref, kc_ref, vc_ref, kl_ref, vl_ref, o_ref = refs
    else:
        q_ref, kc_ref, vc_ref, o_ref = refs
        kl_ref = vl_ref = None
    for ob, maps in enumerate(plan):
        res = []
        for qb, lo, hi, kb, vb in maps:
            qm = _mask_lanes(q_ref[:, qb * LANES:(qb + 1) * LANES], lo, hi)
            res.append(_flash(qm, kc_ref[:, kb * LANES:(kb + 1) * LANES],
                              vc_ref[:, vb * LANES:(vb + 1) * LANES],
                              kl_ref, vl_ref, kb, vb, n_chunks))
        o_ref[:, ob * LANES:(ob + 1) * LANES] = _halves(res[0], res[1], res[0].shape).astype(BF16)


def diff_kernel(*refs, n_chunks, lambda_init):
    if n_chunks:
        lam_ref, g_ref, q_ref, kc_ref, vc_ref, kl_ref, vl_ref, o_ref = refs
    else:
        lam_ref, g_ref, q_ref, kc_ref, vc_ref, o_ref = refs
        kl_ref = vl_ref = None
    lv = lam_ref[...]
    lam = (jnp.exp(jnp.sum(lv[0:1] * lv[1:2], axis=-1, keepdims=True))
           - jnp.exp(jnp.sum(lv[2:3] * lv[3:4], axis=-1, keepdims=True)) + lambda_init)
    for ob in range(2):
        q = q_ref[:, ob * LANES:(ob + 1) * LANES]
        kc = kc_ref[:, ob * LANES:(ob + 1) * LANES]
        vc = vc_ref[:, ob * LANES:(ob + 1) * LANES]
        o = []
        for half in range(2):
            a = [_flash(_mask_lanes(q, half * HALF + i * DIFF_QK_DIM,
                                    half * HALF + (i + 1) * DIFF_QK_DIM),
                        kc, vc, kl_ref, vl_ref, ob, ob, n_chunks) for i in range(2)]
            o.append(a[0] - lam * a[1])
        o = _halves(o[0], o[1], o[0].shape)
        o = o * _head_rms(o) * g_ref[...] * (1.0 - lambda_init)
        o_ref[:, ob * LANES:(ob + 1) * LANES] = o.astype(BF16)


def _qkv_specs(s_lat, tq, q_cols, q_blk, k_cols, k_blk, v_cols, v_blk, with_lat, q_off):
    specs = [
        pl.BlockSpec((None, tq, q_cols), lambda i, t: (i, t + q_off, q_blk)),
        pl.BlockSpec((None, TOK_TILE, k_cols), lambda i, t: (i, 0, k_blk)),
        pl.BlockSpec((None, TOK_TILE, v_cols), lambda i, t: (i, 0, v_blk)),
    ]
    if with_lat:
        specs += [
            pl.BlockSpec((None, s_lat, k_cols), lambda i, t: (i, 0, k_blk)),
            pl.BlockSpec((None, s_lat, v_cols), lambda i, t: (i, 0, v_blk)),
        ]
    return specs


def dense_attn_call(kind, qkv_q, qkv_ctx, qkv_lat, extra=None, lambda_init=None):
    b, sq, _ = qkv_q.shape
    tq = TOK_TILE
    with_lat = qkv_lat is not None
    s_lat = qkv_lat.shape[1] if with_lat else 0
    n_chunks = s_lat // KV_CHUNK
    if kind == "diff":
        lay = (256, 0, 256, 1, 256, 2)
    elif kind == "mla":
        lay = (512, 3, 512, 4, 256, 10)
    else:
        lay = (256, 11, 256, 12, 256, 13)
    specs = _qkv_specs(s_lat, tq, *lay, with_lat, 0)
    args = [qkv_q, qkv_ctx, qkv_ctx] + ([qkv_lat, qkv_lat] if with_lat else [])
    if kind == "diff":
        body = functools.partial(diff_kernel, n_chunks=n_chunks, lambda_init=lambda_init)
        specs = [pl.BlockSpec((4, DIFF_QK_DIM), lambda i, t: (0, 0)),
                 pl.BlockSpec((1, LANES), lambda i, t: (0, 0))] + specs
        args = list(extra) + args
    else:
        if kind == "mla":
            plan = (((0, 0, LANES, 0, 0), (1, 0, LANES, 1, 0)),
                    ((2, 0, LANES, 2, 1), (3, 0, LANES, 3, 1)))
        else:
            plan = (((0, 0, HALF, 0, 0), (0, HALF, LANES, 1, 1)),
                    ((1, 0, HALF, 1, 1), (1, HALF, LANES, 0, 0)))
        body = functools.partial(attn_kernel, plan=plan, n_chunks=n_chunks)
    return pl.pallas_call(
        body,
        grid=(b, sq // tq),
        in_specs=specs,
        out_specs=pl.BlockSpec((None, tq, MIX_COLS), lambda i, t: (i, t, 0)),
        out_shape=jax.ShapeDtypeStruct((b, sq, MIX_COLS), BF16),
        compiler_params=pltpu.CompilerParams(
            dimension_semantics=("arbitrary", "arbitrary"), vmem_limit_bytes=VMEM_LIMIT),
    )(*args)


def swa_kernel(*refs, s_lat):
    if s_lat:
        sink_ref, q_ref, kc_ref, vc_ref, kl_ref, vl_ref, o_ref = refs
    else:
        sink_ref, q_ref, kc_ref, vc_ref, o_ref = refs
    tq = q_ref.shape[0]
    t = pl.program_id(1)
    n_win = tq // WINDOW + 2
    plan = (((0, 0, HALF, 0, 0), (0, HALF, LANES, 1, 1)),
            ((1, 0, HALF, 1, 1), (1, HALF, LANES, 0, 0)))
    chunks = []
    if s_lat:
        qpos = t * tq + lax.broadcasted_iota(jnp.int32, (tq, WINDOW), 0)
        for c in range(n_win):
            start = t * tq + (c - 1) * WINDOW
            inside = (start >= 0) & (start < s_lat)
            rows = pl.ds(pl.multiple_of(jnp.clip(start, 0, s_lat - WINDOW), WINDOW), WINDOW)
            kpos = start + lax.broadcasted_iota(jnp.int32, (tq, WINDOW), 1)
            chunks.append((rows, (jnp.abs(kpos - qpos) <= WINDOW) & inside))
    for ob, maps in enumerate(plan):
        res = []
        for hh, (qb, lo, hi, kb, vb) in enumerate(maps):
            qm = _mask_lanes(q_ref[:, qb * LANES:(qb + 1) * LANES], lo, hi)
            kcol = slice(kb * LANES, (kb + 1) * LANES)
            vcol = slice(vb * LANES, (vb + 1) * LANES)
            sink = sink_ref[2 * ob + hh]
            s_ctx = _dot_nt(qm, kc_ref[:, kcol])
            m = jnp.maximum(jnp.max(s_ctx, axis=-1, keepdims=True), sink)
            s_loc = []
            for rows, valid in chunks:
                s = jnp.where(valid, _dot_nt(qm, kl_ref[rows, kcol]), NEG_INF)
                m = jnp.maximum(m, jnp.max(s, axis=-1, keepdims=True))
                s_loc.append(s)
            p = jnp.exp(s_ctx - m)
            l = jnp.sum(p, axis=-1, keepdims=True) + jnp.exp(sink - m)
            acc = _dot(p.astype(BF16), vc_ref[:, vcol])
            for (rows, _), s in zip(chunks, s_loc):
                p = jnp.exp(s - m)
                l = l + jnp.sum(p, axis=-1, keepdims=True)
                acc = acc + _dot(p.astype(BF16), vl_ref[rows, vcol])
            res.append(acc / l)
        o_ref[:, ob * LANES:(ob + 1) * LANES] = _halves(res[0], res[1], res[0].shape).astype(BF16)


def swa_call(sink, qkv_q, qkv_ctx, qkv_lat):
    b, sq, _ = qkv_q.shape
    tq = TOK_TILE
    with_lat = qkv_lat is not None
    s_lat = qkv_lat.shape[1] if with_lat else 0
    specs = [pl.BlockSpec(memory_space=pltpu.SMEM)]
    specs += _qkv_specs(s_lat, tq, 256, 3, 256, 4, 256, 5, with_lat, 0)
    args = [sink, qkv_q, qkv_ctx, qkv_ctx] + ([qkv_lat, qkv_lat] if with_lat else [])
    return pl.pallas_call(
        functools.partial(swa_kernel, s_lat=s_lat),
        grid=(b, sq // tq),
        in_specs=specs,
        out_specs=pl.BlockSpec((None, tq, MIX_COLS), lambda i, t: (i, t, 0)),
        out_shape=jax.ShapeDtypeStruct((b, sq, MIX_COLS), BF16),
        compiler_params=pltpu.CompilerParams(
            dimension_semantics=("arbitrary", "arbitrary"), vmem_limit_bytes=VMEM_LIMIT),
    )(*args)


def mixout_kernel(x_ref, ya_ref, yb_ref, ym_ref, yd_ref, w_ref, mod_ref, gpost_ref,
                  gfpre_ref, x1_ref, h2_ref):
    y = jnp.concatenate([ya_ref[...], yb_ref[...], ym_ref[...], yd_ref[...]], axis=1)
    z = _dot(y, w_ref[...])
    x1 = x_ref[...] + mod_ref[2:3, :] * (z * _rms(z, D_MODEL) * gpost_ref[...])
    x1_ref[...] = x1
    h2 = x1 * _rms(x1, D_MODEL) * gfpre_ref[...]
    h2_ref[...] = (h2 * (1.0 + mod_ref[4:5, :]) + mod_ref[3:4, :]).astype(BF16)


def mixout_call(x, ys, mod, mod_row, lw):
    b, s, d = x.shape
    tq = TOK_TILE
    if mod_row is None:
        mod_map = lambda i, t: (i, 0, 0)
    else:
        mod_map = lambda i, t: (mod_row, 0, 0)
    const2 = lambda i, t: (0, 0)
    tok = lambda i, t: (i, t, 0)
    y_spec = pl.BlockSpec((None, tq, MIX_COLS), tok)
    return pl.pallas_call(
        mixout_kernel,
        grid=(b, s // tq),
        in_specs=[pl.BlockSpec((None, tq, d), tok), y_spec, y_spec, y_spec, y_spec,
                  pl.BlockSpec((4 * MIX_COLS, d), const2),
                  pl.BlockSpec((None, N_MOD, d), mod_map),
                  pl.BlockSpec((1, d), const2), pl.BlockSpec((1, d), const2)],
        out_specs=[pl.BlockSpec((None, tq, d), tok), pl.BlockSpec((None, tq, d), tok)],
        out_shape=[jax.ShapeDtypeStruct((b, s, d), F32), jax.ShapeDtypeStruct((b, s, d), BF16)],
        compiler_params=pltpu.CompilerParams(
            dimension_semantics=("arbitrary", "arbitrary"), vmem_limit_bytes=VMEM_LIMIT),
    )(x, *ys, lw["w_out"], mod, lw["g_mix_post"], lw["g_ffn_pre"])


def ffn_kernel(x1_ref, hc_ref, hp_ref, hn_ref, wup_ref, cw_ref, cb_ref, wdn_ref, mod_ref,
               gpost_ref, o_ref, acc_ref):
    t = pl.program_id(1)
    tq = hc_ref.shape[0]
    halo = hp_ref.shape[0]
    hp = jnp.where(t > 0, hp_ref[...], jnp.zeros_like(hp_ref))
    hn = jnp.where(t < pl.num_programs(1) - 1, hn_ref[...], jnp.zeros_like(hn_ref))
    hext = jnp.concatenate([hp, hc_ref[...], hn], axis=0)
    rows = hext.shape[0]
    acc_ref[...] = jnp.zeros_like(acc_ref)

    def body(j, carry):
        u = _dot(hext, wup_ref[j])
        w = cw_ref[j]
        conv = (pltpu.roll(u, 1, 0)[halo:halo + tq] * w[0:1]
                + u[halo:halo + tq] * w[1:2]
                + pltpu.roll(u, rows - 1, 0)[halo:halo + tq] * w[2:3] + cb_ref[j])
        a = conv[:, :FFN_CHUNK]
        gated = a * (1.0 / (1.0 + jnp.exp(-a))) * conv[:, FFN_CHUNK:]
        acc_ref[...] += _dot(gated.astype(BF16), wdn_ref[j])
        return carry

    lax.fori_loop(0, wup_ref.shape[0], body, 0)
    f = acc_ref[...]
    o_ref[...] = x1_ref[...] + mod_ref[5:6, :] * (f * _rms(f, D_MODEL) * gpost_ref[...])


def ffn_call(x1, h2, mod, mod_row, lw):
    b, s, d = x1.shape
    tq = min(FFN_TILE, s)
    halo = BF16_SUBLANES
    per_tile = tq // halo
    last = s // halo - 1
    n_ch = FFN_DIM // FFN_CHUNK
    if mod_row is None:
        mod_map = lambda i, t: (i, 0, 0)
    else:
        mod_map = lambda i, t: (mod_row, 0, 0)
    const2 = lambda i, t: (0, 0)
    const3 = lambda i, t: (0, 0, 0)
    tok = lambda i, t: (i, t, 0)
    return pl.pallas_call(
        ffn_kernel,
        grid=(b, s // tq),
        in_specs=[
            pl.BlockSpec((None, tq, d), tok),
            pl.BlockSpec((None, tq, d), tok),
            pl.BlockSpec((None, halo, d), lambda i, t: (i, jnp.maximum(t * per_tile - 1, 0), 0)),
            pl.BlockSpec((None, halo, d), lambda i, t: (i, jnp.minimum((t + 1) * per_tile, last), 0)),
            pl.BlockSpec((n_ch, d, 2 * FFN_CHUNK), const3),
            pl.BlockSpec((n_ch, 3, 2 * FFN_CHUNK), const3),
            pl.BlockSpec((n_ch, 1, 2 * FFN_CHUNK), const3),
            pl.BlockSpec((n_ch, FFN_CHUNK, d), const3),
            pl.BlockSpec((None, N_MOD, d), mod_map),
            pl.BlockSpec((1, d), const2),
        ],
        out_specs=pl.BlockSpec((None, tq, d), tok),
        out_shape=jax.ShapeDtypeStruct((b, s, d), F32),
        scratch_shapes=[pltpu.VMEM((tq, d), F32)],
        compiler_params=pltpu.CompilerParams(
            dimension_semantics=("arbitrary", "arbitrary"), vmem_limit_bytes=VMEM_LIMIT),
    )(x1, h2, h2, h2, lw["w_up"], lw["conv_w"], lw["conv_b"], lw["w_down"], mod,
      lw["g_ffn_post"])


def _rope_table(s, rot_dim, lane_lo, lane_hi, period):
    n = rot_dim // 4
    tok = np.arange(s)
    pos = np.stack([tok // GRID_W, tok % GRID_W], axis=1).astype(np.float32)
    lane = np.arange(LANES)
    o = (lane - lane_lo) % period
    axis = o // (2 * n)
    second = (o % (2 * n)) // n
    active = (lane >= lane_lo) & (lane < lane_hi)
    inv_freq = ROPE_THETA ** (-jnp.arange(n, dtype=F32) / n)
    ang = jnp.asarray(pos)[:, axis] * inv_freq[o % n][None, :]
    cos = jnp.where(active[None], jnp.cos(ang), 1.0)
    sin = jnp.where(active[None], jnp.sin(ang), 0.0)
    sa = jnp.where((second == 0)[None], -sin, 0.0)
    sb = jnp.where((second == 1)[None], sin, 0.0)
    return jnp.stack([cos, sa, sb]).astype(F32)


def _layer_weights(l, w):
    d = D_MODEL
    w_in = w["w_in"][l]
    mla0 = 768 + 512
    cq_hi = mla0 + MLA_Q_RANK
    kr0 = cq_hi + MLA_KV_RANK
    w_in_p = jnp.concatenate([
        w_in[:, :mla0 + LANES],
        w_in[:, mla0 + LANES:cq_hi], w_in[:, kr0:kr0 + MLA_ROPE_DIM],
        jnp.zeros((d, LANES - HALF - MLA_ROPE_DIM), F32),
        w_in[:, cq_hi:kr0],
        w_in[:, kr0 + MLA_ROPE_DIM:],
    ], axis=1)
    dqk = MLA_NOPE_DIM + MLA_ROPE_DIM
    wq = jnp.zeros((2 * LANES, MLA_HEADS, LANES), F32)
    wq = wq.at[:MLA_Q_RANK, :, :dqk].set(w["mla_w_uq"][l].reshape(MLA_Q_RANK, MLA_HEADS, dqk))
    wukv = w["mla_w_ukv"][l].reshape(MLA_KV_RANK, MLA_HEADS, MLA_NOPE_DIM + MLA_V_DIM)
    wk = jnp.zeros((2 * LANES, MLA_HEADS, LANES), F32)
    wk = wk.at[:MLA_KV_RANK, :, :MLA_NOPE_DIM].set(wukv[:, :, :MLA_NOPE_DIM])
    place = jnp.eye(MLA_ROPE_DIM, dtype=F32)[:, None, :]
    wk = wk.at[LANES + HALF:LANES + HALF + MLA_ROPE_DIM, :, MLA_NOPE_DIM:dqk].set(
        jnp.broadcast_to(place, (MLA_ROPE_DIM, MLA_HEADS, MLA_ROPE_DIM)))
    wv = jnp.zeros((2 * LANES, MLA_HEADS, MLA_V_DIM), F32)
    wv = wv.at[:MLA_KV_RANK].set(wukv[:, :, MLA_NOPE_DIM:])
    wkv = jnp.concatenate([wk.reshape(2 * LANES, -1), wv.reshape(2 * LANES, -1)], axis=1)
    gq = jnp.zeros((1, 2 * LANES), F32).at[0, :MLA_Q_RANK].set(w["mla_g_q"][l])
    n_ch = FFN_DIM // FFN_CHUNK

    def chunked(a):
        lead = a.shape[:-1]
        a = a.reshape(lead + (2, n_ch, FFN_CHUNK))
        a = jnp.moveaxis(a, -2, 0)
        return a.reshape((n_ch,) + lead + (2 * FFN_CHUNK,))

    return {
        "g_mix_pre": w["g_mix_pre"][l][None], "g_mix_post": w["g_mix_post"][l][None],
        "g_ffn_pre": w["g_ffn_pre"][l][None], "g_ffn_post": w["g_ffn_post"][l][None],
        "w_in": w_in_p.astype(BF16),
        "wq": wq.reshape(2 * LANES, -1).astype(BF16), "wkv": wkv.astype(BF16),
        "gq": gq, "gkv": w["mla_g_kv"][l][None],
        "gnq": jnp.tile(w["gqa_g_q"][l], 2)[None], "gnk": jnp.tile(w["gqa_g_k"][l], 2)[None],
        "lam": jnp.stack([w["diff_lam_q1"][l], w["diff_lam_k1"][l],
                          w["diff_lam_q2"][l], w["diff_lam_k2"][l]]),
        "g_sub": jnp.tile(w["diff_g_sub"][l], 2)[None],
        "sink": w["swa_sink"][l],
        "w_out": w["w_out"][l].astype(BF16),
        "w_up": chunked(w["ffn_w_up"][l]).astype(BF16),
        "conv_w": chunked(w["ffn_conv_w"][l]),
        "conv_b": chunked(w["ffn_conv_b"][l][None]),
        "w_down": w["ffn_w_down"][l].reshape(n_ch, FFN_CHUNK, d).astype(BF16),
    }


def kernel(x, c, ctx, c_ctx, w_mod, b_mod, g_mix_pre, g_mix_post, g_ffn_pre, g_ffn_post, w_in, diff_lam_q1, diff_lam_k1, diff_lam_q2, diff_lam_k2, diff_g_sub, swa_sink, mla_g_q, mla_g_kv, mla_w_uq, mla_w_ukv, gqa_g_q, gqa_g_k, w_out, ffn_w_up, ffn_conv_w, ffn_conv_b, ffn_w_down):
    w = dict(g_mix_pre=g_mix_pre, g_mix_post=g_mix_post, g_ffn_pre=g_ffn_pre,
             g_ffn_post=g_ffn_post, w_in=w_in, diff_lam_q1=diff_lam_q1, diff_lam_k1=diff_lam_k1,
             diff_lam_q2=diff_lam_q2, diff_lam_k2=diff_lam_k2, diff_g_sub=diff_g_sub,
             swa_sink=swa_sink, mla_g_q=mla_g_q, mla_g_kv=mla_g_kv, mla_w_uq=mla_w_uq,
             mla_w_ukv=mla_w_ukv, gqa_g_q=gqa_g_q, gqa_g_k=gqa_g_k, w_out=w_out,
             ffn_w_up=ffn_w_up, ffn_conv_w=ffn_conv_w, ffn_conv_b=ffn_conv_b,
             ffn_w_down=ffn_w_down)
    b, s, d = x.shape
    depth = w_mod.shape[0]
    ctx_row = b
    mod_rows = 16
    cc = jnp.zeros((mod_rows, d), F32).at[:b].set(c).at[ctx_row].set(c_ctx)
    mod_all = mod_call(cc, w_mod, b_mod).reshape(depth, mod_rows, N_MOD, d)

    tabs = (_rope_table(s, DIFF_QK_DIM, 0, LANES, DIFF_QK_DIM),
            _rope_table(s, GQA_HEAD_DIM, 0, LANES, GQA_HEAD_DIM),
            _rope_table(s, MLA_ROPE_DIM, MLA_NOPE_DIM, MLA_NOPE_DIM + MLA_ROPE_DIM, MLA_ROPE_DIM))

    xc = ctx
    for l in range(depth):
        need_ctx = l < depth - 1
        lambda_init = 0.8 - 0.6 * math.exp(-0.3 * l)
        lw = _layer_weights(l, w)
        mod = mod_all[l]
        qkv = proj_call(x, mod, None, lw, tabs, True)
        qkv_c = proj_call(xc, mod, ctx_row, lw, tabs, False)
        diff_extra = (lw["lam"], lw["g_sub"])
        ys = (dense_attn_call("diff", qkv, qkv_c, qkv, diff_extra, lambda_init),
              swa_call(lw["sink"], qkv, qkv_c, qkv),
              dense_attn_call("mla", qkv, qkv_c, qkv),
              dense_attn_call("gqa", qkv, qkv_c, qkv))
        x1, h2 = mixout_call(x, ys, mod, None, lw)
        x = ffn_call(x1, h2, mod, None, lw)
        if need_ctx:
            ysc = (dense_attn_call("diff", qkv_c, qkv_c, None, diff_extra, lambda_init),
                   swa_call(lw["sink"], qkv_c, qkv_c, None),
                   dense_attn_call("mla", qkv_c, qkv_c, None),
                   dense_attn_call("gqa", qkv_c, qkv_c, None))
            xc1, hc2 = mixout_call(xc, ysc, mod, ctx_row, lw)
            xc = ffn_call(xc1, hc2, mod, ctx_row, lw)
    return x
```

```python
import functools
import math

import jax
import jax.numpy as jnp
import numpy as np
from jax import lax
from jax.experimental import pallas as pl
from jax.experimental.pallas import tpu as pltpu

F32 = jnp.float32
BF16 = jnp.bfloat16

D_MODEL = 1024
GRID_W = 64
ROPE_THETA = 10000.0
NORM_EPS = 1e-6
NEG_INF = -1e30
N_MOD = 6

DIFF_HEADS, DIFF_QK_DIM, DIFF_V_DIM = 4, 32, 64
SWA_HEAD_DIM, WINDOW = 64, 128
MLA_Q_RANK, MLA_KV_RANK = 192, 128
MLA_NOPE_DIM, MLA_ROPE_DIM, MLA_V_DIM, MLA_HEADS = 64, 32, 64, 4
GQA_HEAD_DIM = 64
FFN_DIM = 2816
MIX_COLS = 256

LANES = 128
BF16_SUBLANES = 16
HALF = LANES // 2
VMEM_LIMIT = 56 * 1024 * 1024

P_COLS = 17 * LANES
G_QD, G_QM, G_KM, G_VM, G_VD, G_QG = 0, 8, 12, 16, 20, 24
G_KD, G_QS, G_KS, G_VS, G_VG, G_KG = 28, 30, 32, 34, 36, 38
QKV_COLS = 39 * LANES
LOG2E = math.log2(math.e)

TOK_TILE = 256
KV_CHUNK = 512
FFN_TILE = 512
FFN_CHUNK = 256


def _rms(x, n):
    return lax.rsqrt(jnp.sum(x * x, axis=-1, keepdims=True) * (1.0 / n) + NORM_EPS)


def _lane(shape):
    return lax.broadcasted_iota(jnp.int32, shape, len(shape) - 1)


def _dot(a, b):
    return jnp.dot(a, b, preferred_element_type=F32)


def _dot_nt(a, b):
    return lax.dot_general(a, b, (((1,), (1,)), ((), ())), preferred_element_type=F32)


def mod_kernel(c_ref, w_ref, b_ref, o_ref):
    c = c_ref[...]
    s = c * (1.0 / (1.0 + jnp.exp(-c)))
    o_ref[...] = jnp.dot(s, w_ref[...], preferred_element_type=F32,
                         precision=lax.Precision.HIGHEST) + b_ref[...]


def mod_call(cc, w_mod, b_mod):
    depth, d, n = w_mod.shape
    rows = cc.shape[0]
    tn = 1024
    return pl.pallas_call(
        mod_kernel,
        name="mod_vectors",
        grid=(depth, n // tn),
        in_specs=[
            pl.BlockSpec((rows, d), lambda l, j: (0, 0)),
            pl.BlockSpec((None, d, tn), lambda l, j: (l, 0, j)),
            pl.BlockSpec((None, 1, tn), lambda l, j: (l, 0, j)),
        ],
        out_specs=pl.BlockSpec((None, rows, tn), lambda l, j: (l, 0, j)),
        out_shape=jax.ShapeDtypeStruct((depth, rows, n), F32),
        compiler_params=pltpu.CompilerParams(
            dimension_semantics=("arbitrary", "arbitrary"), vmem_limit_bytes=VMEM_LIMIT),
    )(cc, w_mod, b_mod.reshape(depth, 1, n))


def _rope(x, tab_ref, n):
    return (x * tab_ref[0] + pltpu.roll(x, LANES - n, 1) * tab_ref[1]
            + pltpu.roll(x, n, 1) * tab_ref[2])


def _halves(lo, hi, shape):
    return jnp.where(_lane(shape) < HALF, lo, hi)


def _head_rms(x):
    low = _lane(x.shape) < HALF
    sq = x * x
    ss_lo = jnp.sum(jnp.where(low, sq, 0.0), axis=-1, keepdims=True)
    ss_hi = jnp.sum(jnp.where(low, 0.0, sq), axis=-1, keepdims=True)
    r_lo = lax.rsqrt(ss_lo * (1.0 / HALF) + NORM_EPS)
    r_hi = lax.rsqrt(ss_hi * (1.0 / HALF) + NORM_EPS)
    return jnp.where(low, r_lo, r_hi)


def proj_kernel(x_ref, mod_ref, gpre_ref, win_ref, wq_ref, wkv_ref, gq_ref, gkv_ref,
                gnq_ref, gnk_ref, t32_ref, t64_ref, tml_ref, o_ref, *, rotate):
    x = x_ref[...]
    h = x * _rms(x, D_MODEL) * gpre_ref[...]
    h = h * (1.0 + mod_ref[1:2, :]) + mod_ref[0:1, :]
    p = _dot(h.astype(BF16), win_ref[...])

    def grp(g):
        return p[:, g * LANES:(g + 1) * LANES]

    def put(g, v):
        o_ref[:, g * LANES:(g + 1) * LANES] = v.astype(BF16)

    def rope32(v):
        return _rope(v, t32_ref, DIFF_QK_DIM // 4) if rotate else v

    def rope64(v):
        return _rope(v, t64_ref, GQA_HEAD_DIM // 4) if rotate else v

    def ropeml(v):
        return _rope(v, tml_ref, MLA_ROPE_DIM // 4) if rotate else v

    lane = _lane((x.shape[0], LANES))
    low = lane < HALF

    def with_ones(v):
        return jnp.where(low, v, 1.0), jnp.where(low, 1.0, v)

    for g in range(2):
        qd = rope32(grp(g)) * (DIFF_QK_DIM ** -0.5 * LOG2E)
        for i in range(4):
            keep = (lane >= i * DIFF_QK_DIM) & (lane < (i + 1) * DIFF_QK_DIM)
            put(G_QD + 4 * g + i, jnp.where(keep, qd, 0.0))
        put(G_KD + g, rope32(grp(2 + g)))
        v_lo, v_hi = with_ones(grp(4 + g))
        put(G_VD + 2 * g, v_lo)
        put(G_VD + 2 * g + 1, v_hi)

    for g in range(2):
        put(G_QS + g, rope64(grp(6 + g)) * (SWA_HEAD_DIM ** -0.5))
    ks = rope64(grp(8))
    put(G_KS, ks)
    put(G_KS + 1, pltpu.roll(ks, HALF, 1))
    put(G_VS, grp(9))
    put(G_VS + 1, pltpu.roll(grp(9), HALF, 1))

    cq0, cq1, ckv = grp(10), grp(11), grp(12)
    ss = (jnp.sum(cq0 * cq0, axis=-1, keepdims=True)
          + jnp.sum(jnp.where(low, cq1 * cq1, 0.0), axis=-1, keepdims=True))
    rq = lax.rsqrt(ss * (1.0 / MLA_Q_RANK) + NORM_EPS)
    cqn = jnp.concatenate([cq0 * rq * gq_ref[:, :LANES], cq1 * rq * gq_ref[:, LANES:]], axis=1)
    qm = _dot(cqn.astype(BF16), wq_ref[...])
    scale_m = (MLA_NOPE_DIM + MLA_ROPE_DIM) ** -0.5 * LOG2E
    for hd in range(MLA_HEADS):
        put(G_QM + hd, ropeml(qm[:, hd * LANES:(hd + 1) * LANES]) * scale_m)
    ckvn = ckv * _rms(ckv, MLA_KV_RANK) * gkv_ref[...]
    kvin = jnp.concatenate([ckvn, ropeml(cq1)], axis=1)
    kv = _dot(kvin.astype(BF16), wkv_ref[...])
    for hd in range(MLA_HEADS):
        put(G_KM + hd, kv[:, hd * LANES:(hd + 1) * LANES])
    for g in range(2):
        v_lo, v_hi = with_ones(kv[:, (MLA_HEADS + g) * LANES:(MLA_HEADS + g + 1) * LANES])
        put(G_VM + 2 * g, v_lo)
        put(G_VM + 2 * g + 1, v_hi)

    scale_g = GQA_HEAD_DIM ** -0.5 * LOG2E
    q01 = grp(13)
    q01 = rope64(q01 * _head_rms(q01) * gnq_ref[...]) * scale_g
    q23 = grp(14)
    q23 = rope64(q23 * _head_rms(q23) * gnq_ref[...]) * scale_g
    put(G_QG, jnp.where(low, q01, 0.0))
    put(G_QG + 1, jnp.where(low, pltpu.roll(q01, HALF, 1), 0.0))
    put(G_QG + 2, jnp.where(low, 0.0, pltpu.roll(q23, HALF, 1)))
    put(G_QG + 3, jnp.where(low, 0.0, q23))
    kg = grp(15)
    put(G_KG, rope64(kg * _head_rms(kg) * gnk_ref[...]))
    v_lo, v_hi = with_ones(grp(16))
    put(G_VG, v_lo)
    put(G_VG + 1, v_hi)


def proj_call(x, mod, mod_row, lw, tabs, rotate):
    b, s, d = x.shape
    tq = TOK_TILE
    if mod_row is None:
        mod_map = lambda i, t: (i, 0, 0)
    else:
        mod_map = lambda i, t: (mod_row, 0, 0)
    const2 = lambda i, t: (0, 0)
    tab_spec = pl.BlockSpec((3, tq, LANES), lambda i, t: (0, t, 0))
    return pl.pallas_call(
        functools.partial(proj_kernel, rotate=rotate),
        name="proj" if rotate else "proj_ctx",
        grid=(b, s // tq),
        in_specs=[
            pl.BlockSpec((None, tq, d), lambda i, t: (i, t, 0)),
            pl.BlockSpec((None, N_MOD, d), mod_map),
            pl.BlockSpec((1, d), const2),
            pl.BlockSpec((d, P_COLS), const2),
            pl.BlockSpec((2 * LANES, 4 * LANES), const2),
            pl.BlockSpec((2 * LANES, 6 * LANES), const2),
            pl.BlockSpec((1, 2 * LANES), const2),
            pl.BlockSpec((1, LANES), const2),
            pl.BlockSpec((1, LANES), const2),
            pl.BlockSpec((1, LANES), const2),
            tab_spec, tab_spec, tab_spec,
        ],
        out_specs=pl.BlockSpec((None, tq, QKV_COLS), lambda i, t: (i, t, 0)),
        out_shape=jax.ShapeDtypeStruct((b, s, QKV_COLS), BF16),
        compiler_params=pltpu.CompilerParams(
            dimension_semantics=("arbitrary", "arbitrary"), vmem_limit_bytes=VMEM_LIMIT),
    )(x, mod, lw["g_mix_pre"], lw["w_in"], lw["wq"], lw["wkv"], lw["gq"], lw["gkv"],
      lw["gnq"], lw["gnk"], tabs[0], tabs[1], tabs[2])


def _group(ref, rows, g):
    return ref[rows, g * LANES:(g + 1) * LANES]


def _flash_step(q, k, parts, v_of, state):
    s = _dot_nt(q, k)
    s_max = jnp.max(s, axis=-1, keepdims=True)
    m_new = s_max if state is None else jnp.maximum(state[0], s_max)
    p = jnp.exp2(s - m_new).astype(BF16)
    tq = q.shape[0] // sum(n for n, _ in parts)
    pv, r0 = [], 0
    for n, vb in parts:
        pv.append(_dot(p[r0:r0 + n * tq], v_of(vb)))
        r0 += n * tq
    pv = pv[0] if len(pv) == 1 else jnp.concatenate(pv, axis=0)
    if state is None:
        return m_new, pv
    return m_new, state[1] * jnp.exp2(state[0] - m_new) + pv


def _flash_loop(q_ref, segs, kc_ref, vc_ref, kl_ref, vl_ref, n_chunks):
    qs = [jnp.concatenate([_group(q_ref, slice(None), g) for g in qg], axis=0) for qg, _, _ in segs]
    ctx_rows = slice(None)
    states = [_flash_step(q, _group(kc_ref, ctx_rows, kb), parts,
                          lambda vb: _group(vc_ref, ctx_rows, vb), None)
              for q, (_, kb, parts) in zip(qs, segs)]

    def body(i, carry):
        rows = pl.ds(pl.multiple_of(i * KV_CHUNK, KV_CHUNK), KV_CHUNK)
        out = []
        for j, (q, (_, kb, parts)) in enumerate(zip(qs, segs)):
            out += _flash_step(q, _group(kl_ref, rows, kb), parts,
                               lambda vb: _group(vl_ref, rows, vb), carry[2 * j:2 * j + 2])
        return tuple(out)

    flat = tuple(x for st in states for x in st)
    if n_chunks:
        flat = lax.fori_loop(0, n_chunks, body, flat)
    return [flat[2 * j + 1] / pltpu.roll(flat[2 * j + 1], HALF, 1) for j in range(len(segs))]


_PLANS = {
    "gqa": ((((0, 1, 2, 3), 0, ((2, 0), (2, 1))),),),
    "mla": (tuple(((h,), h, ((1, h),)) for h in range(MLA_HEADS)),),
    "diff": ((((0, 1, 2, 3), 0, ((2, 0), (2, 1))),), (((4, 5, 6, 7), 1, ((2, 2), (2, 3))),)),
}


def dense_kernel(*refs, kind, n_chunks, lambda_init):
    refs = list(refs)
    if kind == "diff":
        lam_ref, g_ref = refs[:2]
        refs = refs[2:]
    if n_chunks:
        q_ref, kc_ref, vc_ref, kl_ref, vl_ref, o_ref = refs
    else:
        q_ref, kc_ref, vc_ref, o_ref = refs
        kl_ref = vl_ref = None
    tq = q_ref.shape[0]
    res = [_flash_loop(q_ref, segs, kc_ref, vc_ref, kl_ref, vl_ref, n_chunks)
           for segs in _PLANS[kind]]

    def tiles(r):
        return [r[i * tq:(i + 1) * tq] for i in range(r.shape[0] // tq)]

    if kind == "gqa":
        h = tiles(res[0][0])
        blocks = [_halves(h[0], pltpu.roll(h[1], HALF, 1), h[0].shape),
                  _halves(pltpu.roll(h[2], HALF, 1), h[3], h[0].shape)]
    elif kind == "mla":
        h = res[0]
        blocks = [_halves(h[0], h[1], h[0].shape), _halves(h[2], h[3], h[0].shape)]
    else:
        lv = lam_ref[...]
        lam = (jnp.exp(jnp.sum(lv[0:1] * lv[1:2], axis=-1, keepdims=True))
               - jnp.exp(jnp.sum(lv[2:3] * lv[3:4], axis=-1, keepdims=True)) + lambda_init)
        blocks = []
        for (r,) in res:
            a = tiles(r)
            o = _halves(a[0] - lam * a[1], a[2] - lam * a[3], a[0].shape)
            blocks.append(o * _head_rms(o) * g_ref[...] * (1.0 - lambda_init))
    for ob, blk in enumerate(blocks):
        o_ref[:, ob * LANES:(ob + 1) * LANES] = blk.astype(BF16)


def _qkv_specs(s_lat, tq, q_cols, q_blk, k_cols, k_blk, v_cols, v_blk, with_lat, q_off):
    specs = [
        pl.BlockSpec((None, tq, q_cols), lambda i, t: (i, t + q_off, q_blk)),
        pl.BlockSpec((None, TOK_TILE, k_cols), lambda i, t: (i, 0, k_blk)),
        pl.BlockSpec((None, TOK_TILE, v_cols), lambda i, t: (i, 0, v_blk)),
    ]
    if with_lat:
        specs += [
            pl.BlockSpec((None, s_lat, k_cols), lambda i, t: (i, 0, k_blk)),
            pl.BlockSpec((None, s_lat, v_cols), lambda i, t: (i, 0, v_blk)),
        ]
    return specs


def dense_attn_call(kind, qkv_q, qkv_ctx, qkv_lat, extra=None, lambda_init=None):
    b, sq, _ = qkv_q.shape
    tq = TOK_TILE
    with_lat = qkv_lat is not None
    s_lat = qkv_lat.shape[1] if with_lat else 0
    n_chunks = s_lat // KV_CHUNK
    lay = {"diff": ((G_QD, 8), (G_KD, 2), (G_VD, 4)),
           "mla": ((G_QM, 4), (G_KM, 4), (G_VM, 4)),
           "gqa": ((G_QG, 4), (G_KG, 1), (G_VG, 2))}[kind]
    blocks = []
    for g0, n in lay:
        assert g0 % n == 0
        blocks += [n * LANES, g0 // n]
    specs = _qkv_specs(s_lat, tq, *blocks, with_lat, 0)
    args = [qkv_q, qkv_ctx, qkv_ctx] + ([qkv_lat, qkv_lat] if with_lat else [])
    if kind == "diff":
        specs = [pl.BlockSpec((4, DIFF_QK_DIM), lambda i, t: (0, 0)),
                 pl.BlockSpec((1, LANES), lambda i, t: (0, 0))] + specs
        args = list(extra) + args
    body = functools.partial(dense_kernel, kind=kind, n_chunks=n_chunks, lambda_init=lambda_init)
    return pl.pallas_call(
        body,
        name=kind + ("_attn" if with_lat else "_attn_ctx"),
        grid=(b, sq // tq),
        in_specs=specs,
        out_specs=pl.BlockSpec((None, tq, MIX_COLS), lambda i, t: (i, t, 0)),
        out_shape=jax.ShapeDtypeStruct((b, sq, MIX_COLS), BF16),
        compiler_params=pltpu.CompilerParams(
            dimension_semantics=("arbitrary", "arbitrary"), vmem_limit_bytes=VMEM_LIMIT),
    )(*args)


def _mask_lanes(q, lo, hi):
    lane = _lane(q.shape)
    return jnp.where((lane >= lo) & (lane < hi), q, jnp.zeros_like(q))


def swa_kernel(*refs, s_lat):
    if s_lat:
        sink_ref, q_ref, kc_ref, vc_ref, kl_ref, vl_ref, o_ref = refs
    else:
        sink_ref, q_ref, kc_ref, vc_ref, o_ref = refs
    tq = q_ref.shape[0]
    t = pl.program_id(1)
    n_win = tq // WINDOW + 2
    plan = (((0, 0, HALF, 0, 0), (0, HALF, LANES, 1, 1)),
            ((1, 0, HALF, 1, 1), (1, HALF, LANES, 0, 0)))
    chunks = []
    if s_lat:
        qpos = t * tq + lax.broadcasted_iota(jnp.int32, (tq, WINDOW), 0)
        for c in range(n_win):
            start = t * tq + (c - 1) * WINDOW
            inside = (start >= 0) & (start < s_lat)
            rows = pl.ds(pl.multiple_of(jnp.clip(start, 0, s_lat - WINDOW), WINDOW), WINDOW)
            kpos = start + lax.broadcasted_iota(jnp.int32, (tq, WINDOW), 1)
            chunks.append((rows, (jnp.abs(kpos - qpos) <= WINDOW) & inside))
    for ob, maps in enumerate(plan):
        res = []
        for hh, (qb, lo, hi, kb, vb) in enumerate(maps):
            qm = _mask_lanes(q_ref[:, qb * LANES:(qb + 1) * LANES], lo, hi)
            kcol = slice(kb * LANES, (kb + 1) * LANES)
            vcol = slice(vb * LANES, (vb + 1) * LANES)
            sink = sink_ref[2 * ob + hh]
            s_ctx = _dot_nt(qm, kc_ref[:, kcol])
            m = jnp.maximum(jnp.max(s_ctx, axis=-1, keepdims=True), sink)
            s_loc = []
            for rows, valid in chunks:
                s = jnp.where(valid, _dot_nt(qm, kl_ref[rows, kcol]), NEG_INF)
                m = jnp.maximum(m, jnp.max(s, axis=-1, keepdims=True))
                s_loc.append(s)
            p = jnp.exp(s_ctx - m)
            l = jnp.sum(p, axis=-1, keepdims=True) + jnp.exp(sink - m)
            acc = _dot(p.astype(BF16), vc_ref[:, vcol])
            for (rows, _), s in zip(chunks, s_loc):
                p = jnp.exp(s - m)
                l = l + jnp.sum(p, axis=-1, keepdims=True)
                acc = acc + _dot(p.astype(BF16), vl_ref[rows, vcol])
            res.append(acc / l)
        o_ref[:, ob * LANES:(ob + 1) * LANES] = _halves(res[0], res[1], res[0].shape).astype(BF16)


def swa_call(sink, qkv_q, qkv_ctx, qkv_lat):
    b, sq, _ = qkv_q.shape
    tq = TOK_TILE
    with_lat = qkv_lat is not None
    s_lat = qkv_lat.shape[1] if with_lat else 0
    specs = [pl.BlockSpec(memory_space=pltpu.SMEM)]
    specs += _qkv_specs(s_lat, tq, 2 * LANES, G_QS // 2, 2 * LANES, G_KS // 2, 2 * LANES,
                        G_VS // 2, with_lat, 0)
    args = [sink, qkv_q, qkv_ctx, qkv_ctx] + ([qkv_lat, qkv_lat] if with_lat else [])
    return pl.pallas_call(
        functools.partial(swa_kernel, s_lat=s_lat),
        name="swa_attn" if with_lat else "swa_attn_ctx",
        grid=(b, sq // tq),
        in_specs=specs,
        out_specs=pl.BlockSpec((None, tq, MIX_COLS), lambda i, t: (i, t, 0)),
        out_shape=jax.ShapeDtypeStruct((b, sq, MIX_COLS), BF16),
        compiler_params=pltpu.CompilerParams(
            dimension_semantics=("arbitrary", "arbitrary"), vmem_limit_bytes=VMEM_LIMIT),
    )(*args)


def mixout_kernel(x_ref, ya_ref, yb_ref, ym_ref, yd_ref, w_ref, mod_ref, gpost_ref,
                  gfpre_ref, x1_ref, h2_ref):
    y = jnp.concatenate([ya_ref[...], yb_ref[...], ym_ref[...], yd_ref[...]], axis=1)
    z = _dot(y, w_ref[...])
    x1 = x_ref[...] + mod_ref[2:3, :] * (z * _rms(z, D_MODEL) * gpost_ref[...])
    x1_ref[...] = x1
    h2 = x1 * _rms(x1, D_MODEL) * gfpre_ref[...]
    h2_ref[...] = (h2 * (1.0 + mod_ref[4:5, :]) + mod_ref[3:4, :]).astype(BF16)


def mixout_call(x, ys, mod, mod_row, lw):
    b, s, d = x.shape
    tq = TOK_TILE
    if mod_row is None:
        mod_map = lambda i, t: (i, 0, 0)
    else:
        mod_map = lambda i, t: (mod_row, 0, 0)
    const2 = lambda i, t: (0, 0)
    tok = lambda i, t: (i, t, 0)
    y_spec = pl.BlockSpec((None, tq, MIX_COLS), tok)
    return pl.pallas_call(
        mixout_kernel,
        name="mixout",
        grid=(b, s // tq),
        in_specs=[pl.BlockSpec((None, tq, d), tok), y_spec, y_spec, y_spec, y_spec,
                  pl.BlockSpec((4 * MIX_COLS, d), const2),
                  pl.BlockSpec((None, N_MOD, d), mod_map),
                  pl.BlockSpec((1, d), const2), pl.BlockSpec((1, d), const2)],
        out_specs=[pl.BlockSpec((None, tq, d), tok), pl.BlockSpec((None, tq, d), tok)],
        out_shape=[jax.ShapeDtypeStruct((b, s, d), F32), jax.ShapeDtypeStruct((b, s, d), BF16)],
        compiler_params=pltpu.CompilerParams(
            dimension_semantics=("arbitrary", "arbitrary"), vmem_limit_bytes=VMEM_LIMIT),
    )(x, *ys, lw["w_out"], mod, lw["g_mix_post"], lw["g_ffn_pre"])


def ffn_kernel(x1_ref, hc_ref, hp_ref, hn_ref, wup_ref, cw_ref, cb_ref, wdn_ref, mod_ref,
               gpost_ref, o_ref, acc_ref):
    t = pl.program_id(1)
    tq = hc_ref.shape[0]
    halo = hp_ref.shape[0]
    hp = jnp.where(t > 0, hp_ref[...], jnp.zeros_like(hp_ref))
    hn = jnp.where(t < pl.num_programs(1) - 1, hn_ref[...], jnp.zeros_like(hn_ref))
    hext = jnp.concatenate([hp, hc_ref[...], hn], axis=0)
    rows = hext.shape[0]
    acc_ref[...] = jnp.zeros_like(acc_ref)

    def body(j, carry):
        u = _dot(hext, wup_ref[j])
        w = cw_ref[j]
        conv = (pltpu.roll(u, 1, 0)[halo:halo + tq] * w[0:1]
                + u[halo:halo + tq] * w[1:2]
                + pltpu.roll(u, rows - 1, 0)[halo:halo + tq] * w[2:3] + cb_ref[j])
        a = conv[:, :FFN_CHUNK]
        gated = a * (1.0 / (1.0 + jnp.exp(-a))) * conv[:, FFN_CHUNK:]
        acc_ref[...] += _dot(gated.astype(BF16), wdn_ref[j])
        return carry

    lax.fori_loop(0, wup_ref.shape[0], body, 0)
    f = acc_ref[...]
    o_ref[...] = x1_ref[...] + mod_ref[5:6, :] * (f * _rms(f, D_MODEL) * gpost_ref[...])


def ffn_call(x1, h2, mod, mod_row, lw):
    b, s, d = x1.shape
    tq = min(FFN_TILE, s)
    halo = BF16_SUBLANES
    per_tile = tq // halo
    last = s // halo - 1
    n_ch = FFN_DIM // FFN_CHUNK
    if mod_row is None:
        mod_map = lambda i, t: (i, 0, 0)
    else:
        mod_map = lambda i, t: (mod_row, 0, 0)
    const2 = lambda i, t: (0, 0)
    const3 = lambda i, t: (0, 0, 0)
    tok = lambda i, t: (i, t, 0)
    return pl.pallas_call(
        ffn_kernel,
        name="conv_ffn",
        grid=(b, s // tq),
        in_specs=[
            pl.BlockSpec((None, tq, d), tok),
            pl.BlockSpec((None, tq, d), tok),
            pl.BlockSpec((None, halo, d), lambda i, t: (i, jnp.maximum(t * per_tile - 1, 0), 0)),
            pl.BlockSpec((None, halo, d), lambda i, t: (i, jnp.minimum((t + 1) * per_tile, last), 0)),
            pl.BlockSpec((n_ch, d, 2 * FFN_CHUNK), const3),
            pl.BlockSpec((n_ch, 3, 2 * FFN_CHUNK), const3),
            pl.BlockSpec((n_ch, 1, 2 * FFN_CHUNK), const3),
            pl.BlockSpec((n_ch, FFN_CHUNK, d), const3),
            pl.BlockSpec((None, N_MOD, d), mod_map),
            pl.BlockSpec((1, d), const2),
        ],
        out_specs=pl.BlockSpec((None, tq, d), tok),
        out_shape=jax.ShapeDtypeStruct((b, s, d), F32),
        scratch_shapes=[pltpu.VMEM((tq, d), F32)],
        compiler_params=pltpu.CompilerParams(
            dimension_semantics=("arbitrary", "arbitrary"), vmem_limit_bytes=VMEM_LIMIT),
    )(x1, h2, h2, h2, lw["w_up"], lw["conv_w"], lw["conv_b"], lw["w_down"], mod,
      lw["g_ffn_post"])


def _rope_table(s, rot_dim, lane_lo, lane_hi, period):
    n = rot_dim // 4
    tok = np.arange(s)
    pos = np.stack([tok // GRID_W, tok % GRID_W], axis=1).astype(np.float32)
    lane = np.arange(LANES)
    o = (lane - lane_lo) % period
    axis = o // (2 * n)
    second = (o % (2 * n)) // n
    active = (lane >= lane_lo) & (lane < lane_hi)
    inv_freq = ROPE_THETA ** (-jnp.arange(n, dtype=F32) / n)
    ang = jnp.asarray(pos)[:, axis] * inv_freq[o % n][None, :]
    cos = jnp.where(active[None], jnp.cos(ang), 1.0)
    sin = jnp.where(active[None], jnp.sin(ang), 0.0)
    sa = jnp.where((second == 0)[None], -sin, 0.0)
    sb = jnp.where((second == 1)[None], sin, 0.0)
    return jnp.stack([cos, sa, sb]).astype(F32)


def _layer_weights(l, w):
    d = D_MODEL
    w_in = w["w_in"][l]
    mla0 = 768 + 512
    cq_hi = mla0 + MLA_Q_RANK
    kr0 = cq_hi + MLA_KV_RANK
    w_in_p = jnp.concatenate([
        w_in[:, :mla0 + LANES],
        w_in[:, mla0 + LANES:cq_hi], w_in[:, kr0:kr0 + MLA_ROPE_DIM],
        jnp.zeros((d, LANES - HALF - MLA_ROPE_DIM), F32),
        w_in[:, cq_hi:kr0],
        w_in[:, kr0 + MLA_ROPE_DIM:],
    ], axis=1)
    dqk = MLA_NOPE_DIM + MLA_ROPE_DIM
    wq = jnp.zeros((2 * LANES, MLA_HEADS, LANES), F32)
    wq = wq.at[:MLA_Q_RANK, :, :dqk].set(w["mla_w_uq"][l].reshape(MLA_Q_RANK, MLA_HEADS, dqk))
    wukv = w["mla_w_ukv"][l].reshape(MLA_KV_RANK, MLA_HEADS, MLA_NOPE_DIM + MLA_V_DIM)
    wk = jnp.zeros((2 * LANES, MLA_HEADS, LANES), F32)
    wk = wk.at[:MLA_KV_RANK, :, :MLA_NOPE_DIM].set(wukv[:, :, :MLA_NOPE_DIM])
    place = jnp.eye(MLA_ROPE_DIM, dtype=F32)[:, None, :]
    wk = wk.at[LANES + HALF:LANES + HALF + MLA_ROPE_DIM, :, MLA_NOPE_DIM:dqk].set(
        jnp.broadcast_to(place, (MLA_ROPE_DIM, MLA_HEADS, MLA_ROPE_DIM)))
    wv = jnp.zeros((2 * LANES, MLA_HEADS, MLA_V_DIM), F32)
    wv = wv.at[:MLA_KV_RANK].set(wukv[:, :, MLA_NOPE_DIM:])
    wkv = jnp.concatenate([wk.reshape(2 * LANES, -1), wv.reshape(2 * LANES, -1)], axis=1)
    gq = jnp.zeros((1, 2 * LANES), F32).at[0, :MLA_Q_RANK].set(w["mla_g_q"][l])
    n_ch = FFN_DIM // FFN_CHUNK

    def chunked(a):
        lead = a.shape[:-1]
        a = a.reshape(lead + (2, n_ch, FFN_CHUNK))
        a = jnp.moveaxis(a, -2, 0)
        return a.reshape((n_ch,) + lead + (2 * FFN_CHUNK,))

    return {
        "g_mix_pre": w["g_mix_pre"][l][None], "g_mix_post": w["g_mix_post"][l][None],
        "g_ffn_pre": w["g_ffn_pre"][l][None], "g_ffn_post": w["g_ffn_post"][l][None],
        "w_in": w_in_p.astype(BF16),
        "wq": wq.reshape(2 * LANES, -1).astype(BF16), "wkv": wkv.astype(BF16),
        "gq": gq, "gkv": w["mla_g_kv"][l][None],
        "gnq": jnp.tile(w["gqa_g_q"][l], 2)[None], "gnk": jnp.tile(w["gqa_g_k"][l], 2)[None],
        "lam": jnp.stack([w["diff_lam_q1"][l], w["diff_lam_k1"][l],
                          w["diff_lam_q2"][l], w["diff_lam_k2"][l]]),
        "g_sub": jnp.tile(w["diff_g_sub"][l], 2)[None],
        "sink": w["swa_sink"][l],
        "w_out": w["w_out"][l].astype(BF16),
        "w_up": chunked(w["ffn_w_up"][l]).astype(BF16),
        "conv_w": chunked(w["ffn_conv_w"][l]),
        "conv_b": chunked(w["ffn_conv_b"][l][None]),
        "w_down": w["ffn_w_down"][l].reshape(n_ch, FFN_CHUNK, d).astype(BF16),
    }


def kernel(x, c, ctx, c_ctx, w_mod, b_mod, g_mix_pre, g_mix_post, g_ffn_pre, g_ffn_post, w_in, diff_lam_q1, diff_lam_k1, diff_lam_q2, diff_lam_k2, diff_g_sub, swa_sink, mla_g_q, mla_g_kv, mla_w_uq, mla_w_ukv, gqa_g_q, gqa_g_k, w_out, ffn_w_up, ffn_conv_w, ffn_conv_b, ffn_w_down):
    w = dict(g_mix_pre=g_mix_pre, g_mix_post=g_mix_post, g_ffn_pre=g_ffn_pre,
             g_ffn_post=g_ffn_post, w_in=w_in, diff_lam_q1=diff_lam_q1, diff_lam_k1=diff_lam_k1,
             diff_lam_q2=diff_lam_q2, diff_lam_k2=diff_lam_k2, diff_g_sub=diff_g_sub,
             swa_sink=swa_sink, mla_g_q=mla_g_q, mla_g_kv=mla_g_kv, mla_w_uq=mla_w_uq,
             mla_w_ukv=mla_w_ukv, gqa_g_q=gqa_g_q, gqa_g_k=gqa_g_k, w_out=w_out,
             ffn_w_up=ffn_w_up, ffn_conv_w=ffn_conv_w, ffn_conv_b=ffn_conv_b,
             ffn_w_down=ffn_w_down)
    b, s, d = x.shape
    depth = w_mod.shape[0]
    ctx_row = b
    mod_rows = 16
    cc = jnp.zeros((mod_rows, d), F32).at[:b].set(c).at[ctx_row].set(c_ctx)
    mod_all = mod_call(cc, w_mod, b_mod).reshape(depth, mod_rows, N_MOD, d)

    tabs = (_rope_table(s, DIFF_QK_DIM, 0, LANES, DIFF_QK_DIM),
            _rope_table(s, GQA_HEAD_DIM, 0, LANES, GQA_HEAD_DIM),
            _rope_table(s, MLA_ROPE_DIM, MLA_NOPE_DIM, MLA_NOPE_DIM + MLA_ROPE_DIM, MLA_ROPE_DIM))

    xc = ctx
    for l in range(depth):
        need_ctx = l < depth - 1
        lambda_init = 0.8 - 0.6 * math.exp(-0.3 * l)
        lw = _layer_weights(l, w)
        mod = mod_all[l]
        qkv = proj_call(x, mod, None, lw, tabs, True)
        qkv_c = proj_call(xc, mod, ctx_row, lw, tabs, False)
        diff_extra = (lw["lam"], lw["g_sub"])
        ys = (dense_attn_call("diff", qkv, qkv_c, qkv, diff_extra, lambda_init),
              swa_call(lw["sink"], qkv, qkv_c, qkv),
              dense_attn_call("mla", qkv, qkv_c, qkv),
              dense_attn_call("gqa", qkv, qkv_c, qkv))
        x1, h2 = mixout_call(x, ys, mod, None, lw)
        x = ffn_call(x1, h2, mod, None, lw)
        if need_ctx:
            ysc = (dense_attn_call("diff", qkv_c, qkv_c, None, diff_extra, lambda_init),
                   swa_call(lw["sink"], qkv_c, qkv_c, None),
                   dense_attn_call("mla", qkv_c, qkv_c, None),
                   dense_attn_call("gqa", qkv_c, qkv_c, None))
            xc1, hc2 = mixout_call(xc, ysc, mod, ctx_row, lw)
            xc = ffn_call(xc1, hc2, mod, ctx_row, lw)
    return x
```

```python
import functools
import math

import jax
import jax.numpy as jnp
import numpy as np
from jax import lax
from jax.experimental import pallas as pl
from jax.experimental.pallas import tpu as pltpu

F32 = jnp.float32
BF16 = jnp.bfloat16

D_MODEL = 1024
GRID_W = 64
ROPE_THETA = 10000.0
NORM_EPS = 1e-6
NEG_INF = -1e30
N_MOD = 6

DIFF_HEADS, DIFF_QK_DIM, DIFF_V_DIM = 4, 32, 64
SWA_HEAD_DIM, WINDOW = 64, 128
MLA_Q_RANK, MLA_KV_RANK = 192, 128
MLA_NOPE_DIM, MLA_ROPE_DIM, MLA_V_DIM, MLA_HEADS = 64, 32, 64, 4
GQA_HEAD_DIM = 64
FFN_DIM = 2816
MIX_COLS = 256

LANES = 128
BF16_SUBLANES = 16
HALF = LANES // 2
VMEM_LIMIT = 56 * 1024 * 1024

P_COLS = 17 * LANES
G_QD, G_QM, G_KM, G_VM, G_VD, G_QG = 0, 8, 12, 16, 20, 24
G_KD, G_QS, G_KS, G_VS, G_VG, G_KG = 28, 30, 32, 34, 36, 38
QKV_COLS = 39 * LANES
LOG2E = math.log2(math.e)

TOK_TILE = 256
CTX_TILE = 256
KV_CHUNKS = 2
FFN_TILE = 512
FFN_CHUNK = 256


def _rms(x, n):
    return lax.rsqrt(jnp.sum(x * x, axis=-1, keepdims=True) * (1.0 / n) + NORM_EPS)


def _lane(shape):
    return lax.broadcasted_iota(jnp.int32, shape, len(shape) - 1)


def _dot(a, b):
    return jnp.dot(a, b, preferred_element_type=F32)


def _dot_nt(a, b):
    return lax.dot_general(a, b, (((1,), (1,)), ((), ())), preferred_element_type=F32)


def mod_kernel(c_ref, w_ref, b_ref, o_ref):
    c = c_ref[...]
    s = c * (1.0 / (1.0 + jnp.exp(-c)))
    o_ref[...] = jnp.dot(s, w_ref[...], preferred_element_type=F32,
                         precision=lax.Precision.HIGHEST) + b_ref[...]


def mod_call(cc, w_mod, b_mod):
    depth, d, n = w_mod.shape
    rows = cc.shape[0]
    tn = 1024
    return pl.pallas_call(
        mod_kernel,
        name="mod_vectors",
        grid=(depth, n // tn),
        in_specs=[
            pl.BlockSpec((rows, d), lambda l, j: (0, 0)),
            pl.BlockSpec((None, d, tn), lambda l, j: (l, 0, j)),
            pl.BlockSpec((None, 1, tn), lambda l, j: (l, 0, j)),
        ],
        out_specs=pl.BlockSpec((None, rows, tn), lambda l, j: (l, 0, j)),
        out_shape=jax.ShapeDtypeStruct((depth, rows, n), F32),
        compiler_params=pltpu.CompilerParams(
            dimension_semantics=("arbitrary", "arbitrary"), vmem_limit_bytes=VMEM_LIMIT),
    )(cc, w_mod, b_mod.reshape(depth, 1, n))


def _rope(x, tab_ref, n):
    return (x * tab_ref[0] + pltpu.roll(x, LANES - n, 1) * tab_ref[1]
            + pltpu.roll(x, n, 1) * tab_ref[2])


def _halves(lo, hi, shape):
    return jnp.where(_lane(shape) < HALF, lo, hi)


def _head_rms(x):
    low = _lane(x.shape) < HALF
    sq = x * x
    ss_lo = jnp.sum(jnp.where(low, sq, 0.0), axis=-1, keepdims=True)
    ss_hi = jnp.sum(jnp.where(low, 0.0, sq), axis=-1, keepdims=True)
    r_lo = lax.rsqrt(ss_lo * (1.0 / HALF) + NORM_EPS)
    r_hi = lax.rsqrt(ss_hi * (1.0 / HALF) + NORM_EPS)
    return jnp.where(low, r_lo, r_hi)


def proj_kernel(x_ref, mod_ref, gpre_ref, win_ref, wq_ref, wkv_ref, gq_ref, gkv_ref,
                gnq_ref, gnk_ref, t32_ref, t64_ref, tml_ref, *rest, rotate):
    o_ref = rest[-1]
    x = x_ref[...]
    h = x * _rms(x, D_MODEL) * gpre_ref[...]
    h = h * (1.0 + mod_ref[1:2, :]) + mod_ref[0:1, :]
    p = _dot(h.astype(BF16), win_ref[...])

    def grp(g):
        return p[:, g * LANES:(g + 1) * LANES]

    def put(g, v):
        o_ref[:, g * LANES:(g + 1) * LANES] = v.astype(BF16)

    def rope32(v):
        return _rope(v, t32_ref, DIFF_QK_DIM // 4) if rotate else v

    def rope64(v):
        return _rope(v, t64_ref, GQA_HEAD_DIM // 4) if rotate else v

    def ropeml(v):
        return _rope(v, tml_ref, MLA_ROPE_DIM // 4) if rotate else v

    lane = _lane((x.shape[0], LANES))
    low = lane < HALF

    def with_ones(v):
        return jnp.where(low, v, 1.0), jnp.where(low, 1.0, v)

    for g in range(2):
        qd = rope32(grp(g)) * (DIFF_QK_DIM ** -0.5 * LOG2E)
        for i in range(4):
            keep = (lane >= i * DIFF_QK_DIM) & (lane < (i + 1) * DIFF_QK_DIM)
            put(G_QD + 4 * g + i, jnp.where(keep, qd, 0.0))
        put(G_KD + g, rope32(grp(2 + g)))
        v_lo, v_hi = with_ones(grp(4 + g))
        put(G_VD + 2 * g, v_lo)
        put(G_VD + 2 * g + 1, v_hi)

    for g in range(2):
        put(G_QS + g, rope64(grp(6 + g)) * (SWA_HEAD_DIM ** -0.5))
    ks = rope64(grp(8))
    put(G_KS, ks)
    put(G_KS + 1, pltpu.roll(ks, HALF, 1))
    put(G_VS, grp(9))
    put(G_VS + 1, pltpu.roll(grp(9), HALF, 1))

    cq0, cq1, ckv = grp(10), grp(11), grp(12)
    ss = (jnp.sum(cq0 * cq0, axis=-1, keepdims=True)
          + jnp.sum(jnp.where(low, cq1 * cq1, 0.0), axis=-1, keepdims=True))
    rq = lax.rsqrt(ss * (1.0 / MLA_Q_RANK) + NORM_EPS)
    cqn = jnp.concatenate([cq0 * rq * gq_ref[:, :LANES], cq1 * rq * gq_ref[:, LANES:]], axis=1)
    qm = _dot(cqn.astype(BF16), wq_ref[...])
    scale_m = (MLA_NOPE_DIM + MLA_ROPE_DIM) ** -0.5 * LOG2E
    for hd in range(MLA_HEADS):
        put(G_QM + hd, ropeml(qm[:, hd * LANES:(hd + 1) * LANES]) * scale_m)
    ckvn = ckv * _rms(ckv, MLA_KV_RANK) * gkv_ref[...]
    kvin = jnp.concatenate([ckvn, ropeml(cq1)], axis=1)
    kv = _dot(kvin.astype(BF16), wkv_ref[...])
    for hd in range(MLA_HEADS):
        put(G_KM + hd, kv[:, hd * LANES:(hd + 1) * LANES])
    for g in range(2):
        v_lo, v_hi = with_ones(kv[:, (MLA_HEADS + g) * LANES:(MLA_HEADS + g + 1) * LANES])
        put(G_VM + 2 * g, v_lo)
        put(G_VM + 2 * g + 1, v_hi)

    scale_g = GQA_HEAD_DIM ** -0.5 * LOG2E
    q01 = grp(13)
    q01 = rope64(q01 * _head_rms(q01) * gnq_ref[...]) * scale_g
    q23 = grp(14)
    q23 = rope64(q23 * _head_rms(q23) * gnq_ref[...]) * scale_g
    put(G_QG, jnp.where(low, q01, 0.0))
    put(G_QG + 1, jnp.where(low, pltpu.roll(q01, HALF, 1), 0.0))
    put(G_QG + 2, jnp.where(low, 0.0, pltpu.roll(q23, HALF, 1)))
    put(G_QG + 3, jnp.where(low, 0.0, q23))
    kg = grp(15)
    put(G_KG, rope64(kg * _head_rms(kg) * gnk_ref[...]))
    v_lo, v_hi = with_ones(grp(16))
    put(G_VG, v_lo)
    put(G_VG + 1, v_hi)


def proj_call(x, mod, mod_row, lw, tabs, rotate, s_all, row0, qkv=None):
    b, s, d = x.shape
    tq = TOK_TILE
    tile0 = row0 // tq
    if mod_row is None:
        mod_map = lambda i, t: (i, 0, 0)
    else:
        mod_map = lambda i, t: (mod_row, 0, 0)
    const2 = lambda i, t: (0, 0)
    tab_spec = pl.BlockSpec((3, tq, LANES), lambda i, t: (0, t, 0))
    extra_specs, extra_args, aliases = [], [], {}
    if qkv is not None:
        extra_specs, extra_args, aliases = [pl.BlockSpec(memory_space=pl.ANY)], [qkv], {13: 0}
    return pl.pallas_call(
        functools.partial(proj_kernel, rotate=rotate),
        name="proj" if rotate else "proj_ctx",
        grid=(b, s // tq),
        input_output_aliases=aliases,
        in_specs=[
            pl.BlockSpec((None, tq, d), lambda i, t: (i, t, 0)),
            pl.BlockSpec((None, N_MOD, d), mod_map),
            pl.BlockSpec((1, d), const2),
            pl.BlockSpec((d, P_COLS), const2),
            pl.BlockSpec((2 * LANES, 4 * LANES), const2),
            pl.BlockSpec((2 * LANES, 6 * LANES), const2),
            pl.BlockSpec((1, 2 * LANES), const2),
            pl.BlockSpec((1, LANES), const2),
            pl.BlockSpec((1, LANES), const2),
            pl.BlockSpec((1, LANES), const2),
            tab_spec, tab_spec, tab_spec,
        ] + extra_specs,
        out_specs=pl.BlockSpec((None, tq, QKV_COLS), lambda i, t: (i, t + tile0, 0)),
        out_shape=jax.ShapeDtypeStruct((b, s_all, QKV_COLS), BF16),
        compiler_params=pltpu.CompilerParams(
            dimension_semantics=("arbitrary", "arbitrary"), vmem_limit_bytes=VMEM_LIMIT),
    )(x, mod, lw["g_mix_pre"], lw["w_in"], lw["wq"], lw["wkv"], lw["gq"], lw["gkv"],
      lw["gnq"], lw["gnk"], tabs[0], tabs[1], tabs[2], *extra_args)


def _group(ref, rows, g):
    return ref[rows, g * LANES:(g + 1) * LANES]


def _attend(q_ref, segs, k_ref, v_ref, n_chunks):
    chunk = k_ref.shape[0] // n_chunks
    qs = [jnp.concatenate([_group(q_ref, slice(None), g) for g in qg], axis=0) for qg, _, _ in segs]

    def body(i, carry):
        rows = pl.ds(pl.multiple_of(i * chunk, LANES), chunk)
        out = []
        for j, (q, (_, kb, parts)) in enumerate(zip(qs, segs)):
            m, acc = carry[2 * j], carry[2 * j + 1]
            s = _dot_nt(q, _group(k_ref, rows, kb))
            m_new = jnp.maximum(m, jnp.max(s, axis=-1, keepdims=True))
            p = jnp.exp2(s - m_new).astype(BF16)
            tq = q.shape[0] // sum(n for n, _ in parts)
            pv, r0 = [], 0
            for n, vb in parts:
                pv.append(_dot(p[r0:r0 + n * tq], _group(v_ref, rows, vb)))
                r0 += n * tq
            pv = pv[0] if len(pv) == 1 else jnp.concatenate(pv, axis=0)
            out += [m_new, acc * jnp.exp2(m - m_new) + pv]
        return tuple(out)

    init = []
    for q in qs:
        init += [jnp.full((q.shape[0], 1), NEG_INF, F32), jnp.zeros((q.shape[0], LANES), F32)]
    carry = lax.fori_loop(0, n_chunks, body, tuple(init))
    return [carry[2 * j + 1] / pltpu.roll(carry[2 * j + 1], HALF, 1) for j in range(len(segs))]


_PLANS = {
    "gqa": ((((0, 1, 2, 3), 0, ((2, 0), (2, 1))),),),
    "mla": (tuple(((h,), h, ((1, h),)) for h in range(MLA_HEADS)),),
    "diff": ((((0, 1, 2, 3), 0, ((2, 0), (2, 1))),), (((4, 5, 6, 7), 1, ((2, 2), (2, 3))),)),
}


def dense_kernel(*refs, kind, n_chunks, lambda_init):
    refs = list(refs)
    if kind == "diff":
        lam_ref, g_ref = refs[:2]
        refs = refs[2:]
    q_ref, k_ref, v_ref, o_ref = refs
    tq = q_ref.shape[0]
    res = [_attend(q_ref, segs, k_ref, v_ref, n_chunks) for segs in _PLANS[kind]]

    def tiles(r):
        return [r[i * tq:(i + 1) * tq] for i in range(r.shape[0] // tq)]

    if kind == "gqa":
        h = tiles(res[0][0])
        blocks = [_halves(h[0], pltpu.roll(h[1], HALF, 1), h[0].shape),
                  _halves(pltpu.roll(h[2], HALF, 1), h[3], h[0].shape)]
    elif kind == "mla":
        h = res[0]
        blocks = [_halves(h[0], h[1], h[0].shape), _halves(h[2], h[3], h[0].shape)]
    else:
        lv = lam_ref[...]
        lam = (jnp.exp(jnp.sum(lv[0:1] * lv[1:2], axis=-1, keepdims=True))
               - jnp.exp(jnp.sum(lv[2:3] * lv[3:4], axis=-1, keepdims=True)) + lambda_init)
        blocks = []
        for (r,) in res:
            a = tiles(r)
            o = _halves(a[0] - lam * a[1], a[2] - lam * a[3], a[0].shape)
            blocks.append(o * _head_rms(o) * g_ref[...] * (1.0 - lambda_init))
    for ob, blk in enumerate(blocks):
        o_ref[:, ob * LANES:(ob + 1) * LANES] = blk.astype(BF16)


def _qkv_specs(lay, latent, s_all):
    tq = TOK_TILE
    (q0, qn), (k0, kn), (v0, vn) = lay
    assert q0 % qn == 0 and k0 % kn == 0 and v0 % vn == 0
    q_off = 1 if latent else 0
    kv_rows = s_all if latent else CTX_TILE
    return [
        pl.BlockSpec((None, tq, qn * LANES), lambda i, t: (i, t + q_off, q0 // qn)),
        pl.BlockSpec((None, kv_rows, kn * LANES), lambda i, t: (i, 0, k0 // kn)),
        pl.BlockSpec((None, kv_rows, vn * LANES), lambda i, t: (i, 0, v0 // vn)),
    ]


def dense_attn_call(kind, qkv, latent, extra=None, lambda_init=None):
    b, s_all, _ = qkv.shape
    tq = TOK_TILE
    sq = s_all - CTX_TILE if latent else CTX_TILE
    lay = {"diff": ((G_QD, 8), (G_KD, 2), (G_VD, 4)),
           "mla": ((G_QM, 4), (G_KM, 4), (G_VM, 4)),
           "gqa": ((G_QG, 4), (G_KG, 1), (G_VG, 2))}[kind]
    specs = _qkv_specs(lay, latent, s_all)
    args = [qkv, qkv, qkv]
    if kind == "diff":
        specs = [pl.BlockSpec((4, DIFF_QK_DIM), lambda i, t: (0, 0)),
                 pl.BlockSpec((1, LANES), lambda i, t: (0, 0))] + specs
        args = list(extra) + args
    body = functools.partial(dense_kernel, kind=kind, n_chunks=KV_CHUNKS if latent else 1,
                             lambda_init=lambda_init)
    return pl.pallas_call(
        body,
        name=kind + ("_attn" if latent else "_attn_ctx"),
        grid=(b, sq // tq),
        in_specs=specs,
        out_specs=pl.BlockSpec((None, tq, MIX_COLS), lambda i, t: (i, t, 0)),
        out_shape=jax.ShapeDtypeStruct((b, sq, MIX_COLS), BF16),
        compiler_params=pltpu.CompilerParams(
            dimension_semantics=("arbitrary", "arbitrary"), vmem_limit_bytes=VMEM_LIMIT),
    )(*args)


def _mask_lanes(q, lo, hi):
    lane = _lane(q.shape)
    return jnp.where((lane >= lo) & (lane < hi), q, jnp.zeros_like(q))


def swa_kernel(sink_ref, q_ref, k_ref, v_ref, o_ref, *, s_lat):
    tq = q_ref.shape[0]
    ctx_rows = slice(0, CTX_TILE)
    t = pl.program_id(1)
    n_win = tq // WINDOW + 2
    plan = (((0, 0, HALF, 0, 0), (0, HALF, LANES, 1, 1)),
            ((1, 0, HALF, 1, 1), (1, HALF, LANES, 0, 0)))
    chunks = []
    if s_lat:
        qpos = t * tq + lax.broadcasted_iota(jnp.int32, (tq, WINDOW), 0)
        for c in range(n_win):
            start = t * tq + (c - 1) * WINDOW
            inside = (start >= 0) & (start < s_lat)
            rows = pl.ds(pl.multiple_of(CTX_TILE + jnp.clip(start, 0, s_lat - WINDOW), WINDOW),
                         WINDOW)
            kpos = start + lax.broadcasted_iota(jnp.int32, (tq, WINDOW), 1)
            chunks.append((rows, (jnp.abs(kpos - qpos) <= WINDOW) & inside))
    for ob, maps in enumerate(plan):
        res = []
        for hh, (qb, lo, hi, kb, vb) in enumerate(maps):
            qm = _mask_lanes(q_ref[:, qb * LANES:(qb + 1) * LANES], lo, hi)
            kcol = slice(kb * LANES, (kb + 1) * LANES)
            vcol = slice(vb * LANES, (vb + 1) * LANES)
            sink = sink_ref[2 * ob + hh]
            s_ctx = _dot_nt(qm, k_ref[ctx_rows, kcol])
            m = jnp.maximum(jnp.max(s_ctx, axis=-1, keepdims=True), sink)
            s_loc = []
            for rows, valid in chunks:
                s = jnp.where(valid, _dot_nt(qm, k_ref[rows, kcol]), NEG_INF)
                m = jnp.maximum(m, jnp.max(s, axis=-1, keepdims=True))
                s_loc.append(s)
            p = jnp.exp(s_ctx - m)
            l = jnp.sum(p, axis=-1, keepdims=True) + jnp.exp(sink - m)
            acc = _dot(p.astype(BF16), v_ref[ctx_rows, vcol])
            for (rows, _), s in zip(chunks, s_loc):
                p = jnp.exp(s - m)
                l = l + jnp.sum(p, axis=-1, keepdims=True)
                acc = acc + _dot(p.astype(BF16), v_ref[rows, vcol])
            res.append(acc / l)
        o_ref[:, ob * LANES:(ob + 1) * LANES] = _halves(res[0], res[1], res[0].shape).astype(BF16)


def swa_call(sink, qkv, latent):
    b, s_all, _ = qkv.shape
    tq = TOK_TILE
    s_lat = s_all - CTX_TILE if latent else 0
    sq = s_lat if latent else CTX_TILE
    specs = [pl.BlockSpec(memory_space=pltpu.SMEM)]
    specs += _qkv_specs(((G_QS, 2), (G_KS, 2), (G_VS, 2)), latent, s_all)
    return pl.pallas_call(
        functools.partial(swa_kernel, s_lat=s_lat),
        name="swa_attn" if latent else "swa_attn_ctx",
        grid=(b, sq // tq),
        in_specs=specs,
        out_specs=pl.BlockSpec((None, tq, MIX_COLS), lambda i, t: (i, t, 0)),
        out_shape=jax.ShapeDtypeStruct((b, sq, MIX_COLS), BF16),
        compiler_params=pltpu.CompilerParams(
            dimension_semantics=("arbitrary", "arbitrary"), vmem_limit_bytes=VMEM_LIMIT),
    )(sink, qkv, qkv, qkv)


def mixout_kernel(x_ref, ya_ref, yb_ref, ym_ref, yd_ref, w_ref, mod_ref, gpost_ref,
                  gfpre_ref, x1_ref, h2_ref):
    y = jnp.concatenate([ya_ref[...], yb_ref[...], ym_ref[...], yd_ref[...]], axis=1)
    z = _dot(y, w_ref[...])
    x1 = x_ref[...] + mod_ref[2:3, :] * (z * _rms(z, D_MODEL) * gpost_ref[...])
    x1_ref[...] = x1
    h2 = x1 * _rms(x1, D_MODEL) * gfpre_ref[...]
    h2_ref[...] = (h2 * (1.0 + mod_ref[4:5, :]) + mod_ref[3:4, :]).astype(BF16)


def mixout_call(x, ys, mod, mod_row, lw):
    b, s, d = x.shape
    tq = TOK_TILE
    if mod_row is None:
        mod_map = lambda i, t: (i, 0, 0)
    else:
        mod_map = lambda i, t: (mod_row, 0, 0)
    const2 = lambda i, t: (0, 0)
    tok = lambda i, t: (i, t, 0)
    y_spec = pl.BlockSpec((None, tq, MIX_COLS), tok)
    return pl.pallas_call(
        mixout_kernel,
        name="mixout",
        grid=(b, s // tq),
        in_specs=[pl.BlockSpec((None, tq, d), tok), y_spec, y_spec, y_spec, y_spec,
                  pl.BlockSpec((4 * MIX_COLS, d), const2),
                  pl.BlockSpec((None, N_MOD, d), mod_map),
                  pl.BlockSpec((1, d), const2), pl.BlockSpec((1, d), const2)],
        out_specs=[pl.BlockSpec((None, tq, d), tok), pl.BlockSpec((None, tq, d), tok)],
        out_shape=[jax.ShapeDtypeStruct((b, s, d), F32), jax.ShapeDtypeStruct((b, s, d), BF16)],
        compiler_params=pltpu.CompilerParams(
            dimension_semantics=("arbitrary", "arbitrary"), vmem_limit_bytes=VMEM_LIMIT),
    )(x, *ys, lw["w_out"], mod, lw["g_mix_post"], lw["g_ffn_pre"])


def ffn_kernel(x1_ref, hc_ref, hp_ref, hn_ref, wup_ref, cw_ref, cb_ref, wdn_ref, mod_ref,
               gpost_ref, o_ref, acc_ref):
    t = pl.program_id(1)
    tq = hc_ref.shape[0]
    halo = hp_ref.shape[0]
    hp = jnp.where(t > 0, hp_ref[...], jnp.zeros_like(hp_ref))
    hn = jnp.where(t < pl.num_programs(1) - 1, hn_ref[...], jnp.zeros_like(hn_ref))
    hext = jnp.concatenate([hp, hc_ref[...], hn], axis=0)
    rows = hext.shape[0]
    acc_ref[...] = jnp.zeros_like(acc_ref)

    def body(j, carry):
        u = _dot(hext, wup_ref[j])
        w = cw_ref[j]
        conv = (pltpu.roll(u, 1, 0)[halo:halo + tq] * w[0:1]
                + u[halo:halo + tq] * w[1:2]
                + pltpu.roll(u, rows - 1, 0)[halo:halo + tq] * w[2:3] + cb_ref[j])
        a = conv[:, :FFN_CHUNK]
        gated = a * (1.0 / (1.0 + jnp.exp(-a))) * conv[:, FFN_CHUNK:]
        acc_ref[...] += _dot(gated.astype(BF16), wdn_ref[j])
        return carry

    lax.fori_loop(0, wup_ref.shape[0], body, 0)
    f = acc_ref[...]
    o_ref[...] = x1_ref[...] + mod_ref[5:6, :] * (f * _rms(f, D_MODEL) * gpost_ref[...])


def ffn_call(x1, h2, mod, mod_row, lw):
    b, s, d = x1.shape
    tq = min(FFN_TILE, s)
    halo = BF16_SUBLANES
    per_tile = tq // halo
    last = s // halo - 1
    n_ch = FFN_DIM // FFN_CHUNK
    if mod_row is None:
        mod_map = lambda i, t: (i, 0, 0)
    else:
        mod_map = lambda i, t: (mod_row, 0, 0)
    const2 = lambda i, t: (0, 0)
    const3 = lambda i, t: (0, 0, 0)
    tok = lambda i, t: (i, t, 0)
    return pl.pallas_call(
        ffn_kernel,
        name="conv_ffn",
        grid=(b, s // tq),
        in_specs=[
            pl.BlockSpec((None, tq, d), tok),
            pl.BlockSpec((None, tq, d), tok),
            pl.BlockSpec((None, halo, d), lambda i, t: (i, jnp.maximum(t * per_tile - 1, 0), 0)),
            pl.BlockSpec((None, halo, d), lambda i, t: (i, jnp.minimum((t + 1) * per_tile, last), 0)),
            pl.BlockSpec((n_ch, d, 2 * FFN_CHUNK), const3),
            pl.BlockSpec((n_ch, 3, 2 * FFN_CHUNK), const3),
            pl.BlockSpec((n_ch, 1, 2 * FFN_CHUNK), const3),
            pl.BlockSpec((n_ch, FFN_CHUNK, d), const3),
            pl.BlockSpec((None, N_MOD, d), mod_map),
            pl.BlockSpec((1, d), const2),
        ],
        out_specs=pl.BlockSpec((None, tq, d), tok),
        out_shape=jax.ShapeDtypeStruct((b, s, d), F32),
        scratch_shapes=[pltpu.VMEM((tq, d), F32)],
        compiler_params=pltpu.CompilerParams(
            dimension_semantics=("arbitrary", "arbitrary"), vmem_limit_bytes=VMEM_LIMIT),
    )(x1, h2, h2, h2, lw["w_up"], lw["conv_w"], lw["conv_b"], lw["w_down"], mod,
      lw["g_ffn_post"])


def _rope_table(s, rot_dim, lane_lo, lane_hi, period):
    n = rot_dim // 4
    tok = np.arange(s)
    pos = np.stack([tok // GRID_W, tok % GRID_W], axis=1).astype(np.float32)
    lane = np.arange(LANES)
    o = (lane - lane_lo) % period
    axis = o // (2 * n)
    second = (o % (2 * n)) // n
    active = (lane >= lane_lo) & (lane < lane_hi)
    inv_freq = ROPE_THETA ** (-jnp.arange(n, dtype=F32) / n)
    ang = jnp.asarray(pos)[:, axis] * inv_freq[o % n][None, :]
    cos = jnp.where(active[None], jnp.cos(ang), 1.0)
    sin = jnp.where(active[None], jnp.sin(ang), 0.0)
    sa = jnp.where((second == 0)[None], -sin, 0.0)
    sb = jnp.where((second == 1)[None], sin, 0.0)
    return jnp.stack([cos, sa, sb]).astype(F32)


def _layer_weights(l, w):
    d = D_MODEL
    w_in = w["w_in"][l]
    mla0 = 768 + 512
    cq_hi = mla0 + MLA_Q_RANK
    kr0 = cq_hi + MLA_KV_RANK
    w_in_p = jnp.concatenate([
        w_in[:, :mla0 + LANES],
        w_in[:, mla0 + LANES:cq_hi], w_in[:, kr0:kr0 + MLA_ROPE_DIM],
        jnp.zeros((d, LANES - HALF - MLA_ROPE_DIM), F32),
        w_in[:, cq_hi:kr0],
        w_in[:, kr0 + MLA_ROPE_DIM:],
    ], axis=1)
    dqk = MLA_NOPE_DIM + MLA_ROPE_DIM
    wq = jnp.zeros((2 * LANES, MLA_HEADS, LANES), F32)
    wq = wq.at[:MLA_Q_RANK, :, :dqk].set(w["mla_w_uq"][l].reshape(MLA_Q_RANK, MLA_HEADS, dqk))
    wukv = w["mla_w_ukv"][l].reshape(MLA_KV_RANK, MLA_HEADS, MLA_NOPE_DIM + MLA_V_DIM)
    wk = jnp.zeros((2 * LANES, MLA_HEADS, LANES), F32)
    wk = wk.at[:MLA_KV_RANK, :, :MLA_NOPE_DIM].set(wukv[:, :, :MLA_NOPE_DIM])
    place = jnp.eye(MLA_ROPE_DIM, dtype=F32)[:, None, :]
    wk = wk.at[LANES + HALF:LANES + HALF + MLA_ROPE_DIM, :, MLA_NOPE_DIM:dqk].set(
        jnp.broadcast_to(place, (MLA_ROPE_DIM, MLA_HEADS, MLA_ROPE_DIM)))
    wv = jnp.zeros((2 * LANES, MLA_HEADS, MLA_V_DIM), F32)
    wv = wv.at[:MLA_KV_RANK].set(wukv[:, :, MLA_NOPE_DIM:])
    wkv = jnp.concatenate([wk.reshape(2 * LANES, -1), wv.reshape(2 * LANES, -1)], axis=1)
    gq = jnp.zeros((1, 2 * LANES), F32).at[0, :MLA_Q_RANK].set(w["mla_g_q"][l])
    n_ch = FFN_DIM // FFN_CHUNK

    def chunked(a):
        lead = a.shape[:-1]
        a = a.reshape(lead + (2, n_ch, FFN_CHUNK))
        a = jnp.moveaxis(a, -2, 0)
        return a.reshape((n_ch,) + lead + (2 * FFN_CHUNK,))

    return {
        "g_mix_pre": w["g_mix_pre"][l][None], "g_mix_post": w["g_mix_post"][l][None],
        "g_ffn_pre": w["g_ffn_pre"][l][None], "g_ffn_post": w["g_ffn_post"][l][None],
        "w_in": w_in_p.astype(BF16),
        "wq": wq.reshape(2 * LANES, -1).astype(BF16), "wkv": wkv.astype(BF16),
        "gq": gq, "gkv": w["mla_g_kv"][l][None],
        "gnq": jnp.tile(w["gqa_g_q"][l], 2)[None], "gnk": jnp.tile(w["gqa_g_k"][l], 2)[None],
        "lam": jnp.stack([w["diff_lam_q1"][l], w["diff_lam_k1"][l],
                          w["diff_lam_q2"][l], w["diff_lam_k2"][l]]),
        "g_sub": jnp.tile(w["diff_g_sub"][l], 2)[None],
        "sink": w["swa_sink"][l],
        "w_out": w["w_out"][l].astype(BF16),
        "w_up": chunked(w["ffn_w_up"][l]).astype(BF16),
        "conv_w": chunked(w["ffn_conv_w"][l]),
        "conv_b": chunked(w["ffn_conv_b"][l][None]),
        "w_down": w["ffn_w_down"][l].reshape(n_ch, FFN_CHUNK, d).astype(BF16),
    }


def kernel(x, c, ctx, c_ctx, w_mod, b_mod, g_mix_pre, g_mix_post, g_ffn_pre, g_ffn_post, w_in, diff_lam_q1, diff_lam_k1, diff_lam_q2, diff_lam_k2, diff_g_sub, swa_sink, mla_g_q, mla_g_kv, mla_w_uq, mla_w_ukv, gqa_g_q, gqa_g_k, w_out, ffn_w_up, ffn_conv_w, ffn_conv_b, ffn_w_down):
    w = dict(g_mix_pre=g_mix_pre, g_mix_post=g_mix_post, g_ffn_pre=g_ffn_pre,
             g_ffn_post=g_ffn_post, w_in=w_in, diff_lam_q1=diff_lam_q1, diff_lam_k1=diff_lam_k1,
             diff_lam_q2=diff_lam_q2, diff_lam_k2=diff_lam_k2, diff_g_sub=diff_g_sub,
             swa_sink=swa_sink, mla_g_q=mla_g_q, mla_g_kv=mla_g_kv, mla_w_uq=mla_w_uq,
             mla_w_ukv=mla_w_ukv, gqa_g_q=gqa_g_q, gqa_g_k=gqa_g_k, w_out=w_out,
             ffn_w_up=ffn_w_up, ffn_conv_w=ffn_conv_w, ffn_conv_b=ffn_conv_b,
             ffn_w_down=ffn_w_down)
    b, s, d = x.shape
    depth = w_mod.shape[0]
    ctx_row = b
    mod_rows = 16
    cc = jnp.zeros((mod_rows, d), F32).at[:b].set(c).at[ctx_row].set(c_ctx)
    mod_all = mod_call(cc, w_mod, b_mod).reshape(depth, mod_rows, N_MOD, d)

    tabs = (_rope_table(s, DIFF_QK_DIM, 0, LANES, DIFF_QK_DIM),
            _rope_table(s, GQA_HEAD_DIM, 0, LANES, GQA_HEAD_DIM),
            _rope_table(s, MLA_ROPE_DIM, MLA_NOPE_DIM, MLA_NOPE_DIM + MLA_ROPE_DIM, MLA_ROPE_DIM))

    xc = ctx
    for l in range(depth):
        need_ctx = l < depth - 1
        lambda_init = 0.8 - 0.6 * math.exp(-0.3 * l)
        lw = _layer_weights(l, w)
        mod = mod_all[l]
        s_all = xc.shape[1] + s
        qkv = proj_call(xc, mod, ctx_row, lw, tabs, False, s_all, 0)
        qkv = proj_call(x, mod, None, lw, tabs, True, s_all, xc.shape[1], qkv)
        diff_extra = (lw["lam"], lw["g_sub"])
        ys = (dense_attn_call("diff", qkv, True, diff_extra, lambda_init),
              swa_call(lw["sink"], qkv, True),
              dense_attn_call("mla", qkv, True),
              dense_attn_call("gqa", qkv, True))
        x1, h2 = mixout_call(x, ys, mod, None, lw)
        x = ffn_call(x1, h2, mod, None, lw)
        if need_ctx:
            ysc = (dense_attn_call("diff", qkv, False, diff_extra, lambda_init),
                   swa_call(lw["sink"], qkv, False),
                   dense_attn_call("mla", qkv, False),
                   dense_attn_call("gqa", qkv, False))
            xc1, hc2 = mixout_call(xc, ysc, mod, ctx_row, lw)
            xc = ffn_call(xc1, hc2, mod, ctx_row, lw)
    return x
```

```python
import functools
import math

import jax
import jax.numpy as jnp
import numpy as np
from jax import lax
from jax.experimental import pallas as pl
from jax.experimental.pallas import tpu as pltpu

F32 = jnp.float32
BF16 = jnp.bfloat16

D_MODEL = 1024
GRID_W = 64
ROPE_THETA = 10000.0
NORM_EPS = 1e-6
NEG_INF = -1e30
N_MOD = 6

DIFF_HEADS, DIFF_QK_DIM, DIFF_V_DIM = 4, 32, 64
SWA_HEAD_DIM, WINDOW = 64, 128
MLA_Q_RANK, MLA_KV_RANK = 192, 128
MLA_NOPE_DIM, MLA_ROPE_DIM, MLA_V_DIM, MLA_HEADS = 64, 32, 64, 4
GQA_HEAD_DIM = 64
FFN_DIM = 2816
MIX_COLS = 256

LANES = 128
BF16_SUBLANES = 16
HALF = LANES // 2
VMEM_LIMIT = 56 * 1024 * 1024

P_COLS = 17 * LANES
G_QD, G_QM, G_KM, G_VM, G_VD, G_QG = 0, 8, 12, 16, 20, 24
G_KD, G_QS, G_KS, G_VS, G_VG, G_KG = 28, 30, 32, 34, 36, 38
QKV_COLS = 39 * LANES
LOG2E = math.log2(math.e)

TOK_TILE = 256
CTX_TILE = 256
KV_CHUNKS = 2
FFN_TILE = 512
FFN_CHUNK = 256


def _rms(x, n):
    return lax.rsqrt(jnp.sum(x * x, axis=-1, keepdims=True) * (1.0 / n) + NORM_EPS)


def _lane(shape):
    return lax.broadcasted_iota(jnp.int32, shape, len(shape) - 1)


def _dot(a, b):
    return jnp.dot(a, b, preferred_element_type=F32)


def _dot_nt(a, b):
    return lax.dot_general(a, b, (((1,), (1,)), ((), ())), preferred_element_type=F32)


def mod_kernel(c_ref, w_ref, b_ref, o_ref):
    c = c_ref[...]
    s = c * (1.0 / (1.0 + jnp.exp(-c)))
    o_ref[...] = jnp.dot(s, w_ref[...], preferred_element_type=F32,
                         precision=lax.Precision.HIGHEST) + b_ref[...]


def mod_call(cc, w_mod, b_mod):
    depth, d, n = w_mod.shape
    rows = cc.shape[0]
    tn = 1024
    return pl.pallas_call(
        mod_kernel,
        name="mod_vectors",
        grid=(depth, n // tn),
        in_specs=[
            pl.BlockSpec((rows, d), lambda l, j: (0, 0)),
            pl.BlockSpec((None, d, tn), lambda l, j: (l, 0, j)),
            pl.BlockSpec((None, 1, tn), lambda l, j: (l, 0, j)),
        ],
        out_specs=pl.BlockSpec((None, rows, tn), lambda l, j: (l, 0, j)),
        out_shape=jax.ShapeDtypeStruct((depth, rows, n), F32),
        compiler_params=pltpu.CompilerParams(
            dimension_semantics=("arbitrary", "arbitrary"), vmem_limit_bytes=VMEM_LIMIT),
    )(cc, w_mod, b_mod.reshape(depth, 1, n))


def _rope(x, tab_ref, n):
    return (x * tab_ref[0] + pltpu.roll(x, LANES - n, 1) * tab_ref[1]
            + pltpu.roll(x, n, 1) * tab_ref[2])


def _halves(lo, hi, shape):
    return jnp.where(_lane(shape) < HALF, lo, hi)


def _head_rms(x):
    low = _lane(x.shape) < HALF
    sq = x * x
    ss_lo = jnp.sum(jnp.where(low, sq, 0.0), axis=-1, keepdims=True)
    ss_hi = jnp.sum(jnp.where(low, 0.0, sq), axis=-1, keepdims=True)
    r_lo = lax.rsqrt(ss_lo * (1.0 / HALF) + NORM_EPS)
    r_hi = lax.rsqrt(ss_hi * (1.0 / HALF) + NORM_EPS)
    return jnp.where(low, r_lo, r_hi)


def proj_kernel(x_ref, mod_ref, gpre_ref, win_ref, wq_ref, wkv_ref, gq_ref, gkv_ref,
                gnq_ref, gnk_ref, t32_ref, t64_ref, tml_ref, *rest, rotate):
    o_ref = rest[-1]
    x = x_ref[...]
    h = x * _rms(x, D_MODEL) * gpre_ref[...]
    h = h * (1.0 + mod_ref[1:2, :]) + mod_ref[0:1, :]
    p = _dot(h.astype(BF16), win_ref[...])

    def grp(g):
        return p[:, g * LANES:(g + 1) * LANES]

    def put(g, v):
        o_ref[:, g * LANES:(g + 1) * LANES] = v.astype(BF16)

    def rope32(v):
        return _rope(v, t32_ref, DIFF_QK_DIM // 4) if rotate else v

    def rope64(v):
        return _rope(v, t64_ref, GQA_HEAD_DIM // 4) if rotate else v

    def ropeml(v):
        return _rope(v, tml_ref, MLA_ROPE_DIM // 4) if rotate else v

    lane = _lane((x.shape[0], LANES))
    low = lane < HALF

    def with_ones(v):
        return jnp.where(low, v, 1.0), jnp.where(low, 1.0, v)

    for g in range(2):
        qd = rope32(grp(g)) * (DIFF_QK_DIM ** -0.5 * LOG2E)
        for i in range(4):
            keep = (lane >= i * DIFF_QK_DIM) & (lane < (i + 1) * DIFF_QK_DIM)
            put(G_QD + 4 * g + i, jnp.where(keep, qd, 0.0))
        put(G_KD + g, rope32(grp(2 + g)))
        v_lo, v_hi = with_ones(grp(4 + g))
        put(G_VD + 2 * g, v_lo)
        put(G_VD + 2 * g + 1, v_hi)

    for g in range(2):
        put(G_QS + g, rope64(grp(6 + g)) * (SWA_HEAD_DIM ** -0.5))
    ks = rope64(grp(8))
    put(G_KS, ks)
    put(G_KS + 1, pltpu.roll(ks, HALF, 1))
    put(G_VS, grp(9))
    put(G_VS + 1, pltpu.roll(grp(9), HALF, 1))

    cq0, cq1, ckv = grp(10), grp(11), grp(12)
    ss = (jnp.sum(cq0 * cq0, axis=-1, keepdims=True)
          + jnp.sum(jnp.where(low, cq1 * cq1, 0.0), axis=-1, keepdims=True))
    rq = lax.rsqrt(ss * (1.0 / MLA_Q_RANK) + NORM_EPS)
    cqn = jnp.concatenate([cq0 * rq * gq_ref[:, :LANES], cq1 * rq * gq_ref[:, LANES:]], axis=1)
    qm = _dot(cqn.astype(BF16), wq_ref[...])
    scale_m = (MLA_NOPE_DIM + MLA_ROPE_DIM) ** -0.5 * LOG2E
    for hd in range(MLA_HEADS):
        put(G_QM + hd, ropeml(qm[:, hd * LANES:(hd + 1) * LANES]) * scale_m)
    ckvn = ckv * _rms(ckv, MLA_KV_RANK) * gkv_ref[...]
    kvin = jnp.concatenate([ckvn, ropeml(cq1)], axis=1)
    kv = _dot(kvin.astype(BF16), wkv_ref[...])
    for hd in range(MLA_HEADS):
        put(G_KM + hd, kv[:, hd * LANES:(hd + 1) * LANES])
    for g in range(2):
        v_lo, v_hi = with_ones(kv[:, (MLA_HEADS + g) * LANES:(MLA_HEADS + g + 1) * LANES])
        put(G_VM + 2 * g, v_lo)
        put(G_VM + 2 * g + 1, v_hi)

    scale_g = GQA_HEAD_DIM ** -0.5 * LOG2E
    q01 = grp(13)
    q01 = rope64(q01 * _head_rms(q01) * gnq_ref[...]) * scale_g
    q23 = grp(14)
    q23 = rope64(q23 * _head_rms(q23) * gnq_ref[...]) * scale_g
    put(G_QG, jnp.where(low, q01, 0.0))
    put(G_QG + 1, jnp.where(low, pltpu.roll(q01, HALF, 1), 0.0))
    put(G_QG + 2, jnp.where(low, 0.0, pltpu.roll(q23, HALF, 1)))
    put(G_QG + 3, jnp.where(low, 0.0, q23))
    kg = grp(15)
    put(G_KG, rope64(kg * _head_rms(kg) * gnk_ref[...]))
    v_lo, v_hi = with_ones(grp(16))
    put(G_VG, v_lo)
    put(G_VG + 1, v_hi)


def proj_call(x, mod, mod_row, lw, tabs, rotate, s_all, row0, qkv=None):
    b, s, d = x.shape
    tq = TOK_TILE
    tile0 = row0 // tq
    if mod_row is None:
        mod_map = lambda i, t: (i, 0, 0)
    else:
        mod_map = lambda i, t: (mod_row, 0, 0)
    const2 = lambda i, t: (0, 0)
    tab_spec = pl.BlockSpec((3, tq, LANES), lambda i, t: (0, t, 0))
    extra_specs, extra_args, aliases = [], [], {}
    if qkv is not None:
        extra_specs, extra_args, aliases = [pl.BlockSpec(memory_space=pl.ANY)], [qkv], {13: 0}
    return pl.pallas_call(
        functools.partial(proj_kernel, rotate=rotate),
        name="proj" if rotate else "proj_ctx",
        grid=(b, s // tq),
        input_output_aliases=aliases,
        in_specs=[
            pl.BlockSpec((None, tq, d), lambda i, t: (i, t, 0)),
            pl.BlockSpec((None, N_MOD, d), mod_map),
            pl.BlockSpec((1, d), const2),
            pl.BlockSpec((d, P_COLS), const2),
            pl.BlockSpec((2 * LANES, 4 * LANES), const2),
            pl.BlockSpec((2 * LANES, 6 * LANES), const2),
            pl.BlockSpec((1, 2 * LANES), const2),
            pl.BlockSpec((1, LANES), const2),
            pl.BlockSpec((1, LANES), const2),
            pl.BlockSpec((1, LANES), const2),
            tab_spec, tab_spec, tab_spec,
        ] + extra_specs,
        out_specs=pl.BlockSpec((None, tq, QKV_COLS), lambda i, t: (i, t + tile0, 0)),
        out_shape=jax.ShapeDtypeStruct((b, s_all, QKV_COLS), BF16),
        compiler_params=pltpu.CompilerParams(
            dimension_semantics=("arbitrary", "arbitrary"), vmem_limit_bytes=VMEM_LIMIT),
    )(x, mod, lw["g_mix_pre"], lw["w_in"], lw["wq"], lw["wkv"], lw["gq"], lw["gkv"],
      lw["gnq"], lw["gnk"], tabs[0], tabs[1], tabs[2], *extra_args)


def _group(ref, rows, g):
    return ref[rows, g * LANES:(g + 1) * LANES]


def _attend(q_ref, segs, k_ref, v_ref, n_chunks):
    chunk = k_ref.shape[0] // n_chunks
    qs = [jnp.concatenate([_group(q_ref, slice(None), g) for g in qg], axis=0) for qg, _, _ in segs]

    def body(i, carry):
        rows = pl.ds(pl.multiple_of(i * chunk, LANES), chunk)
        out = []
        for j, (q, (_, kb, parts)) in enumerate(zip(qs, segs)):
            m, acc = carry[2 * j], carry[2 * j + 1]
            s = _dot_nt(q, _group(k_ref, rows, kb))
            m_new = jnp.maximum(m, jnp.max(s, axis=-1, keepdims=True))
            p = jnp.exp2(s - m_new).astype(BF16)
            tq = q.shape[0] // sum(n for n, _ in parts)
            pv, r0 = [], 0
            for n, vb in parts:
                pv.append(_dot(p[r0:r0 + n * tq], _group(v_ref, rows, vb)))
                r0 += n * tq
            pv = pv[0] if len(pv) == 1 else jnp.concatenate(pv, axis=0)
            out += [m_new, acc * jnp.exp2(m - m_new) + pv]
        return tuple(out)

    init = []
    for q in qs:
        init += [jnp.full((q.shape[0], 1), NEG_INF, F32), jnp.zeros((q.shape[0], LANES), F32)]
    carry = lax.fori_loop(0, n_chunks, body, tuple(init))
    return [carry[2 * j + 1] / pltpu.roll(carry[2 * j + 1], HALF, 1) for j in range(len(segs))]


_PLANS = {
    "gqa": ((((0, 1, 2, 3), 0, ((2, 0), (2, 1))),),),
    "mla": (tuple(((h,), h, ((1, h),)) for h in range(MLA_HEADS)),),
    "diff": ((((0, 1, 2, 3), 0, ((2, 0), (2, 1))),), (((4, 5, 6, 7), 1, ((2, 2), (2, 3))),)),
}


def dense_kernel(*refs, kind, n_chunks, lambda_init):
    refs = list(refs)
    if kind == "diff":
        lam_ref, g_ref = refs[:2]
        refs = refs[2:]
    q_ref, k_ref, v_ref, o_ref = refs
    tq = q_ref.shape[0]
    res = [_attend(q_ref, segs, k_ref, v_ref, n_chunks) for segs in _PLANS[kind]]

    def tiles(r):
        return [r[i * tq:(i + 1) * tq] for i in range(r.shape[0] // tq)]

    if kind == "gqa":
        h = tiles(res[0][0])
        blocks = [_halves(h[0], pltpu.roll(h[1], HALF, 1), h[0].shape),
                  _halves(pltpu.roll(h[2], HALF, 1), h[3], h[0].shape)]
    elif kind == "mla":
        h = res[0]
        blocks = [_halves(h[0], h[1], h[0].shape), _halves(h[2], h[3], h[0].shape)]
    else:
        lv = lam_ref[...]
        lam = (jnp.exp(jnp.sum(lv[0:1] * lv[1:2], axis=-1, keepdims=True))
               - jnp.exp(jnp.sum(lv[2:3] * lv[3:4], axis=-1, keepdims=True)) + lambda_init)
        blocks = []
        for (r,) in res:
            a = tiles(r)
            o = _halves(a[0] - lam * a[1], a[2] - lam * a[3], a[0].shape)
            blocks.append(o * _head_rms(o) * g_ref[...] * (1.0 - lambda_init))
    for ob, blk in enumerate(blocks):
        o_ref[:, ob * LANES:(ob + 1) * LANES] = blk.astype(BF16)


def _qkv_specs(lay, latent, s_all):
    tq = TOK_TILE
    (q0, qn), (k0, kn), (v0, vn) = lay
    assert q0 % qn == 0 and k0 % kn == 0 and v0 % vn == 0
    q_off = 1 if latent else 0
    kv_rows = s_all if latent else CTX_TILE
    return [
        pl.BlockSpec((None, tq, qn * LANES), lambda i, t: (i, t + q_off, q0 // qn)),
        pl.BlockSpec((None, kv_rows, kn * LANES), lambda i, t: (i, 0, k0 // kn)),
        pl.BlockSpec((None, kv_rows, vn * LANES), lambda i, t: (i, 0, v0 // vn)),
    ]


def dense_attn_call(kind, qkv, latent, extra=None, lambda_init=None):
    b, s_all, _ = qkv.shape
    tq = TOK_TILE
    sq = s_all - CTX_TILE if latent else CTX_TILE
    lay = {"diff": ((G_QD, 8), (G_KD, 2), (G_VD, 4)),
           "mla": ((G_QM, 4), (G_KM, 4), (G_VM, 4)),
           "gqa": ((G_QG, 4), (G_KG, 1), (G_VG, 2))}[kind]
    specs = _qkv_specs(lay, latent, s_all)
    args = [qkv, qkv, qkv]
    if kind == "diff":
        specs = [pl.BlockSpec((4, DIFF_QK_DIM), lambda i, t: (0, 0)),
                 pl.BlockSpec((1, LANES), lambda i, t: (0, 0))] + specs
        args = list(extra) + args
    body = functools.partial(dense_kernel, kind=kind, n_chunks=KV_CHUNKS if latent else 1,
                             lambda_init=lambda_init)
    return pl.pallas_call(
        body,
        name=kind + ("_attn" if latent else "_attn_ctx"),
        grid=(b, sq // tq),
        in_specs=specs,
        out_specs=pl.BlockSpec((None, tq, MIX_COLS), lambda i, t: (i, t, 0)),
        out_shape=jax.ShapeDtypeStruct((b, sq, MIX_COLS), BF16),
        compiler_params=pltpu.CompilerParams(
            dimension_semantics=("arbitrary", "arbitrary"), vmem_limit_bytes=VMEM_LIMIT),
    )(*args)


def _mask_lanes(q, lo, hi):
    lane = _lane(q.shape)
    return jnp.where((lane >= lo) & (lane < hi), q, jnp.zeros_like(q))


def swa_kernel(sink_ref, q_ref, k_ref, v_ref, o_ref, *, s_lat):
    tq = q_ref.shape[0]
    ctx_rows = slice(0, CTX_TILE)
    t = pl.program_id(1)
    n_win = tq // WINDOW + 2
    plan = (((0, 0, HALF, 0, 0), (0, HALF, LANES, 1, 1)),
            ((1, 0, HALF, 1, 1), (1, HALF, LANES, 0, 0)))
    chunks = []
    if s_lat:
        qpos = t * tq + lax.broadcasted_iota(jnp.int32, (tq, WINDOW), 0)
        for c in range(n_win):
            start = t * tq + (c - 1) * WINDOW
            inside = (start >= 0) & (start < s_lat)
            rows = pl.ds(pl.multiple_of(CTX_TILE + jnp.clip(start, 0, s_lat - WINDOW), WINDOW),
                         WINDOW)
            kpos = start + lax.broadcasted_iota(jnp.int32, (tq, WINDOW), 1)
            chunks.append((rows, (jnp.abs(kpos - qpos) <= WINDOW) & inside))
    for ob, maps in enumerate(plan):
        res = []
        for hh, (qb, lo, hi, kb, vb) in enumerate(maps):
            qm = _mask_lanes(q_ref[:, qb * LANES:(qb + 1) * LANES], lo, hi)
            kcol = slice(kb * LANES, (kb + 1) * LANES)
            vcol = slice(vb * LANES, (vb + 1) * LANES)
            sink = sink_ref[2 * ob + hh]
            s_ctx = _dot_nt(qm, k_ref[ctx_rows, kcol])
            m = jnp.maximum(jnp.max(s_ctx, axis=-1, keepdims=True), sink)
            s_loc = []
            for rows, valid in chunks:
                s = jnp.where(valid, _dot_nt(qm, k_ref[rows, kcol]), NEG_INF)
                m = jnp.maximum(m, jnp.max(s, axis=-1, keepdims=True))
                s_loc.append(s)
            p = jnp.exp(s_ctx - m)
            l = jnp.sum(p, axis=-1, keepdims=True) + jnp.exp(sink - m)
            acc = _dot(p.astype(BF16), v_ref[ctx_rows, vcol])
            for (rows, _), s in zip(chunks, s_loc):
                p = jnp.exp(s - m)
                l = l + jnp.sum(p, axis=-1, keepdims=True)
                acc = acc + _dot(p.astype(BF16), v_ref[rows, vcol])
            res.append(acc / l)
        o_ref[:, ob * LANES:(ob + 1) * LANES] = _halves(res[0], res[1], res[0].shape).astype(BF16)


def swa_call(sink, qkv, latent):
    b, s_all, _ = qkv.shape
    tq = TOK_TILE
    s_lat = s_all - CTX_TILE if latent else 0
    sq = s_lat if latent else CTX_TILE
    specs = [pl.BlockSpec(memory_space=pltpu.SMEM)]
    specs += _qkv_specs(((G_QS, 2), (G_KS, 2), (G_VS, 2)), latent, s_all)
    return pl.pallas_call(
        functools.partial(swa_kernel, s_lat=s_lat),
        name="swa_attn" if latent else "swa_attn_ctx",
        grid=(b, sq // tq),
        in_specs=specs,
        out_specs=pl.BlockSpec((None, tq, MIX_COLS), lambda i, t: (i, t, 0)),
        out_shape=jax.ShapeDtypeStruct((b, sq, MIX_COLS), BF16),
        compiler_params=pltpu.CompilerParams(
            dimension_semantics=("arbitrary", "arbitrary"), vmem_limit_bytes=VMEM_LIMIT),
    )(sink, qkv, qkv, qkv)


def mixout_kernel(x_ref, ya_ref, yb_ref, ym_ref, yd_ref, w_ref, mod_ref, gpost_ref,
                  gfpre_ref, x1_ref, h2_ref):
    y = jnp.concatenate([ya_ref[...], yb_ref[...], ym_ref[...], yd_ref[...]], axis=1)
    z = _dot(y, w_ref[...])
    x1 = x_ref[...] + mod_ref[2:3, :] * (z * _rms(z, D_MODEL) * gpost_ref[...])
    x1_ref[...] = x1
    h2 = x1 * _rms(x1, D_MODEL) * gfpre_ref[...]
    h2_ref[...] = (h2 * (1.0 + mod_ref[4:5, :]) + mod_ref[3:4, :]).astype(BF16)


def mixout_call(x, ys, mod, mod_row, lw):
    b, s, d = x.shape
    tq = TOK_TILE
    if mod_row is None:
        mod_map = lambda i, t: (i, 0, 0)
    else:
        mod_map = lambda i, t: (mod_row, 0, 0)
    const2 = lambda i, t: (0, 0)
    tok = lambda i, t: (i, t, 0)
    y_spec = pl.BlockSpec((None, tq, MIX_COLS), tok)
    return pl.pallas_call(
        mixout_kernel,
        name="mixout",
        grid=(b, s // tq),
        in_specs=[pl.BlockSpec((None, tq, d), tok), y_spec, y_spec, y_spec, y_spec,
                  pl.BlockSpec((4 * MIX_COLS, d), const2),
                  pl.BlockSpec((None, N_MOD, d), mod_map),
                  pl.BlockSpec((1, d), const2), pl.BlockSpec((1, d), const2)],
        out_specs=[pl.BlockSpec((None, tq, d), tok), pl.BlockSpec((None, tq, d), tok)],
        out_shape=[jax.ShapeDtypeStruct((b, s, d), F32), jax.ShapeDtypeStruct((b, s, d), BF16)],
        compiler_params=pltpu.CompilerParams(
            dimension_semantics=("arbitrary", "arbitrary"), vmem_limit_bytes=VMEM_LIMIT),
    )(x, *ys, lw["w_out"], mod, lw["g_mix_post"], lw["g_ffn_pre"])


def ffn_kernel(x1_ref, hc_ref, hp_ref, hn_ref, wup_ref, cw_ref, cb_ref, wdn_ref, mod_ref,
               gpost_ref, o_ref, g_ref):
    t = pl.program_id(1)
    tq = hc_ref.shape[0]
    halo = hp_ref.shape[0]
    hp = jnp.where(t > 0, hp_ref[...], jnp.zeros_like(hp_ref))
    hn = jnp.where(t < pl.num_programs(1) - 1, hn_ref[...], jnp.zeros_like(hn_ref))
    hext = jnp.concatenate([hp, hc_ref[...], hn], axis=0)
    rows = hext.shape[0]

    for j in range(wup_ref.shape[0]):
        u = _dot(hext, wup_ref[j])
        w = cw_ref[j]
        conv = (pltpu.roll(u, 1, 0)[halo:halo + tq] * w[0:1]
                + u[halo:halo + tq] * w[1:2]
                + pltpu.roll(u, rows - 1, 0)[halo:halo + tq] * w[2:3] + cb_ref[j])
        a = conv[:, :FFN_CHUNK]
        gated = a * (1.0 / (1.0 + jnp.exp(-a))) * conv[:, FFN_CHUNK:]
        g_ref[:, j * FFN_CHUNK:(j + 1) * FFN_CHUNK] = gated.astype(BF16)

    f = _dot(g_ref[...], wdn_ref[...])
    o_ref[...] = x1_ref[...] + mod_ref[5:6, :] * (f * _rms(f, D_MODEL) * gpost_ref[...])


def ffn_call(x1, h2, mod, mod_row, lw):
    b, s, d = x1.shape
    tq = min(FFN_TILE, s)
    halo = BF16_SUBLANES
    per_tile = tq // halo
    last = s // halo - 1
    n_ch = FFN_DIM // FFN_CHUNK
    if mod_row is None:
        mod_map = lambda i, t: (i, 0, 0)
    else:
        mod_map = lambda i, t: (mod_row, 0, 0)
    const2 = lambda i, t: (0, 0)
    const3 = lambda i, t: (0, 0, 0)
    tok = lambda i, t: (i, t, 0)
    return pl.pallas_call(
        ffn_kernel,
        name="conv_ffn",
        grid=(b, s // tq),
        in_specs=[
            pl.BlockSpec((None, tq, d), tok),
            pl.BlockSpec((None, tq, d), tok),
            pl.BlockSpec((None, halo, d), lambda i, t: (i, jnp.maximum(t * per_tile - 1, 0), 0)),
            pl.BlockSpec((None, halo, d), lambda i, t: (i, jnp.minimum((t + 1) * per_tile, last), 0)),
            pl.BlockSpec((n_ch, d, 2 * FFN_CHUNK), const3),
            pl.BlockSpec((n_ch, 3, 2 * FFN_CHUNK), const3),
            pl.BlockSpec((n_ch, 1, 2 * FFN_CHUNK), const3),
            pl.BlockSpec((FFN_DIM, d), const2),
            pl.BlockSpec((None, N_MOD, d), mod_map),
            pl.BlockSpec((1, d), const2),
        ],
        out_specs=pl.BlockSpec((None, tq, d), tok),
        out_shape=jax.ShapeDtypeStruct((b, s, d), F32),
        scratch_shapes=[pltpu.VMEM((tq, FFN_DIM), BF16)],
        compiler_params=pltpu.CompilerParams(
            dimension_semantics=("arbitrary", "arbitrary"), vmem_limit_bytes=VMEM_LIMIT),
    )(x1, h2, h2, h2, lw["w_up"], lw["conv_w"], lw["conv_b"], lw["w_down"], mod,
      lw["g_ffn_post"])


def _rope_table(s, rot_dim, lane_lo, lane_hi, period):
    n = rot_dim // 4
    tok = np.arange(s)
    pos = np.stack([tok // GRID_W, tok % GRID_W], axis=1).astype(np.float32)
    lane = np.arange(LANES)
    o = (lane - lane_lo) % period
    axis = o // (2 * n)
    second = (o % (2 * n)) // n
    active = (lane >= lane_lo) & (lane < lane_hi)
    inv_freq = ROPE_THETA ** (-jnp.arange(n, dtype=F32) / n)
    ang = jnp.asarray(pos)[:, axis] * inv_freq[o % n][None, :]
    cos = jnp.where(active[None], jnp.cos(ang), 1.0)
    sin = jnp.where(active[None], jnp.sin(ang), 0.0)
    sa = jnp.where((second == 0)[None], -sin, 0.0)
    sb = jnp.where((second == 1)[None], sin, 0.0)
    return jnp.stack([cos, sa, sb]).astype(F32)


def _layer_weights(l, w):
    d = D_MODEL
    w_in = w["w_in"][l]
    mla0 = 768 + 512
    cq_hi = mla0 + MLA_Q_RANK
    kr0 = cq_hi + MLA_KV_RANK
    w_in_p = jnp.concatenate([
        w_in[:, :mla0 + LANES],
        w_in[:, mla0 + LANES:cq_hi], w_in[:, kr0:kr0 + MLA_ROPE_DIM],
        jnp.zeros((d, LANES - HALF - MLA_ROPE_DIM), F32),
        w_in[:, cq_hi:kr0],
        w_in[:, kr0 + MLA_ROPE_DIM:],
    ], axis=1)
    dqk = MLA_NOPE_DIM + MLA_ROPE_DIM
    wq = jnp.zeros((2 * LANES, MLA_HEADS, LANES), F32)
    wq = wq.at[:MLA_Q_RANK, :, :dqk].set(w["mla_w_uq"][l].reshape(MLA_Q_RANK, MLA_HEADS, dqk))
    wukv = w["mla_w_ukv"][l].reshape(MLA_KV_RANK, MLA_HEADS, MLA_NOPE_DIM + MLA_V_DIM)
    wk = jnp.zeros((2 * LANES, MLA_HEADS, LANES), F32)
    wk = wk.at[:MLA_KV_RANK, :, :MLA_NOPE_DIM].set(wukv[:, :, :MLA_NOPE_DIM])
    place = jnp.eye(MLA_ROPE_DIM, dtype=F32)[:, None, :]
    wk = wk.at[LANES + HALF:LANES + HALF + MLA_ROPE_DIM, :, MLA_NOPE_DIM:dqk].set(
        jnp.broadcast_to(place, (MLA_ROPE_DIM, MLA_HEADS, MLA_ROPE_DIM)))
    wv = jnp.zeros((2 * LANES, MLA_HEADS, MLA_V_DIM), F32)
    wv = wv.at[:MLA_KV_RANK].set(wukv[:, :, MLA_NOPE_DIM:])
    wkv = jnp.concatenate([wk.reshape(2 * LANES, -1), wv.reshape(2 * LANES, -1)], axis=1)
    gq = jnp.zeros((1, 2 * LANES), F32).at[0, :MLA_Q_RANK].set(w["mla_g_q"][l])
    n_ch = FFN_DIM // FFN_CHUNK

    def chunked(a):
        lead = a.shape[:-1]
        a = a.reshape(lead + (2, n_ch, FFN_CHUNK))
        a = jnp.moveaxis(a, -2, 0)
        return a.reshape((n_ch,) + lead + (2 * FFN_CHUNK,))

    return {
        "g_mix_pre": w["g_mix_pre"][l][None], "g_mix_post": w["g_mix_post"][l][None],
        "g_ffn_pre": w["g_ffn_pre"][l][None], "g_ffn_post": w["g_ffn_post"][l][None],
        "w_in": w_in_p.astype(BF16),
        "wq": wq.reshape(2 * LANES, -1).astype(BF16), "wkv": wkv.astype(BF16),
        "gq": gq, "gkv": w["mla_g_kv"][l][None],
        "gnq": jnp.tile(w["gqa_g_q"][l], 2)[None], "gnk": jnp.tile(w["gqa_g_k"][l], 2)[None],
        "lam": jnp.stack([w["diff_lam_q1"][l], w["diff_lam_k1"][l],
                          w["diff_lam_q2"][l], w["diff_lam_k2"][l]]),
        "g_sub": jnp.tile(w["diff_g_sub"][l], 2)[None],
        "sink": w["swa_sink"][l],
        "w_out": w["w_out"][l].astype(BF16),
        "w_up": chunked(w["ffn_w_up"][l]).astype(BF16),
        "conv_w": chunked(w["ffn_conv_w"][l]),
        "conv_b": chunked(w["ffn_conv_b"][l][None]),
        "w_down": w["ffn_w_down"][l].astype(BF16),
    }


def kernel(x, c, ctx, c_ctx, w_mod, b_mod, g_mix_pre, g_mix_post, g_ffn_pre, g_ffn_post, w_in, diff_lam_q1, diff_lam_k1, diff_lam_q2, diff_lam_k2, diff_g_sub, swa_sink, mla_g_q, mla_g_kv, mla_w_uq, mla_w_ukv, gqa_g_q, gqa_g_k, w_out, ffn_w_up, ffn_conv_w, ffn_conv_b, ffn_w_down):
    w = dict(g_mix_pre=g_mix_pre, g_mix_post=g_mix_post, g_ffn_pre=g_ffn_pre,
             g_ffn_post=g_ffn_post, w_in=w_in, diff_lam_q1=diff_lam_q1, diff_lam_k1=diff_lam_k1,
             diff_lam_q2=diff_lam_q2, diff_lam_k2=diff_lam_k2, diff_g_sub=diff_g_sub,
             swa_sink=swa_sink, mla_g_q=mla_g_q, mla_g_kv=mla_g_kv, mla_w_uq=mla_w_uq,
             mla_w_ukv=mla_w_ukv, gqa_g_q=gqa_g_q, gqa_g_k=gqa_g_k, w_out=w_out,
             ffn_w_up=ffn_w_up, ffn_conv_w=ffn_conv_w, ffn_conv_b=ffn_conv_b,
             ffn_w_down=ffn_w_down)
    b, s, d = x.shape
    depth = w_mod.shape[0]
    ctx_row = b
    mod_rows = 16
    cc = jnp.zeros((mod_rows, d), F32).at[:b].set(c).at[ctx_row].set(c_ctx)
    mod_all = mod_call(cc, w_mod, b_mod).reshape(depth, mod_rows, N_MOD, d)

    tabs = (_rope_table(s, DIFF_QK_DIM, 0, LANES, DIFF_QK_DIM),
            _rope_table(s, GQA_HEAD_DIM, 0, LANES, GQA_HEAD_DIM),
            _rope_table(s, MLA_ROPE_DIM, MLA_NOPE_DIM, MLA_NOPE_DIM + MLA_ROPE_DIM, MLA_ROPE_DIM))

    xc = ctx
    for l in range(depth):
        need_ctx = l < depth - 1
        lambda_init = 0.8 - 0.6 * math.exp(-0.3 * l)
        lw = _layer_weights(l, w)
        mod = mod_all[l]
        s_all = xc.shape[1] + s
        qkv = proj_call(xc, mod, ctx_row, lw, tabs, False, s_all, 0)
        qkv = proj_call(x, mod, None, lw, tabs, True, s_all, xc.shape[1], qkv)
        diff_extra = (lw["lam"], lw["g_sub"])
        ys = (dense_attn_call("diff", qkv, True, diff_extra, lambda_init),
              swa_call(lw["sink"], qkv, True),
              dense_attn_call("mla", qkv, True),
              dense_attn_call("gqa", qkv, True))
        x1, h2 = mixout_call(x, ys, mod, None, lw)
        x = ffn_call(x1, h2, mod, None, lw)
        if need_ctx:
            ysc = (dense_attn_call("diff", qkv, False, diff_extra, lambda_init),
                   swa_call(lw["sink"], qkv, False),
                   dense_attn_call("mla", qkv, False),
                   dense_attn_call("gqa", qkv, False))
            xc1, hc2 = mixout_call(xc, ysc, mod, ctx_row, lw)
            xc = ffn_call(xc1, hc2, mod, ctx_row, lw)
    return x
```

```python
import functools
import math

import jax
import jax.numpy as jnp
import numpy as np
from jax import lax
from jax.experimental import pallas as pl
from jax.experimental.pallas import tpu as pltpu

F32 = jnp.float32
BF16 = jnp.bfloat16

D_MODEL = 1024
GRID_W = 64
ROPE_THETA = 10000.0
NORM_EPS = 1e-6
NEG_INF = -1e30
N_MOD = 6

DIFF_HEADS, DIFF_QK_DIM, DIFF_V_DIM = 4, 32, 64
SWA_HEAD_DIM, WINDOW = 64, 128
MLA_Q_RANK, MLA_KV_RANK = 192, 128
MLA_NOPE_DIM, MLA_ROPE_DIM, MLA_V_DIM, MLA_HEADS = 64, 32, 64, 4
GQA_HEAD_DIM = 64
FFN_DIM = 2816
MIX_COLS = 256

LANES = 128
BF16_SUBLANES = 16
HALF = LANES // 2
VMEM_LIMIT = 56 * 1024 * 1024

P_COLS = 17 * LANES
G_QD, G_QM, G_VD, G_QG = 0, 8, 16, 20
G_KD, G_QS, G_KS, G_VS, G_VG, G_KM, G_VM, G_KG = 24, 26, 28, 30, 32, 34, 36, 38
QKV_COLS = 39 * LANES
LOG2E = math.log2(math.e)

TOK_TILE = 256
CTX_TILE = 256
KV_CHUNKS = 2
FFN_TILE = 512
FFN_CHUNK = 256


def _rms(x, n):
    return lax.rsqrt(jnp.sum(x * x, axis=-1, keepdims=True) * (1.0 / n) + NORM_EPS)


def _lane(shape):
    return lax.broadcasted_iota(jnp.int32, shape, len(shape) - 1)


def _dot(a, b):
    return jnp.dot(a, b, preferred_element_type=F32)


def _dot_nt(a, b):
    return lax.dot_general(a, b, (((1,), (1,)), ((), ())), preferred_element_type=F32)


def mod_kernel(c_ref, w_ref, b_ref, o_ref):
    c = c_ref[...]
    s = c * (1.0 / (1.0 + jnp.exp(-c)))
    o_ref[...] = jnp.dot(s, w_ref[...], preferred_element_type=F32,
                         precision=lax.Precision.HIGHEST) + b_ref[...]


def mod_call(cc, w_mod, b_mod):
    depth, d, n = w_mod.shape
    rows = cc.shape[0]
    tn = 1024
    return pl.pallas_call(
        mod_kernel,
        name="mod_vectors",
        grid=(depth, n // tn),
        in_specs=[
            pl.BlockSpec((rows, d), lambda l, j: (0, 0)),
            pl.BlockSpec((None, d, tn), lambda l, j: (l, 0, j)),
            pl.BlockSpec((None, 1, tn), lambda l, j: (l, 0, j)),
        ],
        out_specs=pl.BlockSpec((None, rows, tn), lambda l, j: (l, 0, j)),
        out_shape=jax.ShapeDtypeStruct((depth, rows, n), F32),
        compiler_params=pltpu.CompilerParams(
            dimension_semantics=("arbitrary", "arbitrary"), vmem_limit_bytes=VMEM_LIMIT),
    )(cc, w_mod, b_mod.reshape(depth, 1, n))


def _rope(x, tab_ref, n):
    return (x * tab_ref[0] + pltpu.roll(x, LANES - n, 1) * tab_ref[1]
            + pltpu.roll(x, n, 1) * tab_ref[2])


def _halves(lo, hi, shape):
    return jnp.where(_lane(shape) < HALF, lo, hi)


def _head_rms(x):
    low = _lane(x.shape) < HALF
    sq = x * x
    ss_lo = jnp.sum(jnp.where(low, sq, 0.0), axis=-1, keepdims=True)
    ss_hi = jnp.sum(jnp.where(low, 0.0, sq), axis=-1, keepdims=True)
    r_lo = lax.rsqrt(ss_lo * (1.0 / HALF) + NORM_EPS)
    r_hi = lax.rsqrt(ss_hi * (1.0 / HALF) + NORM_EPS)
    return jnp.where(low, r_lo, r_hi)


def proj_kernel(xc_ref, x_ref, mod_ref, gpre_ref, win_ref, wq_ref, wkt_ref, gq_ref, gkv_ref,
                gnq_ref, gnk_ref, t32_ref, t64_ref, tml_ref, o_ref):
    x = jnp.where(pl.program_id(1) == 0, xc_ref[...], x_ref[...])
    h = x * _rms(x, D_MODEL) * gpre_ref[...]
    h = h * (1.0 + mod_ref[1:2, :]) + mod_ref[0:1, :]
    p = _dot(h.astype(BF16), win_ref[...])

    def grp(g):
        return p[:, g * LANES:(g + 1) * LANES]

    def put(g, v):
        o_ref[:, g * LANES:(g + 1) * LANES] = v.astype(BF16)

    def rope32(v):
        return _rope(v, t32_ref, DIFF_QK_DIM // 4)

    def rope64(v):
        return _rope(v, t64_ref, GQA_HEAD_DIM // 4)

    def ropeml(v):
        return _rope(v, tml_ref, MLA_ROPE_DIM // 4)

    lane = _lane((x.shape[0], LANES))
    low = lane < HALF

    def with_ones(v):
        return jnp.where(low, v, 1.0), jnp.where(low, 1.0, v)

    for g in range(2):
        qd = rope32(grp(g)) * (DIFF_QK_DIM ** -0.5 * LOG2E)
        for i in range(4):
            keep = (lane >= i * DIFF_QK_DIM) & (lane < (i + 1) * DIFF_QK_DIM)
            put(G_QD + 4 * g + i, jnp.where(keep, qd, 0.0))
        put(G_KD + g, rope32(grp(2 + g)))
        v_lo, v_hi = with_ones(grp(4 + g))
        put(G_VD + 2 * g, v_lo)
        put(G_VD + 2 * g + 1, v_hi)

    for g in range(2):
        put(G_QS + g, rope64(grp(6 + g)) * (SWA_HEAD_DIM ** -0.5))
    ks = rope64(grp(8))
    put(G_KS, ks)
    put(G_KS + 1, pltpu.roll(ks, HALF, 1))
    put(G_VS, grp(9))
    put(G_VS + 1, pltpu.roll(grp(9), HALF, 1))

    cq0, cq1, ckv = grp(10), grp(11), grp(12)
    ss = (jnp.sum(cq0 * cq0, axis=-1, keepdims=True)
          + jnp.sum(jnp.where(low, cq1 * cq1, 0.0), axis=-1, keepdims=True))
    rq = lax.rsqrt(ss * (1.0 / MLA_Q_RANK) + NORM_EPS)
    cqn = jnp.concatenate([cq0 * rq * gq_ref[:, :LANES], cq1 * rq * gq_ref[:, LANES:]], axis=1)
    qm = _dot(cqn.astype(BF16), wq_ref[...])
    scale_m = (MLA_NOPE_DIM + MLA_ROPE_DIM) ** -0.5 * LOG2E
    rope_lanes = (lane >= MLA_NOPE_DIM) & (lane < MLA_NOPE_DIM + MLA_ROPE_DIM)
    for hd in range(MLA_HEADS):
        qh = ropeml(qm[:, hd * LANES:(hd + 1) * LANES]) * scale_m
        put(G_QM + 2 * hd, _dot(qh.astype(BF16), wkt_ref[hd]))
        put(G_QM + 2 * hd + 1, jnp.where(rope_lanes, qh, 0.0))
    ckvn = ckv * _rms(ckv, MLA_KV_RANK) * gkv_ref[...]
    put(G_KM, ckvn)
    put(G_KM + 1, jnp.where(rope_lanes, ropeml(cq1), 0.0))
    put(G_VM, ckvn)
    put(G_VM + 1, jnp.ones_like(ckvn))

    scale_g = GQA_HEAD_DIM ** -0.5 * LOG2E
    q01 = grp(13)
    q01 = rope64(q01 * _head_rms(q01) * gnq_ref[...]) * scale_g
    q23 = grp(14)
    q23 = rope64(q23 * _head_rms(q23) * gnq_ref[...]) * scale_g
    put(G_QG, jnp.where(low, q01, 0.0))
    put(G_QG + 1, jnp.where(low, pltpu.roll(q01, HALF, 1), 0.0))
    put(G_QG + 2, jnp.where(low, 0.0, pltpu.roll(q23, HALF, 1)))
    put(G_QG + 3, jnp.where(low, 0.0, q23))
    kg = grp(15)
    put(G_KG, rope64(kg * _head_rms(kg) * gnk_ref[...]))
    v_lo, v_hi = with_ones(grp(16))
    put(G_VG, v_lo)
    put(G_VG + 1, v_hi)


def proj_call(xc, x, mod, ctx_row, lw, tabs):
    b, s, d = x.shape
    tq = TOK_TILE
    assert xc.shape[1] == CTX_TILE == tq
    s_all = CTX_TILE + s
    const2 = lambda i, t: (0, 0)
    tab_spec = pl.BlockSpec((3, tq, LANES), lambda i, t: (0, t, 0))
    return pl.pallas_call(
        proj_kernel,
        name="proj",
        grid=(b, s_all // tq),
        in_specs=[
            pl.BlockSpec((None, tq, d), lambda i, t: (i, 0, 0)),
            pl.BlockSpec((None, tq, d), lambda i, t: (i, jnp.maximum(t - 1, 0), 0)),
            pl.BlockSpec((None, N_MOD, d), lambda i, t: (jnp.where(t == 0, ctx_row, i), 0, 0)),
            pl.BlockSpec((1, d), const2),
            pl.BlockSpec((d, P_COLS), const2),
            pl.BlockSpec((2 * LANES, 4 * LANES), const2),
            pl.BlockSpec((MLA_HEADS, LANES, LANES), lambda i, t: (0, 0, 0)),
            pl.BlockSpec((1, 2 * LANES), const2),
            pl.BlockSpec((1, LANES), const2),
            pl.BlockSpec((1, LANES), const2),
            pl.BlockSpec((1, LANES), const2),
            tab_spec, tab_spec, tab_spec,
        ],
        out_specs=pl.BlockSpec((None, tq, QKV_COLS), lambda i, t: (i, t, 0)),
        out_shape=jax.ShapeDtypeStruct((b, s_all, QKV_COLS), BF16),
        compiler_params=pltpu.CompilerParams(
            dimension_semantics=("arbitrary", "arbitrary"), vmem_limit_bytes=VMEM_LIMIT),
    )(xc, x, mod, lw["g_mix_pre"], lw["w_in"], lw["wq"], lw["wkt"], lw["gq"], lw["gkv"],
      lw["gnq"], lw["gnk"], tabs[0], tabs[1], tabs[2])


def _group(ref, rows, g, width=1):
    return ref[rows, g * width * LANES:(g + 1) * width * LANES]


def _attend(q_ref, segs, k_ref, v_ref, n_chunks, width):
    chunk = k_ref.shape[0] // n_chunks
    qs = [jnp.concatenate([_group(q_ref, slice(None), g, width) for g in qg], axis=0)
          for qg, _, _ in segs]

    def body(i, carry):
        rows = pl.ds(pl.multiple_of(i * chunk, LANES), chunk)
        out = []
        for j, (q, (_, kb, parts)) in enumerate(zip(qs, segs)):
            m, acc = carry[2 * j], carry[2 * j + 1]
            s = _dot_nt(q, _group(k_ref, rows, kb, width))
            m_new = jnp.maximum(m, jnp.max(s, axis=-1, keepdims=True))
            p = jnp.exp2(s - m_new).astype(BF16)
            tq = q.shape[0] // sum(n for n, _ in parts)
            pv, r0 = [], 0
            for n, vb in parts:
                pv.append(_dot(p[r0:r0 + n * tq], _group(v_ref, rows, vb, width)))
                r0 += n * tq
            pv = pv[0] if len(pv) == 1 else jnp.concatenate(pv, axis=0)
            out += [m_new, acc * jnp.exp2(m - m_new) + pv]
        return tuple(out)

    init = []
    for q in qs:
        init += [jnp.full((q.shape[0], 1), NEG_INF, F32),
                 jnp.zeros((q.shape[0], width * LANES), F32)]
    carry = lax.fori_loop(0, n_chunks, body, tuple(init))
    half = width * HALF
    return [carry[2 * j + 1] / pltpu.roll(carry[2 * j + 1], half, 1) for j in range(len(segs))]


_PLANS = {
    "gqa": (1, ((((0, 1, 2, 3), 0, ((2, 0), (2, 1))),),)),
    "mla": (2, ((((0, 1, 2, 3), 0, ((2, 0), (2, 0))),),)),
    "diff": (1, ((((0, 1, 2, 3), 0, ((2, 0), (2, 1))),), (((4, 5, 6, 7), 1, ((2, 2), (2, 3))),))),
}


def dense_kernel(*refs, kind, n_chunks, lambda_init):
    refs = list(refs)
    if kind == "diff":
        lam_ref, g_ref = refs[:2]
        refs = refs[2:]
    elif kind == "mla":
        wv_ref = refs[0]
        refs = refs[1:]
    q_ref, k_ref, v_ref, o_ref = refs
    tq = q_ref.shape[0]
    width, loops = _PLANS[kind]
    res = [_attend(q_ref, segs, k_ref, v_ref, n_chunks, width) for segs in loops]

    def tiles(r):
        return [r[i * tq:(i + 1) * tq] for i in range(r.shape[0] // tq)]

    if kind == "gqa":
        h = [t for r in res[0] for t in tiles(r)]
        blocks = [_halves(h[0], pltpu.roll(h[1], HALF, 1), h[0].shape),
                  _halves(pltpu.roll(h[2], HALF, 1), h[3], h[0].shape)]
    elif kind == "mla":
        h = [r[:, :LANES].astype(BF16) for r in tiles(res[0][0])]
        blocks = [_dot(jnp.concatenate(h[2 * ob:2 * ob + 2], axis=1), wv_ref[ob]) for ob in range(2)]
    else:
        lv = lam_ref[...]
        lam = (jnp.exp(jnp.sum(lv[0:1] * lv[1:2], axis=-1, keepdims=True))
               - jnp.exp(jnp.sum(lv[2:3] * lv[3:4], axis=-1, keepdims=True)) + lambda_init)
        blocks = []
        for (r,) in res:
            a = tiles(r)
            o = _halves(a[0] - lam * a[1], a[2] - lam * a[3], a[0].shape)
            blocks.append(o * _head_rms(o) * g_ref[...] * (1.0 - lambda_init))
    for ob, blk in enumerate(blocks):
        o_ref[:, ob * LANES:(ob + 1) * LANES] = blk.astype(BF16)


def _qkv_specs(lay, latent, s_all):
    tq = TOK_TILE
    (q0, qn), (k0, kn), (v0, vn) = lay
    assert q0 % qn == 0 and k0 % kn == 0 and v0 % vn == 0
    q_off = 1 if latent else 0
    kv_rows = s_all if latent else CTX_TILE
    return [
        pl.BlockSpec((None, tq, qn * LANES), lambda i, t: (i, t + q_off, q0 // qn)),
        pl.BlockSpec((None, kv_rows, kn * LANES), lambda i, t: (i, 0, k0 // kn)),
        pl.BlockSpec((None, kv_rows, vn * LANES), lambda i, t: (i, 0, v0 // vn)),
    ]


def dense_attn_call(kind, qkv, latent, extra=None, lambda_init=None):
    b, s_all, _ = qkv.shape
    tq = TOK_TILE
    sq = s_all - CTX_TILE if latent else CTX_TILE
    lay = {"diff": ((G_QD, 8), (G_KD, 2), (G_VD, 4)),
           "mla": ((G_QM, 8), (G_KM, 2), (G_VM, 2)),
           "gqa": ((G_QG, 4), (G_KG, 1), (G_VG, 2))}[kind]
    specs = _qkv_specs(lay, latent, s_all)
    args = [qkv, qkv, qkv]
    if kind == "diff":
        specs = [pl.BlockSpec((4, DIFF_QK_DIM), lambda i, t: (0, 0)),
                 pl.BlockSpec((1, LANES), lambda i, t: (0, 0))] + specs
        args = list(extra) + args
    elif kind == "mla":
        specs = [pl.BlockSpec((2, 2 * LANES, LANES), lambda i, t: (0, 0, 0))] + specs
        args = list(extra) + args
    body = functools.partial(dense_kernel, kind=kind, n_chunks=KV_CHUNKS if latent else 1,
                             lambda_init=lambda_init)
    return pl.pallas_call(
        body,
        name=kind + ("_attn" if latent else "_attn_ctx"),
        grid=(b, sq // tq),
        in_specs=specs,
        out_specs=pl.BlockSpec((None, tq, MIX_COLS), lambda i, t: (i, t, 0)),
        out_shape=jax.ShapeDtypeStruct((b, sq, MIX_COLS), BF16),
        compiler_params=pltpu.CompilerParams(
            dimension_semantics=("arbitrary", "arbitrary"), vmem_limit_bytes=VMEM_LIMIT),
    )(*args)


def _mask_lanes(q, lo, hi):
    lane = _lane(q.shape)
    return jnp.where((lane >= lo) & (lane < hi), q, jnp.zeros_like(q))


def swa_kernel(sink_ref, q_ref, k_ref, v_ref, o_ref, *, s_lat):
    tq = q_ref.shape[0]
    ctx_rows = slice(0, CTX_TILE)
    t = pl.program_id(1)
    n_win = tq // WINDOW + 2
    plan = (((0, 0, HALF, 0, 0), (0, HALF, LANES, 1, 1)),
            ((1, 0, HALF, 1, 1), (1, HALF, LANES, 0, 0)))
    chunks = []
    if s_lat:
        qpos = t * tq + lax.broadcasted_iota(jnp.int32, (tq, WINDOW), 0)
        for c in range(n_win):
            start = t * tq + (c - 1) * WINDOW
            inside = (start >= 0) & (start < s_lat)
            rows = pl.ds(pl.multiple_of(CTX_TILE + jnp.clip(start, 0, s_lat - WINDOW), WINDOW),
                         WINDOW)
            kpos = start + lax.broadcasted_iota(jnp.int32, (tq, WINDOW), 1)
            chunks.append((rows, (jnp.abs(kpos - qpos) <= WINDOW) & inside))
    for ob, maps in enumerate(plan):
        res = []
        for hh, (qb, lo, hi, kb, vb) in enumerate(maps):
            qm = _mask_lanes(q_ref[:, qb * LANES:(qb + 1) * LANES], lo, hi)
            kcol = slice(kb * LANES, (kb + 1) * LANES)
            vcol = slice(vb * LANES, (vb + 1) * LANES)
            sink = sink_ref[2 * ob + hh]
            s_ctx = _dot_nt(qm, k_ref[ctx_rows, kcol])
            m = jnp.maximum(jnp.max(s_ctx, axis=-1, keepdims=True), sink)
            s_loc = []
            for rows, valid in chunks:
                s = jnp.where(valid, _dot_nt(qm, k_ref[rows, kcol]), NEG_INF)
                m = jnp.maximum(m, jnp.max(s, axis=-1, keepdims=True))
                s_loc.append(s)
            p = jnp.exp(s_ctx - m)
            l = jnp.sum(p, axis=-1, keepdims=True) + jnp.exp(sink - m)
            acc = _dot(p.astype(BF16), v_ref[ctx_rows, vcol])
            for (rows, _), s in zip(chunks, s_loc):
                p = jnp.exp(s - m)
                l = l + jnp.sum(p, axis=-1, keepdims=True)
                acc = acc + _dot(p.astype(BF16), v_ref[rows, vcol])
            res.append(acc / l)
        o_ref[:, ob * LANES:(ob + 1) * LANES] = _halves(res[0], res[1], res[0].shape).astype(BF16)


def swa_call(sink, qkv, latent):
    b, s_all, _ = qkv.shape
    tq = TOK_TILE
    s_lat = s_all - CTX_TILE if latent else 0
    sq = s_lat if latent else CTX_TILE
    specs = [pl.BlockSpec(memory_space=pltpu.SMEM)]
    specs += _qkv_specs(((G_QS, 2), (G_KS, 2), (G_VS, 2)), latent, s_all)
    return pl.pallas_call(
        functools.partial(swa_kernel, s_lat=s_lat),
        name="swa_attn" if latent else "swa_attn_ctx",
        grid=(b, sq // tq),
        in_specs=specs,
        out_specs=pl.BlockSpec((None, tq, MIX_COLS), lambda i, t: (i, t, 0)),
        out_shape=jax.ShapeDtypeStruct((b, sq, MIX_COLS), BF16),
        compiler_params=pltpu.CompilerParams(
            dimension_semantics=("arbitrary", "arbitrary"), vmem_limit_bytes=VMEM_LIMIT),
    )(sink, qkv, qkv, qkv)


def mixout_kernel(x_ref, ya_ref, yb_ref, ym_ref, yd_ref, w_ref, mod_ref, gpost_ref,
                  gfpre_ref, x1_ref, h2_ref):
    y = jnp.concatenate([ya_ref[...], yb_ref[...], ym_ref[...], yd_ref[...]], axis=1)
    z = _dot(y, w_ref[...])
    x1 = x_ref[...] + mod_ref[2:3, :] * (z * _rms(z, D_MODEL) * gpost_ref[...])
    x1_ref[...] = x1
    h2 = x1 * _rms(x1, D_MODEL) * gfpre_ref[...]
    h2_ref[...] = (h2 * (1.0 + mod_ref[4:5, :]) + mod_ref[3:4, :]).astype(BF16)


def mixout_call(x, ys, mod, mod_row, lw):
    b, s, d = x.shape
    tq = TOK_TILE
    if mod_row is None:
        mod_map = lambda i, t: (i, 0, 0)
    else:
        mod_map = lambda i, t: (mod_row, 0, 0)
    const2 = lambda i, t: (0, 0)
    tok = lambda i, t: (i, t, 0)
    y_spec = pl.BlockSpec((None, tq, MIX_COLS), tok)
    return pl.pallas_call(
        mixout_kernel,
        name="mixout",
        grid=(b, s // tq),
        in_specs=[pl.BlockSpec((None, tq, d), tok), y_spec, y_spec, y_spec, y_spec,
                  pl.BlockSpec((4 * MIX_COLS, d), const2),
                  pl.BlockSpec((None, N_MOD, d), mod_map),
                  pl.BlockSpec((1, d), const2), pl.BlockSpec((1, d), const2)],
        out_specs=[pl.BlockSpec((None, tq, d), tok), pl.BlockSpec((None, tq, d), tok)],
        out_shape=[jax.ShapeDtypeStruct((b, s, d), F32), jax.ShapeDtypeStruct((b, s, d), BF16)],
        compiler_params=pltpu.CompilerParams(
            dimension_semantics=("arbitrary", "arbitrary"), vmem_limit_bytes=VMEM_LIMIT),
    )(x, *ys, lw["w_out"], mod, lw["g_mix_post"], lw["g_ffn_pre"])


def ffn_kernel(x1_ref, hc_ref, hp_ref, hn_ref, wup_ref, cw_ref, cb_ref, wdn_ref, mod_ref,
               gpost_ref, o_ref, g_ref):
    t = pl.program_id(1)
    tq = hc_ref.shape[0]
    halo = hp_ref.shape[0]
    hp = jnp.where(t > 0, hp_ref[...], jnp.zeros_like(hp_ref))
    hn = jnp.where(t < pl.num_programs(1) - 1, hn_ref[...], jnp.zeros_like(hn_ref))
    hext = jnp.concatenate([hp, hc_ref[...], hn], axis=0)
    rows = hext.shape[0]

    for j in range(wup_ref.shape[0]):
        u = _dot(hext, wup_ref[j])
        w = cw_ref[j]
        conv = (pltpu.roll(u, 1, 0)[halo:halo + tq] * w[0:1]
                + u[halo:halo + tq] * w[1:2]
                + pltpu.roll(u, rows - 1, 0)[halo:halo + tq] * w[2:3] + cb_ref[j])
        a = conv[:, :FFN_CHUNK]
        gated = a * (1.0 / (1.0 + jnp.exp(-a))) * conv[:, FFN_CHUNK:]
        g_ref[:, j * FFN_CHUNK:(j + 1) * FFN_CHUNK] = gated.astype(BF16)

    f = _dot(g_ref[...], wdn_ref[...])
    o_ref[...] = x1_ref[...] + mod_ref[5:6, :] * (f * _rms(f, D_MODEL) * gpost_ref[...])


def ffn_call(x1, h2, mod, mod_row, lw):
    b, s, d = x1.shape
    tq = min(FFN_TILE, s)
    halo = BF16_SUBLANES
    per_tile = tq // halo
    last = s // halo - 1
    n_ch = FFN_DIM // FFN_CHUNK
    if mod_row is None:
        mod_map = lambda i, t: (i, 0, 0)
    else:
        mod_map = lambda i, t: (mod_row, 0, 0)
    const2 = lambda i, t: (0, 0)
    const3 = lambda i, t: (0, 0, 0)
    tok = lambda i, t: (i, t, 0)
    return pl.pallas_call(
        ffn_kernel,
        name="conv_ffn",
        grid=(b, s // tq),
        in_specs=[
            pl.BlockSpec((None, tq, d), tok),
            pl.BlockSpec((None, tq, d), tok),
            pl.BlockSpec((None, halo, d), lambda i, t: (i, jnp.maximum(t * per_tile - 1, 0), 0)),
            pl.BlockSpec((None, halo, d), lambda i, t: (i, jnp.minimum((t + 1) * per_tile, last), 0)),
            pl.BlockSpec((n_ch, d, 2 * FFN_CHUNK), const3),
            pl.BlockSpec((n_ch, 3, 2 * FFN_CHUNK), const3),
            pl.BlockSpec((n_ch, 1, 2 * FFN_CHUNK), const3),
            pl.BlockSpec((FFN_DIM, d), const2),
            pl.BlockSpec((None, N_MOD, d), mod_map),
            pl.BlockSpec((1, d), const2),
        ],
        out_specs=pl.BlockSpec((None, tq, d), tok),
        out_shape=jax.ShapeDtypeStruct((b, s, d), F32),
        scratch_shapes=[pltpu.VMEM((tq, FFN_DIM), BF16)],
        compiler_params=pltpu.CompilerParams(
            dimension_semantics=("arbitrary", "arbitrary"), vmem_limit_bytes=VMEM_LIMIT),
    )(x1, h2, h2, h2, lw["w_up"], lw["conv_w"], lw["conv_b"], lw["w_down"], mod,
      lw["g_ffn_post"])


def _rope_table(n_ctx, s, rot_dim, lane_lo, lane_hi, period):
    n = rot_dim // 4
    tok = np.arange(s)
    pos = np.stack([tok // GRID_W, tok % GRID_W], axis=1).astype(np.float32)
    lane = np.arange(LANES)
    o = (lane - lane_lo) % period
    axis = o // (2 * n)
    second = (o % (2 * n)) // n
    active = (lane >= lane_lo) & (lane < lane_hi)
    inv_freq = ROPE_THETA ** (-jnp.arange(n, dtype=F32) / n)
    ang = jnp.asarray(pos)[:, axis] * inv_freq[o % n][None, :]
    cos = jnp.where(active[None], jnp.cos(ang), 1.0)
    sin = jnp.where(active[None], jnp.sin(ang), 0.0)
    sa = jnp.where((second == 0)[None], -sin, 0.0)
    sb = jnp.where((second == 1)[None], sin, 0.0)
    tab = jnp.stack([cos, sa, sb]).astype(F32)
    ident = jnp.stack([jnp.ones((n_ctx, LANES), F32), jnp.zeros((n_ctx, LANES), F32),
                       jnp.zeros((n_ctx, LANES), F32)])
    return jnp.concatenate([ident, tab], axis=1)


def _layer_weights(l, w):
    d = D_MODEL
    w_in = w["w_in"][l]
    mla0 = 768 + 512
    cq_hi = mla0 + MLA_Q_RANK
    kr0 = cq_hi + MLA_KV_RANK
    w_in_p = jnp.concatenate([
        w_in[:, :mla0 + LANES],
        w_in[:, mla0 + LANES:cq_hi], w_in[:, kr0:kr0 + MLA_ROPE_DIM],
        jnp.zeros((d, LANES - HALF - MLA_ROPE_DIM), F32),
        w_in[:, cq_hi:kr0],
        w_in[:, kr0 + MLA_ROPE_DIM:],
    ], axis=1)
    dqk = MLA_NOPE_DIM + MLA_ROPE_DIM
    wq = jnp.zeros((2 * LANES, MLA_HEADS, LANES), F32)
    wq = wq.at[:MLA_Q_RANK, :, :dqk].set(w["mla_w_uq"][l].reshape(MLA_Q_RANK, MLA_HEADS, dqk))
    assert MLA_KV_RANK == LANES
    wukv = w["mla_w_ukv"][l].reshape(MLA_KV_RANK, MLA_HEADS, MLA_NOPE_DIM + MLA_V_DIM)
    wkt = jnp.zeros((MLA_HEADS, LANES, LANES), F32)
    wkt = wkt.at[:, :MLA_NOPE_DIM, :].set(jnp.transpose(wukv[:, :, :MLA_NOPE_DIM], (1, 2, 0)))
    wuv = jnp.transpose(wukv[:, :, MLA_NOPE_DIM:], (1, 0, 2))
    wv = jnp.zeros((2, 2, LANES, 2, MLA_V_DIM), F32)
    for hd in range(MLA_HEADS):
        wv = wv.at[hd // 2, hd % 2, :, hd % 2, :].set(wuv[hd])
    wv = wv.reshape(2, 2 * LANES, LANES)
    gq = jnp.zeros((1, 2 * LANES), F32).at[0, :MLA_Q_RANK].set(w["mla_g_q"][l])
    n_ch = FFN_DIM // FFN_CHUNK

    def chunked(a):
        lead = a.shape[:-1]
        a = a.reshape(lead + (2, n_ch, FFN_CHUNK))
        a = jnp.moveaxis(a, -2, 0)
        return a.reshape((n_ch,) + lead + (2 * FFN_CHUNK,))

    return {
        "g_mix_pre": w["g_mix_pre"][l][None], "g_mix_post": w["g_mix_post"][l][None],
        "g_ffn_pre": w["g_ffn_pre"][l][None], "g_ffn_post": w["g_ffn_post"][l][None],
        "w_in": w_in_p.astype(BF16),
        "wq": wq.reshape(2 * LANES, -1).astype(BF16), "wkt": wkt.astype(BF16),
        "wv": wv.astype(BF16),
        "gq": gq, "gkv": w["mla_g_kv"][l][None],
        "gnq": jnp.tile(w["gqa_g_q"][l], 2)[None], "gnk": jnp.tile(w["gqa_g_k"][l], 2)[None],
        "lam": jnp.stack([w["diff_lam_q1"][l], w["diff_lam_k1"][l],
                          w["diff_lam_q2"][l], w["diff_lam_k2"][l]]),
        "g_sub": jnp.tile(w["diff_g_sub"][l], 2)[None],
        "sink": w["swa_sink"][l],
        "w_out": w["w_out"][l].astype(BF16),
        "w_up": chunked(w["ffn_w_up"][l]).astype(BF16),
        "conv_w": chunked(w["ffn_conv_w"][l]),
        "conv_b": chunked(w["ffn_conv_b"][l][None]),
        "w_down": w["ffn_w_down"][l].astype(BF16),
    }


def kernel(x, c, ctx, c_ctx, w_mod, b_mod, g_mix_pre, g_mix_post, g_ffn_pre, g_ffn_post, w_in, diff_lam_q1, diff_lam_k1, diff_lam_q2, diff_lam_k2, diff_g_sub, swa_sink, mla_g_q, mla_g_kv, mla_w_uq, mla_w_ukv, gqa_g_q, gqa_g_k, w_out, ffn_w_up, ffn_conv_w, ffn_conv_b, ffn_w_down):
    w = dict(g_mix_pre=g_mix_pre, g_mix_post=g_mix_post, g_ffn_pre=g_ffn_pre,
             g_ffn_post=g_ffn_post, w_in=w_in, diff_lam_q1=diff_lam_q1, diff_lam_k1=diff_lam_k1,
             diff_lam_q2=diff_lam_q2, diff_lam_k2=diff_lam_k2, diff_g_sub=diff_g_sub,
             swa_sink=swa_sink, mla_g_q=mla_g_q, mla_g_kv=mla_g_kv, mla_w_uq=mla_w_uq,
             mla_w_ukv=mla_w_ukv, gqa_g_q=gqa_g_q, gqa_g_k=gqa_g_k, w_out=w_out,
             ffn_w_up=ffn_w_up, ffn_conv_w=ffn_conv_w, ffn_conv_b=ffn_conv_b,
             ffn_w_down=ffn_w_down)
    b, s, d = x.shape
    depth = w_mod.shape[0]
    ctx_row = b
    mod_rows = 16
    cc = jnp.zeros((mod_rows, d), F32).at[:b].set(c).at[ctx_row].set(c_ctx)
    mod_all = mod_call(cc, w_mod, b_mod).reshape(depth, mod_rows, N_MOD, d)

    n_ctx = ctx.shape[1]
    tabs = (_rope_table(n_ctx, s, DIFF_QK_DIM, 0, LANES, DIFF_QK_DIM),
            _rope_table(n_ctx, s, GQA_HEAD_DIM, 0, LANES, GQA_HEAD_DIM),
            _rope_table(n_ctx, s, MLA_ROPE_DIM, MLA_NOPE_DIM, MLA_NOPE_DIM + MLA_ROPE_DIM,
                        MLA_ROPE_DIM))

    xc = ctx
    for l in range(depth):
        need_ctx = l < depth - 1
        lambda_init = 0.8 - 0.6 * math.exp(-0.3 * l)
        lw = _layer_weights(l, w)
        mod = mod_all[l]
        qkv = proj_call(xc, x, mod, ctx_row, lw, tabs)
        diff_extra = (lw["lam"], lw["g_sub"])
        mla_extra = (lw["wv"],)
        ys = (dense_attn_call("diff", qkv, True, diff_extra, lambda_init),
              swa_call(lw["sink"], qkv, True),
              dense_attn_call("mla", qkv, True, mla_extra),
              dense_attn_call("gqa", qkv, True))
        x1, h2 = mixout_call(x, ys, mod, None, lw)
        x = ffn_call(x1, h2, mod, None, lw)
        if need_ctx:
            ysc = (dense_attn_call("diff", qkv, False, diff_extra, lambda_init),
                   swa_call(lw["sink"], qkv, False),
                   dense_attn_call("mla", qkv, False, mla_extra),
                   dense_attn_call("gqa", qkv, False))
            xc1, hc2 = mixout_call(xc, ysc, mod, ctx_row, lw)
            xc = ffn_call(xc1, hc2, mod, ctx_row, lw)
    return x
```

```python
import functools
import math

import jax
import jax.numpy as jnp
import numpy as np
from jax import lax
from jax.experimental import pallas as pl
from jax.experimental.pallas import tpu as pltpu

F32 = jnp.float32
BF16 = jnp.bfloat16

D_MODEL = 1024
GRID_W = 64
ROPE_THETA = 10000.0
NORM_EPS = 1e-6
NEG_INF = -1e30
N_MOD = 6

DIFF_HEADS, DIFF_QK_DIM, DIFF_V_DIM = 4, 32, 64
SWA_HEAD_DIM, WINDOW = 64, 128
MLA_Q_RANK, MLA_KV_RANK = 192, 128
MLA_NOPE_DIM, MLA_ROPE_DIM, MLA_V_DIM, MLA_HEADS = 64, 32, 64, 4
GQA_HEAD_DIM = 64
FFN_DIM = 2816
MIX_COLS = 256

LANES = 128
BF16_SUBLANES = 16
HALF = LANES // 2
VMEM_LIMIT = 56 * 1024 * 1024

P_COLS = 17 * LANES
G_QD, G_QM, G_VD, G_QG = 0, 8, 16, 20
G_KD, G_QS, G_KS, G_VS, G_VG, G_KM, G_VM, G_KG = 24, 26, 28, 30, 32, 34, 36, 38
QKV_COLS = 39 * LANES
LOG2E = math.log2(math.e)

TOK_TILE = 256
CTX_TILE = 256
KV_CHUNKS = 2
FFN_TILE = 512
FFN_CHUNK = 256


def _rms(x, n):
    return lax.rsqrt(jnp.sum(x * x, axis=-1, keepdims=True) * (1.0 / n) + NORM_EPS)


def _lane(shape):
    return lax.broadcasted_iota(jnp.int32, shape, len(shape) - 1)


def _dot(a, b):
    return jnp.dot(a, b, preferred_element_type=F32)


def _dot_nt(a, b):
    return lax.dot_general(a, b, (((1,), (1,)), ((), ())), preferred_element_type=F32)


def mod_kernel(c_ref, w_ref, b_ref, o_ref):
    c = c_ref[...]
    s = c * (1.0 / (1.0 + jnp.exp(-c)))
    o_ref[...] = jnp.dot(s, w_ref[...], preferred_element_type=F32,
                         precision=lax.Precision.HIGHEST) + b_ref[...]


def mod_call(cc, w_mod, b_mod):
    depth, d, n = w_mod.shape
    rows = cc.shape[0]
    tn = 1024
    return pl.pallas_call(
        mod_kernel,
        name="mod_vectors",
        grid=(depth, n // tn),
        in_specs=[
            pl.BlockSpec((rows, d), lambda l, j: (0, 0)),
            pl.BlockSpec((None, d, tn), lambda l, j: (l, 0, j)),
            pl.BlockSpec((None, 1, tn), lambda l, j: (l, 0, j)),
        ],
        out_specs=pl.BlockSpec((None, rows, tn), lambda l, j: (l, 0, j)),
        out_shape=jax.ShapeDtypeStruct((depth, rows, n), F32),
        compiler_params=pltpu.CompilerParams(
            dimension_semantics=("arbitrary", "arbitrary"), vmem_limit_bytes=VMEM_LIMIT),
    )(cc, w_mod, b_mod.reshape(depth, 1, n))


def _rope(x, tab_ref, n):
    return (x * tab_ref[0] + pltpu.roll(x, LANES - n, 1) * tab_ref[1]
            + pltpu.roll(x, n, 1) * tab_ref[2])


def _halves(lo, hi, shape):
    return jnp.where(_lane(shape) < HALF, lo, hi)


def _head_rms(x):
    low = _lane(x.shape) < HALF
    sq = x * x
    ss_lo = jnp.sum(jnp.where(low, sq, 0.0), axis=-1, keepdims=True)
    ss_hi = jnp.sum(jnp.where(low, 0.0, sq), axis=-1, keepdims=True)
    r_lo = lax.rsqrt(ss_lo * (1.0 / HALF) + NORM_EPS)
    r_hi = lax.rsqrt(ss_hi * (1.0 / HALF) + NORM_EPS)
    return jnp.where(low, r_lo, r_hi)


def proj_kernel(xc_ref, x_ref, mod_ref, gpre_ref, win_ref, wq_ref, wkt_ref, gq_ref, gkv_ref,
                gnq_ref, gnk_ref, t32_ref, t64_ref, tml_ref, o_ref, pa_ref, pb_ref, *,
                tiles_per_sample, n_tiles):
    g = pl.program_id(0)
    is_ctx = jnp.minimum(g, n_tiles - 1) % tiles_per_sample == 0

    @pl.when(g == 0)
    def _():
        pb_ref[...] = jnp.zeros_like(pb_ref)

    def project(dst_ref):
        x = jnp.where(is_ctx, xc_ref[...], x_ref[...])
        h = x * _rms(x, D_MODEL) * gpre_ref[...]
        h = h * (1.0 + mod_ref[1:2, :]) + mod_ref[0:1, :]
        dst_ref[...] = _dot(h.astype(BF16), win_ref[...])

    finish = functools.partial(_proj_finish, wq_ref, wkt_ref, gq_ref, gkv_ref, gnq_ref, gnk_ref,
                               t32_ref, t64_ref, tml_ref, o_ref)

    @pl.when(g % 2 == 0)
    def _():
        project(pa_ref)
        finish(pb_ref)

    @pl.when(g % 2 == 1)
    def _():
        project(pb_ref)
        finish(pa_ref)


def _proj_finish(wq_ref, wkt_ref, gq_ref, gkv_ref, gnq_ref, gnk_ref, t32_ref, t64_ref, tml_ref,
                 o_ref, p_ref):
    def grp(g):
        return p_ref[:, g * LANES:(g + 1) * LANES]

    def put(g, v):
        o_ref[:, g * LANES:(g + 1) * LANES] = v.astype(BF16)

    def rope32(v):
        return _rope(v, t32_ref, DIFF_QK_DIM // 4)

    def rope64(v):
        return _rope(v, t64_ref, GQA_HEAD_DIM // 4)

    def ropeml(v):
        return _rope(v, tml_ref, MLA_ROPE_DIM // 4)

    lane = _lane((p_ref.shape[0], LANES))
    low = lane < HALF

    def with_ones(v):
        return jnp.where(low, v, 1.0), jnp.where(low, 1.0, v)

    for g in range(2):
        qd = rope32(grp(g)) * (DIFF_QK_DIM ** -0.5 * LOG2E)
        for i in range(4):
            keep = (lane >= i * DIFF_QK_DIM) & (lane < (i + 1) * DIFF_QK_DIM)
            put(G_QD + 4 * g + i, jnp.where(keep, qd, 0.0))
        put(G_KD + g, rope32(grp(2 + g)))
        v_lo, v_hi = with_ones(grp(4 + g))
        put(G_VD + 2 * g, v_lo)
        put(G_VD + 2 * g + 1, v_hi)

    for g in range(2):
        put(G_QS + g, rope64(grp(6 + g)) * (SWA_HEAD_DIM ** -0.5))
    ks = rope64(grp(8))
    put(G_KS, ks)
    put(G_KS + 1, pltpu.roll(ks, HALF, 1))
    put(G_VS, grp(9))
    put(G_VS + 1, pltpu.roll(grp(9), HALF, 1))

    cq0, cq1, ckv = grp(10), grp(11), grp(12)
    ss = (jnp.sum(cq0 * cq0, axis=-1, keepdims=True)
          + jnp.sum(jnp.where(low, cq1 * cq1, 0.0), axis=-1, keepdims=True))
    rq = lax.rsqrt(ss * (1.0 / MLA_Q_RANK) + NORM_EPS)
    cqn = jnp.concatenate([cq0 * rq * gq_ref[:, :LANES], cq1 * rq * gq_ref[:, LANES:]], axis=1)
    qm = _dot(cqn.astype(BF16), wq_ref[...])
    scale_m = (MLA_NOPE_DIM + MLA_ROPE_DIM) ** -0.5 * LOG2E
    rope_lanes = (lane >= MLA_NOPE_DIM) & (lane < MLA_NOPE_DIM + MLA_ROPE_DIM)
    for hd in range(MLA_HEADS):
        qh = ropeml(qm[:, hd * LANES:(hd + 1) * LANES]) * scale_m
        put(G_QM + 2 * hd, _dot(qh.astype(BF16), wkt_ref[hd]))
        put(G_QM + 2 * hd + 1, jnp.where(rope_lanes, qh, 0.0))
    ckvn = ckv * _rms(ckv, MLA_KV_RANK) * gkv_ref[...]
    put(G_KM, ckvn)
    put(G_KM + 1, jnp.where(rope_lanes, ropeml(cq1), 0.0))
    put(G_VM, ckvn)
    put(G_VM + 1, jnp.ones_like(ckvn))

    scale_g = GQA_HEAD_DIM ** -0.5 * LOG2E
    q01 = grp(13)
    q01 = rope64(q01 * _head_rms(q01) * gnq_ref[...]) * scale_g
    q23 = grp(14)
    q23 = rope64(q23 * _head_rms(q23) * gnq_ref[...]) * scale_g
    put(G_QG, jnp.where(low, q01, 0.0))
    put(G_QG + 1, jnp.where(low, pltpu.roll(q01, HALF, 1), 0.0))
    put(G_QG + 2, jnp.where(low, 0.0, pltpu.roll(q23, HALF, 1)))
    put(G_QG + 3, jnp.where(low, 0.0, q23))
    kg = grp(15)
    put(G_KG, rope64(kg * _head_rms(kg) * gnk_ref[...]))
    v_lo, v_hi = with_ones(grp(16))
    put(G_VG, v_lo)
    put(G_VG + 1, v_hi)


def proj_call(xc, x, mod, ctx_row, lw, tabs):
    b, s, d = x.shape
    tq = TOK_TILE
    assert xc.shape[1] == CTX_TILE == tq
    s_all = CTX_TILE + s
    per = s_all // tq
    n = b * per
    const2 = lambda g: (0, 0)

    def proj_tile(g):
        ga = jnp.minimum(g, n - 1)
        return ga // per, ga % per

    def done_tile(g):
        gb = jnp.maximum(g - 1, 0)
        return gb // per, gb % per

    def x_map(g):
        i, t = proj_tile(g)
        return i, jnp.maximum(t - 1, 0), 0

    def mod_map(g):
        i, t = proj_tile(g)
        return jnp.where(t == 0, ctx_row, i), 0, 0

    tab_spec = pl.BlockSpec((3, tq, LANES), lambda g: (0, done_tile(g)[1], 0))
    return pl.pallas_call(
        functools.partial(proj_kernel, tiles_per_sample=per, n_tiles=n),
        name="proj",
        grid=(n + 1,),
        in_specs=[
            pl.BlockSpec((None, tq, d), lambda g: (proj_tile(g)[0], 0, 0)),
            pl.BlockSpec((None, tq, d), x_map),
            pl.BlockSpec((None, N_MOD, d), mod_map),
            pl.BlockSpec((1, d), const2),
            pl.BlockSpec((d, P_COLS), const2),
            pl.BlockSpec((2 * LANES, 4 * LANES), const2),
            pl.BlockSpec((MLA_HEADS, LANES, LANES), lambda g: (0, 0, 0)),
            pl.BlockSpec((1, 2 * LANES), const2),
            pl.BlockSpec((1, LANES), const2),
            pl.BlockSpec((1, LANES), const2),
            pl.BlockSpec((1, LANES), const2),
            tab_spec, tab_spec, tab_spec,
        ],
        out_specs=pl.BlockSpec((None, tq, QKV_COLS), lambda g: done_tile(g) + (0,)),
        out_shape=jax.ShapeDtypeStruct((b, s_all, QKV_COLS), BF16),
        scratch_shapes=[pltpu.VMEM((tq, P_COLS), F32), pltpu.VMEM((tq, P_COLS), F32)],
        compiler_params=pltpu.CompilerParams(
            dimension_semantics=("arbitrary",), vmem_limit_bytes=VMEM_LIMIT),
    )(xc, x, mod, lw["g_mix_pre"], lw["w_in"], lw["wq"], lw["wkt"], lw["gq"], lw["gkv"],
      lw["gnq"], lw["gnk"], tabs[0], tabs[1], tabs[2])


def _group(ref, rows, g, width=1):
    return ref[rows, g * width * LANES:(g + 1) * width * LANES]


def _attend(q_ref, segs, k_ref, v_ref, n_chunks, width):
    chunk = k_ref.shape[0] // n_chunks
    qs = [jnp.concatenate([_group(q_ref, slice(None), g, width) for g in qg], axis=0)
          for qg, _, _ in segs]

    def body(i, carry):
        rows = pl.ds(pl.multiple_of(i * chunk, LANES), chunk)
        out = []
        for j, (q, (_, kb, parts)) in enumerate(zip(qs, segs)):
            m, acc = carry[2 * j], carry[2 * j + 1]
            s = _dot_nt(q, _group(k_ref, rows, kb, width))
            m_new = jnp.maximum(m, jnp.max(s, axis=-1, keepdims=True))
            p = jnp.exp2(s - m_new).astype(BF16)
            tq = q.shape[0] // sum(n for n, _ in parts)
            pv, r0 = [], 0
            for n, vb in parts:
                pv.append(_dot(p[r0:r0 + n * tq], _group(v_ref, rows, vb, width)))
                r0 += n * tq
            pv = pv[0] if len(pv) == 1 else jnp.concatenate(pv, axis=0)
            out += [m_new, acc * jnp.exp2(m - m_new) + pv]
        return tuple(out)

    init = []
    for q in qs:
        init += [jnp.full((q.shape[0], 1), NEG_INF, F32),
                 jnp.zeros((q.shape[0], width * LANES), F32)]
    carry = lax.fori_loop(0, n_chunks, body, tuple(init))
    half = width * HALF
    return [carry[2 * j + 1] / pltpu.roll(carry[2 * j + 1], half, 1) for j in range(len(segs))]


_PLANS = {
    "gqa": (1, ((((0, 1, 2, 3), 0, ((2, 0), (2, 1))),),)),
    "mla": (2, ((((0, 1, 2, 3), 0, ((2, 0), (2, 0))),),)),
    "diff": (1, ((((0, 1, 2, 3), 0, ((2, 0), (2, 1))),), (((4, 5, 6, 7), 1, ((2, 2), (2, 3))),))),
}


def dense_kernel(*refs, kind, n_chunks, lambda_init):
    refs = list(refs)
    if kind == "diff":
        lam_ref, g_ref = refs[:2]
        refs = refs[2:]
    elif kind == "mla":
        wv_ref = refs[0]
        refs = refs[1:]
    q_ref, k_ref, v_ref, o_ref = refs
    tq = q_ref.shape[0]
    width, loops = _PLANS[kind]
    res = [_attend(q_ref, segs, k_ref, v_ref, n_chunks, width) for segs in loops]

    def tiles(r):
        return [r[i * tq:(i + 1) * tq] for i in range(r.shape[0] // tq)]

    if kind == "gqa":
        h = [t for r in res[0] for t in tiles(r)]
        blocks = [_halves(h[0], pltpu.roll(h[1], HALF, 1), h[0].shape),
                  _halves(pltpu.roll(h[2], HALF, 1), h[3], h[0].shape)]
    elif kind == "mla":
        h = [r[:, :LANES].astype(BF16) for r in tiles(res[0][0])]
        blocks = [_dot(jnp.concatenate(h[2 * ob:2 * ob + 2], axis=1), wv_ref[ob]) for ob in range(2)]
    else:
        lv = lam_ref[...]
        lam = (jnp.exp(jnp.sum(lv[0:1] * lv[1:2], axis=-1, keepdims=True))
               - jnp.exp(jnp.sum(lv[2:3] * lv[3:4], axis=-1, keepdims=True)) + lambda_init)
        blocks = []
        for (r,) in res:
            a = tiles(r)
            o = _halves(a[0] - lam * a[1], a[2] - lam * a[3], a[0].shape)
            blocks.append(o * _head_rms(o) * g_ref[...] * (1.0 - lambda_init))
    for ob, blk in enumerate(blocks):
        o_ref[:, ob * LANES:(ob + 1) * LANES] = blk.astype(BF16)


def _qkv_specs(lay, latent, s_all):
    tq = TOK_TILE
    (q0, qn), (k0, kn), (v0, vn) = lay
    assert q0 % qn == 0 and k0 % kn == 0 and v0 % vn == 0
    q_off = 1 if latent else 0
    kv_rows = s_all if latent else CTX_TILE
    return [
        pl.BlockSpec((None, tq, qn * LANES), lambda i, t: (i, t + q_off, q0 // qn)),
        pl.BlockSpec((None, kv_rows, kn * LANES), lambda i, t: (i, 0, k0 // kn)),
        pl.BlockSpec((None, kv_rows, vn * LANES), lambda i, t: (i, 0, v0 // vn)),
    ]


def dense_attn_call(kind, qkv, latent, extra=None, lambda_init=None):
    b, s_all, _ = qkv.shape
    tq = TOK_TILE
    sq = s_all - CTX_TILE if latent else CTX_TILE
    lay = {"diff": ((G_QD, 8), (G_KD, 2), (G_VD, 4)),
           "mla": ((G_QM, 8), (G_KM, 2), (G_VM, 2)),
           "gqa": ((G_QG, 4), (G_KG, 1), (G_VG, 2))}[kind]
    specs = _qkv_specs(lay, latent, s_all)
    args = [qkv, qkv, qkv]
    if kind == "diff":
        specs = [pl.BlockSpec((4, DIFF_QK_DIM), lambda i, t: (0, 0)),
                 pl.BlockSpec((1, LANES), lambda i, t: (0, 0))] + specs
        args = list(extra) + args
    elif kind == "mla":
        specs = [pl.BlockSpec((2, 2 * LANES, LANES), lambda i, t: (0, 0, 0))] + specs
        args = list(extra) + args
    body = functools.partial(dense_kernel, kind=kind, n_chunks=KV_CHUNKS if latent else 1,
                             lambda_init=lambda_init)
    return pl.pallas_call(
        body,
        name=kind + ("_attn" if latent else "_attn_ctx"),
        grid=(b, sq // tq),
        in_specs=specs,
        out_specs=pl.BlockSpec((None, tq, MIX_COLS), lambda i, t: (i, t, 0)),
        out_shape=jax.ShapeDtypeStruct((b, sq, MIX_COLS), BF16),
        compiler_params=pltpu.CompilerParams(
            dimension_semantics=("arbitrary", "arbitrary"), vmem_limit_bytes=VMEM_LIMIT),
    )(*args)


def _mask_lanes(q, lo, hi):
    lane = _lane(q.shape)
    return jnp.where((lane >= lo) & (lane < hi), q, jnp.zeros_like(q))


def swa_kernel(sink_ref, q_ref, k_ref, v_ref, o_ref, *, s_lat):
    tq = q_ref.shape[0]
    ctx_rows = slice(0, CTX_TILE)
    t = pl.program_id(1)
    n_win = tq // WINDOW + 2
    plan = (((0, 0, HALF, 0, 0), (0, HALF, LANES, 1, 1)),
            ((1, 0, HALF, 1, 1), (1, HALF, LANES, 0, 0)))
    chunks = []
    if s_lat:
        qpos = t * tq + lax.broadcasted_iota(jnp.int32, (tq, WINDOW), 0)
        for c in range(n_win):
            start = t * tq + (c - 1) * WINDOW
            inside = (start >= 0) & (start < s_lat)
            rows = pl.ds(pl.multiple_of(CTX_TILE + jnp.clip(start, 0, s_lat - WINDOW), WINDOW),
                         WINDOW)
            kpos = start + lax.broadcasted_iota(jnp.int32, (tq, WINDOW), 1)
            chunks.append((rows, (jnp.abs(kpos - qpos) <= WINDOW) & inside))
    for ob, maps in enumerate(plan):
        res = []
        for hh, (qb, lo, hi, kb, vb) in enumerate(maps):
            qm = _mask_lanes(q_ref[:, qb * LANES:(qb + 1) * LANES], lo, hi)
            kcol = slice(kb * LANES, (kb + 1) * LANES)
            vcol = slice(vb * LANES, (vb + 1) * LANES)
            sink = sink_ref[2 * ob + hh]
            s_ctx = _dot_nt(qm, k_ref[ctx_rows, kcol])
            m = jnp.maximum(jnp.max(s_ctx, axis=-1, keepdims=True), sink)
            s_loc = []
            for rows, valid in chunks:
                s = jnp.where(valid, _dot_nt(qm, k_ref[rows, kcol]), NEG_INF)
                m = jnp.maximum(m, jnp.max(s, axis=-1, keepdims=True))
                s_loc.append(s)
            p = jnp.exp(s_ctx - m)
            l = jnp.sum(p, axis=-1, keepdims=True) + jnp.exp(sink - m)
            acc = _dot(p.astype(BF16), v_ref[ctx_rows, vcol])
            for (rows, _), s in zip(chunks, s_loc):
                p = jnp.exp(s - m)
                l = l + jnp.sum(p, axis=-1, keepdims=True)
                acc = acc + _dot(p.astype(BF16), v_ref[rows, vcol])
            res.append(acc / l)
        o_ref[:, ob * LANES:(ob + 1) * LANES] = _halves(res[0], res[1], res[0].shape).astype(BF16)


def swa_call(sink, qkv, latent):
    b, s_all, _ = qkv.shape
    tq = TOK_TILE
    s_lat = s_all - CTX_TILE if latent else 0
    sq = s_lat if latent else CTX_TILE
    specs = [pl.BlockSpec(memory_space=pltpu.SMEM)]
    specs += _qkv_specs(((G_QS, 2), (G_KS, 2), (G_VS, 2)), latent, s_all)
    return pl.pallas_call(
        functools.partial(swa_kernel, s_lat=s_lat),
        name="swa_attn" if latent else "swa_attn_ctx",
        grid=(b, sq // tq),
        in_specs=specs,
        out_specs=pl.BlockSpec((None, tq, MIX_COLS), lambda i, t: (i, t, 0)),
        out_shape=jax.ShapeDtypeStruct((b, sq, MIX_COLS), BF16),
        compiler_params=pltpu.CompilerParams(
            dimension_semantics=("arbitrary", "arbitrary"), vmem_limit_bytes=VMEM_LIMIT),
    )(sink, qkv, qkv, qkv)


def mixout_kernel(x_ref, ya_ref, yb_ref, ym_ref, yd_ref, w_ref, mod_ref, gpost_ref,
                  gfpre_ref, x1_ref, h2_ref, za_ref, zb_ref):
    g = pl.program_id(0)

    @pl.when(g == 0)
    def _():
        zb_ref[...] = jnp.zeros_like(zb_ref)

    def project(dst_ref):
        y = jnp.concatenate([ya_ref[...], yb_ref[...], ym_ref[...], yd_ref[...]], axis=1)
        dst_ref[...] = _dot(y, w_ref[...])

    def finish(z_ref):
        z = z_ref[...]
        x1 = x_ref[...] + mod_ref[2:3, :] * (z * _rms(z, D_MODEL) * gpost_ref[...])
        x1_ref[...] = x1
        h2 = x1 * _rms(x1, D_MODEL) * gfpre_ref[...]
        h2_ref[...] = (h2 * (1.0 + mod_ref[4:5, :]) + mod_ref[3:4, :]).astype(BF16)

    @pl.when(g % 2 == 0)
    def _():
        project(za_ref)
        finish(zb_ref)

    @pl.when(g % 2 == 1)
    def _():
        project(zb_ref)
        finish(za_ref)


def mixout_call(x, ys, mod, mod_row, lw):
    b, s, d = x.shape
    tq = TOK_TILE
    per = s // tq
    n = b * per

    def proj_tile(g):
        ga = jnp.minimum(g, n - 1)
        return ga // per, ga % per, 0

    def done_tile(g):
        gb = jnp.maximum(g - 1, 0)
        return gb // per, gb % per, 0

    if mod_row is None:
        mod_map = lambda g: (done_tile(g)[0], 0, 0)
    else:
        mod_map = lambda g: (mod_row, 0, 0)
    const2 = lambda g: (0, 0)
    y_spec = pl.BlockSpec((None, tq, MIX_COLS), proj_tile)
    return pl.pallas_call(
        mixout_kernel,
        name="mixout",
        grid=(n + 1,),
        in_specs=[pl.BlockSpec((None, tq, d), done_tile), y_spec, y_spec, y_spec, y_spec,
                  pl.BlockSpec((4 * MIX_COLS, d), const2),
                  pl.BlockSpec((None, N_MOD, d), mod_map),
                  pl.BlockSpec((1, d), const2), pl.BlockSpec((1, d), const2)],
        out_specs=[pl.BlockSpec((None, tq, d), done_tile), pl.BlockSpec((None, tq, d), done_tile)],
        out_shape=[jax.ShapeDtypeStruct((b, s, d), F32), jax.ShapeDtypeStruct((b, s, d), BF16)],
        scratch_shapes=[pltpu.VMEM((tq, d), F32), pltpu.VMEM((tq, d), F32)],
        compiler_params=pltpu.CompilerParams(
            dimension_semantics=("arbitrary",), vmem_limit_bytes=VMEM_LIMIT),
    )(x, *ys, lw["w_out"], mod, lw["g_mix_post"], lw["g_ffn_pre"])


def ffn_kernel(x1_ref, hc_ref, hp_ref, hn_ref, wup_ref, cw_ref, cb_ref, wdn_ref, mod_ref,
               gpost_ref, o_ref, g_ref):
    t = pl.program_id(1)
    tq = hc_ref.shape[0]
    halo = hp_ref.shape[0]
    hp = jnp.where(t > 0, hp_ref[...], jnp.zeros_like(hp_ref))
    hn = jnp.where(t < pl.num_programs(1) - 1, hn_ref[...], jnp.zeros_like(hn_ref))
    hext = jnp.concatenate([hp, hc_ref[...], hn], axis=0)
    rows = hext.shape[0]

    for j in range(wup_ref.shape[0]):
        u = _dot(hext, wup_ref[j])
        w = cw_ref[j]
        conv = (pltpu.roll(u, 1, 0)[halo:halo + tq] * w[0:1]
                + u[halo:halo + tq] * w[1:2]
                + pltpu.roll(u, rows - 1, 0)[halo:halo + tq] * w[2:3] + cb_ref[j])
        a = conv[:, :FFN_CHUNK]
        gated = a * (1.0 / (1.0 + jnp.exp(-a))) * conv[:, FFN_CHUNK:]
        g_ref[:, j * FFN_CHUNK:(j + 1) * FFN_CHUNK] = gated.astype(BF16)

    f = _dot(g_ref[...], wdn_ref[...])
    o_ref[...] = x1_ref[...] + mod_ref[5:6, :] * (f * _rms(f, D_MODEL) * gpost_ref[...])


def ffn_call(x1, h2, mod, mod_row, lw):
    b, s, d = x1.shape
    tq = min(FFN_TILE, s)
    halo = BF16_SUBLANES
    per_tile = tq // halo
    last = s // halo - 1
    n_ch = FFN_DIM // FFN_CHUNK
    if mod_row is None:
        mod_map = lambda i, t: (i, 0, 0)
    else:
        mod_map = lambda i, t: (mod_row, 0, 0)
    const2 = lambda i, t: (0, 0)
    const3 = lambda i, t: (0, 0, 0)
    tok = lambda i, t: (i, t, 0)
    return pl.pallas_call(
        ffn_kernel,
        name="conv_ffn",
        grid=(b, s // tq),
        in_specs=[
            pl.BlockSpec((None, tq, d), tok),
            pl.BlockSpec((None, tq, d), tok),
            pl.BlockSpec((None, halo, d), lambda i, t: (i, jnp.maximum(t * per_tile - 1, 0), 0)),
            pl.BlockSpec((None, halo, d), lambda i, t: (i, jnp.minimum((t + 1) * per_tile, last), 0)),
            pl.BlockSpec((n_ch, d, 2 * FFN_CHUNK), const3),
            pl.BlockSpec((n_ch, 3, 2 * FFN_CHUNK), const3),
            pl.BlockSpec((n_ch, 1, 2 * FFN_CHUNK), const3),
            pl.BlockSpec((FFN_DIM, d), const2),
            pl.BlockSpec((None, N_MOD, d), mod_map),
            pl.BlockSpec((1, d), const2),
        ],
        out_specs=pl.BlockSpec((None, tq, d), tok),
        out_shape=jax.ShapeDtypeStruct((b, s, d), F32),
        scratch_shapes=[pltpu.VMEM((tq, FFN_DIM), BF16)],
        compiler_params=pltpu.CompilerParams(
            dimension_semantics=("arbitrary", "arbitrary"), vmem_limit_bytes=VMEM_LIMIT),
    )(x1, h2, h2, h2, lw["w_up"], lw["conv_w"], lw["conv_b"], lw["w_down"], mod,
      lw["g_ffn_post"])


def _rope_table(n_ctx, s, rot_dim, lane_lo, lane_hi, period):
    n = rot_dim // 4
    tok = np.arange(s)
    pos = np.stack([tok // GRID_W, tok % GRID_W], axis=1).astype(np.float32)
    lane = np.arange(LANES)
    o = (lane - lane_lo) % period
    axis = o // (2 * n)
    second = (o % (2 * n)) // n
    active = (lane >= lane_lo) & (lane < lane_hi)
    inv_freq = ROPE_THETA ** (-jnp.arange(n, dtype=F32) / n)
    ang = jnp.asarray(pos)[:, axis] * inv_freq[o % n][None, :]
    cos = jnp.where(active[None], jnp.cos(ang), 1.0)
    sin = jnp.where(active[None], jnp.sin(ang), 0.0)
    sa = jnp.where((second == 0)[None], -sin, 0.0)
    sb = jnp.where((second == 1)[None], sin, 0.0)
    tab = jnp.stack([cos, sa, sb]).astype(F32)
    ident = jnp.stack([jnp.ones((n_ctx, LANES), F32), jnp.zeros((n_ctx, LANES), F32),
                       jnp.zeros((n_ctx, LANES), F32)])
    return jnp.concatenate([ident, tab], axis=1)


def _layer_weights(l, w):
    d = D_MODEL
    w_in = w["w_in"][l]
    mla0 = 768 + 512
    cq_hi = mla0 + MLA_Q_RANK
    kr0 = cq_hi + MLA_KV_RANK
    w_in_p = jnp.concatenate([
        w_in[:, :mla0 + LANES],
        w_in[:, mla0 + LANES:cq_hi], w_in[:, kr0:kr0 + MLA_ROPE_DIM],
        jnp.zeros((d, LANES - HALF - MLA_ROPE_DIM), F32),
        w_in[:, cq_hi:kr0],
        w_in[:, kr0 + MLA_ROPE_DIM:],
    ], axis=1)
    dqk = MLA_NOPE_DIM + MLA_ROPE_DIM
    wq = jnp.zeros((2 * LANES, MLA_HEADS, LANES), F32)
    wq = wq.at[:MLA_Q_RANK, :, :dqk].set(w["mla_w_uq"][l].reshape(MLA_Q_RANK, MLA_HEADS, dqk))
    assert MLA_KV_RANK == LANES
    wukv = w["mla_w_ukv"][l].reshape(MLA_KV_RANK, MLA_HEADS, MLA_NOPE_DIM + MLA_V_DIM)
    wkt = jnp.zeros((MLA_HEADS, LANES, LANES), F32)
    wkt = wkt.at[:, :MLA_NOPE_DIM, :].set(jnp.transpose(wukv[:, :, :MLA_NOPE_DIM], (1, 2, 0)))
    wuv = jnp.transpose(wukv[:, :, MLA_NOPE_DIM:], (1, 0, 2))
    wv = jnp.zeros((2, 2, LANES, 2, MLA_V_DIM), F32)
    for hd in range(MLA_HEADS):
        wv = wv.at[hd // 2, hd % 2, :, hd % 2, :].set(wuv[hd])
    wv = wv.reshape(2, 2 * LANES, LANES)
    gq = jnp.zeros((1, 2 * LANES), F32).at[0, :MLA_Q_RANK].set(w["mla_g_q"][l])
    n_ch = FFN_DIM // FFN_CHUNK

    def chunked(a):
        lead = a.shape[:-1]
        a = a.reshape(lead + (2, n_ch, FFN_CHUNK))
        a = jnp.moveaxis(a, -2, 0)
        return a.reshape((n_ch,) + lead + (2 * FFN_CHUNK,))

    return {
        "g_mix_pre": w["g_mix_pre"][l][None], "g_mix_post": w["g_mix_post"][l][None],
        "g_ffn_pre": w["g_ffn_pre"][l][None], "g_ffn_post": w["g_ffn_post"][l][None],
        "w_in": w_in_p.astype(BF16),
        "wq": wq.reshape(2 * LANES, -1).astype(BF16), "wkt": wkt.astype(BF16),
        "wv": wv.astype(BF16),
        "gq": gq, "gkv": w["mla_g_kv"][l][None],
        "gnq": jnp.tile(w["gqa_g_q"][l], 2)[None], "gnk": jnp.tile(w["gqa_g_k"][l], 2)[None],
        "lam": jnp.stack([w["diff_lam_q1"][l], w["diff_lam_k1"][l],
                          w["diff_lam_q2"][l], w["diff_lam_k2"][l]]),
        "g_sub": jnp.tile(w["diff_g_sub"][l], 2)[None],
        "sink": w["swa_sink"][l],
        "w_out": w["w_out"][l].astype(BF16),
        "w_up": chunked(w["ffn_w_up"][l]).astype(BF16),
        "conv_w": chunked(w["ffn_conv_w"][l]),
        "conv_b": chunked(w["ffn_conv_b"][l][None]),
        "w_down": w["ffn_w_down"][l].astype(BF16),
    }


def kernel(x, c, ctx, c_ctx, w_mod, b_mod, g_mix_pre, g_mix_post, g_ffn_pre, g_ffn_post, w_in, diff_lam_q1, diff_lam_k1, diff_lam_q2, diff_lam_k2, diff_g_sub, swa_sink, mla_g_q, mla_g_kv, mla_w_uq, mla_w_ukv, gqa_g_q, gqa_g_k, w_out, ffn_w_up, ffn_conv_w, ffn_conv_b, ffn_w_down):
    w = dict(g_mix_pre=g_mix_pre, g_mix_post=g_mix_post, g_ffn_pre=g_ffn_pre,
             g_ffn_post=g_ffn_post, w_in=w_in, diff_lam_q1=diff_lam_q1, diff_lam_k1=diff_lam_k1,
             diff_lam_q2=diff_lam_q2, diff_lam_k2=diff_lam_k2, diff_g_sub=diff_g_sub,
             swa_sink=swa_sink, mla_g_q=mla_g_q, mla_g_kv=mla_g_kv, mla_w_uq=mla_w_uq,
             mla_w_ukv=mla_w_ukv, gqa_g_q=gqa_g_q, gqa_g_k=gqa_g_k, w_out=w_out,
             ffn_w_up=ffn_w_up, ffn_conv_w=ffn_conv_w, ffn_conv_b=ffn_conv_b,
             ffn_w_down=ffn_w_down)
    b, s, d = x.shape
    depth = w_mod.shape[0]
    ctx_row = b
    mod_rows = 16
    cc = jnp.zeros((mod_rows, d), F32).at[:b].set(c).at[ctx_row].set(c_ctx)
    mod_all = mod_call(cc, w_mod, b_mod).reshape(depth, mod_rows, N_MOD, d)

    n_ctx = ctx.shape[1]
    tabs = (_rope_table(n_ctx, s, DIFF_QK_DIM, 0, LANES, DIFF_QK_DIM),
            _rope_table(n_ctx, s, GQA_HEAD_DIM, 0, LANES, GQA_HEAD_DIM),
            _rope_table(n_ctx, s, MLA_ROPE_DIM, MLA_NOPE_DIM, MLA_NOPE_DIM + MLA_ROPE_DIM,
                        MLA_ROPE_DIM))

    xc = ctx
    for l in range(depth):
        need_ctx = l < depth - 1
        lambda_init = 0.8 - 0.6 * math.exp(-0.3 * l)
        lw = _layer_weights(l, w)
        mod = mod_all[l]
        qkv = proj_call(xc, x, mod, ctx_row, lw, tabs)
        diff_extra = (lw["lam"], lw["g_sub"])
        mla_extra = (lw["wv"],)
        ys = (dense_attn_call("diff", qkv, True, diff_extra, lambda_init),
              swa_call(lw["sink"], qkv, True),
              dense_attn_call("mla", qkv, True, mla_extra),
              dense_attn_call("gqa", qkv, True))
        x1, h2 = mixout_call(x, ys, mod, None, lw)
        x = ffn_call(x1, h2, mod, None, lw)
        if need_ctx:
            ysc = (dense_attn_call("diff", qkv, False, diff_extra, lambda_init),
                   swa_call(lw["sink"], qkv, False),
                   dense_attn_call("mla", qkv, False, mla_extra),
                   dense_attn_call("gqa", qkv, False))
            xc1, hc2 = mixout_call(xc, ysc, mod, ctx_row, lw)
            xc = ffn_call(xc1, hc2, mod, ctx_row, lw)
    return x
```

```python
import functools
import math

import jax
import jax.numpy as jnp
import numpy as np
from jax import lax
from jax.experimental import pallas as pl
from jax.experimental.pallas import tpu as pltpu

F32 = jnp.float32
BF16 = jnp.bfloat16

D_MODEL = 1024
GRID_W = 64
ROPE_THETA = 10000.0
NORM_EPS = 1e-6
NEG_INF = -1e30
N_MOD = 6

DIFF_HEADS, DIFF_QK_DIM, DIFF_V_DIM = 4, 32, 64
SWA_HEAD_DIM, WINDOW = 64, 128
MLA_Q_RANK, MLA_KV_RANK = 192, 128
MLA_NOPE_DIM, MLA_ROPE_DIM, MLA_V_DIM, MLA_HEADS = 64, 32, 64, 4
GQA_HEAD_DIM = 64
FFN_DIM = 2816
MIX_COLS = 256

LANES = 128
BF16_SUBLANES = 16
HALF = LANES // 2
VMEM_LIMIT = 56 * 1024 * 1024

P_COLS = 17 * LANES
G_QD, G_QM, G_VD, G_QG = 0, 8, 16, 20
G_KD, G_QS, G_KS, G_VS, G_VG, G_KM, G_VM, G_KG = 24, 26, 28, 30, 32, 34, 36, 38
QKV_COLS = 39 * LANES
LOG2E = math.log2(math.e)

TOK_TILE = 256
CTX_TILE = 256
ATT_TILE = 512
KV_CHUNKS = 2
FFN_TILE = 512
FFN_CHUNK = 256


def _rms(x, n):
    return lax.rsqrt(jnp.sum(x * x, axis=-1, keepdims=True) * (1.0 / n) + NORM_EPS)


def _lane(shape):
    return lax.broadcasted_iota(jnp.int32, shape, len(shape) - 1)


def _dot(a, b):
    return jnp.dot(a, b, preferred_element_type=F32)


def _dot_nt(a, b):
    return lax.dot_general(a, b, (((1,), (1,)), ((), ())), preferred_element_type=F32)


def mod_kernel(c_ref, w_ref, b_ref, o_ref):
    c = c_ref[...]
    s = c * (1.0 / (1.0 + jnp.exp(-c)))
    o_ref[...] = jnp.dot(s, w_ref[...], preferred_element_type=F32,
                         precision=lax.Precision.HIGHEST) + b_ref[...]


def mod_call(cc, w_mod, b_mod):
    depth, d, n = w_mod.shape
    rows = cc.shape[0]
    tn = 1024
    return pl.pallas_call(
        mod_kernel,
        name="mod_vectors",
        grid=(depth, n // tn),
        in_specs=[
            pl.BlockSpec((rows, d), lambda l, j: (0, 0)),
            pl.BlockSpec((None, d, tn), lambda l, j: (l, 0, j)),
            pl.BlockSpec((None, 1, tn), lambda l, j: (l, 0, j)),
        ],
        out_specs=pl.BlockSpec((None, rows, tn), lambda l, j: (l, 0, j)),
        out_shape=jax.ShapeDtypeStruct((depth, rows, n), F32),
        compiler_params=pltpu.CompilerParams(
            dimension_semantics=("arbitrary", "arbitrary"), vmem_limit_bytes=VMEM_LIMIT),
    )(cc, w_mod, b_mod.reshape(depth, 1, n))


def _rope(x, tab_ref, n):
    return (x * tab_ref[0] + pltpu.roll(x, LANES - n, 1) * tab_ref[1]
            + pltpu.roll(x, n, 1) * tab_ref[2])


def _halves(lo, hi, shape):
    return jnp.where(_lane(shape) < HALF, lo, hi)


def _head_rms(x):
    low = _lane(x.shape) < HALF
    sq = x * x
    ss_lo = jnp.sum(jnp.where(low, sq, 0.0), axis=-1, keepdims=True)
    ss_hi = jnp.sum(jnp.where(low, 0.0, sq), axis=-1, keepdims=True)
    r_lo = lax.rsqrt(ss_lo * (1.0 / HALF) + NORM_EPS)
    r_hi = lax.rsqrt(ss_hi * (1.0 / HALF) + NORM_EPS)
    return jnp.where(low, r_lo, r_hi)


def proj_kernel(xc_ref, x_ref, mod_ref, gpre_ref, win_ref, wq_ref, wkt_ref, gq_ref, gkv_ref,
                gnq_ref, gnk_ref, t32_ref, t64_ref, tml_ref, o_ref, pa_ref, pb_ref, *,
                tiles_per_sample, n_tiles):
    g = pl.program_id(0)
    is_ctx = jnp.minimum(g, n_tiles - 1) % tiles_per_sample == tiles_per_sample - 1

    @pl.when(g == 0)
    def _():
        pb_ref[...] = jnp.zeros_like(pb_ref)

    def project(dst_ref):
        x = jnp.where(is_ctx, xc_ref[...], x_ref[...])
        h = x * _rms(x, D_MODEL) * gpre_ref[...]
        h = h * (1.0 + mod_ref[1:2, :]) + mod_ref[0:1, :]
        dst_ref[...] = _dot(h.astype(BF16), win_ref[...])

    finish = functools.partial(_proj_finish, wq_ref, wkt_ref, gq_ref, gkv_ref, gnq_ref, gnk_ref,
                               t32_ref, t64_ref, tml_ref, o_ref)

    @pl.when(g % 2 == 0)
    def _():
        project(pa_ref)
        finish(pb_ref)

    @pl.when(g % 2 == 1)
    def _():
        project(pb_ref)
        finish(pa_ref)


def _proj_finish(wq_ref, wkt_ref, gq_ref, gkv_ref, gnq_ref, gnk_ref, t32_ref, t64_ref, tml_ref,
                 o_ref, p_ref):
    def grp(g):
        return p_ref[:, g * LANES:(g + 1) * LANES]

    def put(g, v):
        o_ref[:, g * LANES:(g + 1) * LANES] = v.astype(BF16)

    def rope32(v):
        return _rope(v, t32_ref, DIFF_QK_DIM // 4)

    def rope64(v):
        return _rope(v, t64_ref, GQA_HEAD_DIM // 4)

    def ropeml(v):
        return _rope(v, tml_ref, MLA_ROPE_DIM // 4)

    lane = _lane((p_ref.shape[0], LANES))
    low = lane < HALF

    def with_ones(v):
        return jnp.where(low, v, 1.0), jnp.where(low, 1.0, v)

    for g in range(2):
        qd = rope32(grp(g)) * (DIFF_QK_DIM ** -0.5 * LOG2E)
        for i in range(4):
            keep = (lane >= i * DIFF_QK_DIM) & (lane < (i + 1) * DIFF_QK_DIM)
            put(G_QD + 4 * g + i, jnp.where(keep, qd, 0.0))
        put(G_KD + g, rope32(grp(2 + g)))
        v_lo, v_hi = with_ones(grp(4 + g))
        put(G_VD + 2 * g, v_lo)
        put(G_VD + 2 * g + 1, v_hi)

    for g in range(2):
        put(G_QS + g, rope64(grp(6 + g)) * (SWA_HEAD_DIM ** -0.5))
    ks = rope64(grp(8))
    put(G_KS, ks)
    put(G_KS + 1, pltpu.roll(ks, HALF, 1))
    put(G_VS, grp(9))
    put(G_VS + 1, pltpu.roll(grp(9), HALF, 1))

    cq0, cq1, ckv = grp(10), grp(11), grp(12)
    ss = (jnp.sum(cq0 * cq0, axis=-1, keepdims=True)
          + jnp.sum(jnp.where(low, cq1 * cq1, 0.0), axis=-1, keepdims=True))
    rq = lax.rsqrt(ss * (1.0 / MLA_Q_RANK) + NORM_EPS)
    cqn = jnp.concatenate([cq0 * rq * gq_ref[:, :LANES], cq1 * rq * gq_ref[:, LANES:]], axis=1)
    qm = _dot(cqn.astype(BF16), wq_ref[...])
    scale_m = (MLA_NOPE_DIM + MLA_ROPE_DIM) ** -0.5 * LOG2E
    rope_lanes = (lane >= MLA_NOPE_DIM) & (lane < MLA_NOPE_DIM + MLA_ROPE_DIM)
    for hd in range(MLA_HEADS):
        qh = ropeml(qm[:, hd * LANES:(hd + 1) * LANES]) * scale_m
        put(G_QM + 2 * hd, _dot(qh.astype(BF16), wkt_ref[hd]))
        put(G_QM + 2 * hd + 1, jnp.where(rope_lanes, qh, 0.0))
    ckvn = ckv * _rms(ckv, MLA_KV_RANK) * gkv_ref[...]
    put(G_KM, ckvn)
    put(G_KM + 1, jnp.where(rope_lanes, ropeml(cq1), 0.0))
    put(G_VM, ckvn)
    put(G_VM + 1, jnp.ones_like(ckvn))

    scale_g = GQA_HEAD_DIM ** -0.5 * LOG2E
    q01 = grp(13)
    q01 = rope64(q01 * _head_rms(q01) * gnq_ref[...]) * scale_g
    q23 = grp(14)
    q23 = rope64(q23 * _head_rms(q23) * gnq_ref[...]) * scale_g
    put(G_QG, jnp.where(low, q01, 0.0))
    put(G_QG + 1, jnp.where(low, pltpu.roll(q01, HALF, 1), 0.0))
    put(G_QG + 2, jnp.where(low, 0.0, pltpu.roll(q23, HALF, 1)))
    put(G_QG + 3, jnp.where(low, 0.0, q23))
    kg = grp(15)
    put(G_KG, rope64(kg * _head_rms(kg) * gnk_ref[...]))
    v_lo, v_hi = with_ones(grp(16))
    put(G_VG, v_lo)
    put(G_VG + 1, v_hi)


def proj_call(xc, x, mod, ctx_row, lw, tabs):
    b, s, d = x.shape
    tq = TOK_TILE
    assert xc.shape[1] == CTX_TILE == tq
    s_all = CTX_TILE + s
    per = s_all // tq
    n = b * per
    const2 = lambda g: (0, 0)

    def proj_tile(g):
        ga = jnp.minimum(g, n - 1)
        return ga // per, ga % per

    def done_tile(g):
        gb = jnp.maximum(g - 1, 0)
        return gb // per, gb % per

    def x_map(g):
        i, t = proj_tile(g)
        return i, jnp.minimum(t, per - 2), 0

    def mod_map(g):
        i, t = proj_tile(g)
        return jnp.where(t == per - 1, ctx_row, i), 0, 0

    tab_spec = pl.BlockSpec((3, tq, LANES), lambda g: (0, done_tile(g)[1], 0))
    return pl.pallas_call(
        functools.partial(proj_kernel, tiles_per_sample=per, n_tiles=n),
        name="proj",
        grid=(n + 1,),
        in_specs=[
            pl.BlockSpec((None, tq, d), lambda g: (proj_tile(g)[0], 0, 0)),
            pl.BlockSpec((None, tq, d), x_map),
            pl.BlockSpec((None, N_MOD, d), mod_map),
            pl.BlockSpec((1, d), const2),
            pl.BlockSpec((d, P_COLS), const2),
            pl.BlockSpec((2 * LANES, 4 * LANES), const2),
            pl.BlockSpec((MLA_HEADS, LANES, LANES), lambda g: (0, 0, 0)),
            pl.BlockSpec((1, 2 * LANES), const2),
            pl.BlockSpec((1, LANES), const2),
            pl.BlockSpec((1, LANES), const2),
            pl.BlockSpec((1, LANES), const2),
            tab_spec, tab_spec, tab_spec,
        ],
        out_specs=pl.BlockSpec((None, tq, QKV_COLS), lambda g: done_tile(g) + (0,)),
        out_shape=jax.ShapeDtypeStruct((b, s_all, QKV_COLS), BF16),
        scratch_shapes=[pltpu.VMEM((tq, P_COLS), F32), pltpu.VMEM((tq, P_COLS), F32)],
        compiler_params=pltpu.CompilerParams(
            dimension_semantics=("arbitrary",), vmem_limit_bytes=VMEM_LIMIT),
    )(xc, x, mod, lw["g_mix_pre"], lw["w_in"], lw["wq"], lw["wkt"], lw["gq"], lw["gkv"],
      lw["gnq"], lw["gnk"], tabs[0], tabs[1], tabs[2])


def _group(ref, rows, g, width=1):
    return ref[rows, g * width * LANES:(g + 1) * width * LANES]


def _attend(q_ref, segs, k_ref, v_ref, n_chunks, width):
    chunk = k_ref.shape[0] // n_chunks
    qs = [jnp.concatenate([_group(q_ref, slice(None), g, width) for g in qg], axis=0)
          for qg, _, _ in segs]

    def body(i, carry):
        rows = pl.ds(pl.multiple_of(i * chunk, LANES), chunk)
        out = []
        for j, (q, (_, kb, parts)) in enumerate(zip(qs, segs)):
            m, acc = carry[2 * j], carry[2 * j + 1]
            s = _dot_nt(q, _group(k_ref, rows, kb, width))
            m_new = jnp.maximum(m, jnp.max(s, axis=-1, keepdims=True))
            p = jnp.exp2(s - m_new).astype(BF16)
            tq = q.shape[0] // sum(n for n, _ in parts)
            pv, r0 = [], 0
            for n, vb in parts:
                pv.append(_dot(p[r0:r0 + n * tq], _group(v_ref, rows, vb, width)))
                r0 += n * tq
            pv = pv[0] if len(pv) == 1 else jnp.concatenate(pv, axis=0)
            out += [m_new, acc * jnp.exp2(m - m_new) + pv]
        return tuple(out)

    init = []
    for q in qs:
        init += [jnp.full((q.shape[0], 1), NEG_INF, F32),
                 jnp.zeros((q.shape[0], width * LANES), F32)]
    carry = lax.fori_loop(0, n_chunks, body, tuple(init))
    half = width * HALF
    return [carry[2 * j + 1] / pltpu.roll(carry[2 * j + 1], half, 1) for j in range(len(segs))]


_PLANS = {
    "gqa": (1, ((((0, 1, 2, 3), 0, ((2, 0), (2, 1))),),)),
    "mla": (2, ((((0, 1, 2, 3), 0, ((2, 0), (2, 0))),),)),
    "diff": (1, ((((0, 1, 2, 3), 0, ((2, 0), (2, 1))),), (((4, 5, 6, 7), 1, ((2, 2), (2, 3))),))),
}


def dense_kernel(*refs, kind, n_chunks, lambda_init):
    refs = list(refs)
    if kind == "diff":
        lam_ref, g_ref = refs[:2]
        refs = refs[2:]
    elif kind == "mla":
        wv_ref = refs[0]
        refs = refs[1:]
    q_ref, k_ref, v_ref, o_ref = refs
    tq = q_ref.shape[0]
    width, loops = _PLANS[kind]
    res = [_attend(q_ref, segs, k_ref, v_ref, n_chunks, width) for segs in loops]

    def tiles(r):
        return [r[i * tq:(i + 1) * tq] for i in range(r.shape[0] // tq)]

    if kind == "gqa":
        h = [t for r in res[0] for t in tiles(r)]
        blocks = [_halves(h[0], pltpu.roll(h[1], HALF, 1), h[0].shape),
                  _halves(pltpu.roll(h[2], HALF, 1), h[3], h[0].shape)]
    elif kind == "mla":
        h = [r[:, :LANES].astype(BF16) for r in tiles(res[0][0])]
        blocks = [_dot(jnp.concatenate(h[2 * ob:2 * ob + 2], axis=1), wv_ref[ob]) for ob in range(2)]
    else:
        lv = lam_ref[...]
        lam = (jnp.exp(jnp.sum(lv[0:1] * lv[1:2], axis=-1, keepdims=True))
               - jnp.exp(jnp.sum(lv[2:3] * lv[3:4], axis=-1, keepdims=True)) + lambda_init)
        blocks = []
        for (r,) in res:
            a = tiles(r)
            o = _halves(a[0] - lam * a[1], a[2] - lam * a[3], a[0].shape)
            blocks.append(o * _head_rms(o) * g_ref[...] * (1.0 - lambda_init))
    for ob, blk in enumerate(blocks):
        o_ref[:, ob * LANES:(ob + 1) * LANES] = blk.astype(BF16)


def _qkv_specs(lay, latent, s_all, tq):
    (q0, qn), (k0, kn), (v0, vn) = lay
    assert q0 % qn == 0 and k0 % kn == 0 and v0 % vn == 0
    ctx_blk = s_all // CTX_TILE - 1
    if latent:
        q_map = lambda i, t: (i, t, q0 // qn)
        k_map = lambda i, t: (i, 0, k0 // kn)
        v_map = lambda i, t: (i, 0, v0 // vn)
    else:
        q_map = lambda i, t: (i, ctx_blk, q0 // qn)
        k_map = lambda i, t: (i, ctx_blk, k0 // kn)
        v_map = lambda i, t: (i, ctx_blk, v0 // vn)
    kv_rows = s_all if latent else CTX_TILE
    return [
        pl.BlockSpec((None, tq, qn * LANES), q_map),
        pl.BlockSpec((None, kv_rows, kn * LANES), k_map),
        pl.BlockSpec((None, kv_rows, vn * LANES), v_map),
    ]


def dense_attn_call(kind, qkv, latent, extra=None, lambda_init=None):
    b, s_all, _ = qkv.shape
    tq = ATT_TILE if latent else CTX_TILE
    sq = s_all - CTX_TILE if latent else CTX_TILE
    lay = {"diff": ((G_QD, 8), (G_KD, 2), (G_VD, 4)),
           "mla": ((G_QM, 8), (G_KM, 2), (G_VM, 2)),
           "gqa": ((G_QG, 4), (G_KG, 1), (G_VG, 2))}[kind]
    specs = _qkv_specs(lay, latent, s_all, tq)
    args = [qkv, qkv, qkv]
    if kind == "diff":
        specs = [pl.BlockSpec((4, DIFF_QK_DIM), lambda i, t: (0, 0)),
                 pl.BlockSpec((1, LANES), lambda i, t: (0, 0))] + specs
        args = list(extra) + args
    elif kind == "mla":
        specs = [pl.BlockSpec((2, 2 * LANES, LANES), lambda i, t: (0, 0, 0))] + specs
        args = list(extra) + args
    body = functools.partial(dense_kernel, kind=kind, n_chunks=KV_CHUNKS if latent else 1,
                             lambda_init=lambda_init)
    return pl.pallas_call(
        body,
        name=kind + ("_attn" if latent else "_attn_ctx"),
        grid=(b, sq // tq),
        in_specs=specs,
        out_specs=pl.BlockSpec((None, tq, MIX_COLS), lambda i, t: (i, t, 0)),
        out_shape=jax.ShapeDtypeStruct((b, sq, MIX_COLS), BF16),
        compiler_params=pltpu.CompilerParams(
            dimension_semantics=("arbitrary", "arbitrary"), vmem_limit_bytes=VMEM_LIMIT),
    )(*args)


def _mask_lanes(q, lo, hi):
    lane = _lane(q.shape)
    return jnp.where((lane >= lo) & (lane < hi), q, jnp.zeros_like(q))


def swa_kernel(sink_ref, q_ref, k_ref, v_ref, o_ref, *, s_lat):
    tq = q_ref.shape[0]
    ctx_rows = slice(s_lat, s_lat + CTX_TILE)
    t = pl.program_id(1)
    n_win = tq // WINDOW + 2
    plan = (((0, 0, HALF, 0, 0), (0, HALF, LANES, 1, 1)),
            ((1, 0, HALF, 1, 1), (1, HALF, LANES, 0, 0)))
    chunks = []
    if s_lat:
        qpos = t * tq + lax.broadcasted_iota(jnp.int32, (tq, WINDOW), 0)
        for c in range(n_win):
            start = t * tq + (c - 1) * WINDOW
            inside = (start >= 0) & (start < s_lat)
            rows = pl.ds(pl.multiple_of(jnp.clip(start, 0, s_lat - WINDOW), WINDOW), WINDOW)
            kpos = start + lax.broadcasted_iota(jnp.int32, (tq, WINDOW), 1)
            chunks.append((rows, (jnp.abs(kpos - qpos) <= WINDOW) & inside))
    for ob, maps in enumerate(plan):
        res = []
        for hh, (qb, lo, hi, kb, vb) in enumerate(maps):
            qm = _mask_lanes(q_ref[:, qb * LANES:(qb + 1) * LANES], lo, hi)
            kcol = slice(kb * LANES, (kb + 1) * LANES)
            vcol = slice(vb * LANES, (vb + 1) * LANES)
            sink = sink_ref[2 * ob + hh]
            s_ctx = _dot_nt(qm, k_ref[ctx_rows, kcol])
            m = jnp.maximum(jnp.max(s_ctx, axis=-1, keepdims=True), sink)
            s_loc = []
            for rows, valid in chunks:
                s = jnp.where(valid, _dot_nt(qm, k_ref[rows, kcol]), NEG_INF)
                m = jnp.maximum(m, jnp.max(s, axis=-1, keepdims=True))
                s_loc.append(s)
            p = jnp.exp(s_ctx - m)
            l = jnp.sum(p, axis=-1, keepdims=True) + jnp.exp(sink - m)
            acc = _dot(p.astype(BF16), v_ref[ctx_rows, vcol])
            for (rows, _), s in zip(chunks, s_loc):
                p = jnp.exp(s - m)
                l = l + jnp.sum(p, axis=-1, keepdims=True)
                acc = acc + _dot(p.astype(BF16), v_ref[rows, vcol])
            res.append(acc / l)
        o_ref[:, ob * LANES:(ob + 1) * LANES] = _halves(res[0], res[1], res[0].shape).astype(BF16)


def swa_call(sink, qkv, latent):
    b, s_all, _ = qkv.shape
    tq = TOK_TILE
    s_lat = s_all - CTX_TILE if latent else 0
    sq = s_lat if latent else CTX_TILE
    specs = [pl.BlockSpec(memory_space=pltpu.SMEM)]
    specs += _qkv_specs(((G_QS, 2), (G_KS, 2), (G_VS, 2)), latent, s_all, tq)
    return pl.pallas_call(
        functools.partial(swa_kernel, s_lat=s_lat),
        name="swa_attn" if latent else "swa_attn_ctx",
        grid=(b, sq // tq),
        in_specs=specs,
        out_specs=pl.BlockSpec((None, tq, MIX_COLS), lambda i, t: (i, t, 0)),
        out_shape=jax.ShapeDtypeStruct((b, sq, MIX_COLS), BF16),
        compiler_params=pltpu.CompilerParams(
            dimension_semantics=("arbitrary", "arbitrary"), vmem_limit_bytes=VMEM_LIMIT),
    )(sink, qkv, qkv, qkv)


def mixout_kernel(x_ref, ya_ref, yb_ref, ym_ref, yd_ref, w_ref, mod_ref, gpost_ref,
                  gfpre_ref, x1_ref, h2_ref, za_ref, zb_ref):
    g = pl.program_id(0)

    @pl.when(g == 0)
    def _():
        zb_ref[...] = jnp.zeros_like(zb_ref)

    def project(dst_ref):
        y = jnp.concatenate([ya_ref[...], yb_ref[...], ym_ref[...], yd_ref[...]], axis=1)
        dst_ref[...] = _dot(y, w_ref[...])

    def finish(z_ref):
        z = z_ref[...]
        x1 = x_ref[...] + mod_ref[2:3, :] * (z * _rms(z, D_MODEL) * gpost_ref[...])
        x1_ref[...] = x1
        h2 = x1 * _rms(x1, D_MODEL) * gfpre_ref[...]
        h2_ref[...] = (h2 * (1.0 + mod_ref[4:5, :]) + mod_ref[3:4, :]).astype(BF16)

    @pl.when(g % 2 == 0)
    def _():
        project(za_ref)
        finish(zb_ref)

    @pl.when(g % 2 == 1)
    def _():
        project(zb_ref)
        finish(za_ref)


def mixout_call(x, ys, mod, mod_row, lw):
    b, s, d = x.shape
    tq = TOK_TILE
    per = s // tq
    n = b * per

    def proj_tile(g):
        ga = jnp.minimum(g, n - 1)
        return ga // per, ga % per, 0

    def done_tile(g):
        gb = jnp.maximum(g - 1, 0)
        return gb // per, gb % per, 0

    if mod_row is None:
        mod_map = lambda g: (done_tile(g)[0], 0, 0)
    else:
        mod_map = lambda g: (mod_row, 0, 0)
    const2 = lambda g: (0, 0)
    y_spec = pl.BlockSpec((None, tq, MIX_COLS), proj_tile)
    return pl.pallas_call(
        mixout_kernel,
        name="mixout",
        grid=(n + 1,),
        in_specs=[pl.BlockSpec((None, tq, d), done_tile), y_spec, y_spec, y_spec, y_spec,
                  pl.BlockSpec((4 * MIX_COLS, d), const2),
                  pl.BlockSpec((None, N_MOD, d), mod_map),
                  pl.BlockSpec((1, d), const2), pl.BlockSpec((1, d), const2)],
        out_specs=[pl.BlockSpec((None, tq, d), done_tile), pl.BlockSpec((None, tq, d), done_tile)],
        out_shape=[jax.ShapeDtypeStruct((b, s, d), F32), jax.ShapeDtypeStruct((b, s, d), BF16)],
        scratch_shapes=[pltpu.VMEM((tq, d), F32), pltpu.VMEM((tq, d), F32)],
        compiler_params=pltpu.CompilerParams(
            dimension_semantics=("arbitrary",), vmem_limit_bytes=VMEM_LIMIT),
    )(x, *ys, lw["w_out"], mod, lw["g_mix_post"], lw["g_ffn_pre"])


def ffn_kernel(x1_ref, hc_ref, hp_ref, hn_ref, wup_ref, cw_ref, cb_ref, wdn_ref, mod_ref,
               gpost_ref, o_ref, g_ref):
    t = pl.program_id(1)
    tq = hc_ref.shape[0]
    halo = hp_ref.shape[0]
    hp = jnp.where(t > 0, hp_ref[...], jnp.zeros_like(hp_ref))
    hn = jnp.where(t < pl.num_programs(1) - 1, hn_ref[...], jnp.zeros_like(hn_ref))
    hext = jnp.concatenate([hp, hc_ref[...], hn], axis=0)
    rows = hext.shape[0]

    for j in range(wup_ref.shape[0]):
        u = _dot(hext, wup_ref[j])
        w = cw_ref[j]
        conv = (pltpu.roll(u, 1, 0)[halo:halo + tq] * w[0:1]
                + u[halo:halo + tq] * w[1:2]
                + pltpu.roll(u, rows - 1, 0)[halo:halo + tq] * w[2:3] + cb_ref[j])
        a = conv[:, :FFN_CHUNK]
        gated = a * (1.0 / (1.0 + jnp.exp(-a))) * conv[:, FFN_CHUNK:]
        g_ref[:, j * FFN_CHUNK:(j + 1) * FFN_CHUNK] = gated.astype(BF16)

    f = _dot(g_ref[...], wdn_ref[...])
    o_ref[...] = x1_ref[...] + mod_ref[5:6, :] * (f * _rms(f, D_MODEL) * gpost_ref[...])


def ffn_call(x1, h2, mod, mod_row, lw):
    b, s, d = x1.shape
    tq = min(FFN_TILE, s)
    halo = BF16_SUBLANES
    per_tile = tq // halo
    last = s // halo - 1
    n_ch = FFN_DIM // FFN_CHUNK
    if mod_row is None:
        mod_map = lambda i, t: (i, 0, 0)
    else:
        mod_map = lambda i, t: (mod_row, 0, 0)
    const2 = lambda i, t: (0, 0)
    const3 = lambda i, t: (0, 0, 0)
    tok = lambda i, t: (i, t, 0)
    return pl.pallas_call(
        ffn_kernel,
        name="conv_ffn",
        grid=(b, s // tq),
        in_specs=[
            pl.BlockSpec((None, tq, d), tok),
            pl.BlockSpec((None, tq, d), tok),
            pl.BlockSpec((None, halo, d), lambda i, t: (i, jnp.maximum(t * per_tile - 1, 0), 0)),
            pl.BlockSpec((None, halo, d), lambda i, t: (i, jnp.minimum((t + 1) * per_tile, last), 0)),
            pl.BlockSpec((n_ch, d, 2 * FFN_CHUNK), const3),
            pl.BlockSpec((n_ch, 3, 2 * FFN_CHUNK), const3),
            pl.BlockSpec((n_ch, 1, 2 * FFN_CHUNK), const3),
            pl.BlockSpec((FFN_DIM, d), const2),
            pl.BlockSpec((None, N_MOD, d), mod_map),
            pl.BlockSpec((1, d), const2),
        ],
        out_specs=pl.BlockSpec((None, tq, d), tok),
        out_shape=jax.ShapeDtypeStruct((b, s, d), F32),
        scratch_shapes=[pltpu.VMEM((tq, FFN_DIM), BF16)],
        compiler_params=pltpu.CompilerParams(
            dimension_semantics=("arbitrary", "arbitrary"), vmem_limit_bytes=VMEM_LIMIT),
    )(x1, h2, h2, h2, lw["w_up"], lw["conv_w"], lw["conv_b"], lw["w_down"], mod,
      lw["g_ffn_post"])


def _rope_table(n_ctx, s, rot_dim, lane_lo, lane_hi, period):
    n = rot_dim // 4
    tok = np.arange(s)
    pos = np.stack([tok // GRID_W, tok % GRID_W], axis=1).astype(np.float32)
    lane = np.arange(LANES)
    o = (lane - lane_lo) % period
    axis = o // (2 * n)
    second = (o % (2 * n)) // n
    active = (lane >= lane_lo) & (lane < lane_hi)
    inv_freq = ROPE_THETA ** (-jnp.arange(n, dtype=F32) / n)
    ang = jnp.asarray(pos)[:, axis] * inv_freq[o % n][None, :]
    cos = jnp.where(active[None], jnp.cos(ang), 1.0)
    sin = jnp.where(active[None], jnp.sin(ang), 0.0)
    sa = jnp.where((second == 0)[None], -sin, 0.0)
    sb = jnp.where((second == 1)[None], sin, 0.0)
    tab = jnp.stack([cos, sa, sb]).astype(F32)
    ident = jnp.stack([jnp.ones((n_ctx, LANES), F32), jnp.zeros((n_ctx, LANES), F32),
                       jnp.zeros((n_ctx, LANES), F32)])
    return jnp.concatenate([tab, ident], axis=1)


def _layer_weights(l, w):
    d = D_MODEL
    w_in = w["w_in"][l]
    mla0 = 768 + 512
    cq_hi = mla0 + MLA_Q_RANK
    kr0 = cq_hi + MLA_KV_RANK
    w_in_p = jnp.concatenate([
        w_in[:, :mla0 + LANES],
        w_in[:, mla0 + LANES:cq_hi], w_in[:, kr0:kr0 + MLA_ROPE_DIM],
        jnp.zeros((d, LANES - HALF - MLA_ROPE_DIM), F32),
        w_in[:, cq_hi:kr0],
        w_in[:, kr0 + MLA_ROPE_DIM:],
    ], axis=1)
    dqk = MLA_NOPE_DIM + MLA_ROPE_DIM
    wq = jnp.zeros((2 * LANES, MLA_HEADS, LANES), F32)
    wq = wq.at[:MLA_Q_RANK, :, :dqk].set(w["mla_w_uq"][l].reshape(MLA_Q_RANK, MLA_HEADS, dqk))
    assert MLA_KV_RANK == LANES
    wukv = w["mla_w_ukv"][l].reshape(MLA_KV_RANK, MLA_HEADS, MLA_NOPE_DIM + MLA_V_DIM)
    wkt = jnp.zeros((MLA_HEADS, LANES, LANES), F32)
    wkt = wkt.at[:, :MLA_NOPE_DIM, :].set(jnp.transpose(wukv[:, :, :MLA_NOPE_DIM], (1, 2, 0)))
    wuv = jnp.transpose(wukv[:, :, MLA_NOPE_DIM:], (1, 0, 2))
    wv = jnp.zeros((2, 2, LANES, 2, MLA_V_DIM), F32)
    for hd in range(MLA_HEADS):
        wv = wv.at[hd // 2, hd % 2, :, hd % 2, :].set(wuv[hd])
    wv = wv.reshape(2, 2 * LANES, LANES)
    gq = jnp.zeros((1, 2 * LANES), F32).at[0, :MLA_Q_RANK].set(w["mla_g_q"][l])
    n_ch = FFN_DIM // FFN_CHUNK

    def chunked(a):
        lead = a.shape[:-1]
        a = a.reshape(lead + (2, n_ch, FFN_CHUNK))
        a = jnp.moveaxis(a, -2, 0)
        return a.reshape((n_ch,) + lead + (2 * FFN_CHUNK,))

    return {
        "g_mix_pre": w["g_mix_pre"][l][None], "g_mix_post": w["g_mix_post"][l][None],
        "g_ffn_pre": w["g_ffn_pre"][l][None], "g_ffn_post": w["g_ffn_post"][l][None],
        "w_in": w_in_p.astype(BF16),
        "wq": wq.reshape(2 * LANES, -1).astype(BF16), "wkt": wkt.astype(BF16),
        "wv": wv.astype(BF16),
        "gq": gq, "gkv": w["mla_g_kv"][l][None],
        "gnq": jnp.tile(w["gqa_g_q"][l], 2)[None], "gnk": jnp.tile(w["gqa_g_k"][l], 2)[None],
        "lam": jnp.stack([w["diff_lam_q1"][l], w["diff_lam_k1"][l],
                          w["diff_lam_q2"][l], w["diff_lam_k2"][l]]),
        "g_sub": jnp.tile(w["diff_g_sub"][l], 2)[None],
        "sink": w["swa_sink"][l],
        "w_out": w["w_out"][l].astype(BF16),
        "w_up": chunked(w["ffn_w_up"][l]).astype(BF16),
        "conv_w": chunked(w["ffn_conv_w"][l]),
        "conv_b": chunked(w["ffn_conv_b"][l][None]),
        "w_down": w["ffn_w_down"][l].astype(BF16),
    }


def kernel(x, c, ctx, c_ctx, w_mod, b_mod, g_mix_pre, g_mix_post, g_ffn_pre, g_ffn_post, w_in, diff_lam_q1, diff_lam_k1, diff_lam_q2, diff_lam_k2, diff_g_sub, swa_sink, mla_g_q, mla_g_kv, mla_w_uq, mla_w_ukv, gqa_g_q, gqa_g_k, w_out, ffn_w_up, ffn_conv_w, ffn_conv_b, ffn_w_down):
    w = dict(g_mix_pre=g_mix_pre, g_mix_post=g_mix_post, g_ffn_pre=g_ffn_pre,
             g_ffn_post=g_ffn_post, w_in=w_in, diff_lam_q1=diff_lam_q1, diff_lam_k1=diff_lam_k1,
             diff_lam_q2=diff_lam_q2, diff_lam_k2=diff_lam_k2, diff_g_sub=diff_g_sub,
             swa_sink=swa_sink, mla_g_q=mla_g_q, mla_g_kv=mla_g_kv, mla_w_uq=mla_w_uq,
             mla_w_ukv=mla_w_ukv, gqa_g_q=gqa_g_q, gqa_g_k=gqa_g_k, w_out=w_out,
             ffn_w_up=ffn_w_up, ffn_conv_w=ffn_conv_w, ffn_conv_b=ffn_conv_b,
             ffn_w_down=ffn_w_down)
    b, s, d = x.shape
    depth = w_mod.shape[0]
    ctx_row = b
    mod_rows = 16
    cc = jnp.zeros((mod_rows, d), F32).at[:b].set(c).at[ctx_row].set(c_ctx)
    mod_all = mod_call(cc, w_mod, b_mod).reshape(depth, mod_rows, N_MOD, d)

    n_ctx = ctx.shape[1]
    tabs = (_rope_table(n_ctx, s, DIFF_QK_DIM, 0, LANES, DIFF_QK_DIM),
            _rope_table(n_ctx, s, GQA_HEAD_DIM, 0, LANES, GQA_HEAD_DIM),
            _rope_table(n_ctx, s, MLA_ROPE_DIM, MLA_NOPE_DIM, MLA_NOPE_DIM + MLA_ROPE_DIM,
                        MLA_ROPE_DIM))

    xc = ctx
    for l in range(depth):
        need_ctx = l < depth - 1
        lambda_init = 0.8 - 0.6 * math.exp(-0.3 * l)
        lw = _layer_weights(l, w)
        mod = mod_all[l]
        qkv = proj_call(xc, x, mod, ctx_row, lw, tabs)
        diff_extra = (lw["lam"], lw["g_sub"])
        mla_extra = (lw["wv"],)
        ys = (dense_attn_call("diff", qkv, True, diff_extra, lambda_init),
              swa_call(lw["sink"], qkv, True),
              dense_attn_call("mla", qkv, True, mla_extra),
              dense_attn_call("gqa", qkv, True))
        x1, h2 = mixout_call(x, ys, mod, None, lw)
        x = ffn_call(x1, h2, mod, None, lw)
        if need_ctx:
            ysc = (dense_attn_call("diff", qkv, False, diff_extra, lambda_init),
                   swa_call(lw["sink"], qkv, False),
                   dense_attn_call("mla", qkv, False, mla_extra),
                   dense_attn_call("gqa", qkv, False))
            xc1, hc2 = mixout_call(xc, ysc, mod, ctx_row, lw)
            xc = ffn_call(xc1, hc2, mod, ctx_row, lw)
    return x
```

```python
import functools
import math

import jax
import jax.numpy as jnp
import numpy as np
from jax import lax
from jax.experimental import pallas as pl
from jax.experimental.pallas import tpu as pltpu

F32 = jnp.float32
BF16 = jnp.bfloat16

D_MODEL = 1024
GRID_W = 64
ROPE_THETA = 10000.0
NORM_EPS = 1e-6
NEG_INF = -1e30
N_MOD = 6

DIFF_HEADS, DIFF_QK_DIM, DIFF_V_DIM = 4, 32, 64
SWA_HEAD_DIM, WINDOW = 64, 128
MLA_Q_RANK, MLA_KV_RANK = 192, 128
MLA_NOPE_DIM, MLA_ROPE_DIM, MLA_V_DIM, MLA_HEADS = 64, 32, 64, 4
GQA_HEAD_DIM = 64
FFN_DIM = 2816
MIX_COLS = 256

LANES = 128
BF16_SUBLANES = 16
HALF = LANES // 2
VMEM_LIMIT = 56 * 1024 * 1024

P_COLS = 17 * LANES
G_QD, G_QM, G_VD, G_QG, G_QS = 0, 8, 16, 20, 24
G_KD, G_VG, G_KM, G_VM, G_VS, G_KG, G_KS = 28, 30, 32, 34, 36, 38, 39
QKV_COLS = 40 * LANES
LOG2E = math.log2(math.e)

TOK_TILE = 256
CTX_TILE = 256
ATT_TILE = 512
KV_CHUNKS = 2
FFN_TILE = 512
FFN_CHUNK = 256


def _rms(x, n):
    return lax.rsqrt(jnp.sum(x * x, axis=-1, keepdims=True) * (1.0 / n) + NORM_EPS)


def _lane(shape):
    return lax.broadcasted_iota(jnp.int32, shape, len(shape) - 1)


def _dot(a, b):
    return jnp.dot(a, b, preferred_element_type=F32)


def _dot_nt(a, b):
    return lax.dot_general(a, b, (((1,), (1,)), ((), ())), preferred_element_type=F32)


def mod_kernel(c_ref, w_ref, b_ref, o_ref):
    c = c_ref[...]
    s = c * (1.0 / (1.0 + jnp.exp(-c)))
    o_ref[...] = jnp.dot(s, w_ref[...], preferred_element_type=F32,
                         precision=lax.Precision.HIGHEST) + b_ref[...]


def mod_call(cc, w_mod, b_mod):
    depth, d, n = w_mod.shape
    rows = cc.shape[0]
    tn = 1024
    return pl.pallas_call(
        mod_kernel,
        name="mod_vectors",
        grid=(depth, n // tn),
        in_specs=[
            pl.BlockSpec((rows, d), lambda l, j: (0, 0)),
            pl.BlockSpec((None, d, tn), lambda l, j: (l, 0, j)),
            pl.BlockSpec((None, 1, tn), lambda l, j: (l, 0, j)),
        ],
        out_specs=pl.BlockSpec((None, rows, tn), lambda l, j: (l, 0, j)),
        out_shape=jax.ShapeDtypeStruct((depth, rows, n), F32),
        compiler_params=pltpu.CompilerParams(
            dimension_semantics=("arbitrary", "arbitrary"), vmem_limit_bytes=VMEM_LIMIT),
    )(cc, w_mod, b_mod.reshape(depth, 1, n))


def _rope(x, tab_ref, n):
    return (x * tab_ref[0] + pltpu.roll(x, LANES - n, 1) * tab_ref[1]
            + pltpu.roll(x, n, 1) * tab_ref[2])


def _halves(lo, hi, shape):
    return jnp.where(_lane(shape) < HALF, lo, hi)


def _head_rms(x):
    low = _lane(x.shape) < HALF
    sq = x * x
    ss_lo = jnp.sum(jnp.where(low, sq, 0.0), axis=-1, keepdims=True)
    ss_hi = jnp.sum(jnp.where(low, 0.0, sq), axis=-1, keepdims=True)
    r_lo = lax.rsqrt(ss_lo * (1.0 / HALF) + NORM_EPS)
    r_hi = lax.rsqrt(ss_hi * (1.0 / HALF) + NORM_EPS)
    return jnp.where(low, r_lo, r_hi)


def proj_kernel(xc_ref, x_ref, mod_ref, gpre_ref, win_ref, wq_ref, wkt_ref, gq_ref, gkv_ref,
                gnq_ref, gnk_ref, t32_ref, t64_ref, tml_ref, o_ref, pa_ref, pb_ref, *,
                tiles_per_sample, n_tiles):
    g = pl.program_id(0)
    is_ctx = jnp.minimum(g, n_tiles - 1) % tiles_per_sample == tiles_per_sample - 1

    @pl.when(g == 0)
    def _():
        pb_ref[...] = jnp.zeros_like(pb_ref)

    def project(dst_ref):
        x = jnp.where(is_ctx, xc_ref[...], x_ref[...])
        h = x * _rms(x, D_MODEL) * gpre_ref[...]
        h = h * (1.0 + mod_ref[1:2, :]) + mod_ref[0:1, :]
        dst_ref[...] = _dot(h.astype(BF16), win_ref[...])

    finish = functools.partial(_proj_finish, wq_ref, wkt_ref, gq_ref, gkv_ref, gnq_ref, gnk_ref,
                               t32_ref, t64_ref, tml_ref, o_ref)

    @pl.when(g % 2 == 0)
    def _():
        project(pa_ref)
        finish(pb_ref)

    @pl.when(g % 2 == 1)
    def _():
        project(pb_ref)
        finish(pa_ref)


def _proj_finish(wq_ref, wkt_ref, gq_ref, gkv_ref, gnq_ref, gnk_ref, t32_ref, t64_ref, tml_ref,
                 o_ref, p_ref):
    def grp(g):
        return p_ref[:, g * LANES:(g + 1) * LANES]

    def put(g, v):
        o_ref[:, g * LANES:(g + 1) * LANES] = v.astype(BF16)

    def rope32(v):
        return _rope(v, t32_ref, DIFF_QK_DIM // 4)

    def rope64(v):
        return _rope(v, t64_ref, GQA_HEAD_DIM // 4)

    def ropeml(v):
        return _rope(v, tml_ref, MLA_ROPE_DIM // 4)

    lane = _lane((p_ref.shape[0], LANES))
    low = lane < HALF

    def with_ones(v):
        return jnp.where(low, v, 1.0), jnp.where(low, 1.0, v)

    for g in range(2):
        qd = rope32(grp(g)) * (DIFF_QK_DIM ** -0.5 * LOG2E)
        for i in range(4):
            keep = (lane >= i * DIFF_QK_DIM) & (lane < (i + 1) * DIFF_QK_DIM)
            put(G_QD + 4 * g + i, jnp.where(keep, qd, 0.0))
        put(G_KD + g, rope32(grp(2 + g)))
        v_lo, v_hi = with_ones(grp(4 + g))
        put(G_VD + 2 * g, v_lo)
        put(G_VD + 2 * g + 1, v_hi)

    def put_gqa_queries(g0, q01, q23):
        put(g0, jnp.where(low, q01, 0.0))
        put(g0 + 1, jnp.where(low, pltpu.roll(q01, HALF, 1), 0.0))
        put(g0 + 2, jnp.where(low, 0.0, pltpu.roll(q23, HALF, 1)))
        put(g0 + 3, jnp.where(low, 0.0, q23))

    scale_s = SWA_HEAD_DIM ** -0.5 * LOG2E
    put_gqa_queries(G_QS, rope64(grp(6)) * scale_s, rope64(grp(7)) * scale_s)
    put(G_KS, rope64(grp(8)))
    v_lo, v_hi = with_ones(grp(9))
    put(G_VS, v_lo)
    put(G_VS + 1, v_hi)

    cq0, cq1, ckv = grp(10), grp(11), grp(12)
    ss = (jnp.sum(cq0 * cq0, axis=-1, keepdims=True)
          + jnp.sum(jnp.where(low, cq1 * cq1, 0.0), axis=-1, keepdims=True))
    rq = lax.rsqrt(ss * (1.0 / MLA_Q_RANK) + NORM_EPS)
    cqn = jnp.concatenate([cq0 * rq * gq_ref[:, :LANES], cq1 * rq * gq_ref[:, LANES:]], axis=1)
    qm = _dot(cqn.astype(BF16), wq_ref[...])
    scale_m = (MLA_NOPE_DIM + MLA_ROPE_DIM) ** -0.5 * LOG2E
    rope_lanes = (lane >= MLA_NOPE_DIM) & (lane < MLA_NOPE_DIM + MLA_ROPE_DIM)
    for hd in range(MLA_HEADS):
        qh = ropeml(qm[:, hd * LANES:(hd + 1) * LANES]) * scale_m
        put(G_QM + 2 * hd, _dot(qh.astype(BF16), wkt_ref[hd]))
        put(G_QM + 2 * hd + 1, jnp.where(rope_lanes, qh, 0.0))
    ckvn = ckv * _rms(ckv, MLA_KV_RANK) * gkv_ref[...]
    put(G_KM, ckvn)
    put(G_KM + 1, jnp.where(rope_lanes, ropeml(cq1), 0.0))
    put(G_VM, ckvn)
    put(G_VM + 1, jnp.ones_like(ckvn))

    scale_g = GQA_HEAD_DIM ** -0.5 * LOG2E
    q01 = grp(13)
    q01 = rope64(q01 * _head_rms(q01) * gnq_ref[...]) * scale_g
    q23 = grp(14)
    q23 = rope64(q23 * _head_rms(q23) * gnq_ref[...]) * scale_g
    put_gqa_queries(G_QG, q01, q23)
    kg = grp(15)
    put(G_KG, rope64(kg * _head_rms(kg) * gnk_ref[...]))
    v_lo, v_hi = with_ones(grp(16))
    put(G_VG, v_lo)
    put(G_VG + 1, v_hi)


def proj_call(xc, x, mod, ctx_row, lw, tabs):
    b, s, d = x.shape
    tq = TOK_TILE
    assert xc.shape[1] == CTX_TILE == tq
    s_all = CTX_TILE + s
    per = s_all // tq
    n = b * per
    const2 = lambda g: (0, 0)

    def proj_tile(g):
        ga = jnp.minimum(g, n - 1)
        return ga // per, ga % per

    def done_tile(g):
        gb = jnp.maximum(g - 1, 0)
        return gb // per, gb % per

    def x_map(g):
        i, t = proj_tile(g)
        return i, jnp.minimum(t, per - 2), 0

    def mod_map(g):
        i, t = proj_tile(g)
        return jnp.where(t == per - 1, ctx_row, i), 0, 0

    tab_spec = pl.BlockSpec((3, tq, LANES), lambda g: (0, done_tile(g)[1], 0))
    return pl.pallas_call(
        functools.partial(proj_kernel, tiles_per_sample=per, n_tiles=n),
        name="proj",
        grid=(n + 1,),
        in_specs=[
            pl.BlockSpec((None, tq, d), lambda g: (proj_tile(g)[0], 0, 0)),
            pl.BlockSpec((None, tq, d), x_map),
            pl.BlockSpec((None, N_MOD, d), mod_map),
            pl.BlockSpec((1, d), const2),
            pl.BlockSpec((d, P_COLS), const2),
            pl.BlockSpec((2 * LANES, 4 * LANES), const2),
            pl.BlockSpec((MLA_HEADS, LANES, LANES), lambda g: (0, 0, 0)),
            pl.BlockSpec((1, 2 * LANES), const2),
            pl.BlockSpec((1, LANES), const2),
            pl.BlockSpec((1, LANES), const2),
            pl.BlockSpec((1, LANES), const2),
            tab_spec, tab_spec, tab_spec,
        ],
        out_specs=pl.BlockSpec((None, tq, QKV_COLS), lambda g: done_tile(g) + (0,)),
        out_shape=jax.ShapeDtypeStruct((b, s_all, QKV_COLS), BF16),
        scratch_shapes=[pltpu.VMEM((tq, P_COLS), F32), pltpu.VMEM((tq, P_COLS), F32)],
        compiler_params=pltpu.CompilerParams(
            dimension_semantics=("arbitrary",), vmem_limit_bytes=VMEM_LIMIT),
    )(xc, x, mod, lw["g_mix_pre"], lw["w_in"], lw["wq"], lw["wkt"], lw["gq"], lw["gkv"],
      lw["gnq"], lw["gnk"], tabs[0], tabs[1], tabs[2])


def _group(ref, rows, g, width=1):
    return ref[rows, g * width * LANES:(g + 1) * width * LANES]


def _attend(q_ref, segs, k_ref, v_ref, n_chunks, width):
    chunk = k_ref.shape[0] // n_chunks
    qs = [jnp.concatenate([_group(q_ref, slice(None), g, width) for g in qg], axis=0)
          for qg, _, _ in segs]

    def body(i, carry):
        rows = pl.ds(pl.multiple_of(i * chunk, LANES), chunk)
        out = []
        for j, (q, (_, kb, parts)) in enumerate(zip(qs, segs)):
            m, acc = carry[2 * j], carry[2 * j + 1]
            s = _dot_nt(q, _group(k_ref, rows, kb, width))
            m_new = jnp.maximum(m, jnp.max(s, axis=-1, keepdims=True))
            p = jnp.exp2(s - m_new).astype(BF16)
            tq = q.shape[0] // sum(n for n, _ in parts)
            pv, r0 = [], 0
            for n, vb in parts:
                pv.append(_dot(p[r0:r0 + n * tq], _group(v_ref, rows, vb, width)))
                r0 += n * tq
            pv = pv[0] if len(pv) == 1 else jnp.concatenate(pv, axis=0)
            out += [m_new, acc * jnp.exp2(m - m_new) + pv]
        return tuple(out)

    init = []
    for q in qs:
        init += [jnp.full((q.shape[0], 1), NEG_INF, F32),
                 jnp.zeros((q.shape[0], width * LANES), F32)]
    carry = lax.fori_loop(0, n_chunks, body, tuple(init))
    half = width * HALF
    return [carry[2 * j + 1] / pltpu.roll(carry[2 * j + 1], half, 1) for j in range(len(segs))]


_PLANS = {
    "gqa": (1, ((((0, 1, 2, 3), 0, ((2, 0), (2, 1))),),)),
    "mla": (2, ((((0, 1, 2, 3), 0, ((2, 0), (2, 0))),),)),
    "diff": (1, ((((0, 1, 2, 3), 0, ((2, 0), (2, 1))),), (((4, 5, 6, 7), 1, ((2, 2), (2, 3))),))),
}


def dense_kernel(*refs, kind, n_chunks, lambda_init):
    refs = list(refs)
    if kind == "diff":
        lam_ref, g_ref = refs[:2]
        refs = refs[2:]
    elif kind == "mla":
        wv_ref = refs[0]
        refs = refs[1:]
    q_ref, k_ref, v_ref, o_ref = refs
    tq = q_ref.shape[0]
    width, loops = _PLANS[kind]
    res = [_attend(q_ref, segs, k_ref, v_ref, n_chunks, width) for segs in loops]

    def tiles(r):
        return [r[i * tq:(i + 1) * tq] for i in range(r.shape[0] // tq)]

    if kind == "gqa":
        h = [t for r in res[0] for t in tiles(r)]
        blocks = [_halves(h[0], pltpu.roll(h[1], HALF, 1), h[0].shape),
                  _halves(pltpu.roll(h[2], HALF, 1), h[3], h[0].shape)]
    elif kind == "mla":
        h = [r[:, :LANES].astype(BF16) for r in tiles(res[0][0])]
        blocks = [_dot(jnp.concatenate(h[2 * ob:2 * ob + 2], axis=1), wv_ref[ob]) for ob in range(2)]
    else:
        lv = lam_ref[...]
        lam = (jnp.exp(jnp.sum(lv[0:1] * lv[1:2], axis=-1, keepdims=True))
               - jnp.exp(jnp.sum(lv[2:3] * lv[3:4], axis=-1, keepdims=True)) + lambda_init)
        blocks = []
        for (r,) in res:
            a = tiles(r)
            o = _halves(a[0] - lam * a[1], a[2] - lam * a[3], a[0].shape)
            blocks.append(o * _head_rms(o) * g_ref[...] * (1.0 - lambda_init))
    for ob, blk in enumerate(blocks):
        o_ref[:, ob * LANES:(ob + 1) * LANES] = blk.astype(BF16)


def _qkv_specs(lay, latent, s_all, tq):
    (q0, qn), (k0, kn), (v0, vn) = lay
    assert q0 % qn == 0 and k0 % kn == 0 and v0 % vn == 0
    ctx_blk = s_all // CTX_TILE - 1
    if latent:
        q_map = lambda i, t: (i, t, q0 // qn)
        k_map = lambda i, t: (i, 0, k0 // kn)
        v_map = lambda i, t: (i, 0, v0 // vn)
    else:
        q_map = lambda i, t: (i, ctx_blk, q0 // qn)
        k_map = lambda i, t: (i, ctx_blk, k0 // kn)
        v_map = lambda i, t: (i, ctx_blk, v0 // vn)
    kv_rows = s_all if latent else CTX_TILE
    return [
        pl.BlockSpec((None, tq, qn * LANES), q_map),
        pl.BlockSpec((None, kv_rows, kn * LANES), k_map),
        pl.BlockSpec((None, kv_rows, vn * LANES), v_map),
    ]


def dense_attn_call(kind, qkv, latent, extra=None, lambda_init=None):
    b, s_all, _ = qkv.shape
    tq = ATT_TILE if latent else CTX_TILE
    sq = s_all - CTX_TILE if latent else CTX_TILE
    lay = {"diff": ((G_QD, 8), (G_KD, 2), (G_VD, 4)),
           "mla": ((G_QM, 8), (G_KM, 2), (G_VM, 2)),
           "gqa": ((G_QG, 4), (G_KG, 1), (G_VG, 2))}[kind]
    specs = _qkv_specs(lay, latent, s_all, tq)
    args = [qkv, qkv, qkv]
    if kind == "diff":
        specs = [pl.BlockSpec((4, DIFF_QK_DIM), lambda i, t: (0, 0)),
                 pl.BlockSpec((1, LANES), lambda i, t: (0, 0))] + specs
        args = list(extra) + args
    elif kind == "mla":
        specs = [pl.BlockSpec((2, 2 * LANES, LANES), lambda i, t: (0, 0, 0))] + specs
        args = list(extra) + args
    body = functools.partial(dense_kernel, kind=kind, n_chunks=KV_CHUNKS if latent else 1,
                             lambda_init=lambda_init)
    return pl.pallas_call(
        body,
        name=kind + ("_attn" if latent else "_attn_ctx"),
        grid=(b, sq // tq),
        in_specs=specs,
        out_specs=pl.BlockSpec((None, tq, MIX_COLS), lambda i, t: (i, t, 0)),
        out_shape=jax.ShapeDtypeStruct((b, sq, MIX_COLS), BF16),
        compiler_params=pltpu.CompilerParams(
            dimension_semantics=("arbitrary", "arbitrary"), vmem_limit_bytes=VMEM_LIMIT),
    )(*args)


def swa_kernel(sink_ref, q_ref, k_ref, v_ref, o_ref, *, s_lat):
    tq = q_ref.shape[0]
    heads = q_ref.shape[1] // LANES
    rows = heads * tq
    t = pl.program_id(1)
    q = jnp.concatenate([_group(q_ref, slice(None), g) for g in range(heads)], axis=0)
    row = lax.broadcasted_iota(jnp.int32, (rows, 1), 0)
    sink = jnp.zeros((rows, 1), F32)
    for h in range(heads):
        sink = jnp.where(row // tq == h, sink_ref[h] * LOG2E, sink)

    spans = [(slice(s_lat, s_lat + CTX_TILE), None)]
    if s_lat:
        width = tq + 2 * WINDOW
        start = jnp.clip(t * tq - WINDOW, 0, s_lat - width)
        rel = (start - t * tq + lax.broadcasted_iota(jnp.int32, (tq, width), 1)
               - lax.broadcasted_iota(jnp.int32, (tq, width), 0))
        valid = jnp.concatenate([jnp.abs(rel) <= WINDOW] * heads, axis=0)
        spans.append((pl.ds(pl.multiple_of(start, WINDOW), width), valid))

    ss = []
    for span, valid in spans:
        s = _dot_nt(q, _group(k_ref, span, 0))
        ss.append(s if valid is None else jnp.where(valid, s, NEG_INF))
    m = functools.reduce(jnp.maximum, [jnp.max(s, axis=-1, keepdims=True) for s in ss] + [sink])
    ps = [jnp.exp2(s - m).astype(BF16) for s in ss]
    half = rows // 2
    acc = jnp.concatenate(
        [sum(_dot(p[vb * half:(vb + 1) * half], _group(v_ref, span, vb))
             for p, (span, _) in zip(ps, spans)) for vb in range(2)], axis=0)
    res = acc / (pltpu.roll(acc, HALF, 1) + jnp.exp2(sink - m))
    h = [res[i * tq:(i + 1) * tq] for i in range(heads)]
    o_ref[:, :LANES] = _halves(h[0], pltpu.roll(h[1], HALF, 1), h[0].shape).astype(BF16)
    o_ref[:, LANES:] = _halves(pltpu.roll(h[2], HALF, 1), h[3], h[0].shape).astype(BF16)


def swa_call(sink, qkv, latent):
    b, s_all, _ = qkv.shape
    tq = TOK_TILE
    s_lat = s_all - CTX_TILE if latent else 0
    sq = s_lat if latent else CTX_TILE
    specs = [pl.BlockSpec(memory_space=pltpu.SMEM)]
    specs += _qkv_specs(((G_QS, 4), (G_KS, 1), (G_VS, 2)), latent, s_all, tq)
    return pl.pallas_call(
        functools.partial(swa_kernel, s_lat=s_lat),
        name="swa_attn" if latent else "swa_attn_ctx",
        grid=(b, sq // tq),
        in_specs=specs,
        out_specs=pl.BlockSpec((None, tq, MIX_COLS), lambda i, t: (i, t, 0)),
        out_shape=jax.ShapeDtypeStruct((b, sq, MIX_COLS), BF16),
        compiler_params=pltpu.CompilerParams(
            dimension_semantics=("arbitrary", "arbitrary"), vmem_limit_bytes=VMEM_LIMIT),
    )(sink, qkv, qkv, qkv)


def mixout_kernel(x_ref, ya_ref, yb_ref, ym_ref, yd_ref, w_ref, mod_ref, gpost_ref,
                  gfpre_ref, x1_ref, h2_ref, za_ref, zb_ref):
    g = pl.program_id(0)

    @pl.when(g == 0)
    def _():
        zb_ref[...] = jnp.zeros_like(zb_ref)

    def project(dst_ref):
        y = jnp.concatenate([ya_ref[...], yb_ref[...], ym_ref[...], yd_ref[...]], axis=1)
        dst_ref[...] = _dot(y, w_ref[...])

    def finish(z_ref):
        z = z_ref[...]
        x1 = x_ref[...] + mod_ref[2:3, :] * (z * _rms(z, D_MODEL) * gpost_ref[...])
        x1_ref[...] = x1
        h2 = x1 * _rms(x1, D_MODEL) * gfpre_ref[...]
        h2_ref[...] = (h2 * (1.0 + mod_ref[4:5, :]) + mod_ref[3:4, :]).astype(BF16)

    @pl.when(g % 2 == 0)
    def _():
        project(za_ref)
        finish(zb_ref)

    @pl.when(g % 2 == 1)
    def _():
        project(zb_ref)
        finish(za_ref)


def mixout_call(x, ys, mod, mod_row, lw):
    b, s, d = x.shape
    tq = TOK_TILE
    per = s // tq
    n = b * per

    def proj_tile(g):
        ga = jnp.minimum(g, n - 1)
        return ga // per, ga % per, 0

    def done_tile(g):
        gb = jnp.maximum(g - 1, 0)
        return gb // per, gb % per, 0

    if mod_row is None:
        mod_map = lambda g: (done_tile(g)[0], 0, 0)
    else:
        mod_map = lambda g: (mod_row, 0, 0)
    const2 = lambda g: (0, 0)
    y_spec = pl.BlockSpec((None, tq, MIX_COLS), proj_tile)
    return pl.pallas_call(
        mixout_kernel,
        name="mixout",
        grid=(n + 1,),
        in_specs=[pl.BlockSpec((None, tq, d), done_tile), y_spec, y_spec, y_spec, y_spec,
                  pl.BlockSpec((4 * MIX_COLS, d), const2),
                  pl.BlockSpec((None, N_MOD, d), mod_map),
                  pl.BlockSpec((1, d), const2), pl.BlockSpec((1, d), const2)],
        out_specs=[pl.BlockSpec((None, tq, d), done_tile), pl.BlockSpec((None, tq, d), done_tile)],
        out_shape=[jax.ShapeDtypeStruct((b, s, d), F32), jax.ShapeDtypeStruct((b, s, d), BF16)],
        scratch_shapes=[pltpu.VMEM((tq, d), F32), pltpu.VMEM((tq, d), F32)],
        compiler_params=pltpu.CompilerParams(
            dimension_semantics=("arbitrary",), vmem_limit_bytes=VMEM_LIMIT),
    )(x, *ys, lw["w_out"], mod, lw["g_mix_post"], lw["g_ffn_pre"])


def ffn_kernel(x1_ref, hc_ref, hp_ref, hn_ref, wup_ref, cw_ref, cb_ref, wdn_ref, mod_ref,
               gpost_ref, o_ref, g_ref):
    t = pl.program_id(1)
    tq = hc_ref.shape[0]
    halo = hp_ref.shape[0]
    hp = jnp.where(t > 0, hp_ref[...], jnp.zeros_like(hp_ref))
    hn = jnp.where(t < pl.num_programs(1) - 1, hn_ref[...], jnp.zeros_like(hn_ref))
    hext = jnp.concatenate([hp, hc_ref[...], hn], axis=0)
    rows = hext.shape[0]

    def conv_cols(cols):
        u = _dot(hext, wup_ref[:, cols])
        w = cw_ref[:, cols]
        return (pltpu.roll(u, 1, 0)[halo:halo + tq] * w[0:1]
                + u[halo:halo + tq] * w[1:2]
                + pltpu.roll(u, rows - 1, 0)[halo:halo + tq] * w[2:3] + cb_ref[:, cols])

    for j in range(FFN_DIM // FFN_CHUNK):
        a = conv_cols(slice(j * FFN_CHUNK, (j + 1) * FFN_CHUNK))
        v = conv_cols(slice(FFN_DIM + j * FFN_CHUNK, FFN_DIM + (j + 1) * FFN_CHUNK))
        g_ref[:, j * FFN_CHUNK:(j + 1) * FFN_CHUNK] = (a * (1.0 / (1.0 + jnp.exp(-a))) * v).astype(BF16)

    f = _dot(g_ref[...], wdn_ref[...])
    o_ref[...] = x1_ref[...] + mod_ref[5:6, :] * (f * _rms(f, D_MODEL) * gpost_ref[...])


def ffn_call(x1, h2, mod, mod_row, lw):
    b, s, d = x1.shape
    tq = min(FFN_TILE, s)
    halo = BF16_SUBLANES
    per_tile = tq // halo
    last = s // halo - 1
    if mod_row is None:
        mod_map = lambda i, t: (i, 0, 0)
    else:
        mod_map = lambda i, t: (mod_row, 0, 0)
    const2 = lambda i, t: (0, 0)
    tok = lambda i, t: (i, t, 0)
    return pl.pallas_call(
        ffn_kernel,
        name="conv_ffn",
        grid=(b, s // tq),
        in_specs=[
            pl.BlockSpec((None, tq, d), tok),
            pl.BlockSpec((None, tq, d), tok),
            pl.BlockSpec((None, halo, d), lambda i, t: (i, jnp.maximum(t * per_tile - 1, 0), 0)),
            pl.BlockSpec((None, halo, d), lambda i, t: (i, jnp.minimum((t + 1) * per_tile, last), 0)),
            pl.BlockSpec((d, 2 * FFN_DIM), const2),
            pl.BlockSpec((3, 2 * FFN_DIM), const2),
            pl.BlockSpec((1, 2 * FFN_DIM), const2),
            pl.BlockSpec((FFN_DIM, d), const2),
            pl.BlockSpec((None, N_MOD, d), mod_map),
            pl.BlockSpec((1, d), const2),
        ],
        out_specs=pl.BlockSpec((None, tq, d), tok),
        out_shape=jax.ShapeDtypeStruct((b, s, d), F32),
        scratch_shapes=[pltpu.VMEM((tq, FFN_DIM), BF16)],
        compiler_params=pltpu.CompilerParams(
            dimension_semantics=("arbitrary", "arbitrary"), vmem_limit_bytes=VMEM_LIMIT),
    )(x1, h2, h2, h2, lw["w_up"], lw["conv_w"], lw["conv_b"], lw["w_down"], mod,
      lw["g_ffn_post"])


def _rope_table(n_ctx, s, rot_dim, lane_lo, lane_hi, period):
    n = rot_dim // 4
    tok = np.arange(s)
    pos = np.stack([tok // GRID_W, tok % GRID_W], axis=1).astype(np.float32)
    lane = np.arange(LANES)
    o = (lane - lane_lo) % period
    axis = o // (2 * n)
    second = (o % (2 * n)) // n
    active = (lane >= lane_lo) & (lane < lane_hi)
    inv_freq = ROPE_THETA ** (-jnp.arange(n, dtype=F32) / n)
    ang = jnp.asarray(pos)[:, axis] * inv_freq[o % n][None, :]
    cos = jnp.where(active[None], jnp.cos(ang), 1.0)
    sin = jnp.where(active[None], jnp.sin(ang), 0.0)
    sa = jnp.where((second == 0)[None], -sin, 0.0)
    sb = jnp.where((second == 1)[None], sin, 0.0)
    tab = jnp.stack([cos, sa, sb]).astype(F32)
    ident = jnp.stack([jnp.ones((n_ctx, LANES), F32), jnp.zeros((n_ctx, LANES), F32),
                       jnp.zeros((n_ctx, LANES), F32)])
    return jnp.concatenate([tab, ident], axis=1)


def _layer_weights(l, w):
    d = D_MODEL
    w_in = w["w_in"][l]
    mla0 = 768 + 512
    cq_hi = mla0 + MLA_Q_RANK
    kr0 = cq_hi + MLA_KV_RANK
    w_in_p = jnp.concatenate([
        w_in[:, :mla0 + LANES],
        w_in[:, mla0 + LANES:cq_hi], w_in[:, kr0:kr0 + MLA_ROPE_DIM],
        jnp.zeros((d, LANES - HALF - MLA_ROPE_DIM), F32),
        w_in[:, cq_hi:kr0],
        w_in[:, kr0 + MLA_ROPE_DIM:],
    ], axis=1)
    dqk = MLA_NOPE_DIM + MLA_ROPE_DIM
    wq = jnp.zeros((2 * LANES, MLA_HEADS, LANES), F32)
    wq = wq.at[:MLA_Q_RANK, :, :dqk].set(w["mla_w_uq"][l].reshape(MLA_Q_RANK, MLA_HEADS, dqk))
    assert MLA_KV_RANK == LANES
    wukv = w["mla_w_ukv"][l].reshape(MLA_KV_RANK, MLA_HEADS, MLA_NOPE_DIM + MLA_V_DIM)
    wkt = jnp.zeros((MLA_HEADS, LANES, LANES), F32)
    wkt = wkt.at[:, :MLA_NOPE_DIM, :].set(jnp.transpose(wukv[:, :, :MLA_NOPE_DIM], (1, 2, 0)))
    wuv = jnp.transpose(wukv[:, :, MLA_NOPE_DIM:], (1, 0, 2))
    wv = jnp.zeros((2, 2, LANES, 2, MLA_V_DIM), F32)
    for hd in range(MLA_HEADS):
        wv = wv.at[hd // 2, hd % 2, :, hd % 2, :].set(wuv[hd])
    wv = wv.reshape(2, 2 * LANES, LANES)
    gq = jnp.zeros((1, 2 * LANES), F32).at[0, :MLA_Q_RANK].set(w["mla_g_q"][l])
    return {
        "g_mix_pre": w["g_mix_pre"][l][None], "g_mix_post": w["g_mix_post"][l][None],
        "g_ffn_pre": w["g_ffn_pre"][l][None], "g_ffn_post": w["g_ffn_post"][l][None],
        "w_in": w_in_p.astype(BF16),
        "wq": wq.reshape(2 * LANES, -1).astype(BF16), "wkt": wkt.astype(BF16),
        "wv": wv.astype(BF16),
        "gq": gq, "gkv": w["mla_g_kv"][l][None],
        "gnq": jnp.tile(w["gqa_g_q"][l], 2)[None], "gnk": jnp.tile(w["gqa_g_k"][l], 2)[None],
        "lam": jnp.stack([w["diff_lam_q1"][l], w["diff_lam_k1"][l],
                          w["diff_lam_q2"][l], w["diff_lam_k2"][l]]),
        "g_sub": jnp.tile(w["diff_g_sub"][l], 2)[None],
        "sink": w["swa_sink"][l],
        "w_out": w["w_out"][l].astype(BF16),
        "w_up": w["ffn_w_up"][l].astype(BF16),
        "conv_w": w["ffn_conv_w"][l],
        "conv_b": w["ffn_conv_b"][l][None],
        "w_down": w["ffn_w_down"][l].astype(BF16),
    }


def kernel(x, c, ctx, c_ctx, w_mod, b_mod, g_mix_pre, g_mix_post, g_ffn_pre, g_ffn_post, w_in, diff_lam_q1, diff_lam_k1, diff_lam_q2, diff_lam_k2, diff_g_sub, swa_sink, mla_g_q, mla_g_kv, mla_w_uq, mla_w_ukv, gqa_g_q, gqa_g_k, w_out, ffn_w_up, ffn_conv_w, ffn_conv_b, ffn_w_down):
    w = dict(g_mix_pre=g_mix_pre, g_mix_post=g_mix_post, g_ffn_pre=g_ffn_pre,
             g_ffn_post=g_ffn_post, w_in=w_in, diff_lam_q1=diff_lam_q1, diff_lam_k1=diff_lam_k1,
             diff_lam_q2=diff_lam_q2, diff_lam_k2=diff_lam_k2, diff_g_sub=diff_g_sub,
             swa_sink=swa_sink, mla_g_q=mla_g_q, mla_g_kv=mla_g_kv, mla_w_uq=mla_w_uq,
             mla_w_ukv=mla_w_ukv, gqa_g_q=gqa_g_q, gqa_g_k=gqa_g_k, w_out=w_out,
             ffn_w_up=ffn_w_up, ffn_conv_w=ffn_conv_w, ffn_conv_b=ffn_conv_b,
             ffn_w_down=ffn_w_down)
    b, s, d = x.shape
    depth = w_mod.shape[0]
    ctx_row = b
    mod_rows = 16
    cc = jnp.zeros((mod_rows, d), F32).at[:b].set(c).at[ctx_row].set(c_ctx)
    mod_all = mod_call(cc, w_mod, b_mod).reshape(depth, mod_rows, N_MOD, d)

    n_ctx = ctx.shape[1]
    tabs = (_rope_table(n_ctx, s, DIFF_QK_DIM, 0, LANES, DIFF_QK_DIM),
            _rope_table(n_ctx, s, GQA_HEAD_DIM, 0, LANES, GQA_HEAD_DIM),
            _rope_table(n_ctx, s, MLA_ROPE_DIM, MLA_NOPE_DIM, MLA_NOPE_DIM + MLA_ROPE_DIM,
                        MLA_ROPE_DIM))

    xc = ctx
    for l in range(depth):
        need_ctx = l < depth - 1
        lambda_init = 0.8 - 0.6 * math.exp(-0.3 * l)
        lw = _layer_weights(l, w)
        mod = mod_all[l]
        qkv = proj_call(xc, x, mod, ctx_row, lw, tabs)
        diff_extra = (lw["lam"], lw["g_sub"])
        mla_extra = (lw["wv"],)
        ys = (dense_attn_call("diff", qkv, True, diff_extra, lambda_init),
              swa_call(lw["sink"], qkv, True),
              dense_attn_call("mla", qkv, True, mla_extra),
              dense_attn_call("gqa", qkv, True))
        x1, h2 = mixout_call(x, ys, mod, None, lw)
        x = ffn_call(x1, h2, mod, None, lw)
        if need_ctx:
            ysc = (dense_attn_call("diff", qkv, False, diff_extra, lambda_init),
                   swa_call(lw["sink"], qkv, False),
                   dense_attn_call("mla", qkv, False, mla_extra),
                   dense_attn_call("gqa", qkv, False))
            xc1, hc2 = mixout_call(xc, ysc, mod, ctx_row, lw)
            xc = ffn_call(xc1, hc2, mod, ctx_row, lw)
    return x
```

```python
import functools
import math

import jax
import jax.numpy as jnp
import numpy as np
from jax import lax
from jax.experimental import pallas as pl
from jax.experimental.pallas import tpu as pltpu

F32 = jnp.float32
BF16 = jnp.bfloat16

D_MODEL = 1024
GRID_W = 64
ROPE_THETA = 10000.0
NORM_EPS = 1e-6
NEG_INF = -1e30
N_MOD = 6

DIFF_HEADS, DIFF_QK_DIM, DIFF_V_DIM = 4, 32, 64
SWA_HEAD_DIM, WINDOW = 64, 128
MLA_Q_RANK, MLA_KV_RANK = 192, 128
MLA_NOPE_DIM, MLA_ROPE_DIM, MLA_V_DIM, MLA_HEADS = 64, 32, 64, 4
GQA_HEAD_DIM = 64
FFN_DIM = 2816
MIX_COLS = 256

LANES = 128
BF16_SUBLANES = 16
HALF = LANES // 2
VMEM_LIMIT = 56 * 1024 * 1024

P_COLS = 17 * LANES
G_QD, G_QM, G_VD, G_QG, G_QS = 0, 8, 16, 20, 24
G_KD, G_VG, G_KM, G_VM, G_VS, G_KG, G_KS = 28, 30, 32, 34, 36, 38, 39
QKV_COLS = 40 * LANES
LOG2E = math.log2(math.e)

TOK_TILE = 256
CTX_TILE = 256
ATT_TILE = 512
KV_CHUNKS = 2
FFN_TILE = 1024
FFN_CHUNK = 256


def _rms(x, n):
    return lax.rsqrt(jnp.sum(x * x, axis=-1, keepdims=True) * (1.0 / n) + NORM_EPS)


def _lane(shape):
    return lax.broadcasted_iota(jnp.int32, shape, len(shape) - 1)


def _dot(a, b):
    return jnp.dot(a, b, preferred_element_type=F32)


def _dot_nt(a, b):
    return lax.dot_general(a, b, (((1,), (1,)), ((), ())), preferred_element_type=F32)


def mod_kernel(c_ref, w_ref, b_ref, o_ref):
    c = c_ref[...]
    s = c * (1.0 / (1.0 + jnp.exp(-c)))
    o_ref[...] = jnp.dot(s, w_ref[...], preferred_element_type=F32,
                         precision=lax.Precision.HIGHEST) + b_ref[...]


def mod_call(cc, w_mod, b_mod):
    depth, d, n = w_mod.shape
    rows = cc.shape[0]
    tn = 1024
    return pl.pallas_call(
        mod_kernel,
        name="mod_vectors",
        grid=(depth, n // tn),
        in_specs=[
            pl.BlockSpec((rows, d), lambda l, j: (0, 0)),
            pl.BlockSpec((None, d, tn), lambda l, j: (l, 0, j)),
            pl.BlockSpec((None, 1, tn), lambda l, j: (l, 0, j)),
        ],
        out_specs=pl.BlockSpec((None, rows, tn), lambda l, j: (l, 0, j)),
        out_shape=jax.ShapeDtypeStruct((depth, rows, n), F32),
        compiler_params=pltpu.CompilerParams(
            dimension_semantics=("arbitrary", "arbitrary"), vmem_limit_bytes=VMEM_LIMIT),
    )(cc, w_mod, b_mod.reshape(depth, 1, n))


def _rope(x, tab_ref, n):
    return (x * tab_ref[0] + pltpu.roll(x, LANES - n, 1) * tab_ref[1]
            + pltpu.roll(x, n, 1) * tab_ref[2])


def _halves(lo, hi, shape):
    return jnp.where(_lane(shape) < HALF, lo, hi)


def _head_rms(x):
    low = _lane(x.shape) < HALF
    sq = x * x
    ss_lo = jnp.sum(jnp.where(low, sq, 0.0), axis=-1, keepdims=True)
    ss_hi = jnp.sum(jnp.where(low, 0.0, sq), axis=-1, keepdims=True)
    r_lo = lax.rsqrt(ss_lo * (1.0 / HALF) + NORM_EPS)
    r_hi = lax.rsqrt(ss_hi * (1.0 / HALF) + NORM_EPS)
    return jnp.where(low, r_lo, r_hi)


def proj_kernel(xc_ref, x_ref, mod_ref, gpre_ref, win_ref, wq_ref, wkt_ref, gq_ref, gkv_ref,
                gnq_ref, gnk_ref, t32_ref, t64_ref, tml_ref, o_ref, pa_ref, pb_ref, *,
                tiles_per_sample, n_tiles):
    g = pl.program_id(0)
    is_ctx = jnp.minimum(g, n_tiles - 1) % tiles_per_sample == tiles_per_sample - 1

    @pl.when(g == 0)
    def _():
        pb_ref[...] = jnp.zeros_like(pb_ref)

    def project(dst_ref):
        x = jnp.where(is_ctx, xc_ref[...], x_ref[...])
        h = x * _rms(x, D_MODEL) * gpre_ref[...]
        h = h * (1.0 + mod_ref[1:2, :]) + mod_ref[0:1, :]
        dst_ref[...] = _dot(h.astype(BF16), win_ref[...])

    finish = functools.partial(_proj_finish, wq_ref, wkt_ref, gq_ref, gkv_ref, gnq_ref, gnk_ref,
                               t32_ref, t64_ref, tml_ref, o_ref)

    @pl.when(g % 2 == 0)
    def _():
        project(pa_ref)
        finish(pb_ref)

    @pl.when(g % 2 == 1)
    def _():
        project(pb_ref)
        finish(pa_ref)


def _proj_finish(wq_ref, wkt_ref, gq_ref, gkv_ref, gnq_ref, gnk_ref, t32_ref, t64_ref, tml_ref,
                 o_ref, p_ref):
    def grp(g):
        return p_ref[:, g * LANES:(g + 1) * LANES]

    def put(g, v):
        o_ref[:, g * LANES:(g + 1) * LANES] = v.astype(BF16)

    def rope32(v):
        return _rope(v, t32_ref, DIFF_QK_DIM // 4)

    def rope64(v):
        return _rope(v, t64_ref, GQA_HEAD_DIM // 4)

    def ropeml(v):
        return _rope(v, tml_ref, MLA_ROPE_DIM // 4)

    lane = _lane((p_ref.shape[0], LANES))
    low = lane < HALF

    def with_ones(v):
        return jnp.where(low, v, 1.0), jnp.where(low, 1.0, v)

    for g in range(2):
        qd = rope32(grp(g)) * (DIFF_QK_DIM ** -0.5 * LOG2E)
        for i in range(4):
            keep = (lane >= i * DIFF_QK_DIM) & (lane < (i + 1) * DIFF_QK_DIM)
            put(G_QD + 4 * g + i, jnp.where(keep, qd, 0.0))
        put(G_KD + g, rope32(grp(2 + g)))
        v_lo, v_hi = with_ones(grp(4 + g))
        put(G_VD + 2 * g, v_lo)
        put(G_VD + 2 * g + 1, v_hi)

    def put_gqa_queries(g0, q01, q23):
        put(g0, jnp.where(low, q01, 0.0))
        put(g0 + 1, jnp.where(low, pltpu.roll(q01, HALF, 1), 0.0))
        put(g0 + 2, jnp.where(low, 0.0, pltpu.roll(q23, HALF, 1)))
        put(g0 + 3, jnp.where(low, 0.0, q23))

    scale_s = SWA_HEAD_DIM ** -0.5 * LOG2E
    put_gqa_queries(G_QS, rope64(grp(6)) * scale_s, rope64(grp(7)) * scale_s)
    put(G_KS, rope64(grp(8)))
    v_lo, v_hi = with_ones(grp(9))
    put(G_VS, v_lo)
    put(G_VS + 1, v_hi)

    cq0, cq1, ckv = grp(10), grp(11), grp(12)
    ss = (jnp.sum(cq0 * cq0, axis=-1, keepdims=True)
          + jnp.sum(jnp.where(low, cq1 * cq1, 0.0), axis=-1, keepdims=True))
    rq = lax.rsqrt(ss * (1.0 / MLA_Q_RANK) + NORM_EPS)
    cqn = jnp.concatenate([cq0 * rq * gq_ref[:, :LANES], cq1 * rq * gq_ref[:, LANES:]], axis=1)
    qm = _dot(cqn.astype(BF16), wq_ref[...])
    scale_m = (MLA_NOPE_DIM + MLA_ROPE_DIM) ** -0.5 * LOG2E
    rope_lanes = (lane >= MLA_NOPE_DIM) & (lane < MLA_NOPE_DIM + MLA_ROPE_DIM)
    for hd in range(MLA_HEADS):
        qh = ropeml(qm[:, hd * LANES:(hd + 1) * LANES]) * scale_m
        put(G_QM + 2 * hd, _dot(qh.astype(BF16), wkt_ref[hd]))
        put(G_QM + 2 * hd + 1, jnp.where(rope_lanes, qh, 0.0))
    ckvn = ckv * _rms(ckv, MLA_KV_RANK) * gkv_ref[...]
    put(G_KM, ckvn)
    put(G_KM + 1, jnp.where(rope_lanes, ropeml(cq1), 0.0))
    put(G_VM, ckvn)
    put(G_VM + 1, jnp.ones_like(ckvn))

    scale_g = GQA_HEAD_DIM ** -0.5 * LOG2E
    q01 = grp(13)
    q01 = rope64(q01 * _head_rms(q01) * gnq_ref[...]) * scale_g
    q23 = grp(14)
    q23 = rope64(q23 * _head_rms(q23) * gnq_ref[...]) * scale_g
    put_gqa_queries(G_QG, q01, q23)
    kg = grp(15)
    put(G_KG, rope64(kg * _head_rms(kg) * gnk_ref[...]))
    v_lo, v_hi = with_ones(grp(16))
    put(G_VG, v_lo)
    put(G_VG + 1, v_hi)


def proj_call(xc, x, mod, ctx_row, lw, tabs):
    b, s, d = x.shape
    tq = TOK_TILE
    assert xc.shape[1] == CTX_TILE == tq
    s_all = CTX_TILE + s
    per = s_all // tq
    n = b * per
    const2 = lambda g: (0, 0)

    def proj_tile(g):
        ga = jnp.minimum(g, n - 1)
        return ga // per, ga % per

    def done_tile(g):
        gb = jnp.maximum(g - 1, 0)
        return gb // per, gb % per

    def x_map(g):
        i, t = proj_tile(g)
        return i, jnp.minimum(t, per - 2), 0

    def mod_map(g):
        i, t = proj_tile(g)
        return jnp.where(t == per - 1, ctx_row, i), 0, 0

    tab_spec = pl.BlockSpec((3, tq, LANES), lambda g: (0, done_tile(g)[1], 0))
    return pl.pallas_call(
        functools.partial(proj_kernel, tiles_per_sample=per, n_tiles=n),
        name="proj",
        grid=(n + 1,),
        in_specs=[
            pl.BlockSpec((None, tq, d), lambda g: (proj_tile(g)[0], 0, 0)),
            pl.BlockSpec((None, tq, d), x_map),
            pl.BlockSpec((None, N_MOD, d), mod_map),
            pl.BlockSpec((1, d), const2),
            pl.BlockSpec((d, P_COLS), const2),
            pl.BlockSpec((2 * LANES, 4 * LANES), const2),
            pl.BlockSpec((MLA_HEADS, LANES, LANES), lambda g: (0, 0, 0)),
            pl.BlockSpec((1, 2 * LANES), const2),
            pl.BlockSpec((1, LANES), const2),
            pl.BlockSpec((1, LANES), const2),
            pl.BlockSpec((1, LANES), const2),
            tab_spec, tab_spec, tab_spec,
        ],
        out_specs=pl.BlockSpec((None, tq, QKV_COLS), lambda g: done_tile(g) + (0,)),
        out_shape=jax.ShapeDtypeStruct((b, s_all, QKV_COLS), BF16),
        scratch_shapes=[pltpu.VMEM((tq, P_COLS), F32), pltpu.VMEM((tq, P_COLS), F32)],
        compiler_params=pltpu.CompilerParams(
            dimension_semantics=("arbitrary",), vmem_limit_bytes=VMEM_LIMIT),
    )(xc, x, mod, lw["g_mix_pre"], lw["w_in"], lw["wq"], lw["wkt"], lw["gq"], lw["gkv"],
      lw["gnq"], lw["gnk"], tabs[0], tabs[1], tabs[2])


def _group(ref, rows, g, width=1):
    return ref[rows, g * width * LANES:(g + 1) * width * LANES]


def _attend(q_ref, segs, k_ref, v_ref, n_chunks, width, unroll=False):
    chunk = k_ref.shape[0] // n_chunks
    qs = [jnp.concatenate([_group(q_ref, slice(None), g, width) for g in qg], axis=0)
          for qg, _, _ in segs]

    def body(i, carry):
        rows = pl.ds(pl.multiple_of(i * chunk, LANES), chunk)
        out = []
        for j, (q, (_, kb, parts)) in enumerate(zip(qs, segs)):
            m, acc = carry[2 * j], carry[2 * j + 1]
            s = _dot_nt(q, _group(k_ref, rows, kb, width))
            m_new = jnp.maximum(m, jnp.max(s, axis=-1, keepdims=True))
            p = jnp.exp2(s - m_new).astype(BF16)
            tq = q.shape[0] // sum(n for n, _ in parts)
            pv, r0 = [], 0
            for n, vb in parts:
                pv.append(_dot(p[r0:r0 + n * tq], _group(v_ref, rows, vb, width)))
                r0 += n * tq
            pv = pv[0] if len(pv) == 1 else jnp.concatenate(pv, axis=0)
            out += [m_new, acc * jnp.exp2(m - m_new) + pv]
        return tuple(out)

    init = []
    for q in qs:
        init += [jnp.full((q.shape[0], 1), NEG_INF, F32),
                 jnp.zeros((q.shape[0], width * LANES), F32)]
    carry = lax.fori_loop(0, n_chunks, body, tuple(init), unroll=unroll)
    half = width * HALF
    return [carry[2 * j + 1] / pltpu.roll(carry[2 * j + 1], half, 1) for j in range(len(segs))]


_PLANS = {
    "gqa": (1, ((((0, 1, 2, 3), 0, ((2, 0), (2, 1))),),)),
    "mla": (2, ((((0, 1, 2, 3), 0, ((2, 0), (2, 0))),),)),
    "diff": (1, ((((0, 1, 2, 3), 0, ((2, 0), (2, 1))),), (((4, 5, 6, 7), 1, ((2, 2), (2, 3))),))),
}


def dense_kernel(*refs, kind, n_chunks, lambda_init):
    refs = list(refs)
    if kind == "diff":
        lam_ref, g_ref = refs[:2]
        refs = refs[2:]
    elif kind == "mla":
        wv_ref = refs[0]
        refs = refs[1:]
    q_ref, k_ref, v_ref, o_ref = refs
    tq = q_ref.shape[0]
    width, loops = _PLANS[kind]
    res = [_attend(q_ref, segs, k_ref, v_ref, n_chunks, width, unroll=True) for segs in loops]

    def tiles(r):
        return [r[i * tq:(i + 1) * tq] for i in range(r.shape[0] // tq)]

    if kind == "gqa":
        h = [t for r in res[0] for t in tiles(r)]
        blocks = [_halves(h[0], pltpu.roll(h[1], HALF, 1), h[0].shape),
                  _halves(pltpu.roll(h[2], HALF, 1), h[3], h[0].shape)]
    elif kind == "mla":
        h = [r[:, :LANES].astype(BF16) for r in tiles(res[0][0])]
        blocks = [_dot(jnp.concatenate(h[2 * ob:2 * ob + 2], axis=1), wv_ref[ob]) for ob in range(2)]
    else:
        lv = lam_ref[...]
        lam = (jnp.exp(jnp.sum(lv[0:1] * lv[1:2], axis=-1, keepdims=True))
               - jnp.exp(jnp.sum(lv[2:3] * lv[3:4], axis=-1, keepdims=True)) + lambda_init)
        blocks = []
        for (r,) in res:
            a = tiles(r)
            o = _halves(a[0] - lam * a[1], a[2] - lam * a[3], a[0].shape)
            blocks.append(o * _head_rms(o) * g_ref[...] * (1.0 - lambda_init))
    for ob, blk in enumerate(blocks):
        o_ref[:, ob * LANES:(ob + 1) * LANES] = blk.astype(BF16)


def _qkv_specs(lay, latent, s_all, tq):
    (q0, qn), (k0, kn), (v0, vn) = lay
    assert q0 % qn == 0 and k0 % kn == 0 and v0 % vn == 0
    ctx_blk = s_all // CTX_TILE - 1
    if latent:
        q_map = lambda i, t: (i, t, q0 // qn)
        k_map = lambda i, t: (i, 0, k0 // kn)
        v_map = lambda i, t: (i, 0, v0 // vn)
    else:
        q_map = lambda i, t: (i, ctx_blk, q0 // qn)
        k_map = lambda i, t: (i, ctx_blk, k0 // kn)
        v_map = lambda i, t: (i, ctx_blk, v0 // vn)
    kv_rows = s_all if latent else CTX_TILE
    return [
        pl.BlockSpec((None, tq, qn * LANES), q_map),
        pl.BlockSpec((None, kv_rows, kn * LANES), k_map),
        pl.BlockSpec((None, kv_rows, vn * LANES), v_map),
    ]


def dense_attn_call(kind, qkv, latent, extra=None, lambda_init=None):
    b, s_all, _ = qkv.shape
    tq = ATT_TILE if latent else CTX_TILE
    sq = s_all - CTX_TILE if latent else CTX_TILE
    lay = {"diff": ((G_QD, 8), (G_KD, 2), (G_VD, 4)),
           "mla": ((G_QM, 8), (G_KM, 2), (G_VM, 2)),
           "gqa": ((G_QG, 4), (G_KG, 1), (G_VG, 2))}[kind]
    specs = _qkv_specs(lay, latent, s_all, tq)
    args = [qkv, qkv, qkv]
    if kind == "diff":
        specs = [pl.BlockSpec((4, DIFF_QK_DIM), lambda i, t: (0, 0)),
                 pl.BlockSpec((1, LANES), lambda i, t: (0, 0))] + specs
        args = list(extra) + args
    elif kind == "mla":
        specs = [pl.BlockSpec((2, 2 * LANES, LANES), lambda i, t: (0, 0, 0))] + specs
        args = list(extra) + args
    body = functools.partial(dense_kernel, kind=kind, n_chunks=KV_CHUNKS if latent else 1,
                             lambda_init=lambda_init)
    return pl.pallas_call(
        body,
        name=kind + ("_attn" if latent else "_attn_ctx"),
        grid=(b, sq // tq),
        in_specs=specs,
        out_specs=pl.BlockSpec((None, tq, MIX_COLS), lambda i, t: (i, t, 0)),
        out_shape=jax.ShapeDtypeStruct((b, sq, MIX_COLS), BF16),
        compiler_params=pltpu.CompilerParams(
            dimension_semantics=("arbitrary", "arbitrary"), vmem_limit_bytes=VMEM_LIMIT),
    )(*args)


def swa_kernel(sink_ref, q_ref, k_ref, v_ref, o_ref, *, s_lat):
    tq = q_ref.shape[0]
    heads = q_ref.shape[1] // LANES
    rows = heads * tq
    t = pl.program_id(1)
    q = jnp.concatenate([_group(q_ref, slice(None), g) for g in range(heads)], axis=0)
    row = lax.broadcasted_iota(jnp.int32, (rows, 1), 0)
    sink = jnp.zeros((rows, 1), F32)
    for h in range(heads):
        sink = jnp.where(row // tq == h, sink_ref[h] * LOG2E, sink)

    spans = [(slice(s_lat, s_lat + CTX_TILE), None)]
    if s_lat:
        width = tq + 2 * WINDOW
        start = jnp.clip(t * tq - WINDOW, 0, s_lat - width)
        rel = (start - t * tq + lax.broadcasted_iota(jnp.int32, (tq, width), 1)
               - lax.broadcasted_iota(jnp.int32, (tq, width), 0))
        valid = jnp.concatenate([jnp.abs(rel) <= WINDOW] * heads, axis=0)
        spans.append((pl.ds(pl.multiple_of(start, WINDOW), width), valid))

    ss = []
    for span, valid in spans:
        s = _dot_nt(q, _group(k_ref, span, 0))
        ss.append(s if valid is None else jnp.where(valid, s, NEG_INF))
    m = functools.reduce(jnp.maximum, [jnp.max(s, axis=-1, keepdims=True) for s in ss] + [sink])
    ps = [jnp.exp2(s - m).astype(BF16) for s in ss]
    half = rows // 2
    acc = jnp.concatenate(
        [sum(_dot(p[vb * half:(vb + 1) * half], _group(v_ref, span, vb))
             for p, (span, _) in zip(ps, spans)) for vb in range(2)], axis=0)
    res = acc / (pltpu.roll(acc, HALF, 1) + jnp.exp2(sink - m))
    h = [res[i * tq:(i + 1) * tq] for i in range(heads)]
    o_ref[:, :LANES] = _halves(h[0], pltpu.roll(h[1], HALF, 1), h[0].shape).astype(BF16)
    o_ref[:, LANES:] = _halves(pltpu.roll(h[2], HALF, 1), h[3], h[0].shape).astype(BF16)


def swa_call(sink, qkv, latent):
    b, s_all, _ = qkv.shape
    tq = TOK_TILE
    s_lat = s_all - CTX_TILE if latent else 0
    sq = s_lat if latent else CTX_TILE
    specs = [pl.BlockSpec(memory_space=pltpu.SMEM)]
    specs += _qkv_specs(((G_QS, 4), (G_KS, 1), (G_VS, 2)), latent, s_all, tq)
    return pl.pallas_call(
        functools.partial(swa_kernel, s_lat=s_lat),
        name="swa_attn" if latent else "swa_attn_ctx",
        grid=(b, sq // tq),
        in_specs=specs,
        out_specs=pl.BlockSpec((None, tq, MIX_COLS), lambda i, t: (i, t, 0)),
        out_shape=jax.ShapeDtypeStruct((b, sq, MIX_COLS), BF16),
        compiler_params=pltpu.CompilerParams(
            dimension_semantics=("arbitrary", "arbitrary"), vmem_limit_bytes=VMEM_LIMIT),
    )(sink, qkv, qkv, qkv)


def mixout_kernel(x_ref, ya_ref, yb_ref, ym_ref, yd_ref, w_ref, mod_ref, gpost_ref,
                  gfpre_ref, x1_ref, h2_ref, za_ref, zb_ref):
    g = pl.program_id(0)

    @pl.when(g == 0)
    def _():
        zb_ref[...] = jnp.zeros_like(zb_ref)

    def project(dst_ref):
        y = jnp.concatenate([ya_ref[...], yb_ref[...], ym_ref[...], yd_ref[...]], axis=1)
        dst_ref[...] = _dot(y, w_ref[...])

    def finish(z_ref):
        z = z_ref[...]
        x1 = x_ref[...] + mod_ref[2:3, :] * (z * _rms(z, D_MODEL) * gpost_ref[...])
        x1_ref[...] = x1
        h2 = x1 * _rms(x1, D_MODEL) * gfpre_ref[...]
        h2_ref[...] = (h2 * (1.0 + mod_ref[4:5, :]) + mod_ref[3:4, :]).astype(BF16)

    @pl.when(g % 2 == 0)
    def _():
        project(za_ref)
        finish(zb_ref)

    @pl.when(g % 2 == 1)
    def _():
        project(zb_ref)
        finish(za_ref)


def mixout_call(x, ys, mod, mod_row, lw):
    b, s, d = x.shape
    tq = TOK_TILE
    per = s // tq
    n = b * per

    def proj_tile(g):
        ga = jnp.minimum(g, n - 1)
        return ga // per, ga % per, 0

    def done_tile(g):
        gb = jnp.maximum(g - 1, 0)
        return gb // per, gb % per, 0

    if mod_row is None:
        mod_map = lambda g: (done_tile(g)[0], 0, 0)
    else:
        mod_map = lambda g: (mod_row, 0, 0)
    const2 = lambda g: (0, 0)
    y_spec = pl.BlockSpec((None, tq, MIX_COLS), proj_tile)
    return pl.pallas_call(
        mixout_kernel,
        name="mixout",
        grid=(n + 1,),
        in_specs=[pl.BlockSpec((None, tq, d), done_tile), y_spec, y_spec, y_spec, y_spec,
                  pl.BlockSpec((4 * MIX_COLS, d), const2),
                  pl.BlockSpec((None, N_MOD, d), mod_map),
                  pl.BlockSpec((1, d), const2), pl.BlockSpec((1, d), const2)],
        out_specs=[pl.BlockSpec((None, tq, d), done_tile), pl.BlockSpec((None, tq, d), done_tile)],
        out_shape=[jax.ShapeDtypeStruct((b, s, d), F32), jax.ShapeDtypeStruct((b, s, d), BF16)],
        scratch_shapes=[pltpu.VMEM((tq, d), F32), pltpu.VMEM((tq, d), F32)],
        compiler_params=pltpu.CompilerParams(
            dimension_semantics=("arbitrary",), vmem_limit_bytes=VMEM_LIMIT),
    )(x, *ys, lw["w_out"], mod, lw["g_mix_post"], lw["g_ffn_pre"])


def ffn_kernel(x1_ref, hc_ref, hp_ref, hn_ref, wup_ref, cw_ref, cb_ref, wdn_ref, mod_ref,
               gpost_ref, o_ref, g_ref):
    t = pl.program_id(1)
    tq = hc_ref.shape[0]
    halo = hp_ref.shape[0]
    hp = jnp.where(t > 0, hp_ref[...], jnp.zeros_like(hp_ref))
    hn = jnp.where(t < pl.num_programs(1) - 1, hn_ref[...], jnp.zeros_like(hn_ref))
    hext = jnp.concatenate([hp, hc_ref[...], hn], axis=0)
    rows = hext.shape[0]

    def conv_cols(cols):
        u = _dot(hext, wup_ref[:, cols])
        w = cw_ref[:, cols]
        return (pltpu.roll(u, 1, 0)[halo:halo + tq] * w[0:1]
                + u[halo:halo + tq] * w[1:2]
                + pltpu.roll(u, rows - 1, 0)[halo:halo + tq] * w[2:3] + cb_ref[:, cols])

    for j in range(FFN_DIM // FFN_CHUNK):
        a = conv_cols(slice(j * FFN_CHUNK, (j + 1) * FFN_CHUNK))
        v = conv_cols(slice(FFN_DIM + j * FFN_CHUNK, FFN_DIM + (j + 1) * FFN_CHUNK))
        g_ref[:, j * FFN_CHUNK:(j + 1) * FFN_CHUNK] = (a * (1.0 / (1.0 + jnp.exp(-a))) * v).astype(BF16)

    f = _dot(g_ref[...], wdn_ref[...])
    o_ref[...] = x1_ref[...] + mod_ref[5:6, :] * (f * _rms(f, D_MODEL) * gpost_ref[...])


def ffn_call(x1, h2, mod, mod_row, lw):
    b, s, d = x1.shape
    tq = min(FFN_TILE, s)
    halo = BF16_SUBLANES
    per_tile = tq // halo
    last = s // halo - 1
    if mod_row is None:
        mod_map = lambda i, t: (i, 0, 0)
    else:
        mod_map = lambda i, t: (mod_row, 0, 0)
    const2 = lambda i, t: (0, 0)
    tok = lambda i, t: (i, t, 0)
    return pl.pallas_call(
        ffn_kernel,
        name="conv_ffn",
        grid=(b, s // tq),
        in_specs=[
            pl.BlockSpec((None, tq, d), tok),
            pl.BlockSpec((None, tq, d), tok),
            pl.BlockSpec((None, halo, d), lambda i, t: (i, jnp.maximum(t * per_tile - 1, 0), 0)),
            pl.BlockSpec((None, halo, d), lambda i, t: (i, jnp.minimum((t + 1) * per_tile, last), 0)),
            pl.BlockSpec((d, 2 * FFN_DIM), const2, pipeline_mode=pl.Buffered(1)),
            pl.BlockSpec((3, 2 * FFN_DIM), const2),
            pl.BlockSpec((1, 2 * FFN_DIM), const2),
            pl.BlockSpec((FFN_DIM, d), const2, pipeline_mode=pl.Buffered(1)),
            pl.BlockSpec((None, N_MOD, d), mod_map),
            pl.BlockSpec((1, d), const2),
        ],
        out_specs=pl.BlockSpec((None, tq, d), tok),
        out_shape=jax.ShapeDtypeStruct((b, s, d), F32),
        scratch_shapes=[pltpu.VMEM((tq, FFN_DIM), BF16)],
        compiler_params=pltpu.CompilerParams(
            dimension_semantics=("arbitrary", "arbitrary"), vmem_limit_bytes=VMEM_LIMIT),
    )(x1, h2, h2, h2, lw["w_up"], lw["conv_w"], lw["conv_b"], lw["w_down"], mod,
      lw["g_ffn_post"])


def _rope_table(n_ctx, s, rot_dim, lane_lo, lane_hi, period):
    n = rot_dim // 4
    tok = np.arange(s)
    pos = np.stack([tok // GRID_W, tok % GRID_W], axis=1).astype(np.float32)
    lane = np.arange(LANES)
    o = (lane - lane_lo) % period
    axis = o // (2 * n)
    second = (o % (2 * n)) // n
    active = (lane >= lane_lo) & (lane < lane_hi)
    inv_freq = ROPE_THETA ** (-jnp.arange(n, dtype=F32) / n)
    ang = jnp.asarray(pos)[:, axis] * inv_freq[o % n][None, :]
    cos = jnp.where(active[None], jnp.cos(ang), 1.0)
    sin = jnp.where(active[None], jnp.sin(ang), 0.0)
    sa = jnp.where((second == 0)[None], -sin, 0.0)
    sb = jnp.where((second == 1)[None], sin, 0.0)
    tab = jnp.stack([cos, sa, sb]).astype(F32)
    ident = jnp.stack([jnp.ones((n_ctx, LANES), F32), jnp.zeros((n_ctx, LANES), F32),
                       jnp.zeros((n_ctx, LANES), F32)])
    return jnp.concatenate([tab, ident], axis=1)


def _layer_weights(l, w):
    d = D_MODEL
    w_in = w["w_in"][l]
    mla0 = 768 + 512
    cq_hi = mla0 + MLA_Q_RANK
    kr0 = cq_hi + MLA_KV_RANK
    w_in_p = jnp.concatenate([
        w_in[:, :mla0 + LANES],
        w_in[:, mla0 + LANES:cq_hi], w_in[:, kr0:kr0 + MLA_ROPE_DIM],
        jnp.zeros((d, LANES - HALF - MLA_ROPE_DIM), F32),
        w_in[:, cq_hi:kr0],
        w_in[:, kr0 + MLA_ROPE_DIM:],
    ], axis=1)
    dqk = MLA_NOPE_DIM + MLA_ROPE_DIM
    wq = jnp.zeros((2 * LANES, MLA_HEADS, LANES), F32)
    wq = wq.at[:MLA_Q_RANK, :, :dqk].set(w["mla_w_uq"][l].reshape(MLA_Q_RANK, MLA_HEADS, dqk))
    assert MLA_KV_RANK == LANES
    wukv = w["mla_w_ukv"][l].reshape(MLA_KV_RANK, MLA_HEADS, MLA_NOPE_DIM + MLA_V_DIM)
    wkt = jnp.zeros((MLA_HEADS, LANES, LANES), F32)
    wkt = wkt.at[:, :MLA_NOPE_DIM, :].set(jnp.transpose(wukv[:, :, :MLA_NOPE_DIM], (1, 2, 0)))
    wuv = jnp.transpose(wukv[:, :, MLA_NOPE_DIM:], (1, 0, 2))
    wv = jnp.zeros((2, 2, LANES, 2, MLA_V_DIM), F32)
    for hd in range(MLA_HEADS):
        wv = wv.at[hd // 2, hd % 2, :, hd % 2, :].set(wuv[hd])
    wv = wv.reshape(2, 2 * LANES, LANES)
    gq = jnp.zeros((1, 2 * LANES), F32).at[0, :MLA_Q_RANK].set(w["mla_g_q"][l])
    return {
        "g_mix_pre": w["g_mix_pre"][l][None], "g_mix_post": w["g_mix_post"][l][None],
        "g_ffn_pre": w["g_ffn_pre"][l][None], "g_ffn_post": w["g_ffn_post"][l][None],
        "w_in": w_in_p.astype(BF16),
        "wq": wq.reshape(2 * LANES, -1).astype(BF16), "wkt": wkt.astype(BF16),
        "wv": wv.astype(BF16),
        "gq": gq, "gkv": w["mla_g_kv"][l][None],
        "gnq": jnp.tile(w["gqa_g_q"][l], 2)[None], "gnk": jnp.tile(w["gqa_g_k"][l], 2)[None],
        "lam": jnp.stack([w["diff_lam_q1"][l], w["diff_lam_k1"][l],
                          w["diff_lam_q2"][l], w["diff_lam_k2"][l]]),
        "g_sub": jnp.tile(w["diff_g_sub"][l], 2)[None],
        "sink": w["swa_sink"][l],
        "w_out": w["w_out"][l].astype(BF16),
        "w_up": w["ffn_w_up"][l].astype(BF16),
        "conv_w": w["ffn_conv_w"][l],
        "conv_b": w["ffn_conv_b"][l][None],
        "w_down": w["ffn_w_down"][l].astype(BF16),
    }


def kernel(x, c, ctx, c_ctx, w_mod, b_mod, g_mix_pre, g_mix_post, g_ffn_pre, g_ffn_post, w_in, diff_lam_q1, diff_lam_k1, diff_lam_q2, diff_lam_k2, diff_g_sub, swa_sink, mla_g_q, mla_g_kv, mla_w_uq, mla_w_ukv, gqa_g_q, gqa_g_k, w_out, ffn_w_up, ffn_conv_w, ffn_conv_b, ffn_w_down):
    w = dict(g_mix_pre=g_mix_pre, g_mix_post=g_mix_post, g_ffn_pre=g_ffn_pre,
             g_ffn_post=g_ffn_post, w_in=w_in, diff_lam_q1=diff_lam_q1, diff_lam_k1=diff_lam_k1,
             diff_lam_q2=diff_lam_q2, diff_lam_k2=diff_lam_k2, diff_g_sub=diff_g_sub,
             swa_sink=swa_sink, mla_g_q=mla_g_q, mla_g_kv=mla_g_kv, mla_w_uq=mla_w_uq,
             mla_w_ukv=mla_w_ukv, gqa_g_q=gqa_g_q, gqa_g_k=gqa_g_k, w_out=w_out,
             ffn_w_up=ffn_w_up, ffn_conv_w=ffn_conv_w, ffn_conv_b=ffn_conv_b,
             ffn_w_down=ffn_w_down)
    b, s, d = x.shape
    depth = w_mod.shape[0]
    ctx_row = b
    mod_rows = 16
    cc = jnp.zeros((mod_rows, d), F32).at[:b].set(c).at[ctx_row].set(c_ctx)
    mod_all = mod_call(cc, w_mod, b_mod).reshape(depth, mod_rows, N_MOD, d)

    n_ctx = ctx.shape[1]
    tabs = (_rope_table(n_ctx, s, DIFF_QK_DIM, 0, LANES, DIFF_QK_DIM),
            _rope_table(n_ctx, s, GQA_HEAD_DIM, 0, LANES, GQA_HEAD_DIM),
            _rope_table(n_ctx, s, MLA_ROPE_DIM, MLA_NOPE_DIM, MLA_NOPE_DIM + MLA_ROPE_DIM,
                        MLA_ROPE_DIM))

    xc = ctx
    for l in range(depth):
        need_ctx = l < depth - 1
        lambda_init = 0.8 - 0.6 * math.exp(-0.3 * l)
        lw = _layer_weights(l, w)
        mod = mod_all[l]
        qkv = proj_call(xc, x, mod, ctx_row, lw, tabs)
        diff_extra = (lw["lam"], lw["g_sub"])
        mla_extra = (lw["wv"],)
        ys = (dense_attn_call("diff", qkv, True, diff_extra, lambda_init),
              swa_call(lw["sink"], qkv, True),
              dense_attn_call("mla", qkv, True, mla_extra),
              dense_attn_call("gqa", qkv, True))
        x1, h2 = mixout_call(x, ys, mod, None, lw)
        x = ffn_call(x1, h2, mod, None, lw)
        if need_ctx:
            ysc = (dense_attn_call("diff", qkv, False, diff_extra, lambda_init),
                   swa_call(lw["sink"], qkv, False),
                   dense_attn_call("mla", qkv, False, mla_extra),
                   dense_attn_call("gqa", qkv, False))
            xc1, hc2 = mixout_call(xc, ysc, mod, ctx_row, lw)
            xc = ffn_call(xc1, hc2, mod, ctx_row, lw)
    return x
```

```python
import functools
import math

import jax
import jax.numpy as jnp
import numpy as np
from jax import lax
from jax.experimental import pallas as pl
from jax.experimental.pallas import tpu as pltpu

F32 = jnp.float32
BF16 = jnp.bfloat16

D_MODEL = 1024
GRID_W = 64
ROPE_THETA = 10000.0
NORM_EPS = 1e-6
NEG_INF = -1e30
N_MOD = 6

DIFF_HEADS, DIFF_QK_DIM, DIFF_V_DIM = 4, 32, 64
SWA_HEAD_DIM, WINDOW = 64, 128
MLA_Q_RANK, MLA_KV_RANK = 192, 128
MLA_NOPE_DIM, MLA_ROPE_DIM, MLA_V_DIM, MLA_HEADS = 64, 32, 64, 4
GQA_HEAD_DIM = 64
FFN_DIM = 2816
MIX_COLS = 256

LANES = 128
BF16_SUBLANES = 16
HALF = LANES // 2
VMEM_LIMIT = 56 * 1024 * 1024

P_COLS = 17 * LANES
G_QM, G_VD, G_QG, G_QS = 0, 8, 12, 16
G_QD, G_KD, G_VG, G_KM, G_VM, G_VS, G_KG, G_KS = 20, 22, 24, 26, 28, 30, 32, 33
QKV_COLS = 34 * LANES
LOG2E = math.log2(math.e)

TOK_TILE = 256
CTX_TILE = 256
ATT_TILE = 512
KV_CHUNKS = 2
MIX_TILE = 512
FFN_TILE = 1024
FFN_CHUNK = 256


def _rms(x, n):
    return lax.rsqrt(jnp.sum(x * x, axis=-1, keepdims=True) * (1.0 / n) + NORM_EPS)


def _lane(shape):
    return lax.broadcasted_iota(jnp.int32, shape, len(shape) - 1)


def _dot(a, b):
    return jnp.dot(a, b, preferred_element_type=F32)


def _dot_nt(a, b):
    return lax.dot_general(a, b, (((1,), (1,)), ((), ())), preferred_element_type=F32)


def mod_kernel(c_ref, w_ref, b_ref, o_ref):
    c = c_ref[...]
    s = c * (1.0 / (1.0 + jnp.exp(-c)))
    o_ref[...] = jnp.dot(s, w_ref[...], preferred_element_type=F32,
                         precision=lax.Precision.HIGHEST) + b_ref[...]


def mod_call(cc, w_mod, b_mod):
    depth, d, n = w_mod.shape
    rows = cc.shape[0]
    tn = 1024
    return pl.pallas_call(
        mod_kernel,
        name="mod_vectors",
        grid=(depth, n // tn),
        in_specs=[
            pl.BlockSpec((rows, d), lambda l, j: (0, 0)),
            pl.BlockSpec((None, d, tn), lambda l, j: (l, 0, j)),
            pl.BlockSpec((None, 1, tn), lambda l, j: (l, 0, j)),
        ],
        out_specs=pl.BlockSpec((None, rows, tn), lambda l, j: (l, 0, j)),
        out_shape=jax.ShapeDtypeStruct((depth, rows, n), F32),
        compiler_params=pltpu.CompilerParams(
            dimension_semantics=("arbitrary", "arbitrary"), vmem_limit_bytes=VMEM_LIMIT),
    )(cc, w_mod, b_mod.reshape(depth, 1, n))


def _rope(x, tab_ref, n):
    return (x * tab_ref[0] + pltpu.roll(x, LANES - n, 1) * tab_ref[1]
            + pltpu.roll(x, n, 1) * tab_ref[2])


def _halves(lo, hi, shape):
    return jnp.where(_lane(shape) < HALF, lo, hi)


def _head_rms(x):
    low = _lane(x.shape) < HALF
    sq = x * x
    ss_lo = jnp.sum(jnp.where(low, sq, 0.0), axis=-1, keepdims=True)
    ss_hi = jnp.sum(jnp.where(low, 0.0, sq), axis=-1, keepdims=True)
    r_lo = lax.rsqrt(ss_lo * (1.0 / HALF) + NORM_EPS)
    r_hi = lax.rsqrt(ss_hi * (1.0 / HALF) + NORM_EPS)
    return jnp.where(low, r_lo, r_hi)


def proj_kernel(xc_ref, x_ref, mod_ref, gpre_ref, win_ref, wq_ref, wkt_ref, gq_ref, gkv_ref,
                gnq_ref, gnk_ref, t32_ref, t64_ref, tml_ref, o_ref, pa_ref, pb_ref, *,
                tiles_per_sample, n_tiles):
    g = pl.program_id(0)
    is_ctx = jnp.minimum(g, n_tiles - 1) % tiles_per_sample == tiles_per_sample - 1

    @pl.when(g == 0)
    def _():
        pb_ref[...] = jnp.zeros_like(pb_ref)

    def project(dst_ref):
        x = jnp.where(is_ctx, xc_ref[...], x_ref[...])
        h = x * _rms(x, D_MODEL) * gpre_ref[...]
        h = h * (1.0 + mod_ref[1:2, :]) + mod_ref[0:1, :]
        dst_ref[...] = _dot(h.astype(BF16), win_ref[...])

    finish = functools.partial(_proj_finish, wq_ref, wkt_ref, gq_ref, gkv_ref, gnq_ref, gnk_ref,
                               t32_ref, t64_ref, tml_ref, o_ref)

    @pl.when(g % 2 == 0)
    def _():
        project(pa_ref)
        finish(pb_ref)

    @pl.when(g % 2 == 1)
    def _():
        project(pb_ref)
        finish(pa_ref)


def _proj_finish(wq_ref, wkt_ref, gq_ref, gkv_ref, gnq_ref, gnk_ref, t32_ref, t64_ref, tml_ref,
                 o_ref, p_ref):
    def grp(g):
        return p_ref[:, g * LANES:(g + 1) * LANES]

    def put(g, v):
        o_ref[:, g * LANES:(g + 1) * LANES] = v.astype(BF16)

    def rope32(v):
        return _rope(v, t32_ref, DIFF_QK_DIM // 4)

    def rope64(v):
        return _rope(v, t64_ref, GQA_HEAD_DIM // 4)

    def ropeml(v):
        return _rope(v, tml_ref, MLA_ROPE_DIM // 4)

    lane = _lane((p_ref.shape[0], LANES))
    low = lane < HALF

    def with_ones(v):
        return jnp.where(low, v, 1.0), jnp.where(low, 1.0, v)

    for g in range(2):
        put(G_QD + g, rope32(grp(g)) * (DIFF_QK_DIM ** -0.5 * LOG2E))
        put(G_KD + g, rope32(grp(2 + g)))
        v_lo, v_hi = with_ones(grp(4 + g))
        put(G_VD + 2 * g, v_lo)
        put(G_VD + 2 * g + 1, v_hi)

    def put_gqa_queries(g0, q01, q23):
        put(g0, jnp.where(low, q01, 0.0))
        put(g0 + 1, jnp.where(low, pltpu.roll(q01, HALF, 1), 0.0))
        put(g0 + 2, jnp.where(low, 0.0, pltpu.roll(q23, HALF, 1)))
        put(g0 + 3, jnp.where(low, 0.0, q23))

    scale_s = SWA_HEAD_DIM ** -0.5 * LOG2E
    put_gqa_queries(G_QS, rope64(grp(6)) * scale_s, rope64(grp(7)) * scale_s)
    put(G_KS, rope64(grp(8)))
    v_lo, v_hi = with_ones(grp(9))
    put(G_VS, v_lo)
    put(G_VS + 1, v_hi)

    cq0, cq1, ckv = grp(10), grp(11), grp(12)
    ss = (jnp.sum(cq0 * cq0, axis=-1, keepdims=True)
          + jnp.sum(jnp.where(low, cq1 * cq1, 0.0), axis=-1, keepdims=True))
    rq = lax.rsqrt(ss * (1.0 / MLA_Q_RANK) + NORM_EPS)
    cqn = jnp.concatenate([cq0 * rq * gq_ref[:, :LANES], cq1 * rq * gq_ref[:, LANES:]], axis=1)
    qm = _dot(cqn.astype(BF16), wq_ref[...])
    scale_m = (MLA_NOPE_DIM + MLA_ROPE_DIM) ** -0.5 * LOG2E
    rope_lanes = (lane >= MLA_NOPE_DIM) & (lane < MLA_NOPE_DIM + MLA_ROPE_DIM)
    for hd in range(MLA_HEADS):
        qh = ropeml(qm[:, hd * LANES:(hd + 1) * LANES]) * scale_m
        put(G_QM + 2 * hd, _dot(qh.astype(BF16), wkt_ref[hd]))
        put(G_QM + 2 * hd + 1, jnp.where(rope_lanes, qh, 0.0))
    ckvn = ckv * _rms(ckv, MLA_KV_RANK) * gkv_ref[...]
    put(G_KM, ckvn)
    put(G_KM + 1, jnp.where(rope_lanes, ropeml(cq1), 0.0))
    put(G_VM, ckvn)
    put(G_VM + 1, jnp.ones_like(ckvn))

    scale_g = GQA_HEAD_DIM ** -0.5 * LOG2E
    q01 = grp(13)
    q01 = rope64(q01 * _head_rms(q01) * gnq_ref[...]) * scale_g
    q23 = grp(14)
    q23 = rope64(q23 * _head_rms(q23) * gnq_ref[...]) * scale_g
    put_gqa_queries(G_QG, q01, q23)
    kg = grp(15)
    put(G_KG, rope64(kg * _head_rms(kg) * gnk_ref[...]))
    v_lo, v_hi = with_ones(grp(16))
    put(G_VG, v_lo)
    put(G_VG + 1, v_hi)


def proj_call(xc, x, mod, ctx_row, lw, tabs):
    b, s, d = x.shape
    tq = TOK_TILE
    assert xc.shape[1] == CTX_TILE == tq
    s_all = CTX_TILE + s
    per = s_all // tq
    n = b * per
    const2 = lambda g: (0, 0)

    def proj_tile(g):
        ga = jnp.minimum(g, n - 1)
        return ga // per, ga % per

    def done_tile(g):
        gb = jnp.maximum(g - 1, 0)
        return gb // per, gb % per

    def x_map(g):
        i, t = proj_tile(g)
        return i, jnp.minimum(t, per - 2), 0

    def mod_map(g):
        i, t = proj_tile(g)
        return jnp.where(t == per - 1, ctx_row, i), 0, 0

    tab_spec = pl.BlockSpec((3, tq, LANES), lambda g: (0, done_tile(g)[1], 0))
    return pl.pallas_call(
        functools.partial(proj_kernel, tiles_per_sample=per, n_tiles=n),
        name="proj",
        grid=(n + 1,),
        in_specs=[
            pl.BlockSpec((None, tq, d), lambda g: (proj_tile(g)[0], 0, 0)),
            pl.BlockSpec((None, tq, d), x_map),
            pl.BlockSpec((None, N_MOD, d), mod_map),
            pl.BlockSpec((1, d), const2),
            pl.BlockSpec((d, P_COLS), const2),
            pl.BlockSpec((2 * LANES, 4 * LANES), const2),
            pl.BlockSpec((MLA_HEADS, LANES, LANES), lambda g: (0, 0, 0)),
            pl.BlockSpec((1, 2 * LANES), const2),
            pl.BlockSpec((1, LANES), const2),
            pl.BlockSpec((1, LANES), const2),
            pl.BlockSpec((1, LANES), const2),
            tab_spec, tab_spec, tab_spec,
        ],
        out_specs=pl.BlockSpec((None, tq, QKV_COLS), lambda g: done_tile(g) + (0,)),
        out_shape=jax.ShapeDtypeStruct((b, s_all, QKV_COLS), BF16),
        scratch_shapes=[pltpu.VMEM((tq, P_COLS), F32), pltpu.VMEM((tq, P_COLS), F32)],
        compiler_params=pltpu.CompilerParams(
            dimension_semantics=("arbitrary",), vmem_limit_bytes=VMEM_LIMIT),
    )(xc, x, mod, lw["g_mix_pre"], lw["w_in"], lw["wq"], lw["wkt"], lw["gq"], lw["gkv"],
      lw["gnq"], lw["gnk"], tabs[0], tabs[1], tabs[2])


def _group(ref, rows, g, width=1):
    return ref[rows, g * width * LANES:(g + 1) * width * LANES]


def _attend(q_ref, segs, k_ref, v_ref, n_chunks, width, unroll=False):
    chunk = k_ref.shape[0] // n_chunks
    def query(g):
        if isinstance(g, int):
            return _group(q_ref, slice(None), g, width)
        g, lo, hi = g
        q = _group(q_ref, slice(None), g, width)
        lane = _lane(q.shape)
        return jnp.where((lane >= lo) & (lane < hi), q, jnp.zeros_like(q))

    qs = [jnp.concatenate([query(g) for g in qg], axis=0) for qg, _, _ in segs]

    def body(i, carry):
        rows = pl.ds(pl.multiple_of(i * chunk, LANES), chunk)
        out = []
        for j, (q, (_, kb, parts)) in enumerate(zip(qs, segs)):
            m, acc = carry[2 * j], carry[2 * j + 1]
            s = _dot_nt(q, _group(k_ref, rows, kb, width))
            m_new = jnp.maximum(m, jnp.max(s, axis=-1, keepdims=True))
            p = jnp.exp2(s - m_new).astype(BF16)
            tq = q.shape[0] // sum(n for n, _ in parts)
            pv, r0 = [], 0
            for n, vb in parts:
                pv.append(_dot(p[r0:r0 + n * tq], _group(v_ref, rows, vb, width)))
                r0 += n * tq
            pv = pv[0] if len(pv) == 1 else jnp.concatenate(pv, axis=0)
            out += [m_new, acc * jnp.exp2(m - m_new) + pv]
        return tuple(out)

    init = []
    for q in qs:
        init += [jnp.full((q.shape[0], 1), NEG_INF, F32),
                 jnp.zeros((q.shape[0], width * LANES), F32)]
    carry = lax.fori_loop(0, n_chunks, body, tuple(init), unroll=unroll)
    half = width * HALF
    return [carry[2 * j + 1] / pltpu.roll(carry[2 * j + 1], half, 1) for j in range(len(segs))]


_PLANS = {
    "gqa": (1, ((((0, 1, 2, 3), 0, ((2, 0), (2, 1))),),)),
    "mla": (2, ((((0, 1, 2, 3), 0, ((2, 0), (2, 0))),),)),
    "diff": (1, tuple(
        ((tuple((ob, i * DIFF_QK_DIM, (i + 1) * DIFF_QK_DIM) for i in range(4)), ob,
          ((2, 2 * ob), (2, 2 * ob + 1))),) for ob in range(2))),
}


def dense_kernel(*refs, kind, n_chunks, lambda_init):
    refs = list(refs)
    if kind == "diff":
        lam_ref, g_ref = refs[:2]
        refs = refs[2:]
    elif kind == "mla":
        wv_ref = refs[0]
        refs = refs[1:]
    q_ref, k_ref, v_ref, o_ref = refs
    tq = q_ref.shape[0]
    width, loops = _PLANS[kind]
    res = [_attend(q_ref, segs, k_ref, v_ref, n_chunks, width, unroll=True) for segs in loops]

    def tiles(r):
        return [r[i * tq:(i + 1) * tq] for i in range(r.shape[0] // tq)]

    if kind == "gqa":
        h = [t for r in res[0] for t in tiles(r)]
        blocks = [_halves(h[0], pltpu.roll(h[1], HALF, 1), h[0].shape),
                  _halves(pltpu.roll(h[2], HALF, 1), h[3], h[0].shape)]
    elif kind == "mla":
        h = [r[:, :LANES].astype(BF16) for r in tiles(res[0][0])]
        blocks = [_dot(jnp.concatenate(h[2 * ob:2 * ob + 2], axis=1), wv_ref[ob]) for ob in range(2)]
    else:
        lv = lam_ref[...]
        lam = (jnp.exp(jnp.sum(lv[0:1] * lv[1:2], axis=-1, keepdims=True))
               - jnp.exp(jnp.sum(lv[2:3] * lv[3:4], axis=-1, keepdims=True)) + lambda_init)
        blocks = []
        for (r,) in res:
            a = tiles(r)
            o = _halves(a[0] - lam * a[1], a[2] - lam * a[3], a[0].shape)
            blocks.append(o * _head_rms(o) * g_ref[...] * (1.0 - lambda_init))
    for ob, blk in enumerate(blocks):
        o_ref[:, ob * LANES:(ob + 1) * LANES] = blk.astype(BF16)


def _qkv_specs(lay, latent, s_all, tq):
    (q0, qn), (k0, kn), (v0, vn) = lay
    assert q0 % qn == 0 and k0 % kn == 0 and v0 % vn == 0
    ctx_blk = s_all // CTX_TILE - 1
    if latent:
        q_map = lambda i, t: (i, t, q0 // qn)
        k_map = lambda i, t: (i, 0, k0 // kn)
        v_map = lambda i, t: (i, 0, v0 // vn)
    else:
        q_map = lambda i, t: (i, ctx_blk, q0 // qn)
        k_map = lambda i, t: (i, ctx_blk, k0 // kn)
        v_map = lambda i, t: (i, ctx_blk, v0 // vn)
    kv_rows = s_all if latent else CTX_TILE
    return [
        pl.BlockSpec((None, tq, qn * LANES), q_map),
        pl.BlockSpec((None, kv_rows, kn * LANES), k_map),
        pl.BlockSpec((None, kv_rows, vn * LANES), v_map),
    ]


def dense_attn_call(kind, qkv, latent, extra=None, lambda_init=None):
    b, s_all, _ = qkv.shape
    tq = ATT_TILE if latent else CTX_TILE
    sq = s_all - CTX_TILE if latent else CTX_TILE
    lay = {"diff": ((G_QD, 2), (G_KD, 2), (G_VD, 4)),
           "mla": ((G_QM, 8), (G_KM, 2), (G_VM, 2)),
           "gqa": ((G_QG, 4), (G_KG, 1), (G_VG, 2))}[kind]
    specs = _qkv_specs(lay, latent, s_all, tq)
    args = [qkv, qkv, qkv]
    if kind == "diff":
        specs = [pl.BlockSpec((4, DIFF_QK_DIM), lambda i, t: (0, 0)),
                 pl.BlockSpec((1, LANES), lambda i, t: (0, 0))] + specs
        args = list(extra) + args
    elif kind == "mla":
        specs = [pl.BlockSpec((2, 2 * LANES, LANES), lambda i, t: (0, 0, 0))] + specs
        args = list(extra) + args
    body = functools.partial(dense_kernel, kind=kind, n_chunks=KV_CHUNKS if latent else 1,
                             lambda_init=lambda_init)
    return pl.pallas_call(
        body,
        name=kind + ("_attn" if latent else "_attn_ctx"),
        grid=(b, sq // tq),
        in_specs=specs,
        out_specs=pl.BlockSpec((None, tq, MIX_COLS), lambda i, t: (i, t, 0)),
        out_shape=jax.ShapeDtypeStruct((b, sq, MIX_COLS), BF16),
        compiler_params=pltpu.CompilerParams(
            dimension_semantics=("arbitrary", "arbitrary"), vmem_limit_bytes=VMEM_LIMIT),
    )(*args)


def swa_kernel(sink_ref, q_ref, k_ref, v_ref, o_ref, *, s_lat):
    tq = q_ref.shape[0]
    heads = q_ref.shape[1] // LANES
    rows = heads * tq
    t = pl.program_id(1)
    q = jnp.concatenate([_group(q_ref, slice(None), g) for g in range(heads)], axis=0)
    row = lax.broadcasted_iota(jnp.int32, (rows, 1), 0)
    sink = jnp.zeros((rows, 1), F32)
    for h in range(heads):
        sink = jnp.where(row // tq == h, sink_ref[h] * LOG2E, sink)

    spans = [(slice(s_lat, s_lat + CTX_TILE), None)]
    if s_lat:
        width = tq + 2 * WINDOW
        start = jnp.clip(t * tq - WINDOW, 0, s_lat - width)
        rel = (start - t * tq + lax.broadcasted_iota(jnp.int32, (tq, width), 1)
               - lax.broadcasted_iota(jnp.int32, (tq, width), 0))
        valid = jnp.concatenate([jnp.abs(rel) <= WINDOW] * heads, axis=0)
        spans.append((pl.ds(pl.multiple_of(start, WINDOW), width), valid))

    ss = []
    for span, valid in spans:
        s = _dot_nt(q, _group(k_ref, span, 0))
        ss.append(s if valid is None else jnp.where(valid, s, NEG_INF))
    m = functools.reduce(jnp.maximum, [jnp.max(s, axis=-1, keepdims=True) for s in ss] + [sink])
    ps = [jnp.exp2(s - m).astype(BF16) for s in ss]
    half = rows // 2
    acc = jnp.concatenate(
        [sum(_dot(p[vb * half:(vb + 1) * half], _group(v_ref, span, vb))
             for p, (span, _) in zip(ps, spans)) for vb in range(2)], axis=0)
    res = acc / (pltpu.roll(acc, HALF, 1) + jnp.exp2(sink - m))
    h = [res[i * tq:(i + 1) * tq] for i in range(heads)]
    o_ref[:, :LANES] = _halves(h[0], pltpu.roll(h[1], HALF, 1), h[0].shape).astype(BF16)
    o_ref[:, LANES:] = _halves(pltpu.roll(h[2], HALF, 1), h[3], h[0].shape).astype(BF16)


def swa_call(sink, qkv, latent):
    b, s_all, _ = qkv.shape
    tq = TOK_TILE
    s_lat = s_all - CTX_TILE if latent else 0
    sq = s_lat if latent else CTX_TILE
    specs = [pl.BlockSpec(memory_space=pltpu.SMEM)]
    specs += _qkv_specs(((G_QS, 4), (G_KS, 1), (G_VS, 2)), latent, s_all, tq)
    return pl.pallas_call(
        functools.partial(swa_kernel, s_lat=s_lat),
        name="swa_attn" if latent else "swa_attn_ctx",
        grid=(b, sq // tq),
        in_specs=specs,
        out_specs=pl.BlockSpec((None, tq, MIX_COLS), lambda i, t: (i, t, 0)),
        out_shape=jax.ShapeDtypeStruct((b, sq, MIX_COLS), BF16),
        compiler_params=pltpu.CompilerParams(
            dimension_semantics=("arbitrary", "arbitrary"), vmem_limit_bytes=VMEM_LIMIT),
    )(sink, qkv, qkv, qkv)


def mixout_kernel(x_ref, ya_ref, yb_ref, ym_ref, yd_ref, w_ref, mod_ref, gpost_ref,
                  gfpre_ref, x1_ref, h2_ref, za_ref, zb_ref):
    g = pl.program_id(0)

    @pl.when(g == 0)
    def _():
        zb_ref[...] = jnp.zeros_like(zb_ref)

    def project(dst_ref):
        y = jnp.concatenate([ya_ref[...], yb_ref[...], ym_ref[...], yd_ref[...]], axis=1)
        dst_ref[...] = _dot(y, w_ref[...])

    def finish(z_ref):
        z = z_ref[...]
        x1 = x_ref[...] + mod_ref[2:3, :] * (z * _rms(z, D_MODEL) * gpost_ref[...])
        x1_ref[...] = x1
        h2 = x1 * _rms(x1, D_MODEL) * gfpre_ref[...]
        h2_ref[...] = (h2 * (1.0 + mod_ref[4:5, :]) + mod_ref[3:4, :]).astype(BF16)

    @pl.when(g % 2 == 0)
    def _():
        project(za_ref)
        finish(zb_ref)

    @pl.when(g % 2 == 1)
    def _():
        project(zb_ref)
        finish(za_ref)


def mixout_call(x, ys, mod, mod_row, lw):
    b, s, d = x.shape
    tq = min(MIX_TILE, s)
    per = s // tq
    n = b * per

    def proj_tile(g):
        ga = jnp.minimum(g, n - 1)
        return ga // per, ga % per, 0

    def done_tile(g):
        gb = jnp.maximum(g - 1, 0)
        return gb // per, gb % per, 0

    if mod_row is None:
        mod_map = lambda g: (done_tile(g)[0], 0, 0)
    else:
        mod_map = lambda g: (mod_row, 0, 0)
    const2 = lambda g: (0, 0)
    y_spec = pl.BlockSpec((None, tq, MIX_COLS), proj_tile)
    return pl.pallas_call(
        mixout_kernel,
        name="mixout",
        grid=(n + 1,),
        in_specs=[pl.BlockSpec((None, tq, d), done_tile), y_spec, y_spec, y_spec, y_spec,
                  pl.BlockSpec((4 * MIX_COLS, d), const2),
                  pl.BlockSpec((None, N_MOD, d), mod_map),
                  pl.BlockSpec((1, d), const2), pl.BlockSpec((1, d), const2)],
        out_specs=[pl.BlockSpec((None, tq, d), done_tile), pl.BlockSpec((None, tq, d), done_tile)],
        out_shape=[jax.ShapeDtypeStruct((b, s, d), F32), jax.ShapeDtypeStruct((b, s, d), BF16)],
        scratch_shapes=[pltpu.VMEM((tq, d), F32), pltpu.VMEM((tq, d), F32)],
        compiler_params=pltpu.CompilerParams(
            dimension_semantics=("arbitrary",), vmem_limit_bytes=VMEM_LIMIT),
    )(x, *ys, lw["w_out"], mod, lw["g_mix_post"], lw["g_ffn_pre"])


def ffn_kernel(x1_ref, hc_ref, hp_ref, hn_ref, wup_ref, cw_ref, cb_ref, wdn_ref, mod_ref,
               gpost_ref, o_ref, g_ref):
    t = pl.program_id(1)
    tq = hc_ref.shape[0]
    halo = hp_ref.shape[0]
    hp = jnp.where(t > 0, hp_ref[...], jnp.zeros_like(hp_ref))
    hn = jnp.where(t < pl.num_programs(1) - 1, hn_ref[...], jnp.zeros_like(hn_ref))
    hext = jnp.concatenate([hp, hc_ref[...], hn], axis=0)
    rows = hext.shape[0]

    def conv_cols(cols):
        u = _dot(hext, wup_ref[:, cols])
        w = cw_ref[:, cols]
        return (pltpu.roll(u, 1, 0)[halo:halo + tq] * w[0:1]
                + u[halo:halo + tq] * w[1:2]
                + pltpu.roll(u, rows - 1, 0)[halo:halo + tq] * w[2:3] + cb_ref[:, cols])

    for j in range(FFN_DIM // FFN_CHUNK):
        a = conv_cols(slice(j * FFN_CHUNK, (j + 1) * FFN_CHUNK))
        v = conv_cols(slice(FFN_DIM + j * FFN_CHUNK, FFN_DIM + (j + 1) * FFN_CHUNK))
        g_ref[:, j * FFN_CHUNK:(j + 1) * FFN_CHUNK] = (a * (1.0 / (1.0 + jnp.exp(-a))) * v).astype(BF16)

    f = _dot(g_ref[...], wdn_ref[...])
    o_ref[...] = x1_ref[...] + mod_ref[5:6, :] * (f * _rms(f, D_MODEL) * gpost_ref[...])


def ffn_call(x1, h2, mod, mod_row, lw):
    b, s, d = x1.shape
    tq = min(FFN_TILE, s)
    halo = BF16_SUBLANES
    per_tile = tq // halo
    last = s // halo - 1
    if mod_row is None:
        mod_map = lambda i, t: (i, 0, 0)
    else:
        mod_map = lambda i, t: (mod_row, 0, 0)
    const2 = lambda i, t: (0, 0)
    tok = lambda i, t: (i, t, 0)
    return pl.pallas_call(
        ffn_kernel,
        name="conv_ffn",
        grid=(b, s // tq),
        in_specs=[
            pl.BlockSpec((None, tq, d), tok),
            pl.BlockSpec((None, tq, d), tok),
            pl.BlockSpec((None, halo, d), lambda i, t: (i, jnp.maximum(t * per_tile - 1, 0), 0)),
            pl.BlockSpec((None, halo, d), lambda i, t: (i, jnp.minimum((t + 1) * per_tile, last), 0)),
            pl.BlockSpec((d, 2 * FFN_DIM), const2, pipeline_mode=pl.Buffered(1)),
            pl.BlockSpec((3, 2 * FFN_DIM), const2),
            pl.BlockSpec((1, 2 * FFN_DIM), const2),
            pl.BlockSpec((FFN_DIM, d), const2, pipeline_mode=pl.Buffered(1)),
            pl.BlockSpec((None, N_MOD, d), mod_map),
            pl.BlockSpec((1, d), const2),
        ],
        out_specs=pl.BlockSpec((None, tq, d), tok),
        out_shape=jax.ShapeDtypeStruct((b, s, d), F32),
        scratch_shapes=[pltpu.VMEM((tq, FFN_DIM), BF16)],
        compiler_params=pltpu.CompilerParams(
            dimension_semantics=("arbitrary", "arbitrary"), vmem_limit_bytes=VMEM_LIMIT),
    )(x1, h2, h2, h2, lw["w_up"], lw["conv_w"], lw["conv_b"], lw["w_down"], mod,
      lw["g_ffn_post"])


def _rope_table(n_ctx, s, rot_dim, lane_lo, lane_hi, period):
    n = rot_dim // 4
    tok = np.arange(s)
    pos = np.stack([tok // GRID_W, tok % GRID_W], axis=1).astype(np.float32)
    lane = np.arange(LANES)
    o = (lane - lane_lo) % period
    axis = o // (2 * n)
    second = (o % (2 * n)) // n
    active = (lane >= lane_lo) & (lane < lane_hi)
    inv_freq = ROPE_THETA ** (-jnp.arange(n, dtype=F32) / n)
    ang = jnp.asarray(pos)[:, axis] * inv_freq[o % n][None, :]
    cos = jnp.where(active[None], jnp.cos(ang), 1.0)
    sin = jnp.where(active[None], jnp.sin(ang), 0.0)
    sa = jnp.where((second == 0)[None], -sin, 0.0)
    sb = jnp.where((second == 1)[None], sin, 0.0)
    tab = jnp.stack([cos, sa, sb]).astype(F32)
    ident = jnp.stack([jnp.ones((n_ctx, LANES), F32), jnp.zeros((n_ctx, LANES), F32),
                       jnp.zeros((n_ctx, LANES), F32)])
    return jnp.concatenate([tab, ident], axis=1)


def _layer_weights(l, w):
    d = D_MODEL
    w_in = w["w_in"][l]
    mla0 = 768 + 512
    cq_hi = mla0 + MLA_Q_RANK
    kr0 = cq_hi + MLA_KV_RANK
    w_in_p = jnp.concatenate([
        w_in[:, :mla0 + LANES],
        w_in[:, mla0 + LANES:cq_hi], w_in[:, kr0:kr0 + MLA_ROPE_DIM],
        jnp.zeros((d, LANES - HALF - MLA_ROPE_DIM), F32),
        w_in[:, cq_hi:kr0],
        w_in[:, kr0 + MLA_ROPE_DIM:],
    ], axis=1)
    dqk = MLA_NOPE_DIM + MLA_ROPE_DIM
    wq = jnp.zeros((2 * LANES, MLA_HEADS, LANES), F32)
    wq = wq.at[:MLA_Q_RANK, :, :dqk].set(w["mla_w_uq"][l].reshape(MLA_Q_RANK, MLA_HEADS, dqk))
    assert MLA_KV_RANK == LANES
    wukv = w["mla_w_ukv"][l].reshape(MLA_KV_RANK, MLA_HEADS, MLA_NOPE_DIM + MLA_V_DIM)
    wkt = jnp.zeros((MLA_HEADS, LANES, LANES), F32)
    wkt = wkt.at[:, :MLA_NOPE_DIM, :].set(jnp.transpose(wukv[:, :, :MLA_NOPE_DIM], (1, 2, 0)))
    wuv = jnp.transpose(wukv[:, :, MLA_NOPE_DIM:], (1, 0, 2))
    wv = jnp.zeros((2, 2, LANES, 2, MLA_V_DIM), F32)
    for hd in range(MLA_HEADS):
        wv = wv.at[hd // 2, hd % 2, :, hd % 2, :].set(wuv[hd])
    wv = wv.reshape(2, 2 * LANES, LANES)
    gq = jnp.zeros((1, 2 * LANES), F32).at[0, :MLA_Q_RANK].set(w["mla_g_q"][l])
    return {
        "g_mix_pre": w["g_mix_pre"][l][None], "g_mix_post": w["g_mix_post"][l][None],
        "g_ffn_pre": w["g_ffn_pre"][l][None], "g_ffn_post": w["g_ffn_post"][l][None],
        "w_in": w_in_p.astype(BF16),
        "wq": wq.reshape(2 * LANES, -1).astype(BF16), "wkt": wkt.astype(BF16),
        "wv": wv.astype(BF16),
        "gq": gq, "gkv": w["mla_g_kv"][l][None],
        "gnq": jnp.tile(w["gqa_g_q"][l], 2)[None], "gnk": jnp.tile(w["gqa_g_k"][l], 2)[None],
        "lam": jnp.stack([w["diff_lam_q1"][l], w["diff_lam_k1"][l],
                          w["diff_lam_q2"][l], w["diff_lam_k2"][l]]),
        "g_sub": jnp.tile(w["diff_g_sub"][l], 2)[None],
        "sink": w["swa_sink"][l],
        "w_out": w["w_out"][l].astype(BF16),
        "w_up": w["ffn_w_up"][l].astype(BF16),
        "conv_w": w["ffn_conv_w"][l],
        "conv_b": w["ffn_conv_b"][l][None],
        "w_down": w["ffn_w_down"][l].astype(BF16),
    }


def kernel(x, c, ctx, c_ctx, w_mod, b_mod, g_mix_pre, g_mix_post, g_ffn_pre, g_ffn_post, w_in, diff_lam_q1, diff_lam_k1, diff_lam_q2, diff_lam_k2, diff_g_sub, swa_sink, mla_g_q, mla_g_kv, mla_w_uq, mla_w_ukv, gqa_g_q, gqa_g_k, w_out, ffn_w_up, ffn_conv_w, ffn_conv_b, ffn_w_down):
    w = dict(g_mix_pre=g_mix_pre, g_mix_post=g_mix_post, g_ffn_pre=g_ffn_pre,
             g_ffn_post=g_ffn_post, w_in=w_in, diff_lam_q1=diff_lam_q1, diff_lam_k1=diff_lam_k1,
             diff_lam_q2=diff_lam_q2, diff_lam_k2=diff_lam_k2, diff_g_sub=diff_g_sub,
             swa_sink=swa_sink, mla_g_q=mla_g_q, mla_g_kv=mla_g_kv, mla_w_uq=mla_w_uq,
             mla_w_ukv=mla_w_ukv, gqa_g_q=gqa_g_q, gqa_g_k=gqa_g_k, w_out=w_out,
             ffn_w_up=ffn_w_up, ffn_conv_w=ffn_conv_w, ffn_conv_b=ffn_conv_b,
             ffn_w_down=ffn_w_down)
    b, s, d = x.shape
    depth = w_mod.shape[0]
    ctx_row = b
    mod_rows = 16
    cc = jnp.zeros((mod_rows, d), F32).at[:b].set(c).at[ctx_row].set(c_ctx)
    mod_all = mod_call(cc, w_mod, b_mod).reshape(depth, mod_rows, N_MOD, d)

    n_ctx = ctx.shape[1]
    tabs = (_rope_table(n_ctx, s, DIFF_QK_DIM, 0, LANES, DIFF_QK_DIM),
            _rope_table(n_ctx, s, GQA_HEAD_DIM, 0, LANES, GQA_HEAD_DIM),
            _rope_table(n_ctx, s, MLA_ROPE_DIM, MLA_NOPE_DIM, MLA_NOPE_DIM + MLA_ROPE_DIM,
                        MLA_ROPE_DIM))

    xc = ctx
    for l in range(depth):
        need_ctx = l < depth - 1
        lambda_init = 0.8 - 0.6 * math.exp(-0.3 * l)
        lw = _layer_weights(l, w)
        mod = mod_all[l]
        qkv = proj_call(xc, x, mod, ctx_row, lw, tabs)
        diff_extra = (lw["lam"], lw["g_sub"])
        mla_extra = (lw["wv"],)
        ys = (dense_attn_call("diff", qkv, True, diff_extra, lambda_init),
              swa_call(lw["sink"], qkv, True),
              dense_attn_call("mla", qkv, True, mla_extra),
              dense_attn_call("gqa", qkv, True))
        x1, h2 = mixout_call(x, ys, mod, None, lw)
        x = ffn_call(x1, h2, mod, None, lw)
        if need_ctx:
            ysc = (dense_attn_call("diff", qkv, False, diff_extra, lambda_init),
                   swa_call(lw["sink"], qkv, False),
                   dense_attn_call("mla", qkv, False, mla_extra),
                   dense_attn_call("gqa", qkv, False))
            xc1, hc2 = mixout_call(xc, ysc, mod, ctx_row, lw)
            xc = ffn_call(xc1, hc2, mod, ctx_row, lw)
    return x
```

```python
import functools
import math

import jax
import jax.numpy as jnp
import numpy as np
from jax import lax
from jax.experimental import pallas as pl
from jax.experimental.pallas import tpu as pltpu

F32 = jnp.float32
BF16 = jnp.bfloat16

D_MODEL = 1024
GRID_W = 64
ROPE_THETA = 10000.0
NORM_EPS = 1e-6
NEG_INF = -1e30
N_MOD = 6

DIFF_HEADS, DIFF_QK_DIM, DIFF_V_DIM = 4, 32, 64
SWA_HEAD_DIM, WINDOW = 64, 128
MLA_Q_RANK, MLA_KV_RANK = 192, 128
MLA_NOPE_DIM, MLA_ROPE_DIM, MLA_V_DIM, MLA_HEADS = 64, 32, 64, 4
GQA_HEAD_DIM = 64
FFN_DIM = 2816
MIX_COLS = 256

LANES = 128
BF16_SUBLANES = 16
HALF = LANES // 2
VMEM_LIMIT = 56 * 1024 * 1024

P_COLS = 17 * LANES
G_QM, G_VD, G_QG, G_QS = 0, 8, 12, 16
G_QD, G_KD, G_VG, G_KM, G_VM, G_VS, G_KG, G_KS = 20, 22, 24, 26, 28, 30, 32, 33
QKV_COLS = 34 * LANES
LOG2E = math.log2(math.e)

TOK_TILE = 256
CTX_TILE = 256
ATT_TILE = 512
KV_CHUNKS = 2
MIX_TILE = 512
FFN_TILE = 1024
FFN_CHUNK = 256


def _rms(x, n):
    return lax.rsqrt(jnp.sum(x * x, axis=-1, keepdims=True) * (1.0 / n) + NORM_EPS)


def _lane(shape):
    return lax.broadcasted_iota(jnp.int32, shape, len(shape) - 1)


def _dot(a, b):
    return jnp.dot(a, b, preferred_element_type=F32)


def _dot_nt(a, b):
    return lax.dot_general(a, b, (((1,), (1,)), ((), ())), preferred_element_type=F32)


def mod_kernel(c_ref, w_ref, b_ref, o_ref):
    c = c_ref[...]
    s = c * (1.0 / (1.0 + jnp.exp(-c)))
    o_ref[...] = jnp.dot(s, w_ref[...], preferred_element_type=F32,
                         precision=lax.Precision.HIGHEST) + b_ref[...]


def mod_call(cc, w_mod, b_mod):
    depth, d, n = w_mod.shape
    rows = cc.shape[0]
    tn = 1024
    return pl.pallas_call(
        mod_kernel,
        name="mod_vectors",
        grid=(depth, n // tn),
        in_specs=[
            pl.BlockSpec((rows, d), lambda l, j: (0, 0)),
            pl.BlockSpec((None, d, tn), lambda l, j: (l, 0, j)),
            pl.BlockSpec((None, 1, tn), lambda l, j: (l, 0, j)),
        ],
        out_specs=pl.BlockSpec((None, rows, tn), lambda l, j: (l, 0, j)),
        out_shape=jax.ShapeDtypeStruct((depth, rows, n), F32),
        compiler_params=pltpu.CompilerParams(
            dimension_semantics=("arbitrary", "arbitrary"), vmem_limit_bytes=VMEM_LIMIT),
    )(cc, w_mod, b_mod.reshape(depth, 1, n))


def absorb_kernel(a_ref, b_ref, o_ref):
    o_ref[...] = jnp.dot(a_ref[...], b_ref[...], preferred_element_type=F32,
                         precision=lax.Precision.HIGHEST)


def absorb_call(a, b):
    heads, m, k = a.shape
    n = b.shape[2]
    return pl.pallas_call(
        absorb_kernel,
        name="mla_absorb",
        grid=(heads,),
        in_specs=[pl.BlockSpec((None, m, k), lambda h: (h, 0, 0)),
                  pl.BlockSpec((None, k, n), lambda h: (h, 0, 0))],
        out_specs=pl.BlockSpec((None, m, n), lambda h: (h, 0, 0)),
        out_shape=jax.ShapeDtypeStruct((heads, m, n), F32),
        compiler_params=pltpu.CompilerParams(dimension_semantics=("arbitrary",)),
    )(a, b)


def _rope(x, tab_ref, n):
    return (x * tab_ref[0] + pltpu.roll(x, LANES - n, 1) * tab_ref[1]
            + pltpu.roll(x, n, 1) * tab_ref[2])


def _halves(lo, hi, shape):
    return jnp.where(_lane(shape) < HALF, lo, hi)


def _head_rms(x):
    low = _lane(x.shape) < HALF
    sq = x * x
    ss_lo = jnp.sum(jnp.where(low, sq, 0.0), axis=-1, keepdims=True)
    ss_hi = jnp.sum(jnp.where(low, 0.0, sq), axis=-1, keepdims=True)
    r_lo = lax.rsqrt(ss_lo * (1.0 / HALF) + NORM_EPS)
    r_hi = lax.rsqrt(ss_hi * (1.0 / HALF) + NORM_EPS)
    return jnp.where(low, r_lo, r_hi)


def proj_kernel(xc_ref, x_ref, mod_ref, gpre_ref, win_ref, wq_ref, gq_ref, gkv_ref,
                gnq_ref, gnk_ref, t32_ref, t64_ref, tml_ref, o_ref, pa_ref, pb_ref, *,
                tiles_per_sample, n_tiles):
    g = pl.program_id(0)
    is_ctx = jnp.minimum(g, n_tiles - 1) % tiles_per_sample == tiles_per_sample - 1

    @pl.when(g == 0)
    def _():
        pb_ref[...] = jnp.zeros_like(pb_ref)

    def project(dst_ref):
        x = jnp.where(is_ctx, xc_ref[...], x_ref[...])
        h = x * _rms(x, D_MODEL) * gpre_ref[...]
        h = h * (1.0 + mod_ref[1:2, :]) + mod_ref[0:1, :]
        dst_ref[...] = _dot(h.astype(BF16), win_ref[...])

    finish = functools.partial(_proj_finish, wq_ref, gq_ref, gkv_ref, gnq_ref, gnk_ref,
                               t32_ref, t64_ref, tml_ref, o_ref)

    @pl.when(g % 2 == 0)
    def _():
        project(pa_ref)
        finish(pb_ref)

    @pl.when(g % 2 == 1)
    def _():
        project(pb_ref)
        finish(pa_ref)


def _proj_finish(wq_ref, gq_ref, gkv_ref, gnq_ref, gnk_ref, t32_ref, t64_ref, tml_ref,
                 o_ref, p_ref):
    def grp(g):
        return p_ref[:, g * LANES:(g + 1) * LANES]

    def put(g, v):
        o_ref[:, g * LANES:(g + 1) * LANES] = v.astype(BF16)

    def rope32(v):
        return _rope(v, t32_ref, DIFF_QK_DIM // 4)

    def rope64(v):
        return _rope(v, t64_ref, GQA_HEAD_DIM // 4)

    def ropeml(v):
        return _rope(v, tml_ref, MLA_ROPE_DIM // 4)

    lane = _lane((p_ref.shape[0], LANES))
    low = lane < HALF

    def with_ones(v):
        return jnp.where(low, v, 1.0), jnp.where(low, 1.0, v)

    for g in range(2):
        put(G_QD + g, rope32(grp(g)) * (DIFF_QK_DIM ** -0.5 * LOG2E))
        put(G_KD + g, rope32(grp(2 + g)))
        v_lo, v_hi = with_ones(grp(4 + g))
        put(G_VD + 2 * g, v_lo)
        put(G_VD + 2 * g + 1, v_hi)

    def put_gqa_queries(g0, q01, q23):
        put(g0, jnp.where(low, q01, 0.0))
        put(g0 + 1, jnp.where(low, pltpu.roll(q01, HALF, 1), 0.0))
        put(g0 + 2, jnp.where(low, 0.0, pltpu.roll(q23, HALF, 1)))
        put(g0 + 3, jnp.where(low, 0.0, q23))

    scale_s = SWA_HEAD_DIM ** -0.5 * LOG2E
    put_gqa_queries(G_QS, rope64(grp(6)) * scale_s, rope64(grp(7)) * scale_s)
    put(G_KS, rope64(grp(8)))
    v_lo, v_hi = with_ones(grp(9))
    put(G_VS, v_lo)
    put(G_VS + 1, v_hi)

    cq0, cq1, ckv = grp(10), grp(11), grp(12)
    ss = (jnp.sum(cq0 * cq0, axis=-1, keepdims=True)
          + jnp.sum(jnp.where(low, cq1 * cq1, 0.0), axis=-1, keepdims=True))
    rq = lax.rsqrt(ss * (1.0 / MLA_Q_RANK) + NORM_EPS)
    cqn = jnp.concatenate([cq0 * rq * gq_ref[:, :LANES], cq1 * rq * gq_ref[:, LANES:]], axis=1)
    qm = _dot(cqn.astype(BF16), wq_ref[...])
    scale_m = (MLA_NOPE_DIM + MLA_ROPE_DIM) ** -0.5 * LOG2E
    rope_lanes = (lane >= MLA_NOPE_DIM) & (lane < MLA_NOPE_DIM + MLA_ROPE_DIM)
    for hd in range(MLA_HEADS):
        put(G_QM + 2 * hd, qm[:, hd * LANES:(hd + 1) * LANES] * scale_m)
        qr = qm[:, (MLA_HEADS + hd) * LANES:(MLA_HEADS + hd + 1) * LANES]
        put(G_QM + 2 * hd + 1, ropeml(qr) * scale_m)
    ckvn = ckv * _rms(ckv, MLA_KV_RANK) * gkv_ref[...]
    put(G_KM, ckvn)
    put(G_KM + 1, jnp.where(rope_lanes, ropeml(cq1), 0.0))
    put(G_VM, ckvn)
    put(G_VM + 1, jnp.ones_like(ckvn))

    scale_g = GQA_HEAD_DIM ** -0.5 * LOG2E
    q01 = grp(13)
    q01 = rope64(q01 * _head_rms(q01) * gnq_ref[...]) * scale_g
    q23 = grp(14)
    q23 = rope64(q23 * _head_rms(q23) * gnq_ref[...]) * scale_g
    put_gqa_queries(G_QG, q01, q23)
    kg = grp(15)
    put(G_KG, rope64(kg * _head_rms(kg) * gnk_ref[...]))
    v_lo, v_hi = with_ones(grp(16))
    put(G_VG, v_lo)
    put(G_VG + 1, v_hi)


def proj_call(xc, x, mod, ctx_row, lw, tabs):
    b, s, d = x.shape
    tq = TOK_TILE
    assert xc.shape[1] == CTX_TILE == tq
    s_all = CTX_TILE + s
    per = s_all // tq
    n = b * per
    const2 = lambda g: (0, 0)

    def proj_tile(g):
        ga = jnp.minimum(g, n - 1)
        return ga // per, ga % per

    def done_tile(g):
        gb = jnp.maximum(g - 1, 0)
        return gb // per, gb % per

    def x_map(g):
        i, t = proj_tile(g)
        return i, jnp.minimum(t, per - 2), 0

    def mod_map(g):
        i, t = proj_tile(g)
        return jnp.where(t == per - 1, ctx_row, i), 0, 0

    tab_spec = pl.BlockSpec((3, tq, LANES), lambda g: (0, done_tile(g)[1], 0))
    return pl.pallas_call(
        functools.partial(proj_kernel, tiles_per_sample=per, n_tiles=n),
        name="proj",
        grid=(n + 1,),
        in_specs=[
            pl.BlockSpec((None, tq, d), lambda g: (proj_tile(g)[0], 0, 0)),
            pl.BlockSpec((None, tq, d), x_map),
            pl.BlockSpec((None, N_MOD, d), mod_map),
            pl.BlockSpec((1, d), const2),
            pl.BlockSpec((d, P_COLS), const2),
            pl.BlockSpec((2 * LANES, 2 * MLA_HEADS * LANES), const2),
            pl.BlockSpec((1, 2 * LANES), const2),
            pl.BlockSpec((1, LANES), const2),
            pl.BlockSpec((1, LANES), const2),
            pl.BlockSpec((1, LANES), const2),
            tab_spec, tab_spec, tab_spec,
        ],
        out_specs=pl.BlockSpec((None, tq, QKV_COLS), lambda g: done_tile(g) + (0,)),
        out_shape=jax.ShapeDtypeStruct((b, s_all, QKV_COLS), BF16),
        scratch_shapes=[pltpu.VMEM((tq, P_COLS), F32), pltpu.VMEM((tq, P_COLS), F32)],
        compiler_params=pltpu.CompilerParams(
            dimension_semantics=("arbitrary",), vmem_limit_bytes=VMEM_LIMIT),
    )(xc, x, mod, lw["g_mix_pre"], lw["w_in"], lw["wq"], lw["gq"], lw["gkv"],
      lw["gnq"], lw["gnk"], tabs[0], tabs[1], tabs[2])


def _group(ref, rows, g, width=1):
    return ref[rows, g * width * LANES:(g + 1) * width * LANES]


def _attend(q_ref, segs, k_ref, v_ref, n_chunks, width, unroll=False):
    chunk = k_ref.shape[0] // n_chunks
    def query(g):
        if isinstance(g, int):
            return _group(q_ref, slice(None), g, width)
        g, lo, hi = g
        q = _group(q_ref, slice(None), g, width)
        lane = _lane(q.shape)
        return jnp.where((lane >= lo) & (lane < hi), q, jnp.zeros_like(q))

    qs = [jnp.concatenate([query(g) for g in qg], axis=0) for qg, _, _ in segs]

    def body(i, carry):
        rows = pl.ds(pl.multiple_of(i * chunk, LANES), chunk)
        out = []
        for j, (q, (_, kb, parts)) in enumerate(zip(qs, segs)):
            m, acc = carry[2 * j], carry[2 * j + 1]
            s = _dot_nt(q, _group(k_ref, rows, kb, width))
            m_new = jnp.maximum(m, jnp.max(s, axis=-1, keepdims=True))
            p = jnp.exp2(s - m_new).astype(BF16)
            tq = q.shape[0] // sum(n for n, _ in parts)
            pv, r0 = [], 0
            for n, vb in parts:
                pv.append(_dot(p[r0:r0 + n * tq], _group(v_ref, rows, vb, width)))
                r0 += n * tq
            pv = pv[0] if len(pv) == 1 else jnp.concatenate(pv, axis=0)
            out += [m_new, acc * jnp.exp2(m - m_new) + pv]
        return tuple(out)

    init = []
    for q in qs:
        init += [jnp.full((q.shape[0], 1), NEG_INF, F32),
                 jnp.zeros((q.shape[0], width * LANES), F32)]
    carry = lax.fori_loop(0, n_chunks, body, tuple(init), unroll=unroll)
    half = width * HALF
    return [carry[2 * j + 1] / pltpu.roll(carry[2 * j + 1], half, 1) for j in range(len(segs))]


_PLANS = {
    "gqa": (1, ((((0, 1, 2, 3), 0, ((2, 0), (2, 1))),),)),
    "mla": (2, ((((0, 1, 2, 3), 0, ((2, 0), (2, 0))),),)),
    "diff": (1, tuple(
        ((tuple((ob, i * DIFF_QK_DIM, (i + 1) * DIFF_QK_DIM) for i in range(4)), ob,
          ((2, 2 * ob), (2, 2 * ob + 1))),) for ob in range(2))),
}


def dense_kernel(*refs, kind, n_chunks, lambda_init):
    refs = list(refs)
    if kind == "diff":
        lam_ref, g_ref = refs[:2]
        refs = refs[2:]
    elif kind == "mla":
        wv_ref = refs[0]
        refs = refs[1:]
    q_ref, k_ref, v_ref, o_ref = refs
    tq = q_ref.shape[0]
    width, loops = _PLANS[kind]
    res = [_attend(q_ref, segs, k_ref, v_ref, n_chunks, width, unroll=True) for segs in loops]

    def tiles(r):
        return [r[i * tq:(i + 1) * tq] for i in range(r.shape[0] // tq)]

    if kind == "gqa":
        h = [t for r in res[0] for t in tiles(r)]
        blocks = [_halves(h[0], pltpu.roll(h[1], HALF, 1), h[0].shape),
                  _halves(pltpu.roll(h[2], HALF, 1), h[3], h[0].shape)]
    elif kind == "mla":
        h = [r[:, :LANES].astype(BF16) for r in tiles(res[0][0])]
        blocks = [_dot(jnp.concatenate(h[2 * ob:2 * ob + 2], axis=1), wv_ref[ob]) for ob in range(2)]
    else:
        lv = lam_ref[...]
        lam = (jnp.exp(jnp.sum(lv[0:1] * lv[1:2], axis=-1, keepdims=True))
               - jnp.exp(jnp.sum(lv[2:3] * lv[3:4], axis=-1, keepdims=True)) + lambda_init)
        blocks = []
        for (r,) in res:
            a = tiles(r)
            o = _halves(a[0] - lam * a[1], a[2] - lam * a[3], a[0].shape)
            blocks.append(o * _head_rms(o) * g_ref[...] * (1.0 - lambda_init))
    for ob, blk in enumerate(blocks):
        o_ref[:, ob * LANES:(ob + 1) * LANES] = blk.astype(BF16)


def _qkv_specs(lay, latent, s_all, tq):
    (q0, qn), (k0, kn), (v0, vn) = lay
    assert q0 % qn == 0 and k0 % kn == 0 and v0 % vn == 0
    ctx_blk = s_all // CTX_TILE - 1
    if latent:
        q_map = lambda i, t: (i, t, q0 // qn)
        k_map = lambda i, t: (i, 0, k0 // kn)
        v_map = lambda i, t: (i, 0, v0 // vn)
    else:
        q_map = lambda i, t: (i, ctx_blk, q0 // qn)
        k_map = lambda i, t: (i, ctx_blk, k0 // kn)
        v_map = lambda i, t: (i, ctx_blk, v0 // vn)
    kv_rows = s_all if latent else CTX_TILE
    return [
        pl.BlockSpec((None, tq, qn * LANES), q_map),
        pl.BlockSpec((None, kv_rows, kn * LANES), k_map),
        pl.BlockSpec((None, kv_rows, vn * LANES), v_map),
    ]


def dense_attn_call(kind, qkv, latent, extra=None, lambda_init=None):
    b, s_all, _ = qkv.shape
    tq = ATT_TILE if latent else CTX_TILE
    sq = s_all - CTX_TILE if latent else CTX_TILE
    lay = {"diff": ((G_QD, 2), (G_KD, 2), (G_VD, 4)),
           "mla": ((G_QM, 8), (G_KM, 2), (G_VM, 2)),
           "gqa": ((G_QG, 4), (G_KG, 1), (G_VG, 2))}[kind]
    specs = _qkv_specs(lay, latent, s_all, tq)
    args = [qkv, qkv, qkv]
    if kind == "diff":
        specs = [pl.BlockSpec((4, DIFF_QK_DIM), lambda i, t: (0, 0)),
                 pl.BlockSpec((1, LANES), lambda i, t: (0, 0))] + specs
        args = list(extra) + args
    elif kind == "mla":
        specs = [pl.BlockSpec((2, 2 * LANES, LANES), lambda i, t: (0, 0, 0))] + specs
        args = list(extra) + args
    body = functools.partial(dense_kernel, kind=kind, n_chunks=KV_CHUNKS if latent else 1,
                             lambda_init=lambda_init)
    return pl.pallas_call(
        body,
        name=kind + ("_attn" if latent else "_attn_ctx"),
        grid=(b, sq // tq),
        in_specs=specs,
        out_specs=pl.BlockSpec((None, tq, MIX_COLS), lambda i, t: (i, t, 0)),
        out_shape=jax.ShapeDtypeStruct((b, sq, MIX_COLS), BF16),
        compiler_params=pltpu.CompilerParams(
            dimension_semantics=("arbitrary", "arbitrary"), vmem_limit_bytes=VMEM_LIMIT),
    )(*args)


def swa_kernel(sink_ref, q_ref, k_ref, v_ref, o_ref, *, s_lat):
    tq = q_ref.shape[0]
    heads = q_ref.shape[1] // LANES
    rows = heads * tq
    t = pl.program_id(1)
    q = jnp.concatenate([_group(q_ref, slice(None), g) for g in range(heads)], axis=0)
    row = lax.broadcasted_iota(jnp.int32, (rows, 1), 0)
    sink = jnp.zeros((rows, 1), F32)
    for h in range(heads):
        sink = jnp.where(row // tq == h, sink_ref[h] * LOG2E, sink)

    spans = [(slice(s_lat, s_lat + CTX_TILE), None)]
    if s_lat:
        width = tq + 2 * WINDOW
        start = jnp.clip(t * tq - WINDOW, 0, s_lat - width)
        rel = (start - t * tq + lax.broadcasted_iota(jnp.int32, (tq, width), 1)
               - lax.broadcasted_iota(jnp.int32, (tq, width), 0))
        valid = jnp.concatenate([jnp.abs(rel) <= WINDOW] * heads, axis=0)
        spans.append((pl.ds(pl.multiple_of(start, WINDOW), width), valid))

    ss = []
    for span, valid in spans:
        s = _dot_nt(q, _group(k_ref, span, 0))
        ss.append(s if valid is None else jnp.where(valid, s, NEG_INF))
    m = functools.reduce(jnp.maximum, [jnp.max(s, axis=-1, keepdims=True) for s in ss] + [sink])
    ps = [jnp.exp2(s - m).astype(BF16) for s in ss]
    half = rows // 2
    acc = jnp.concatenate(
        [sum(_dot(p[vb * half:(vb + 1) * half], _group(v_ref, span, vb))
             for p, (span, _) in zip(ps, spans)) for vb in range(2)], axis=0)
    res = acc / (pltpu.roll(acc, HALF, 1) + jnp.exp2(sink - m))
    h = [res[i * tq:(i + 1) * tq] for i in range(heads)]
    o_ref[:, :LANES] = _halves(h[0], pltpu.roll(h[1], HALF, 1), h[0].shape).astype(BF16)
    o_ref[:, LANES:] = _halves(pltpu.roll(h[2], HALF, 1), h[3], h[0].shape).astype(BF16)


def swa_call(sink, qkv, latent):
    b, s_all, _ = qkv.shape
    tq = TOK_TILE
    s_lat = s_all - CTX_TILE if latent else 0
    sq = s_lat if latent else CTX_TILE
    specs = [pl.BlockSpec(memory_space=pltpu.SMEM)]
    specs += _qkv_specs(((G_QS, 4), (G_KS, 1), (G_VS, 2)), latent, s_all, tq)
    return pl.pallas_call(
        functools.partial(swa_kernel, s_lat=s_lat),
        name="swa_attn" if latent else "swa_attn_ctx",
        grid=(b, sq // tq),
        in_specs=specs,
        out_specs=pl.BlockSpec((None, tq, MIX_COLS), lambda i, t: (i, t, 0)),
        out_shape=jax.ShapeDtypeStruct((b, sq, MIX_COLS), BF16),
        compiler_params=pltpu.CompilerParams(
            dimension_semantics=("arbitrary", "arbitrary"), vmem_limit_bytes=VMEM_LIMIT),
    )(sink, qkv, qkv, qkv)


def mixout_kernel(x_ref, ya_ref, yb_ref, ym_ref, yd_ref, w_ref, mod_ref, gpost_ref,
                  gfpre_ref, x1_ref, h2_ref, za_ref, zb_ref):
    g = pl.program_id(0)

    @pl.when(g == 0)
    def _():
        zb_ref[...] = jnp.zeros_like(zb_ref)

    def project(dst_ref):
        y = jnp.concatenate([ya_ref[...], yb_ref[...], ym_ref[...], yd_ref[...]], axis=1)
        dst_ref[...] = _dot(y, w_ref[...])

    def finish(z_ref):
        z = z_ref[...]
        x1 = x_ref[...] + mod_ref[2:3, :] * (z * _rms(z, D_MODEL) * gpost_ref[...])
        x1_ref[...] = x1
        h2 = x1 * _rms(x1, D_MODEL) * gfpre_ref[...]
        h2_ref[...] = (h2 * (1.0 + mod_ref[4:5, :]) + mod_ref[3:4, :]).astype(BF16)

    @pl.when(g % 2 == 0)
    def _():
        project(za_ref)
        finish(zb_ref)

    @pl.when(g % 2 == 1)
    def _():
        project(zb_ref)
        finish(za_ref)


def mixout_call(x, ys, mod, mod_row, lw):
    b, s, d = x.shape
    tq = min(MIX_TILE, s)
    per = s // tq
    n = b * per

    def proj_tile(g):
        ga = jnp.minimum(g, n - 1)
        return ga // per, ga % per, 0

    def done_tile(g):
        gb = jnp.maximum(g - 1, 0)
        return gb // per, gb % per, 0

    if mod_row is None:
        mod_map = lambda g: (done_tile(g)[0], 0, 0)
    else:
        mod_map = lambda g: (mod_row, 0, 0)
    const2 = lambda g: (0, 0)
    y_spec = pl.BlockSpec((None, tq, MIX_COLS), proj_tile)
    return pl.pallas_call(
        mixout_kernel,
        name="mixout",
        grid=(n + 1,),
        in_specs=[pl.BlockSpec((None, tq, d), done_tile), y_spec, y_spec, y_spec, y_spec,
                  pl.BlockSpec((4 * MIX_COLS, d), const2),
                  pl.BlockSpec((None, N_MOD, d), mod_map),
                  pl.BlockSpec((1, d), const2), pl.BlockSpec((1, d), const2)],
        out_specs=[pl.BlockSpec((None, tq, d), done_tile), pl.BlockSpec((None, tq, d), done_tile)],
        out_shape=[jax.ShapeDtypeStruct((b, s, d), F32), jax.ShapeDtypeStruct((b, s, d), BF16)],
        scratch_shapes=[pltpu.VMEM((tq, d), F32), pltpu.VMEM((tq, d), F32)],
        compiler_params=pltpu.CompilerParams(
            dimension_semantics=("arbitrary",), vmem_limit_bytes=VMEM_LIMIT),
    )(x, *ys, lw["w_out"], mod, lw["g_mix_post"], lw["g_ffn_pre"])


def ffn_kernel(x1_ref, hc_ref, hp_ref, hn_ref, wup_ref, cw_ref, cb_ref, wdn_ref, mod_ref,
               gpost_ref, o_ref, g_ref):
    t = pl.program_id(1)
    tq = hc_ref.shape[0]
    halo = hp_ref.shape[0]
    hp = jnp.where(t > 0, hp_ref[...], jnp.zeros_like(hp_ref))
    hn = jnp.where(t < pl.num_programs(1) - 1, hn_ref[...], jnp.zeros_like(hn_ref))
    hext = jnp.concatenate([hp, hc_ref[...], hn], axis=0)
    rows = hext.shape[0]

    def conv_cols(cols):
        u = _dot(hext, wup_ref[:, cols])
        w = cw_ref[:, cols]
        return (pltpu.roll(u, 1, 0)[halo:halo + tq] * w[0:1]
                + u[halo:halo + tq] * w[1:2]
                + pltpu.roll(u, rows - 1, 0)[halo:halo + tq] * w[2:3] + cb_ref[:, cols])

    for j in range(FFN_DIM // FFN_CHUNK):
        a = conv_cols(slice(j * FFN_CHUNK, (j + 1) * FFN_CHUNK))
        v = conv_cols(slice(FFN_DIM + j * FFN_CHUNK, FFN_DIM + (j + 1) * FFN_CHUNK))
        g_ref[:, j * FFN_CHUNK:(j + 1) * FFN_CHUNK] = (a * (1.0 / (1.0 + jnp.exp(-a))) * v).astype(BF16)

    f = _dot(g_ref[...], wdn_ref[...])
    o_ref[...] = x1_ref[...] + mod_ref[5:6, :] * (f * _rms(f, D_MODEL) * gpost_ref[...])


def ffn_call(x1, h2, mod, mod_row, lw):
    b, s, d = x1.shape
    tq = min(FFN_TILE, s)
    halo = BF16_SUBLANES
    per_tile = tq // halo
    last = s // halo - 1
    if mod_row is None:
        mod_map = lambda i, t: (i, 0, 0)
    else:
        mod_map = lambda i, t: (mod_row, 0, 0)
    const2 = lambda i, t: (0, 0)
    tok = lambda i, t: (i, t, 0)
    return pl.pallas_call(
        ffn_kernel,
        name="conv_ffn",
        grid=(b, s // tq),
        in_specs=[
            pl.BlockSpec((None, tq, d), tok),
            pl.BlockSpec((None, tq, d), tok),
            pl.BlockSpec((None, halo, d), lambda i, t: (i, jnp.maximum(t * per_tile - 1, 0), 0)),
            pl.BlockSpec((None, halo, d), lambda i, t: (i, jnp.minimum((t + 1) * per_tile, last), 0)),
            pl.BlockSpec((d, 2 * FFN_DIM), const2, pipeline_mode=pl.Buffered(1)),
            pl.BlockSpec((3, 2 * FFN_DIM), const2),
            pl.BlockSpec((1, 2 * FFN_DIM), const2),
            pl.BlockSpec((FFN_DIM, d), const2, pipeline_mode=pl.Buffered(1)),
            pl.BlockSpec((None, N_MOD, d), mod_map),
            pl.BlockSpec((1, d), const2),
        ],
        out_specs=pl.BlockSpec((None, tq, d), tok),
        out_shape=jax.ShapeDtypeStruct((b, s, d), F32),
        scratch_shapes=[pltpu.VMEM((tq, FFN_DIM), BF16)],
        compiler_params=pltpu.CompilerParams(
            dimension_semantics=("arbitrary", "arbitrary"), vmem_limit_bytes=VMEM_LIMIT),
    )(x1, h2, h2, h2, lw["w_up"], lw["conv_w"], lw["conv_b"], lw["w_down"], mod,
      lw["g_ffn_post"])


def _rope_table(n_ctx, s, rot_dim, lane_lo, lane_hi, period):
    n = rot_dim // 4
    tok = np.arange(s)
    pos = np.stack([tok // GRID_W, tok % GRID_W], axis=1).astype(np.float32)
    lane = np.arange(LANES)
    o = (lane - lane_lo) % period
    axis = o // (2 * n)
    second = (o % (2 * n)) // n
    active = (lane >= lane_lo) & (lane < lane_hi)
    inv_freq = ROPE_THETA ** (-jnp.arange(n, dtype=F32) / n)
    ang = jnp.asarray(pos)[:, axis] * inv_freq[o % n][None, :]
    cos = jnp.where(active[None], jnp.cos(ang), 1.0)
    sin = jnp.where(active[None], jnp.sin(ang), 0.0)
    sa = jnp.where((second == 0)[None], -sin, 0.0)
    sb = jnp.where((second == 1)[None], sin, 0.0)
    tab = jnp.stack([cos, sa, sb]).astype(F32)
    ident = jnp.stack([jnp.ones((n_ctx, LANES), F32), jnp.zeros((n_ctx, LANES), F32),
                       jnp.zeros((n_ctx, LANES), F32)])
    return jnp.concatenate([tab, ident], axis=1)


def _layer_weights(l, w):
    d = D_MODEL
    w_in = w["w_in"][l]
    mla0 = 768 + 512
    cq_hi = mla0 + MLA_Q_RANK
    kr0 = cq_hi + MLA_KV_RANK
    w_in_p = jnp.concatenate([
        w_in[:, :mla0 + LANES],
        w_in[:, mla0 + LANES:cq_hi], w_in[:, kr0:kr0 + MLA_ROPE_DIM],
        jnp.zeros((d, LANES - HALF - MLA_ROPE_DIM), F32),
        w_in[:, cq_hi:kr0],
        w_in[:, kr0 + MLA_ROPE_DIM:],
    ], axis=1)
    dqk = MLA_NOPE_DIM + MLA_ROPE_DIM
    assert MLA_KV_RANK == LANES
    wuq = w["mla_w_uq"][l].reshape(MLA_Q_RANK, MLA_HEADS, dqk)
    wukv = w["mla_w_ukv"][l].reshape(MLA_KV_RANK, MLA_HEADS, MLA_NOPE_DIM + MLA_V_DIM)
    wuq_n = jnp.zeros((MLA_HEADS, 2 * LANES, MLA_NOPE_DIM), F32)
    wuq_n = wuq_n.at[:, :MLA_Q_RANK].set(jnp.transpose(wuq[:, :, :MLA_NOPE_DIM], (1, 0, 2)))
    w_abs = absorb_call(wuq_n, jnp.transpose(wukv[:, :, :MLA_NOPE_DIM], (1, 2, 0)))
    wq_r = jnp.zeros((2 * LANES, MLA_HEADS, LANES), F32)
    wq_r = wq_r.at[:MLA_Q_RANK, :, MLA_NOPE_DIM:dqk].set(wuq[:, :, MLA_NOPE_DIM:])
    wq = jnp.concatenate([jnp.transpose(w_abs, (1, 0, 2)).reshape(2 * LANES, -1),
                          wq_r.reshape(2 * LANES, -1)], axis=1)
    wuv = jnp.transpose(wukv[:, :, MLA_NOPE_DIM:], (1, 0, 2))
    wv = jnp.zeros((2, 2, LANES, 2, MLA_V_DIM), F32)
    for hd in range(MLA_HEADS):
        wv = wv.at[hd // 2, hd % 2, :, hd % 2, :].set(wuv[hd])
    wv = wv.reshape(2, 2 * LANES, LANES)
    gq = jnp.zeros((1, 2 * LANES), F32).at[0, :MLA_Q_RANK].set(w["mla_g_q"][l])
    return {
        "g_mix_pre": w["g_mix_pre"][l][None], "g_mix_post": w["g_mix_post"][l][None],
        "g_ffn_pre": w["g_ffn_pre"][l][None], "g_ffn_post": w["g_ffn_post"][l][None],
        "w_in": w_in_p.astype(BF16),
        "wq": wq.astype(BF16), "wv": wv.astype(BF16),
        "gq": gq, "gkv": w["mla_g_kv"][l][None],
        "gnq": jnp.tile(w["gqa_g_q"][l], 2)[None], "gnk": jnp.tile(w["gqa_g_k"][l], 2)[None],
        "lam": jnp.stack([w["diff_lam_q1"][l], w["diff_lam_k1"][l],
                          w["diff_lam_q2"][l], w["diff_lam_k2"][l]]),
        "g_sub": jnp.tile(w["diff_g_sub"][l], 2)[None],
        "sink": w["swa_sink"][l],
        "w_out": w["w_out"][l].astype(BF16),
        "w_up": w["ffn_w_up"][l].astype(BF16),
        "conv_w": w["ffn_conv_w"][l],
        "conv_b": w["ffn_conv_b"][l][None],
        "w_down": w["ffn_w_down"][l].astype(BF16),
    }


def kernel(x, c, ctx, c_ctx, w_mod, b_mod, g_mix_pre, g_mix_post, g_ffn_pre, g_ffn_post, w_in, diff_lam_q1, diff_lam_k1, diff_lam_q2, diff_lam_k2, diff_g_sub, swa_sink, mla_g_q, mla_g_kv, mla_w_uq, mla_w_ukv, gqa_g_q, gqa_g_k, w_out, ffn_w_up, ffn_conv_w, ffn_conv_b, ffn_w_down):
    w = dict(g_mix_pre=g_mix_pre, g_mix_post=g_mix_post, g_ffn_pre=g_ffn_pre,
             g_ffn_post=g_ffn_post, w_in=w_in, diff_lam_q1=diff_lam_q1, diff_lam_k1=diff_lam_k1,
             diff_lam_q2=diff_lam_q2, diff_lam_k2=diff_lam_k2, diff_g_sub=diff_g_sub,
             swa_sink=swa_sink, mla_g_q=mla_g_q, mla_g_kv=mla_g_kv, mla_w_uq=mla_w_uq,
             mla_w_ukv=mla_w_ukv, gqa_g_q=gqa_g_q, gqa_g_k=gqa_g_k, w_out=w_out,
             ffn_w_up=ffn_w_up, ffn_conv_w=ffn_conv_w, ffn_conv_b=ffn_conv_b,
             ffn_w_down=ffn_w_down)
    b, s, d = x.shape
    depth = w_mod.shape[0]
    ctx_row = b
    mod_rows = 16
    cc = jnp.zeros((mod_rows, d), F32).at[:b].set(c).at[ctx_row].set(c_ctx)
    mod_all = mod_call(cc, w_mod, b_mod).reshape(depth, mod_rows, N_MOD, d)

    n_ctx = ctx.shape[1]
    tabs = (_rope_table(n_ctx, s, DIFF_QK_DIM, 0, LANES, DIFF_QK_DIM),
            _rope_table(n_ctx, s, GQA_HEAD_DIM, 0, LANES, GQA_HEAD_DIM),
            _rope_table(n_ctx, s, MLA_ROPE_DIM, MLA_NOPE_DIM, MLA_NOPE_DIM + MLA_ROPE_DIM,
                        MLA_ROPE_DIM))

    xc = ctx
    for l in range(depth):
        need_ctx = l < depth - 1
        lambda_init = 0.8 - 0.6 * math.exp(-0.3 * l)
        lw = _layer_weights(l, w)
        mod = mod_all[l]
        qkv = proj_call(xc, x, mod, ctx_row, lw, tabs)
        diff_extra = (lw["lam"], lw["g_sub"])
        mla_extra = (lw["wv"],)
        ys = (dense_attn_call("diff", qkv, True, diff_extra, lambda_init),
              swa_call(lw["sink"], qkv, True),
              dense_attn_call("mla", qkv, True, mla_extra),
              dense_attn_call("gqa", qkv, True))
        x1, h2 = mixout_call(x, ys, mod, None, lw)
        x = ffn_call(x1, h2, mod, None, lw)
        if need_ctx:
            ysc = (dense_attn_call("diff", qkv, False, diff_extra, lambda_init),
                   swa_call(lw["sink"], qkv, False),
                   dense_attn_call("mla", qkv, False, mla_extra),
                   dense_attn_call("gqa", qkv, False))
            xc1, hc2 = mixout_call(xc, ysc, mod, ctx_row, lw)
            xc = ffn_call(xc1, hc2, mod, ctx_row, lw)
    return x
```

```python
import functools
import math

import jax
import jax.numpy as jnp
import numpy as np
from jax import lax
from jax.experimental import pallas as pl
from jax.experimental.pallas import tpu as pltpu

F32 = jnp.float32
BF16 = jnp.bfloat16

D_MODEL = 1024
GRID_W = 64
ROPE_THETA = 10000.0
NORM_EPS = 1e-6
NEG_INF = -1e30
N_MOD = 6

DIFF_HEADS, DIFF_QK_DIM, DIFF_V_DIM = 4, 32, 64
SWA_HEAD_DIM, WINDOW = 64, 128
MLA_Q_RANK, MLA_KV_RANK = 192, 128
MLA_NOPE_DIM, MLA_ROPE_DIM, MLA_V_DIM, MLA_HEADS = 64, 32, 64, 4
GQA_HEAD_DIM = 64
FFN_DIM = 2816
MIX_COLS = 256

LANES = 128
BF16_SUBLANES = 16
HALF = LANES // 2
VMEM_LIMIT = 56 * 1024 * 1024

P_COLS = 17 * LANES
G_QM, G_VD, G_QG, G_QS = 0, 8, 12, 16
G_QD, G_KD, G_VG, G_KM, G_VM, G_VS, G_KG, G_KS = 20, 22, 24, 26, 28, 30, 32, 33
QKV_COLS = 34 * LANES
LOG2E = math.log2(math.e)

TOK_TILE = 256
CTX_TILE = 256
ATT_TILE = 512
KV_CHUNKS = 2
MIX_TILE = 512
FFN_TILE = 1024
FFN_CHUNK = 256


def _rms(x, n):
    return lax.rsqrt(jnp.sum(x * x, axis=-1, keepdims=True) * (1.0 / n) + NORM_EPS)


def _lane(shape):
    return lax.broadcasted_iota(jnp.int32, shape, len(shape) - 1)


def _dot(a, b):
    return jnp.dot(a, b, preferred_element_type=F32)


def _dot_nt(a, b):
    return lax.dot_general(a, b, (((1,), (1,)), ((), ())), preferred_element_type=F32)


def mod_kernel(c_ref, w_ref, b_ref, o_ref):
    c = c_ref[...]
    s = c * (1.0 / (1.0 + jnp.exp(-c)))
    o_ref[...] = jnp.dot(s, w_ref[...], preferred_element_type=F32,
                         precision=lax.Precision.HIGHEST) + b_ref[...]


def mod_call(cc, w_mod, b_mod):
    depth, d, n = w_mod.shape
    rows = cc.shape[0]
    tn = 1024
    return pl.pallas_call(
        mod_kernel,
        name="mod_vectors",
        grid=(depth, n // tn),
        in_specs=[
            pl.BlockSpec((rows, d), lambda l, j: (0, 0)),
            pl.BlockSpec((None, d, tn), lambda l, j: (l, 0, j)),
            pl.BlockSpec((None, 1, tn), lambda l, j: (l, 0, j)),
        ],
        out_specs=pl.BlockSpec((None, rows, tn), lambda l, j: (l, 0, j)),
        out_shape=jax.ShapeDtypeStruct((depth, rows, n), F32),
        compiler_params=pltpu.CompilerParams(
            dimension_semantics=("arbitrary", "arbitrary"), vmem_limit_bytes=VMEM_LIMIT),
    )(cc, w_mod, b_mod.reshape(depth, 1, n))


def absorb_kernel(a_ref, b_ref, o_ref):
    o_ref[...] = jnp.dot(a_ref[...], b_ref[...], preferred_element_type=F32,
                         precision=lax.Precision.HIGHEST)


def absorb_call(a, b):
    heads, m, k = a.shape
    n = b.shape[2]
    return pl.pallas_call(
        absorb_kernel,
        name="mla_absorb",
        grid=(heads,),
        in_specs=[pl.BlockSpec((None, m, k), lambda h: (h, 0, 0)),
                  pl.BlockSpec((None, k, n), lambda h: (h, 0, 0))],
        out_specs=pl.BlockSpec((None, m, n), lambda h: (h, 0, 0)),
        out_shape=jax.ShapeDtypeStruct((heads, m, n), F32),
        compiler_params=pltpu.CompilerParams(dimension_semantics=("arbitrary",)),
    )(a, b)


def _rope(x, tab_ref):
    return x * tab_ref[0] + pltpu.roll(x, HALF, 1) * tab_ref[1]


def _halves(lo, hi, shape):
    return jnp.where(_lane(shape) < HALF, lo, hi)


def _head_rms(x, first):
    sq = x * x
    ss_a = jnp.sum(jnp.where(first, sq, 0.0), axis=-1, keepdims=True)
    ss_b = jnp.sum(jnp.where(first, 0.0, sq), axis=-1, keepdims=True)
    r_a = lax.rsqrt(ss_a * (1.0 / HALF) + NORM_EPS)
    r_b = lax.rsqrt(ss_b * (1.0 / HALF) + NORM_EPS)
    return jnp.where(first, r_a, r_b)


def proj_kernel(xc_ref, x_ref, mod_ref, gpre_ref, win_ref, wq_ref, gq_ref, gkv_ref,
                gnq_ref, gnk_ref, t32_ref, t64_ref, tml_ref, o_ref, pa_ref, pb_ref, *,
                tiles_per_sample, n_tiles):
    g = pl.program_id(0)
    is_ctx = jnp.minimum(g, n_tiles - 1) % tiles_per_sample == tiles_per_sample - 1

    @pl.when(g == 0)
    def _():
        pb_ref[...] = jnp.zeros_like(pb_ref)

    def project(dst_ref):
        x = jnp.where(is_ctx, xc_ref[...], x_ref[...])
        h = x * _rms(x, D_MODEL) * gpre_ref[...]
        h = h * (1.0 + mod_ref[1:2, :]) + mod_ref[0:1, :]
        dst_ref[...] = _dot(h.astype(BF16), win_ref[...])

    finish = functools.partial(_proj_finish, wq_ref, gq_ref, gkv_ref, gnq_ref, gnk_ref,
                               t32_ref, t64_ref, tml_ref, o_ref)

    @pl.when(g % 2 == 0)
    def _():
        project(pa_ref)
        finish(pb_ref)

    @pl.when(g % 2 == 1)
    def _():
        project(pb_ref)
        finish(pa_ref)


def _proj_finish(wq_ref, gq_ref, gkv_ref, gnq_ref, gnk_ref, t32_ref, t64_ref, tml_ref,
                 o_ref, p_ref):
    def grp(g):
        return p_ref[:, g * LANES:(g + 1) * LANES]

    def put(g, v):
        o_ref[:, g * LANES:(g + 1) * LANES] = v.astype(BF16)

    def rope32(v):
        return _rope(v, t32_ref)

    def rope64(v):
        return _rope(v, t64_ref)

    def ropeml(v):
        return _rope(v, tml_ref)

    lane = _lane((p_ref.shape[0], LANES))
    low = lane < HALF
    head_a = lane % HALF < HALF // 2

    def with_ones(v):
        return jnp.where(low, v, 1.0), jnp.where(low, 1.0, v)

    for g in range(2):
        put(G_QD + g, rope32(grp(g)) * (DIFF_QK_DIM ** -0.5 * LOG2E))
        put(G_KD + g, rope32(grp(2 + g)))
        v_lo, v_hi = with_ones(grp(4 + g))
        put(G_VD + 2 * g, v_lo)
        put(G_VD + 2 * g + 1, v_hi)

    def put_gqa_queries(g0, q01, q23):
        put(g0, jnp.where(head_a, q01, 0.0))
        put(g0 + 1, jnp.where(head_a, pltpu.roll(q01, LANES - HALF // 2, 1), 0.0))
        put(g0 + 2, jnp.where(head_a, 0.0, pltpu.roll(q23, HALF // 2, 1)))
        put(g0 + 3, jnp.where(head_a, 0.0, q23))

    scale_s = SWA_HEAD_DIM ** -0.5 * LOG2E
    put_gqa_queries(G_QS, rope64(grp(6)) * scale_s, rope64(grp(7)) * scale_s)
    put(G_KS, rope64(grp(8)))
    v_lo, v_hi = with_ones(grp(9))
    put(G_VS, v_lo)
    put(G_VS + 1, v_hi)

    cq0, cq1, ckv = grp(10), grp(11), grp(12)
    rope_lanes = lane % HALF < MLA_ROPE_DIM // 2
    cq_lanes = ~rope_lanes & (lane < HALF + MLA_ROPE_DIM)
    ss = (jnp.sum(cq0 * cq0, axis=-1, keepdims=True)
          + jnp.sum(jnp.where(cq_lanes, cq1 * cq1, 0.0), axis=-1, keepdims=True))
    rq = lax.rsqrt(ss * (1.0 / MLA_Q_RANK) + NORM_EPS)
    cqn = jnp.concatenate([cq0 * rq * gq_ref[:, :LANES], cq1 * rq * gq_ref[:, LANES:]], axis=1)
    qm = _dot(cqn.astype(BF16), wq_ref[...])
    scale_m = (MLA_NOPE_DIM + MLA_ROPE_DIM) ** -0.5 * LOG2E
    for hd in range(MLA_HEADS):
        put(G_QM + 2 * hd, qm[:, hd * LANES:(hd + 1) * LANES] * scale_m)
        qr = qm[:, (MLA_HEADS + hd) * LANES:(MLA_HEADS + hd + 1) * LANES]
        put(G_QM + 2 * hd + 1, ropeml(qr) * scale_m)
    ckvn = ckv * _rms(ckv, MLA_KV_RANK) * gkv_ref[...]
    put(G_KM, ckvn)
    put(G_KM + 1, jnp.where(rope_lanes, ropeml(cq1), 0.0))
    put(G_VM, ckvn)
    put(G_VM + 1, jnp.ones_like(ckvn))

    scale_g = GQA_HEAD_DIM ** -0.5 * LOG2E
    q01 = grp(13)
    q01 = rope64(q01 * _head_rms(q01, head_a) * gnq_ref[...]) * scale_g
    q23 = grp(14)
    q23 = rope64(q23 * _head_rms(q23, head_a) * gnq_ref[...]) * scale_g
    put_gqa_queries(G_QG, q01, q23)
    kg = grp(15)
    put(G_KG, rope64(kg * _head_rms(kg, head_a) * gnk_ref[...]))
    v_lo, v_hi = with_ones(grp(16))
    put(G_VG, v_lo)
    put(G_VG + 1, v_hi)


def proj_call(xc, x, mod, ctx_row, lw, tabs):
    b, s, d = x.shape
    tq = TOK_TILE
    assert xc.shape[1] == CTX_TILE == tq
    s_all = CTX_TILE + s
    per = s_all // tq
    n = b * per
    const2 = lambda g: (0, 0)

    def proj_tile(g):
        ga = jnp.minimum(g, n - 1)
        return ga // per, ga % per

    def done_tile(g):
        gb = jnp.maximum(g - 1, 0)
        return gb // per, gb % per

    def x_map(g):
        i, t = proj_tile(g)
        return i, jnp.minimum(t, per - 2), 0

    def mod_map(g):
        i, t = proj_tile(g)
        return jnp.where(t == per - 1, ctx_row, i), 0, 0

    tab_spec = pl.BlockSpec((2, tq, LANES), lambda g: (0, done_tile(g)[1], 0))
    return pl.pallas_call(
        functools.partial(proj_kernel, tiles_per_sample=per, n_tiles=n),
        name="proj",
        grid=(n + 1,),
        in_specs=[
            pl.BlockSpec((None, tq, d), lambda g: (proj_tile(g)[0], 0, 0)),
            pl.BlockSpec((None, tq, d), x_map),
            pl.BlockSpec((None, N_MOD, d), mod_map),
            pl.BlockSpec((1, d), const2),
            pl.BlockSpec((d, P_COLS), const2),
            pl.BlockSpec((2 * LANES, 2 * MLA_HEADS * LANES), const2),
            pl.BlockSpec((1, 2 * LANES), const2),
            pl.BlockSpec((1, LANES), const2),
            pl.BlockSpec((1, LANES), const2),
            pl.BlockSpec((1, LANES), const2),
            tab_spec, tab_spec, tab_spec,
        ],
        out_specs=pl.BlockSpec((None, tq, QKV_COLS), lambda g: done_tile(g) + (0,)),
        out_shape=jax.ShapeDtypeStruct((b, s_all, QKV_COLS), BF16),
        scratch_shapes=[pltpu.VMEM((tq, P_COLS), F32), pltpu.VMEM((tq, P_COLS), F32)],
        compiler_params=pltpu.CompilerParams(
            dimension_semantics=("arbitrary",), vmem_limit_bytes=VMEM_LIMIT),
    )(xc, x, mod, lw["g_mix_pre"], lw["w_in"], lw["wq"], lw["gq"], lw["gkv"],
      lw["gnq"], lw["gnk"], tabs[0], tabs[1], tabs[2])


def _group(ref, rows, g, width=1):
    return ref[rows, g * width * LANES:(g + 1) * width * LANES]


def _attend(q_ref, segs, k_ref, v_ref, n_chunks, width, unroll=False):
    chunk = k_ref.shape[0] // n_chunks
    def query(g):
        if isinstance(g, int):
            return _group(q_ref, slice(None), g, width)
        g, unit, n_units = g
        q = _group(q_ref, slice(None), g, width)
        per = HALF // n_units
        keep = _lane(q.shape) % HALF // per == unit
        return jnp.where(keep, q, jnp.zeros_like(q))

    qs = [jnp.concatenate([query(g) for g in qg], axis=0) for qg, _, _ in segs]

    def body(i, carry):
        rows = pl.ds(pl.multiple_of(i * chunk, LANES), chunk)
        out = []
        for j, (q, (_, kb, parts)) in enumerate(zip(qs, segs)):
            m, acc = carry[2 * j], carry[2 * j + 1]
            s = _dot_nt(q, _group(k_ref, rows, kb, width))
            m_new = jnp.maximum(m, jnp.max(s, axis=-1, keepdims=True))
            p = jnp.exp2(s - m_new).astype(BF16)
            tq = q.shape[0] // sum(n for n, _ in parts)
            pv, r0 = [], 0
            for n, vb in parts:
                pv.append(_dot(p[r0:r0 + n * tq], _group(v_ref, rows, vb, width)))
                r0 += n * tq
            pv = pv[0] if len(pv) == 1 else jnp.concatenate(pv, axis=0)
            out += [m_new, acc * jnp.exp2(m - m_new) + pv]
        return tuple(out)

    init = []
    for q in qs:
        init += [jnp.full((q.shape[0], 1), NEG_INF, F32),
                 jnp.zeros((q.shape[0], width * LANES), F32)]
    carry = lax.fori_loop(0, n_chunks, body, tuple(init), unroll=unroll)
    half = width * HALF
    return [carry[2 * j + 1] / pltpu.roll(carry[2 * j + 1], half, 1) for j in range(len(segs))]


_PLANS = {
    "gqa": (1, ((((0, 1, 2, 3), 0, ((2, 0), (2, 1))),),)),
    "mla": (2, ((((0, 1, 2, 3), 0, ((2, 0), (2, 0))),),)),
    "diff": (1, tuple(
        ((tuple((ob, i, 4) for i in range(4)), ob,
          ((2, 2 * ob), (2, 2 * ob + 1))),) for ob in range(2))),
}


def dense_kernel(*refs, kind, n_chunks, lambda_init):
    refs = list(refs)
    if kind == "diff":
        lam_ref, g_ref = refs[:2]
        refs = refs[2:]
    elif kind == "mla":
        wv_ref = refs[0]
        refs = refs[1:]
    q_ref, k_ref, v_ref, o_ref = refs
    tq = q_ref.shape[0]
    width, loops = _PLANS[kind]
    res = [_attend(q_ref, segs, k_ref, v_ref, n_chunks, width, unroll=True) for segs in loops]

    def tiles(r):
        return [r[i * tq:(i + 1) * tq] for i in range(r.shape[0] // tq)]

    if kind == "gqa":
        h = [t for r in res[0] for t in tiles(r)]
        blocks = [_halves(h[0], pltpu.roll(h[1], HALF, 1), h[0].shape),
                  _halves(pltpu.roll(h[2], HALF, 1), h[3], h[0].shape)]
    elif kind == "mla":
        h = [r[:, :LANES].astype(BF16) for r in tiles(res[0][0])]
        blocks = [_dot(jnp.concatenate(h[2 * ob:2 * ob + 2], axis=1), wv_ref[ob]) for ob in range(2)]
    else:
        lv = lam_ref[...]
        lam = (jnp.exp(jnp.sum(lv[0:1] * lv[1:2], axis=-1, keepdims=True))
               - jnp.exp(jnp.sum(lv[2:3] * lv[3:4], axis=-1, keepdims=True)) + lambda_init)
        blocks = []
        for (r,) in res:
            a = tiles(r)
            o = _halves(a[0] - lam * a[1], a[2] - lam * a[3], a[0].shape)
            blocks.append(o * _head_rms(o, _lane(o.shape) < HALF) * g_ref[...] * (1.0 - lambda_init))
    for ob, blk in enumerate(blocks):
        o_ref[:, ob * LANES:(ob + 1) * LANES] = blk.astype(BF16)


def _qkv_specs(lay, latent, s_all, tq):
    (q0, qn), (k0, kn), (v0, vn) = lay
    assert q0 % qn == 0 and k0 % kn == 0 and v0 % vn == 0
    ctx_blk = s_all // CTX_TILE - 1
    if latent:
        q_map = lambda i, t: (i, t, q0 // qn)
        k_map = lambda i, t: (i, 0, k0 // kn)
        v_map = lambda i, t: (i, 0, v0 // vn)
    else:
        q_map = lambda i, t: (i, ctx_blk, q0 // qn)
        k_map = lambda i, t: (i, ctx_blk, k0 // kn)
        v_map = lambda i, t: (i, ctx_blk, v0 // vn)
    kv_rows = s_all if latent else CTX_TILE
    return [
        pl.BlockSpec((None, tq, qn * LANES), q_map),
        pl.BlockSpec((None, kv_rows, kn * LANES), k_map),
        pl.BlockSpec((None, kv_rows, vn * LANES), v_map),
    ]


def dense_attn_call(kind, qkv, latent, extra=None, lambda_init=None):
    b, s_all, _ = qkv.shape
    tq = ATT_TILE if latent else CTX_TILE
    sq = s_all - CTX_TILE if latent else CTX_TILE
    lay = {"diff": ((G_QD, 2), (G_KD, 2), (G_VD, 4)),
           "mla": ((G_QM, 8), (G_KM, 2), (G_VM, 2)),
           "gqa": ((G_QG, 4), (G_KG, 1), (G_VG, 2))}[kind]
    specs = _qkv_specs(lay, latent, s_all, tq)
    args = [qkv, qkv, qkv]
    if kind == "diff":
        specs = [pl.BlockSpec((4, DIFF_QK_DIM), lambda i, t: (0, 0)),
                 pl.BlockSpec((1, LANES), lambda i, t: (0, 0))] + specs
        args = list(extra) + args
    elif kind == "mla":
        specs = [pl.BlockSpec((2, 2 * LANES, LANES), lambda i, t: (0, 0, 0))] + specs
        args = list(extra) + args
    body = functools.partial(dense_kernel, kind=kind, n_chunks=KV_CHUNKS if latent else 1,
                             lambda_init=lambda_init)
    return pl.pallas_call(
        body,
        name=kind + ("_attn" if latent else "_attn_ctx"),
        grid=(b, sq // tq),
        in_specs=specs,
        out_specs=pl.BlockSpec((None, tq, MIX_COLS), lambda i, t: (i, t, 0)),
        out_shape=jax.ShapeDtypeStruct((b, sq, MIX_COLS), BF16),
        compiler_params=pltpu.CompilerParams(
            dimension_semantics=("arbitrary", "arbitrary"), vmem_limit_bytes=VMEM_LIMIT),
    )(*args)


def swa_kernel(sink_ref, q_ref, k_ref, v_ref, o_ref, *, s_lat):
    tq = q_ref.shape[0]
    heads = q_ref.shape[1] // LANES
    rows = heads * tq
    t = pl.program_id(1)
    q = jnp.concatenate([_group(q_ref, slice(None), g) for g in range(heads)], axis=0)
    row = lax.broadcasted_iota(jnp.int32, (rows, 1), 0)
    sink = jnp.zeros((rows, 1), F32)
    for h in range(heads):
        sink = jnp.where(row // tq == h, sink_ref[h] * LOG2E, sink)

    spans = [(slice(s_lat, s_lat + CTX_TILE), None)]
    if s_lat:
        width = tq + 2 * WINDOW
        start = jnp.clip(t * tq - WINDOW, 0, s_lat - width)
        rel = (start - t * tq + lax.broadcasted_iota(jnp.int32, (tq, width), 1)
               - lax.broadcasted_iota(jnp.int32, (tq, width), 0))
        valid = jnp.concatenate([jnp.abs(rel) <= WINDOW] * heads, axis=0)
        spans.append((pl.ds(pl.multiple_of(start, WINDOW), width), valid))

    ss = []
    for span, valid in spans:
        s = _dot_nt(q, _group(k_ref, span, 0))
        ss.append(s if valid is None else jnp.where(valid, s, NEG_INF))
    m = functools.reduce(jnp.maximum, [jnp.max(s, axis=-1, keepdims=True) for s in ss] + [sink])
    ps = [jnp.exp2(s - m).astype(BF16) for s in ss]
    half = rows // 2
    acc = jnp.concatenate(
        [sum(_dot(p[vb * half:(vb + 1) * half], _group(v_ref, span, vb))
             for p, (span, _) in zip(ps, spans)) for vb in range(2)], axis=0)
    res = acc / (pltpu.roll(acc, HALF, 1) + jnp.exp2(sink - m))
    h = [res[i * tq:(i + 1) * tq] for i in range(heads)]
    o_ref[:, :LANES] = _halves(h[0], pltpu.roll(h[1], HALF, 1), h[0].shape).astype(BF16)
    o_ref[:, LANES:] = _halves(pltpu.roll(h[2], HALF, 1), h[3], h[0].shape).astype(BF16)


def swa_call(sink, qkv, latent):
    b, s_all, _ = qkv.shape
    tq = TOK_TILE
    s_lat = s_all - CTX_TILE if latent else 0
    sq = s_lat if latent else CTX_TILE
    specs = [pl.BlockSpec(memory_space=pltpu.SMEM)]
    specs += _qkv_specs(((G_QS, 4), (G_KS, 1), (G_VS, 2)), latent, s_all, tq)
    return pl.pallas_call(
        functools.partial(swa_kernel, s_lat=s_lat),
        name="swa_attn" if latent else "swa_attn_ctx",
        grid=(b, sq // tq),
        in_specs=specs,
        out_specs=pl.BlockSpec((None, tq, MIX_COLS), lambda i, t: (i, t, 0)),
        out_shape=jax.ShapeDtypeStruct((b, sq, MIX_COLS), BF16),
        compiler_params=pltpu.CompilerParams(
            dimension_semantics=("arbitrary", "arbitrary"), vmem_limit_bytes=VMEM_LIMIT),
    )(sink, qkv, qkv, qkv)


def mixout_kernel(x_ref, ya_ref, yb_ref, ym_ref, yd_ref, w_ref, mod_ref, gpost_ref,
                  gfpre_ref, x1_ref, h2_ref, za_ref, zb_ref):
    g = pl.program_id(0)

    @pl.when(g == 0)
    def _():
        zb_ref[...] = jnp.zeros_like(zb_ref)

    def project(dst_ref):
        y = jnp.concatenate([ya_ref[...], yb_ref[...], ym_ref[...], yd_ref[...]], axis=1)
        dst_ref[...] = _dot(y, w_ref[...])

    def finish(z_ref):
        z = z_ref[...]
        x1 = x_ref[...] + mod_ref[2:3, :] * (z * _rms(z, D_MODEL) * gpost_ref[...])
        x1_ref[...] = x1
        h2 = x1 * _rms(x1, D_MODEL) * gfpre_ref[...]
        h2_ref[...] = (h2 * (1.0 + mod_ref[4:5, :]) + mod_ref[3:4, :]).astype(BF16)

    @pl.when(g % 2 == 0)
    def _():
        project(za_ref)
        finish(zb_ref)

    @pl.when(g % 2 == 1)
    def _():
        project(zb_ref)
        finish(za_ref)


def mixout_call(x, ys, mod, mod_row, lw):
    b, s, d = x.shape
    tq = min(MIX_TILE, s)
    per = s // tq
    n = b * per

    def proj_tile(g):
        ga = jnp.minimum(g, n - 1)
        return ga // per, ga % per, 0

    def done_tile(g):
        gb = jnp.maximum(g - 1, 0)
        return gb // per, gb % per, 0

    if mod_row is None:
        mod_map = lambda g: (done_tile(g)[0], 0, 0)
    else:
        mod_map = lambda g: (mod_row, 0, 0)
    const2 = lambda g: (0, 0)
    y_spec = pl.BlockSpec((None, tq, MIX_COLS), proj_tile)
    return pl.pallas_call(
        mixout_kernel,
        name="mixout",
        grid=(n + 1,),
        in_specs=[pl.BlockSpec((None, tq, d), done_tile), y_spec, y_spec, y_spec, y_spec,
                  pl.BlockSpec((4 * MIX_COLS, d), const2),
                  pl.BlockSpec((None, N_MOD, d), mod_map),
                  pl.BlockSpec((1, d), const2), pl.BlockSpec((1, d), const2)],
        out_specs=[pl.BlockSpec((None, tq, d), done_tile), pl.BlockSpec((None, tq, d), done_tile)],
        out_shape=[jax.ShapeDtypeStruct((b, s, d), F32), jax.ShapeDtypeStruct((b, s, d), BF16)],
        scratch_shapes=[pltpu.VMEM((tq, d), F32), pltpu.VMEM((tq, d), F32)],
        compiler_params=pltpu.CompilerParams(
            dimension_semantics=("arbitrary",), vmem_limit_bytes=VMEM_LIMIT),
    )(x, *ys, lw["w_out"], mod, lw["g_mix_post"], lw["g_ffn_pre"])


def ffn_kernel(x1_ref, hc_ref, hp_ref, hn_ref, wup_ref, cw_ref, cb_ref, wdn_ref, mod_ref,
               gpost_ref, o_ref, g_ref):
    t = pl.program_id(1)
    tq = hc_ref.shape[0]
    halo = hp_ref.shape[0]
    hp = jnp.where(t > 0, hp_ref[...], jnp.zeros_like(hp_ref))
    hn = jnp.where(t < pl.num_programs(1) - 1, hn_ref[...], jnp.zeros_like(hn_ref))
    hext = jnp.concatenate([hp, hc_ref[...], hn], axis=0)
    rows = hext.shape[0]

    def conv_cols(cols):
        u = _dot(hext, wup_ref[:, cols])
        w = cw_ref[:, cols]
        return (pltpu.roll(u, 1, 0)[halo:halo + tq] * w[0:1]
                + u[halo:halo + tq] * w[1:2]
                + pltpu.roll(u, rows - 1, 0)[halo:halo + tq] * w[2:3] + cb_ref[:, cols])

    for j in range(FFN_DIM // FFN_CHUNK):
        a = conv_cols(slice(j * FFN_CHUNK, (j + 1) * FFN_CHUNK))
        v = conv_cols(slice(FFN_DIM + j * FFN_CHUNK, FFN_DIM + (j + 1) * FFN_CHUNK))
        g_ref[:, j * FFN_CHUNK:(j + 1) * FFN_CHUNK] = (a * (1.0 / (1.0 + jnp.exp(-a))) * v).astype(BF16)

    f = _dot(g_ref[...], wdn_ref[...])
    o_ref[...] = x1_ref[...] + mod_ref[5:6, :] * (f * _rms(f, D_MODEL) * gpost_ref[...])


def ffn_call(x1, h2, mod, mod_row, lw):
    b, s, d = x1.shape
    tq = min(FFN_TILE, s)
    halo = BF16_SUBLANES
    per_tile = tq // halo
    last = s // halo - 1
    if mod_row is None:
        mod_map = lambda i, t: (i, 0, 0)
    else:
        mod_map = lambda i, t: (mod_row, 0, 0)
    const2 = lambda i, t: (0, 0)
    tok = lambda i, t: (i, t, 0)
    return pl.pallas_call(
        ffn_kernel,
        name="conv_ffn",
        grid=(b, s // tq),
        in_specs=[
            pl.BlockSpec((None, tq, d), tok),
            pl.BlockSpec((None, tq, d), tok),
            pl.BlockSpec((None, halo, d), lambda i, t: (i, jnp.maximum(t * per_tile - 1, 0), 0)),
            pl.BlockSpec((None, halo, d), lambda i, t: (i, jnp.minimum((t + 1) * per_tile, last), 0)),
            pl.BlockSpec((d, 2 * FFN_DIM), const2, pipeline_mode=pl.Buffered(1)),
            pl.BlockSpec((3, 2 * FFN_DIM), const2),
            pl.BlockSpec((1, 2 * FFN_DIM), const2),
            pl.BlockSpec((FFN_DIM, d), const2, pipeline_mode=pl.Buffered(1)),
            pl.BlockSpec((None, N_MOD, d), mod_map),
            pl.BlockSpec((1, d), const2),
        ],
        out_specs=pl.BlockSpec((None, tq, d), tok),
        out_shape=jax.ShapeDtypeStruct((b, s, d), F32),
        scratch_shapes=[pltpu.VMEM((tq, FFN_DIM), BF16)],
        compiler_params=pltpu.CompilerParams(
            dimension_semantics=("arbitrary", "arbitrary"), vmem_limit_bytes=VMEM_LIMIT),
    )(x1, h2, h2, h2, lw["w_up"], lw["conv_w"], lw["conv_b"], lw["w_down"], mod,
      lw["g_ffn_post"])


def _pair_layout(rot_dim, n_units):
    n = rot_dim // 4
    lane = np.arange(LANES)
    half, mu = lane // HALF, lane % HALF
    unit, r = mu // (2 * n), mu % (2 * n)
    axis, f = r // n, r % n
    used = unit < n_units
    orig = np.where(used, unit * rot_dim + axis * 2 * n + half * n + f, -1)
    return orig, axis, f, used


def _rope_table(n_ctx, s, rot_dim, n_units):
    n = rot_dim // 4
    _, axis, f, used = _pair_layout(rot_dim, n_units)
    tok = np.arange(s)
    pos = np.stack([tok // GRID_W, tok % GRID_W], axis=1).astype(np.float32)
    inv_freq = ROPE_THETA ** (-jnp.arange(n, dtype=F32) / n)
    ang = jnp.asarray(pos)[:, axis] * inv_freq[f][None, :]
    sign = np.where(np.arange(LANES) < HALF, -1.0, 1.0).astype(np.float32)
    cos = jnp.where(used[None], jnp.cos(ang), 1.0)
    sin = jnp.where(used[None], jnp.sin(ang) * sign[None], 0.0)
    tab = jnp.stack([cos, sin]).astype(F32)
    ident = jnp.stack([jnp.ones((n_ctx, LANES), F32), jnp.zeros((n_ctx, LANES), F32)])
    return jnp.concatenate([tab, ident], axis=1)


def _take_cols(a, idx):
    idx = np.asarray(idx)
    return jnp.where(idx >= 0, jnp.take(a, np.maximum(idx, 0), axis=-1), 0.0)


def _layer_weights(l, w):
    d = D_MODEL
    w_in = w["w_in"][l]
    mla0 = 768 + 512
    cq_hi = mla0 + MLA_Q_RANK
    kr0 = cq_hi + MLA_KV_RANK
    w_in_p = jnp.concatenate([
        w_in[:, :mla0 + LANES],
        w_in[:, mla0 + LANES:cq_hi], w_in[:, kr0:kr0 + MLA_ROPE_DIM],
        jnp.zeros((d, LANES - HALF - MLA_ROPE_DIM), F32),
        w_in[:, cq_hi:kr0],
        w_in[:, kr0 + MLA_ROPE_DIM:],
    ], axis=1)
    o32, _, _, _ = _pair_layout(DIFF_QK_DIM, LANES // DIFF_QK_DIM)
    o64, _, _, _ = _pair_layout(GQA_HEAD_DIM, LANES // GQA_HEAD_DIM)
    oml, _, _, rope_lane = _pair_layout(MLA_ROPE_DIM, 1)
    cols = np.arange(P_COLS)
    for g, o in ((0, o32), (1, o32), (2, o32), (3, o32), (6, o64), (7, o64), (8, o64),
                 (13, o64), (14, o64), (15, o64)):
        cols[g * LANES:(g + 1) * LANES] = g * LANES + o
    cq_lanes = [i for i in range(LANES) if not rope_lane[i]][:MLA_Q_RANK - LANES]
    g11 = np.full(LANES, -1)
    g11[rope_lane] = HALF + oml[rope_lane]
    g11[cq_lanes] = np.arange(len(cq_lanes))
    cols[11 * LANES:12 * LANES] = np.where(g11 >= 0, 11 * LANES + g11, -1)
    w_in_p = _take_cols(w_in_p, cols)
    cq_src = np.full(2 * LANES, -1)
    cq_src[:LANES] = np.arange(LANES)
    cq_src[LANES + np.asarray(cq_lanes)] = LANES + np.arange(len(cq_lanes))

    dqk = MLA_NOPE_DIM + MLA_ROPE_DIM
    assert MLA_KV_RANK == LANES
    wuq = w["mla_w_uq"][l].reshape(MLA_Q_RANK, MLA_HEADS, dqk)
    wukv = w["mla_w_ukv"][l].reshape(MLA_KV_RANK, MLA_HEADS, MLA_NOPE_DIM + MLA_V_DIM)
    row_pad = 2 * LANES - MLA_Q_RANK
    wuq_n = jnp.pad(jnp.transpose(wuq[:, :, :MLA_NOPE_DIM], (1, 0, 2)), ((0, 0), (0, row_pad), (0, 0)))
    w_abs = absorb_call(wuq_n, jnp.transpose(wukv[:, :, :MLA_NOPE_DIM], (1, 2, 0)))
    wq_r = jnp.pad(_take_cols(wuq[:, :, MLA_NOPE_DIM:], np.where(rope_lane, oml, -1)),
                   ((0, row_pad), (0, 0), (0, 0)))
    wq = jnp.concatenate([jnp.transpose(w_abs, (1, 0, 2)).reshape(2 * LANES, -1),
                          wq_r.reshape(2 * LANES, -1)], axis=1)
    wq = _take_cols(wq.T, cq_src).T
    wuv = jnp.transpose(wukv[:, :, MLA_NOPE_DIM:], (1, 0, 2))
    wv = jnp.zeros((2, 2, LANES, 2, MLA_V_DIM), F32)
    for hd in range(MLA_HEADS):
        wv = wv.at[hd // 2, hd % 2, :, hd % 2, :].set(wuv[hd])
    wv = wv.reshape(2, 2 * LANES, LANES)
    gq = _take_cols(jnp.pad(w["mla_g_q"][l], (0, row_pad)), cq_src)[None]
    return {
        "g_mix_pre": w["g_mix_pre"][l][None], "g_mix_post": w["g_mix_post"][l][None],
        "g_ffn_pre": w["g_ffn_pre"][l][None], "g_ffn_post": w["g_ffn_post"][l][None],
        "w_in": w_in_p.astype(BF16),
        "wq": wq.astype(BF16), "wv": wv.astype(BF16),
        "gq": gq, "gkv": w["mla_g_kv"][l][None],
        "gnq": w["gqa_g_q"][l][o64 % GQA_HEAD_DIM][None],
        "gnk": w["gqa_g_k"][l][o64 % GQA_HEAD_DIM][None],
        "lam": jnp.stack([w["diff_lam_q1"][l], w["diff_lam_k1"][l],
                          w["diff_lam_q2"][l], w["diff_lam_k2"][l]]),
        "g_sub": jnp.tile(w["diff_g_sub"][l], 2)[None],
        "sink": w["swa_sink"][l],
        "w_out": w["w_out"][l].astype(BF16),
        "w_up": w["ffn_w_up"][l].astype(BF16),
        "conv_w": w["ffn_conv_w"][l],
        "conv_b": w["ffn_conv_b"][l][None],
        "w_down": w["ffn_w_down"][l].astype(BF16),
    }


def kernel(x, c, ctx, c_ctx, w_mod, b_mod, g_mix_pre, g_mix_post, g_ffn_pre, g_ffn_post, w_in, diff_lam_q1, diff_lam_k1, diff_lam_q2, diff_lam_k2, diff_g_sub, swa_sink, mla_g_q, mla_g_kv, mla_w_uq, mla_w_ukv, gqa_g_q, gqa_g_k, w_out, ffn_w_up, ffn_conv_w, ffn_conv_b, ffn_w_down):
    w = dict(g_mix_pre=g_mix_pre, g_mix_post=g_mix_post, g_ffn_pre=g_ffn_pre,
             g_ffn_post=g_ffn_post, w_in=w_in, diff_lam_q1=diff_lam_q1, diff_lam_k1=diff_lam_k1,
             diff_lam_q2=diff_lam_q2, diff_lam_k2=diff_lam_k2, diff_g_sub=diff_g_sub,
             swa_sink=swa_sink, mla_g_q=mla_g_q, mla_g_kv=mla_g_kv, mla_w_uq=mla_w_uq,
             mla_w_ukv=mla_w_ukv, gqa_g_q=gqa_g_q, gqa_g_k=gqa_g_k, w_out=w_out,
             ffn_w_up=ffn_w_up, ffn_conv_w=ffn_conv_w, ffn_conv_b=ffn_conv_b,
             ffn_w_down=ffn_w_down)
    b, s, d = x.shape
    depth = w_mod.shape[0]
    ctx_row = b
    mod_rows = 16
    cc = jnp.zeros((mod_rows, d), F32).at[:b].set(c).at[ctx_row].set(c_ctx)
    mod_all = mod_call(cc, w_mod, b_mod).reshape(depth, mod_rows, N_MOD, d)

    n_ctx = ctx.shape[1]
    tabs = (_rope_table(n_ctx, s, DIFF_QK_DIM, LANES // DIFF_QK_DIM),
            _rope_table(n_ctx, s, GQA_HEAD_DIM, LANES // GQA_HEAD_DIM),
            _rope_table(n_ctx, s, MLA_ROPE_DIM, 1))

    xc = ctx
    for l in range(depth):
        need_ctx = l < depth - 1
        lambda_init = 0.8 - 0.6 * math.exp(-0.3 * l)
        lw = _layer_weights(l, w)
        mod = mod_all[l]
        qkv = proj_call(xc, x, mod, ctx_row, lw, tabs)
        diff_extra = (lw["lam"], lw["g_sub"])
        mla_extra = (lw["wv"],)
        ys = (dense_attn_call("diff", qkv, True, diff_extra, lambda_init),
              swa_call(lw["sink"], qkv, True),
              dense_attn_call("mla", qkv, True, mla_extra),
              dense_attn_call("gqa", qkv, True))
        x1, h2 = mixout_call(x, ys, mod, None, lw)
        x = ffn_call(x1, h2, mod, None, lw)
        if need_ctx:
            ysc = (dense_attn_call("diff", qkv, False, diff_extra, lambda_init),
                   swa_call(lw["sink"], qkv, False),
                   dense_attn_call("mla", qkv, False, mla_extra),
                   dense_attn_call("gqa", qkv, False))
            xc1, hc2 = mixout_call(xc, ysc, mod, ctx_row, lw)
            xc = ffn_call(xc1, hc2, mod, ctx_row, lw)
    return x
```

```python
import functools
import math

import jax
import jax.numpy as jnp
import numpy as np
from jax import lax
from jax.experimental import pallas as pl
from jax.experimental.pallas import tpu as pltpu

F32 = jnp.float32
BF16 = jnp.bfloat16

D_MODEL = 1024
GRID_W = 64
ROPE_THETA = 10000.0
NORM_EPS = 1e-6
NEG_INF = -1e30
N_MOD = 6

DIFF_QK_DIM = 32
SWA_HEAD_DIM, WINDOW = 64, 128
MLA_Q_RANK, MLA_KV_RANK = 192, 128
MLA_NOPE_DIM, MLA_ROPE_DIM, MLA_V_DIM, MLA_HEADS = 64, 32, 64, 4
GQA_HEAD_DIM = 64
FFN_DIM = 2816
MIX_COLS = 256

LANES = 128
BF16_SUBLANES = 16
HALF = LANES // 2
VMEM_LIMIT = 56 * 1024 * 1024

P_COLS = 17 * LANES
G_QM, G_VD, G_QG, G_QS = 0, 8, 12, 16
G_QD, G_KD, G_VG, G_KM, G_VM, G_VS, G_KG, G_KS = 20, 22, 24, 26, 28, 30, 32, 33
QKV_COLS = 34 * LANES
LOG2E = math.log2(math.e)

TOK_TILE = 256
CTX_TILE = 256
ATT_TILE = 512
KV_CHUNKS = 2
MIX_TILE = 1024
FFN_TILE = 1024
FFN_CHUNK = 256
MOD_TILE = 2048


def _rms(x, n):
    return lax.rsqrt(jnp.sum(x * x, axis=-1, keepdims=True) * (1.0 / n) + NORM_EPS)


def _lane(shape):
    return lax.broadcasted_iota(jnp.int32, shape, len(shape) - 1)


def _dot(a, b):
    return jnp.dot(a, b, preferred_element_type=F32)


def _dot_nt(a, b):
    return lax.dot_general(a, b, (((1,), (1,)), ((), ())), preferred_element_type=F32)


def mod_kernel(c_ref, w_ref, b_ref, o_ref):
    c = c_ref[...]
    s = c * (1.0 / (1.0 + jnp.exp(-c)))
    o_ref[...] = jnp.dot(s, w_ref[...], preferred_element_type=F32,
                         precision=lax.Precision.HIGHEST) + b_ref[...]


def mod_call(cc, w_mod, b_mod):
    depth, d, n = w_mod.shape
    rows = cc.shape[0]
    tn = MOD_TILE
    return pl.pallas_call(
        mod_kernel,
        name="mod_vectors",
        grid=(depth, n // tn),
        in_specs=[
            pl.BlockSpec((rows, d), lambda l, j: (0, 0)),
            pl.BlockSpec((None, d, tn), lambda l, j: (l, 0, j)),
            pl.BlockSpec((None, 1, tn), lambda l, j: (l, 0, j)),
        ],
        out_specs=pl.BlockSpec((None, rows, tn), lambda l, j: (l, 0, j)),
        out_shape=jax.ShapeDtypeStruct((depth, rows, n), F32),
        compiler_params=pltpu.CompilerParams(
            dimension_semantics=("arbitrary", "arbitrary"), vmem_limit_bytes=VMEM_LIMIT),
    )(cc, w_mod, b_mod.reshape(depth, 1, n))


def absorb_kernel(a_ref, b_ref, o_ref):
    o_ref[...] = jnp.dot(a_ref[...], b_ref[...], preferred_element_type=F32,
                         precision=lax.Precision.HIGHEST)


def absorb_call(a, b):
    heads, m, k = a.shape
    n = b.shape[2]
    return pl.pallas_call(
        absorb_kernel,
        name="mla_absorb",
        grid=(heads,),
        in_specs=[pl.BlockSpec((None, m, k), lambda h: (h, 0, 0)),
                  pl.BlockSpec((None, k, n), lambda h: (h, 0, 0))],
        out_specs=pl.BlockSpec((None, m, n), lambda h: (h, 0, 0)),
        out_shape=jax.ShapeDtypeStruct((heads, m, n), F32),
        compiler_params=pltpu.CompilerParams(dimension_semantics=("arbitrary",)),
    )(a, b)


def _rope(x, tab_ref):
    return x * tab_ref[0] + pltpu.roll(x, HALF, 1) * tab_ref[1]


def _halves(lo, hi, shape):
    return jnp.where(_lane(shape) < HALF, lo, hi)


def _head_rms(x, first):
    sq = x * x
    ss_a = jnp.sum(jnp.where(first, sq, 0.0), axis=-1, keepdims=True)
    ss_b = jnp.sum(jnp.where(first, 0.0, sq), axis=-1, keepdims=True)
    r_a = lax.rsqrt(ss_a * (1.0 / HALF) + NORM_EPS)
    r_b = lax.rsqrt(ss_b * (1.0 / HALF) + NORM_EPS)
    return jnp.where(first, r_a, r_b)


def proj_kernel(xc_ref, x_ref, mod_ref, gpre_ref, win_ref, wq_ref, gq_ref, gkv_ref,
                gnq_ref, gnk_ref, t32_ref, t64_ref, tml_ref, o_ref, pa_ref, pb_ref, *,
                tiles_per_sample, n_tiles):
    g = pl.program_id(0)
    is_ctx = jnp.minimum(g, n_tiles - 1) % tiles_per_sample == tiles_per_sample - 1

    @pl.when(g == 0)
    def _():
        pb_ref[...] = jnp.zeros_like(pb_ref)

    def project(dst_ref):
        x = jnp.where(is_ctx, xc_ref[...], x_ref[...])
        h = x * _rms(x, D_MODEL) * gpre_ref[...]
        h = h * (1.0 + mod_ref[1:2, :]) + mod_ref[0:1, :]
        dst_ref[...] = _dot(h.astype(BF16), win_ref[...])

    finish = functools.partial(_proj_finish, wq_ref, gq_ref, gkv_ref, gnq_ref, gnk_ref,
                               t32_ref, t64_ref, tml_ref, o_ref)

    @pl.when(g % 2 == 0)
    def _():
        project(pa_ref)
        finish(pb_ref)

    @pl.when(g % 2 == 1)
    def _():
        project(pb_ref)
        finish(pa_ref)


def _proj_finish(wq_ref, gq_ref, gkv_ref, gnq_ref, gnk_ref, t32_ref, t64_ref, tml_ref,
                 o_ref, p_ref):
    def grp(g):
        return p_ref[:, g * LANES:(g + 1) * LANES]

    def put(g, v):
        o_ref[:, g * LANES:(g + 1) * LANES] = v.astype(BF16)

    def rope32(v):
        return _rope(v, t32_ref)

    def rope64(v):
        return _rope(v, t64_ref)

    def ropeml(v):
        return _rope(v, tml_ref)

    lane = _lane((p_ref.shape[0], LANES))
    low = lane < HALF
    head_a = lane % HALF < HALF // 2

    def with_ones(v):
        return jnp.where(low, v, 1.0), jnp.where(low, 1.0, v)

    for g in range(2):
        put(G_QD + g, rope32(grp(g)) * (DIFF_QK_DIM ** -0.5 * LOG2E))
        put(G_KD + g, rope32(grp(2 + g)))
        v_lo, v_hi = with_ones(grp(4 + g))
        put(G_VD + 2 * g, v_lo)
        put(G_VD + 2 * g + 1, v_hi)

    def put_gqa_queries(g0, q01, q23):
        put(g0, jnp.where(head_a, q01, 0.0))
        put(g0 + 1, jnp.where(head_a, pltpu.roll(q01, LANES - HALF // 2, 1), 0.0))
        put(g0 + 2, jnp.where(head_a, 0.0, pltpu.roll(q23, HALF // 2, 1)))
        put(g0 + 3, jnp.where(head_a, 0.0, q23))

    scale_s = SWA_HEAD_DIM ** -0.5 * LOG2E
    put_gqa_queries(G_QS, rope64(grp(6)) * scale_s, rope64(grp(7)) * scale_s)
    put(G_KS, rope64(grp(8)))
    v_lo, v_hi = with_ones(grp(9))
    put(G_VS, v_lo)
    put(G_VS + 1, v_hi)

    cq0, cq1, ckv = grp(10), grp(11), grp(12)
    rope_lanes = lane % HALF < MLA_ROPE_DIM // 2
    cq_lanes = ~rope_lanes & (lane < HALF + MLA_ROPE_DIM)
    ss = (jnp.sum(cq0 * cq0, axis=-1, keepdims=True)
          + jnp.sum(jnp.where(cq_lanes, cq1 * cq1, 0.0), axis=-1, keepdims=True))
    rq = lax.rsqrt(ss * (1.0 / MLA_Q_RANK) + NORM_EPS)
    cqn = jnp.concatenate([cq0 * rq * gq_ref[:, :LANES], cq1 * rq * gq_ref[:, LANES:]], axis=1)
    qm = _dot(cqn.astype(BF16), wq_ref[...])
    scale_m = (MLA_NOPE_DIM + MLA_ROPE_DIM) ** -0.5 * LOG2E
    for hd in range(MLA_HEADS):
        put(G_QM + 2 * hd, qm[:, hd * LANES:(hd + 1) * LANES] * scale_m)
        qr = qm[:, (MLA_HEADS + hd) * LANES:(MLA_HEADS + hd + 1) * LANES]
        put(G_QM + 2 * hd + 1, ropeml(qr) * scale_m)
    ckvn = ckv * _rms(ckv, MLA_KV_RANK) * gkv_ref[...]
    put(G_KM, ckvn)
    put(G_KM + 1, jnp.where(rope_lanes, ropeml(cq1), 0.0))
    put(G_VM, ckvn)
    put(G_VM + 1, jnp.ones_like(ckvn))

    scale_g = GQA_HEAD_DIM ** -0.5 * LOG2E
    q01 = grp(13)
    q01 = rope64(q01 * _head_rms(q01, head_a) * gnq_ref[...]) * scale_g
    q23 = grp(14)
    q23 = rope64(q23 * _head_rms(q23, head_a) * gnq_ref[...]) * scale_g
    put_gqa_queries(G_QG, q01, q23)
    kg = grp(15)
    put(G_KG, rope64(kg * _head_rms(kg, head_a) * gnk_ref[...]))
    v_lo, v_hi = with_ones(grp(16))
    put(G_VG, v_lo)
    put(G_VG + 1, v_hi)


def proj_call(xc, x, mod, ctx_row, lw, tabs):
    b, s, d = x.shape
    tq = TOK_TILE
    assert xc.shape[1] == CTX_TILE == tq
    s_all = CTX_TILE + s
    per = s_all // tq
    n = b * per
    const2 = lambda g: (0, 0)

    def proj_tile(g):
        ga = jnp.minimum(g, n - 1)
        return ga // per, ga % per

    def done_tile(g):
        gb = jnp.maximum(g - 1, 0)
        return gb // per, gb % per

    def x_map(g):
        i, t = proj_tile(g)
        return i, jnp.minimum(t, per - 2), 0

    def mod_map(g):
        i, t = proj_tile(g)
        return jnp.where(t == per - 1, ctx_row, i), 0, 0

    tab_spec = pl.BlockSpec((2, tq, LANES), lambda g: (0, done_tile(g)[1], 0))
    return pl.pallas_call(
        functools.partial(proj_kernel, tiles_per_sample=per, n_tiles=n),
        name="proj",
        grid=(n + 1,),
        in_specs=[
            pl.BlockSpec((None, tq, d), lambda g: (proj_tile(g)[0], 0, 0)),
            pl.BlockSpec((None, tq, d), x_map),
            pl.BlockSpec((None, N_MOD, d), mod_map),
            pl.BlockSpec((1, d), const2),
            pl.BlockSpec((d, P_COLS), const2),
            pl.BlockSpec((2 * LANES, 2 * MLA_HEADS * LANES), const2),
            pl.BlockSpec((1, 2 * LANES), const2),
            pl.BlockSpec((1, LANES), const2),
            pl.BlockSpec((1, LANES), const2),
            pl.BlockSpec((1, LANES), const2),
            tab_spec, tab_spec, tab_spec,
        ],
        out_specs=pl.BlockSpec((None, tq, QKV_COLS), lambda g: done_tile(g) + (0,)),
        out_shape=jax.ShapeDtypeStruct((b, s_all, QKV_COLS), BF16),
        scratch_shapes=[pltpu.VMEM((tq, P_COLS), F32), pltpu.VMEM((tq, P_COLS), F32)],
        compiler_params=pltpu.CompilerParams(
            dimension_semantics=("arbitrary",), vmem_limit_bytes=VMEM_LIMIT),
    )(xc, x, mod, lw["g_mix_pre"], lw["w_in"], lw["wq"], lw["gq"], lw["gkv"],
      lw["gnq"], lw["gnk"], tabs[0], tabs[1], tabs[2])


def _group(ref, rows, g, width=1):
    return ref[rows, g * width * LANES:(g + 1) * width * LANES]


def _attend(q_ref, segs, k_ref, v_ref, n_chunks, width):
    chunk = k_ref.shape[0] // n_chunks

    def query(g):
        if isinstance(g, int):
            return _group(q_ref, slice(None), g, width)
        g, unit, n_units = g
        q = _group(q_ref, slice(None), g, width)
        per = HALF // n_units
        keep = _lane(q.shape) % HALF // per == unit
        return jnp.where(keep, q, jnp.zeros_like(q))

    qs = [jnp.concatenate([query(g) for g in qg], axis=0) for qg, _, _ in segs]

    def body(i, carry):
        rows = pl.ds(pl.multiple_of(i * chunk, LANES), chunk)
        out = []
        for j, (q, (_, kb, parts)) in enumerate(zip(qs, segs)):
            m, acc = carry[2 * j], carry[2 * j + 1]
            s = _dot_nt(q, _group(k_ref, rows, kb, width))
            m_new = jnp.maximum(m, jnp.max(s, axis=-1, keepdims=True))
            p = jnp.exp2(s - m_new).astype(BF16)
            tq = q.shape[0] // sum(n for n, _ in parts)
            pv, r0 = [], 0
            for n, vb in parts:
                pv.append(_dot(p[r0:r0 + n * tq], _group(v_ref, rows, vb, width)))
                r0 += n * tq
            pv = pv[0] if len(pv) == 1 else jnp.concatenate(pv, axis=0)
            out += [m_new, acc * jnp.exp2(m - m_new) + pv]
        return tuple(out)

    init = []
    for q in qs:
        init += [jnp.full((q.shape[0], 1), NEG_INF, F32),
                 jnp.zeros((q.shape[0], width * LANES), F32)]
    carry = lax.fori_loop(0, n_chunks, body, tuple(init), unroll=True)
    half = width * HALF
    return [carry[2 * j + 1] / pltpu.roll(carry[2 * j + 1], half, 1) for j in range(len(segs))]


_PLANS = {
    "gqa": (1, ((((0, 1, 2, 3), 0, ((2, 0), (2, 1))),),)),
    "mla": (2, ((((0, 1, 2, 3), 0, ((2, 0), (2, 0))),),)),
    "diff": (1, tuple(
        ((tuple((ob, i, 4) for i in range(4)), ob,
          ((2, 2 * ob), (2, 2 * ob + 1))),) for ob in range(2))),
}


def dense_kernel(*refs, kind, n_chunks, lambda_init):
    refs = list(refs)
    if kind == "diff":
        lam_ref, g_ref = refs[:2]
        refs = refs[2:]
    elif kind == "mla":
        wv_ref = refs[0]
        refs = refs[1:]
    q_ref, k_ref, v_ref, o_ref = refs
    tq = q_ref.shape[0]
    width, loops = _PLANS[kind]
    res = [_attend(q_ref, segs, k_ref, v_ref, n_chunks, width) for segs in loops]

    def tiles(r):
        return [r[i * tq:(i + 1) * tq] for i in range(r.shape[0] // tq)]

    if kind == "gqa":
        h = [t for r in res[0] for t in tiles(r)]
        blocks = [_halves(h[0], pltpu.roll(h[1], HALF, 1), h[0].shape),
                  _halves(pltpu.roll(h[2], HALF, 1), h[3], h[0].shape)]
    elif kind == "mla":
        h = [r[:, :LANES].astype(BF16) for r in tiles(res[0][0])]
        blocks = [_dot(jnp.concatenate(h[2 * ob:2 * ob + 2], axis=1), wv_ref[ob]) for ob in range(2)]
    else:
        lv = lam_ref[...]
        lam = (jnp.exp(jnp.sum(lv[0:1] * lv[1:2], axis=-1, keepdims=True))
               - jnp.exp(jnp.sum(lv[2:3] * lv[3:4], axis=-1, keepdims=True)) + lambda_init)
        blocks = []
        for (r,) in res:
            a = tiles(r)
            o = _halves(a[0] - lam * a[1], a[2] - lam * a[3], a[0].shape)
            blocks.append(o * _head_rms(o, _lane(o.shape) < HALF) * g_ref[...] * (1.0 - lambda_init))
    for ob, blk in enumerate(blocks):
        o_ref[:, ob * LANES:(ob + 1) * LANES] = blk.astype(BF16)


def _qkv_specs(lay, latent, s_all, tq):
    (q0, qn), (k0, kn), (v0, vn) = lay
    assert q0 % qn == 0 and k0 % kn == 0 and v0 % vn == 0
    ctx_blk = s_all // CTX_TILE - 1
    if latent:
        q_map = lambda i, t: (i, t, q0 // qn)
        k_map = lambda i, t: (i, 0, k0 // kn)
        v_map = lambda i, t: (i, 0, v0 // vn)
    else:
        q_map = lambda i, t: (i, ctx_blk, q0 // qn)
        k_map = lambda i, t: (i, ctx_blk, k0 // kn)
        v_map = lambda i, t: (i, ctx_blk, v0 // vn)
    kv_rows = s_all if latent else CTX_TILE
    return [
        pl.BlockSpec((None, tq, qn * LANES), q_map),
        pl.BlockSpec((None, kv_rows, kn * LANES), k_map),
        pl.BlockSpec((None, kv_rows, vn * LANES), v_map),
    ]


def dense_attn_call(kind, qkv, latent, extra=None, lambda_init=None):
    b, s_all, _ = qkv.shape
    tq = ATT_TILE if latent else CTX_TILE
    sq = s_all - CTX_TILE if latent else CTX_TILE
    lay = {"diff": ((G_QD, 2), (G_KD, 2), (G_VD, 4)),
           "mla": ((G_QM, 8), (G_KM, 2), (G_VM, 2)),
           "gqa": ((G_QG, 4), (G_KG, 1), (G_VG, 2))}[kind]
    specs = _qkv_specs(lay, latent, s_all, tq)
    args = [qkv, qkv, qkv]
    if kind == "diff":
        specs = [pl.BlockSpec((4, DIFF_QK_DIM), lambda i, t: (0, 0)),
                 pl.BlockSpec((1, LANES), lambda i, t: (0, 0))] + specs
        args = list(extra) + args
    elif kind == "mla":
        specs = [pl.BlockSpec((2, 2 * LANES, LANES), lambda i, t: (0, 0, 0))] + specs
        args = list(extra) + args
    body = functools.partial(dense_kernel, kind=kind, n_chunks=KV_CHUNKS if latent else 1,
                             lambda_init=lambda_init)
    return pl.pallas_call(
        body,
        name=kind + ("_attn" if latent else "_attn_ctx"),
        grid=(b, sq // tq),
        in_specs=specs,
        out_specs=pl.BlockSpec((None, tq, MIX_COLS), lambda i, t: (i, t, 0)),
        out_shape=jax.ShapeDtypeStruct((b, sq, MIX_COLS), BF16),
        compiler_params=pltpu.CompilerParams(
            dimension_semantics=("arbitrary", "arbitrary"), vmem_limit_bytes=VMEM_LIMIT),
    )(*args)


def swa_kernel(sink_ref, q_ref, k_ref, v_ref, o_ref, *, s_lat):
    tq = q_ref.shape[0]
    heads = q_ref.shape[1] // LANES
    rows = heads * tq
    t = pl.program_id(1)
    q = jnp.concatenate([_group(q_ref, slice(None), g) for g in range(heads)], axis=0)
    row = lax.broadcasted_iota(jnp.int32, (rows, 1), 0)
    sink = jnp.zeros((rows, 1), F32)
    for h in range(heads):
        sink = jnp.where(row // tq == h, sink_ref[h] * LOG2E, sink)

    spans = [(slice(s_lat, s_lat + CTX_TILE), None)]
    if s_lat:
        width = tq + 2 * WINDOW
        start = jnp.clip(t * tq - WINDOW, 0, s_lat - width)
        rel = (start - t * tq + lax.broadcasted_iota(jnp.int32, (tq, width), 1)
               - lax.broadcasted_iota(jnp.int32, (tq, width), 0))
        valid = jnp.concatenate([jnp.abs(rel) <= WINDOW] * heads, axis=0)
        spans.append((pl.ds(pl.multiple_of(start, WINDOW), width), valid))

    ss = []
    for span, valid in spans:
        s = _dot_nt(q, _group(k_ref, span, 0))
        ss.append(s if valid is None else jnp.where(valid, s, NEG_INF))
    m = functools.reduce(jnp.maximum, [jnp.max(s, axis=-1, keepdims=True) for s in ss] + [sink])
    ps = [jnp.exp2(s - m).astype(BF16) for s in ss]
    half = rows // 2
    acc = jnp.concatenate(
        [sum(_dot(p[vb * half:(vb + 1) * half], _group(v_ref, span, vb))
             for p, (span, _) in zip(ps, spans)) for vb in range(2)], axis=0)
    res = acc / (pltpu.roll(acc, HALF, 1) + jnp.exp2(sink - m))
    h = [res[i * tq:(i + 1) * tq] for i in range(heads)]
    o_ref[:, :LANES] = _halves(h[0], pltpu.roll(h[1], HALF, 1), h[0].shape).astype(BF16)
    o_ref[:, LANES:] = _halves(pltpu.roll(h[2], HALF, 1), h[3], h[0].shape).astype(BF16)


def swa_call(sink, qkv, latent):
    b, s_all, _ = qkv.shape
    tq = TOK_TILE
    s_lat = s_all - CTX_TILE if latent else 0
    sq = s_lat if latent else CTX_TILE
    specs = [pl.BlockSpec(memory_space=pltpu.SMEM)]
    specs += _qkv_specs(((G_QS, 4), (G_KS, 1), (G_VS, 2)), latent, s_all, tq)
    return pl.pallas_call(
        functools.partial(swa_kernel, s_lat=s_lat),
        name="swa_attn" if latent else "swa_attn_ctx",
        grid=(b, sq // tq),
        in_specs=specs,
        out_specs=pl.BlockSpec((None, tq, MIX_COLS), lambda i, t: (i, t, 0)),
        out_shape=jax.ShapeDtypeStruct((b, sq, MIX_COLS), BF16),
        compiler_params=pltpu.CompilerParams(
            dimension_semantics=("arbitrary", "arbitrary"), vmem_limit_bytes=VMEM_LIMIT),
    )(sink, qkv, qkv, qkv)


def mixout_kernel(x_ref, ya_ref, yb_ref, ym_ref, yd_ref, w_ref, mod_ref, gpost_ref,
                  gfpre_ref, x1_ref, h2_ref, za_ref, zb_ref):
    g = pl.program_id(0)

    @pl.when(g == 0)
    def _():
        zb_ref[...] = jnp.zeros_like(zb_ref)

    def project(dst_ref):
        y = jnp.concatenate([ya_ref[...], yb_ref[...], ym_ref[...], yd_ref[...]], axis=1)
        dst_ref[...] = _dot(y, w_ref[...])

    def finish(z_ref):
        z = z_ref[...]
        x1 = x_ref[...] + mod_ref[2:3, :] * (z * _rms(z, D_MODEL) * gpost_ref[...])
        x1_ref[...] = x1
        h2 = x1 * _rms(x1, D_MODEL) * gfpre_ref[...]
        h2_ref[...] = (h2 * (1.0 + mod_ref[4:5, :]) + mod_ref[3:4, :]).astype(BF16)

    @pl.when(g % 2 == 0)
    def _():
        project(za_ref)
        finish(zb_ref)

    @pl.when(g % 2 == 1)
    def _():
        project(zb_ref)
        finish(za_ref)


def mixout_call(x, ys, mod, mod_row, lw):
    b, s, d = x.shape
    tq = min(MIX_TILE, s)
    per = s // tq
    n = b * per

    def proj_tile(g):
        ga = jnp.minimum(g, n - 1)
        return ga // per, ga % per, 0

    def done_tile(g):
        gb = jnp.maximum(g - 1, 0)
        return gb // per, gb % per, 0

    if mod_row is None:
        mod_map = lambda g: (done_tile(g)[0], 0, 0)
    else:
        mod_map = lambda g: (mod_row, 0, 0)
    const2 = lambda g: (0, 0)
    y_spec = pl.BlockSpec((None, tq, MIX_COLS), proj_tile)
    return pl.pallas_call(
        mixout_kernel,
        name="mixout",
        grid=(n + 1,),
        in_specs=[pl.BlockSpec((None, tq, d), done_tile), y_spec, y_spec, y_spec, y_spec,
                  pl.BlockSpec((4 * MIX_COLS, d), const2),
                  pl.BlockSpec((None, N_MOD, d), mod_map),
                  pl.BlockSpec((1, d), const2), pl.BlockSpec((1, d), const2)],
        out_specs=[pl.BlockSpec((None, tq, d), done_tile), pl.BlockSpec((None, tq, d), done_tile)],
        out_shape=[jax.ShapeDtypeStruct((b, s, d), F32), jax.ShapeDtypeStruct((b, s, d), BF16)],
        scratch_shapes=[pltpu.VMEM((tq, d), F32), pltpu.VMEM((tq, d), F32)],
        compiler_params=pltpu.CompilerParams(
            dimension_semantics=("arbitrary",), vmem_limit_bytes=VMEM_LIMIT),
    )(x, *ys, lw["w_out"], mod, lw["g_mix_post"], lw["g_ffn_pre"])


def ffn_kernel(x1_ref, hc_ref, hp_ref, hn_ref, wup_ref, cw_ref, cb_ref, wdn_ref, mod_ref,
               gpost_ref, o_ref, g_ref):
    t = pl.program_id(1)
    tq = hc_ref.shape[0]
    halo = hp_ref.shape[0]
    hp = jnp.where(t > 0, hp_ref[...], jnp.zeros_like(hp_ref))
    hn = jnp.where(t < pl.num_programs(1) - 1, hn_ref[...], jnp.zeros_like(hn_ref))
    hext = jnp.concatenate([hp, hc_ref[...], hn], axis=0)
    rows = hext.shape[0]

    def conv_cols(cols):
        u = _dot(hext, wup_ref[:, cols])
        w = cw_ref[:, cols]
        return (pltpu.roll(u, 1, 0)[halo:halo + tq] * w[0:1]
                + u[halo:halo + tq] * w[1:2]
                + pltpu.roll(u, rows - 1, 0)[halo:halo + tq] * w[2:3] + cb_ref[:, cols])

    for j in range(FFN_DIM // FFN_CHUNK):
        a = conv_cols(slice(j * FFN_CHUNK, (j + 1) * FFN_CHUNK))
        v = conv_cols(slice(FFN_DIM + j * FFN_CHUNK, FFN_DIM + (j + 1) * FFN_CHUNK))
        g_ref[:, j * FFN_CHUNK:(j + 1) * FFN_CHUNK] = (a * (1.0 / (1.0 + jnp.exp(-a))) * v).astype(BF16)

    f = _dot(g_ref[...], wdn_ref[...])
    o_ref[...] = x1_ref[...] + mod_ref[5:6, :] * (f * _rms(f, D_MODEL) * gpost_ref[...])


def ffn_call(x1, h2, mod, mod_row, lw):
    b, s, d = x1.shape
    tq = min(FFN_TILE, s)
    halo = BF16_SUBLANES
    per_tile = tq // halo
    last = s // halo - 1
    if mod_row is None:
        mod_map = lambda i, t: (i, 0, 0)
    else:
        mod_map = lambda i, t: (mod_row, 0, 0)
    const2 = lambda i, t: (0, 0)
    tok = lambda i, t: (i, t, 0)
    return pl.pallas_call(
        ffn_kernel,
        name="conv_ffn",
        grid=(b, s // tq),
        in_specs=[
            pl.BlockSpec((None, tq, d), tok),
            pl.BlockSpec((None, tq, d), tok),
            pl.BlockSpec((None, halo, d), lambda i, t: (i, jnp.maximum(t * per_tile - 1, 0), 0)),
            pl.BlockSpec((None, halo, d), lambda i, t: (i, jnp.minimum((t + 1) * per_tile, last), 0)),
            pl.BlockSpec((d, 2 * FFN_DIM), const2, pipeline_mode=pl.Buffered(1)),
            pl.BlockSpec((3, 2 * FFN_DIM), const2),
            pl.BlockSpec((1, 2 * FFN_DIM), const2),
            pl.BlockSpec((FFN_DIM, d), const2, pipeline_mode=pl.Buffered(1)),
            pl.BlockSpec((None, N_MOD, d), mod_map),
            pl.BlockSpec((1, d), const2),
        ],
        out_specs=pl.BlockSpec((None, tq, d), tok),
        out_shape=jax.ShapeDtypeStruct((b, s, d), F32),
        scratch_shapes=[pltpu.VMEM((tq, FFN_DIM), BF16)],
        compiler_params=pltpu.CompilerParams(
            dimension_semantics=("arbitrary", "arbitrary"), vmem_limit_bytes=VMEM_LIMIT),
    )(x1, h2, h2, h2, lw["w_up"], lw["conv_w"], lw["conv_b"], lw["w_down"], mod,
      lw["g_ffn_post"])


def _pair_layout(rot_dim, n_units):
    n = rot_dim // 4
    lane = np.arange(LANES)
    half, mu = lane // HALF, lane % HALF
    unit, r = mu // (2 * n), mu % (2 * n)
    axis, f = r // n, r % n
    used = unit < n_units
    orig = np.where(used, unit * rot_dim + axis * 2 * n + half * n + f, -1)
    return orig, axis, f, used


def _rope_table(n_ctx, s, rot_dim, n_units):
    n = rot_dim // 4
    _, axis, f, used = _pair_layout(rot_dim, n_units)
    tok = np.arange(s)
    pos = np.stack([tok // GRID_W, tok % GRID_W], axis=1).astype(np.float32)
    inv_freq = ROPE_THETA ** (-jnp.arange(n, dtype=F32) / n)
    ang = jnp.asarray(pos)[:, axis] * inv_freq[f][None, :]
    sign = np.where(np.arange(LANES) < HALF, -1.0, 1.0).astype(np.float32)
    cos = jnp.where(used[None], jnp.cos(ang), 1.0)
    sin = jnp.where(used[None], jnp.sin(ang) * sign[None], 0.0)
    tab = jnp.stack([cos, sin]).astype(F32)
    ident = jnp.stack([jnp.ones((n_ctx, LANES), F32), jnp.zeros((n_ctx, LANES), F32)])
    return jnp.concatenate([tab, ident], axis=1)


def _take_cols(a, idx):
    idx = np.asarray(idx)
    return jnp.where(idx >= 0, jnp.take(a, np.maximum(idx, 0), axis=-1), 0.0)


def _layer_weights(l, w):
    w_in = w["w_in"][l]
    mla0 = 768 + 512
    cq_hi = mla0 + MLA_Q_RANK
    kr0 = cq_hi + MLA_KV_RANK
    src = np.concatenate([
        np.arange(cq_hi), np.arange(kr0, kr0 + MLA_ROPE_DIM),
        np.full(LANES - HALF - MLA_ROPE_DIM, -1),
        np.arange(cq_hi, kr0), np.arange(kr0 + MLA_ROPE_DIM, w_in.shape[1])])
    assert src.shape == (P_COLS,)
    o32, _, _, _ = _pair_layout(DIFF_QK_DIM, LANES // DIFF_QK_DIM)
    o64, _, _, _ = _pair_layout(GQA_HEAD_DIM, LANES // GQA_HEAD_DIM)
    oml, _, _, rope_lane = _pair_layout(MLA_ROPE_DIM, 1)
    cols = np.arange(P_COLS)
    for g, o in ((0, o32), (1, o32), (2, o32), (3, o32), (6, o64), (7, o64), (8, o64),
                 (13, o64), (14, o64), (15, o64)):
        cols[g * LANES:(g + 1) * LANES] = g * LANES + o
    cq_lanes = [i for i in range(LANES) if not rope_lane[i]][:MLA_Q_RANK - LANES]
    g11 = np.full(LANES, -1)
    g11[rope_lane] = HALF + oml[rope_lane]
    g11[cq_lanes] = np.arange(len(cq_lanes))
    cols[11 * LANES:12 * LANES] = np.where(g11 >= 0, 11 * LANES + g11, -1)
    w_in_p = _take_cols(w_in, np.where(cols >= 0, src[np.maximum(cols, 0)], -1))
    cq_src = np.full(2 * LANES, -1)
    cq_src[:LANES] = np.arange(LANES)
    cq_src[LANES + np.asarray(cq_lanes)] = LANES + np.arange(len(cq_lanes))

    dqk = MLA_NOPE_DIM + MLA_ROPE_DIM
    assert MLA_KV_RANK == LANES
    wuq = w["mla_w_uq"][l].reshape(MLA_Q_RANK, MLA_HEADS, dqk)
    wukv = w["mla_w_ukv"][l].reshape(MLA_KV_RANK, MLA_HEADS, MLA_NOPE_DIM + MLA_V_DIM)
    row_pad = 2 * LANES - MLA_Q_RANK
    wuq_n = jnp.pad(jnp.transpose(wuq[:, :, :MLA_NOPE_DIM], (1, 0, 2)), ((0, 0), (0, row_pad), (0, 0)))
    w_abs = absorb_call(wuq_n, jnp.transpose(wukv[:, :, :MLA_NOPE_DIM], (1, 2, 0)))
    wq_r = jnp.pad(_take_cols(wuq[:, :, MLA_NOPE_DIM:], np.where(rope_lane, oml, -1)),
                   ((0, row_pad), (0, 0), (0, 0)))
    wq = jnp.concatenate([jnp.transpose(w_abs, (1, 0, 2)).reshape(2 * LANES, -1),
                          wq_r.reshape(2 * LANES, -1)], axis=1)
    wq = _take_cols(wq.T, cq_src).T
    wuv = jnp.transpose(wukv[:, :, MLA_NOPE_DIM:], (1, 0, 2))
    wv = jnp.zeros((2, 2, LANES, 2, MLA_V_DIM), F32)
    for hd in range(MLA_HEADS):
        wv = wv.at[hd // 2, hd % 2, :, hd % 2, :].set(wuv[hd])
    wv = wv.reshape(2, 2 * LANES, LANES)
    gq = _take_cols(jnp.pad(w["mla_g_q"][l], (0, row_pad)), cq_src)[None]
    return {
        "g_mix_pre": w["g_mix_pre"][l][None], "g_mix_post": w["g_mix_post"][l][None],
        "g_ffn_pre": w["g_ffn_pre"][l][None], "g_ffn_post": w["g_ffn_post"][l][None],
        "w_in": w_in_p.astype(BF16),
        "wq": wq.astype(BF16), "wv": wv.astype(BF16),
        "gq": gq, "gkv": w["mla_g_kv"][l][None],
        "gnq": w["gqa_g_q"][l][o64 % GQA_HEAD_DIM][None],
        "gnk": w["gqa_g_k"][l][o64 % GQA_HEAD_DIM][None],
        "lam": jnp.stack([w["diff_lam_q1"][l], w["diff_lam_k1"][l],
                          w["diff_lam_q2"][l], w["diff_lam_k2"][l]]),
        "g_sub": jnp.tile(w["diff_g_sub"][l], 2)[None],
        "sink": w["swa_sink"][l],
        "w_out": w["w_out"][l].astype(BF16),
        "w_up": w["ffn_w_up"][l].astype(BF16),
        "conv_w": w["ffn_conv_w"][l],
        "conv_b": w["ffn_conv_b"][l][None],
        "w_down": w["ffn_w_down"][l].astype(BF16),
    }


def kernel(x, c, ctx, c_ctx, w_mod, b_mod, g_mix_pre, g_mix_post, g_ffn_pre, g_ffn_post, w_in, diff_lam_q1, diff_lam_k1, diff_lam_q2, diff_lam_k2, diff_g_sub, swa_sink, mla_g_q, mla_g_kv, mla_w_uq, mla_w_ukv, gqa_g_q, gqa_g_k, w_out, ffn_w_up, ffn_conv_w, ffn_conv_b, ffn_w_down):
    w = dict(g_mix_pre=g_mix_pre, g_mix_post=g_mix_post, g_ffn_pre=g_ffn_pre,
             g_ffn_post=g_ffn_post, w_in=w_in, diff_lam_q1=diff_lam_q1, diff_lam_k1=diff_lam_k1,
             diff_lam_q2=diff_lam_q2, diff_lam_k2=diff_lam_k2, diff_g_sub=diff_g_sub,
             swa_sink=swa_sink, mla_g_q=mla_g_q, mla_g_kv=mla_g_kv, mla_w_uq=mla_w_uq,
             mla_w_ukv=mla_w_ukv, gqa_g_q=gqa_g_q, gqa_g_k=gqa_g_k, w_out=w_out,
             ffn_w_up=ffn_w_up, ffn_conv_w=ffn_conv_w, ffn_conv_b=ffn_conv_b,
             ffn_w_down=ffn_w_down)
    b, s, d = x.shape
    depth = w_mod.shape[0]
    ctx_row = b
    mod_rows = 16
    cc = jnp.zeros((mod_rows, d), F32).at[:b].set(c).at[ctx_row].set(c_ctx)
    mod_all = mod_call(cc, w_mod, b_mod).reshape(depth, mod_rows, N_MOD, d)

    n_ctx = ctx.shape[1]
    tabs = (_rope_table(n_ctx, s, DIFF_QK_DIM, LANES // DIFF_QK_DIM),
            _rope_table(n_ctx, s, GQA_HEAD_DIM, LANES // GQA_HEAD_DIM),
            _rope_table(n_ctx, s, MLA_ROPE_DIM, 1))

    xc = ctx
    for l in range(depth):
        need_ctx = l < depth - 1
        lambda_init = 0.8 - 0.6 * math.exp(-0.3 * l)
        lw = _layer_weights(l, w)
        mod = mod_all[l]
        qkv = proj_call(xc, x, mod, ctx_row, lw, tabs)
        diff_extra = (lw["lam"], lw["g_sub"])
        mla_extra = (lw["wv"],)
        ys = (dense_attn_call("diff", qkv, True, diff_extra, lambda_init),
              swa_call(lw["sink"], qkv, True),
              dense_attn_call("mla", qkv, True, mla_extra),
              dense_attn_call("gqa", qkv, True))
        x1, h2 = mixout_call(x, ys, mod, None, lw)
        x = ffn_call(x1, h2, mod, None, lw)
        if need_ctx:
            ysc = (dense_attn_call("diff", qkv, False, diff_extra, lambda_init),
                   swa_call(lw["sink"], qkv, False),
                   dense_attn_call("mla", qkv, False, mla_extra),
                   dense_attn_call("gqa", qkv, False))
            xc1, hc2 = mixout_call(xc, ysc, mod, ctx_row, lw)
            xc = ffn_call(xc1, hc2, mod, ctx_row, lw)
    return x
```

```python
import functools
import math

import jax
import jax.numpy as jnp
import numpy as np
from jax import lax
from jax.experimental import pallas as pl
from jax.experimental.pallas import tpu as pltpu

F32 = jnp.float32
BF16 = jnp.bfloat16

D_MODEL = 1024
GRID_W = 64
ROPE_THETA = 10000.0
NORM_EPS = 1e-6
NEG_INF = -1e30
N_MOD = 6

DIFF_QK_DIM = 32
SWA_HEAD_DIM, WINDOW = 64, 128
MLA_Q_RANK, MLA_KV_RANK = 192, 128
MLA_NOPE_DIM, MLA_ROPE_DIM, MLA_V_DIM, MLA_HEADS = 64, 32, 64, 4
GQA_HEAD_DIM = 64
FFN_DIM = 2816
MIX_COLS = 256

LANES = 128
BF16_SUBLANES = 16
HALF = LANES // 2
VMEM_LIMIT = 56 * 1024 * 1024

P_COLS = 17 * LANES
G_QM, G_VD, G_QG, G_QS = 0, 8, 12, 16
G_QD, G_KD, G_VG, G_KM, G_VM, G_VS, G_KG, G_KS = 20, 22, 24, 26, 28, 30, 32, 33
QKV_COLS = 34 * LANES
LOG2E = math.log2(math.e)

TOK_TILE = 256
CTX_TILE = 256
ATT_TILE = 512
KV_CHUNKS = 2
MIX_TILE = 1024
FFN_TILE = 1024
FFN_CHUNK = 256
MOD_TILE = 2048


def _rms(x, n):
    return lax.rsqrt(jnp.sum(x * x, axis=-1, keepdims=True) * (1.0 / n) + NORM_EPS)


def _lane(shape):
    return lax.broadcasted_iota(jnp.int32, shape, len(shape) - 1)


def _dot(a, b):
    return jnp.dot(a, b, preferred_element_type=F32)


def _dot_nt(a, b):
    return lax.dot_general(a, b, (((1,), (1,)), ((), ())), preferred_element_type=F32)


def _layer_spec(l, shape):
    return pl.BlockSpec((None,) + tuple(shape), lambda *_: (l,) + (0,) * len(shape))


def mod_kernel(c_ref, w_ref, b_ref, o_ref):
    c = c_ref[...]
    s = c * (1.0 / (1.0 + jnp.exp(-c)))
    o_ref[...] = jnp.dot(s, w_ref[...], preferred_element_type=F32,
                         precision=lax.Precision.HIGHEST) + b_ref[...]


def mod_call(cc, w_mod, b_mod):
    depth, d, n = w_mod.shape
    rows = cc.shape[0]
    tn = MOD_TILE
    return pl.pallas_call(
        mod_kernel,
        name="mod_vectors",
        grid=(depth, n // tn),
        in_specs=[
            pl.BlockSpec((rows, d), lambda l, j: (0, 0)),
            pl.BlockSpec((None, d, tn), lambda l, j: (l, 0, j)),
            pl.BlockSpec((None, 1, tn), lambda l, j: (l, 0, j)),
        ],
        out_specs=pl.BlockSpec((None, rows, tn), lambda l, j: (l, 0, j)),
        out_shape=jax.ShapeDtypeStruct((depth, rows, n), F32),
        compiler_params=pltpu.CompilerParams(
            dimension_semantics=("arbitrary", "arbitrary"), vmem_limit_bytes=VMEM_LIMIT),
    )(cc, w_mod, b_mod.reshape(depth, 1, n))


def absorb_kernel(a_ref, b_ref, o_ref):
    o_ref[...] = jnp.dot(a_ref[...], b_ref[...], preferred_element_type=F32,
                         precision=lax.Precision.HIGHEST)


def absorb_call(a, b):
    heads, m, k = a.shape
    n = b.shape[2]
    return pl.pallas_call(
        absorb_kernel,
        name="mla_absorb",
        grid=(heads,),
        in_specs=[pl.BlockSpec((None, m, k), lambda h: (h, 0, 0)),
                  pl.BlockSpec((None, k, n), lambda h: (h, 0, 0))],
        out_specs=pl.BlockSpec((None, m, n), lambda h: (h, 0, 0)),
        out_shape=jax.ShapeDtypeStruct((heads, m, n), F32),
        compiler_params=pltpu.CompilerParams(dimension_semantics=("arbitrary",)),
    )(a, b)


def _rope(x, tab_ref):
    return x * tab_ref[0] + pltpu.roll(x, HALF, 1) * tab_ref[1]


def _halves(lo, hi, shape):
    return jnp.where(_lane(shape) < HALF, lo, hi)


def _head_rms(x, first):
    sq = x * x
    ss_a = jnp.sum(jnp.where(first, sq, 0.0), axis=-1, keepdims=True)
    ss_b = jnp.sum(jnp.where(first, 0.0, sq), axis=-1, keepdims=True)
    r_a = lax.rsqrt(ss_a * (1.0 / HALF) + NORM_EPS)
    r_b = lax.rsqrt(ss_b * (1.0 / HALF) + NORM_EPS)
    return jnp.where(first, r_a, r_b)


def proj_kernel(xc_ref, x_ref, mod_ref, gpre_ref, win_ref, wq_ref, gq_ref, gkv_ref,
                gnq_ref, gnk_ref, t32_ref, t64_ref, tml_ref, o_ref, pa_ref, pb_ref, *,
                tiles_per_sample, n_tiles):
    g = pl.program_id(0)
    is_ctx = jnp.minimum(g, n_tiles - 1) % tiles_per_sample == tiles_per_sample - 1

    @pl.when(g == 0)
    def _():
        pb_ref[...] = jnp.zeros_like(pb_ref)

    def project(dst_ref):
        x = jnp.where(is_ctx, xc_ref[...], x_ref[...])
        h = x * _rms(x, D_MODEL) * gpre_ref[...]
        h = h * (1.0 + mod_ref[1:2, :]) + mod_ref[0:1, :]
        dst_ref[...] = _dot(h.astype(BF16), win_ref[...])

    finish = functools.partial(_proj_finish, wq_ref, gq_ref, gkv_ref, gnq_ref, gnk_ref,
                               t32_ref, t64_ref, tml_ref, o_ref)

    @pl.when(g % 2 == 0)
    def _():
        project(pa_ref)
        finish(pb_ref)

    @pl.when(g % 2 == 1)
    def _():
        project(pb_ref)
        finish(pa_ref)


def _proj_finish(wq_ref, gq_ref, gkv_ref, gnq_ref, gnk_ref, t32_ref, t64_ref, tml_ref,
                 o_ref, p_ref):
    def grp(g):
        return p_ref[:, g * LANES:(g + 1) * LANES]

    def put(g, v):
        o_ref[:, g * LANES:(g + 1) * LANES] = v.astype(BF16)

    def rope32(v):
        return _rope(v, t32_ref)

    def rope64(v):
        return _rope(v, t64_ref)

    def ropeml(v):
        return _rope(v, tml_ref)

    lane = _lane((p_ref.shape[0], LANES))
    low = lane < HALF
    head_a = lane % HALF < HALF // 2

    def with_ones(v):
        return jnp.where(low, v, 1.0), jnp.where(low, 1.0, v)

    for g in range(2):
        put(G_QD + g, rope32(grp(g)) * (DIFF_QK_DIM ** -0.5 * LOG2E))
        put(G_KD + g, rope32(grp(2 + g)))
        v_lo, v_hi = with_ones(grp(4 + g))
        put(G_VD + 2 * g, v_lo)
        put(G_VD + 2 * g + 1, v_hi)

    def put_gqa_queries(g0, q01, q23):
        put(g0, jnp.where(head_a, q01, 0.0))
        put(g0 + 1, jnp.where(head_a, pltpu.roll(q01, LANES - HALF // 2, 1), 0.0))
        put(g0 + 2, jnp.where(head_a, 0.0, pltpu.roll(q23, HALF // 2, 1)))
        put(g0 + 3, jnp.where(head_a, 0.0, q23))

    scale_s = SWA_HEAD_DIM ** -0.5 * LOG2E
    put_gqa_queries(G_QS, rope64(grp(6)) * scale_s, rope64(grp(7)) * scale_s)
    put(G_KS, rope64(grp(8)))
    v_lo, v_hi = with_ones(grp(9))
    put(G_VS, v_lo)
    put(G_VS + 1, v_hi)

    cq0, cq1, ckv = grp(10), grp(11), grp(12)
    rope_lanes = lane % HALF < MLA_ROPE_DIM // 2
    cq_lanes = ~rope_lanes & (lane < HALF + MLA_ROPE_DIM)
    ss = (jnp.sum(cq0 * cq0, axis=-1, keepdims=True)
          + jnp.sum(jnp.where(cq_lanes, cq1 * cq1, 0.0), axis=-1, keepdims=True))
    rq = lax.rsqrt(ss * (1.0 / MLA_Q_RANK) + NORM_EPS)
    cqn = jnp.concatenate([cq0 * rq * gq_ref[:, :LANES], cq1 * rq * gq_ref[:, LANES:]], axis=1)
    qm = _dot(cqn.astype(BF16), wq_ref[...])
    scale_m = (MLA_NOPE_DIM + MLA_ROPE_DIM) ** -0.5 * LOG2E
    for hd in range(MLA_HEADS):
        put(G_QM + 2 * hd, qm[:, hd * LANES:(hd + 1) * LANES] * scale_m)
        qr = qm[:, (MLA_HEADS + hd) * LANES:(MLA_HEADS + hd + 1) * LANES]
        put(G_QM + 2 * hd + 1, ropeml(qr) * scale_m)
    ckvn = ckv * _rms(ckv, MLA_KV_RANK) * gkv_ref[...]
    put(G_KM, ckvn)
    put(G_KM + 1, jnp.where(rope_lanes, ropeml(cq1), 0.0))
    put(G_VM, ckvn)
    put(G_VM + 1, jnp.ones_like(ckvn))

    scale_g = GQA_HEAD_DIM ** -0.5 * LOG2E
    q01 = grp(13)
    q01 = rope64(q01 * _head_rms(q01, head_a) * gnq_ref[...]) * scale_g
    q23 = grp(14)
    q23 = rope64(q23 * _head_rms(q23, head_a) * gnq_ref[...]) * scale_g
    put_gqa_queries(G_QG, q01, q23)
    kg = grp(15)
    put(G_KG, rope64(kg * _head_rms(kg, head_a) * gnk_ref[...]))
    v_lo, v_hi = with_ones(grp(16))
    put(G_VG, v_lo)
    put(G_VG + 1, v_hi)


def proj_call(xc, x, mod, ctx_row, l, lw, tabs):
    b, s, d = x.shape
    tq = TOK_TILE
    assert xc.shape[1] == CTX_TILE == tq
    s_all = CTX_TILE + s
    per = s_all // tq
    n = b * per
    const2 = lambda g: (0, 0)

    def proj_tile(g):
        ga = jnp.minimum(g, n - 1)
        return ga // per, ga % per

    def done_tile(g):
        gb = jnp.maximum(g - 1, 0)
        return gb // per, gb % per

    def x_map(g):
        i, t = proj_tile(g)
        return i, jnp.minimum(t, per - 2), 0

    def mod_map(g):
        i, t = proj_tile(g)
        return l, jnp.where(t == per - 1, ctx_row, i), 0, 0

    tab_spec = pl.BlockSpec((2, tq, LANES), lambda g: (0, done_tile(g)[1], 0))
    return pl.pallas_call(
        functools.partial(proj_kernel, tiles_per_sample=per, n_tiles=n),
        name="proj",
        grid=(n + 1,),
        in_specs=[
            pl.BlockSpec((None, tq, d), lambda g: (proj_tile(g)[0], 0, 0)),
            pl.BlockSpec((None, tq, d), x_map),
            pl.BlockSpec((None, None, N_MOD, d), mod_map),
            _layer_spec(l, (1, d)),
            pl.BlockSpec((d, P_COLS), const2),
            _layer_spec(l, (2 * LANES, 2 * MLA_HEADS * LANES)),
            _layer_spec(l, (1, 2 * LANES)),
            _layer_spec(l, (1, LANES)),
            _layer_spec(l, (1, LANES)),
            _layer_spec(l, (1, LANES)),
            tab_spec, tab_spec, tab_spec,
        ],
        out_specs=pl.BlockSpec((None, tq, QKV_COLS), lambda g: done_tile(g) + (0,)),
        out_shape=jax.ShapeDtypeStruct((b, s_all, QKV_COLS), BF16),
        scratch_shapes=[pltpu.VMEM((tq, P_COLS), F32), pltpu.VMEM((tq, P_COLS), F32)],
        compiler_params=pltpu.CompilerParams(
            dimension_semantics=("arbitrary",), vmem_limit_bytes=VMEM_LIMIT),
    )(xc, x, mod, lw["g_mix_pre"], lw["w_in"][l], lw["wq"], lw["gq"], lw["gkv"],
      lw["gnq"], lw["gnk"], tabs[0], tabs[1], tabs[2])


def _group(ref, rows, g, width=1):
    return ref[rows, g * width * LANES:(g + 1) * width * LANES]


def _attend(q_ref, segs, k_ref, v_ref, n_chunks, width):
    chunk = k_ref.shape[0] // n_chunks

    def query(g):
        if isinstance(g, int):
            return _group(q_ref, slice(None), g, width)
        g, unit, n_units = g
        q = _group(q_ref, slice(None), g, width)
        per = HALF // n_units
        keep = _lane(q.shape) % HALF // per == unit
        return jnp.where(keep, q, jnp.zeros_like(q))

    qs = [jnp.concatenate([query(g) for g in qg], axis=0) for qg, _, _ in segs]

    def body(i, carry):
        rows = pl.ds(pl.multiple_of(i * chunk, LANES), chunk)
        out = []
        for j, (q, (_, kb, parts)) in enumerate(zip(qs, segs)):
            m, acc = carry[2 * j], carry[2 * j + 1]
            s = _dot_nt(q, _group(k_ref, rows, kb, width))
            m_new = jnp.maximum(m, jnp.max(s, axis=-1, keepdims=True))
            p = jnp.exp2(s - m_new).astype(BF16)
            tq = q.shape[0] // sum(n for n, _ in parts)
            pv, r0 = [], 0
            for n, vb in parts:
                pv.append(_dot(p[r0:r0 + n * tq], _group(v_ref, rows, vb, width)))
                r0 += n * tq
            pv = pv[0] if len(pv) == 1 else jnp.concatenate(pv, axis=0)
            out += [m_new, acc * jnp.exp2(m - m_new) + pv]
        return tuple(out)

    init = []
    for q in qs:
        init += [jnp.full((q.shape[0], 1), NEG_INF, F32),
                 jnp.zeros((q.shape[0], width * LANES), F32)]
    carry = lax.fori_loop(0, n_chunks, body, tuple(init), unroll=True)
    half = width * HALF
    return [carry[2 * j + 1] / pltpu.roll(carry[2 * j + 1], half, 1) for j in range(len(segs))]


_PLANS = {
    "gqa": (1, ((((0, 1, 2, 3), 0, ((2, 0), (2, 1))),),)),
    "mla": (2, ((((0, 1, 2, 3), 0, ((2, 0), (2, 0))),),)),
    "diff": (1, tuple(
        ((tuple((ob, i, 4) for i in range(4)), ob,
          ((2, 2 * ob), (2, 2 * ob + 1))),) for ob in range(2))),
}


def dense_kernel(*refs, kind, n_chunks, lambda_init):
    refs = list(refs)
    if kind == "diff":
        lam_ref, g_ref = refs[:2]
        refs = refs[2:]
    elif kind == "mla":
        wv_ref = refs[0]
        refs = refs[1:]
    q_ref, k_ref, v_ref, o_ref = refs
    tq = q_ref.shape[0]
    width, loops = _PLANS[kind]
    res = [_attend(q_ref, segs, k_ref, v_ref, n_chunks, width) for segs in loops]

    def tiles(r):
        return [r[i * tq:(i + 1) * tq] for i in range(r.shape[0] // tq)]

    if kind == "gqa":
        h = [t for r in res[0] for t in tiles(r)]
        blocks = [_halves(h[0], pltpu.roll(h[1], HALF, 1), h[0].shape),
                  _halves(pltpu.roll(h[2], HALF, 1), h[3], h[0].shape)]
    elif kind == "mla":
        h = [r[:, :LANES].astype(BF16) for r in tiles(res[0][0])]
        blocks = [_dot(jnp.concatenate(h[2 * ob:2 * ob + 2], axis=1), wv_ref[ob]) for ob in range(2)]
    else:
        lv = lam_ref[...]
        lam = (jnp.exp(jnp.sum(lv[0:1] * lv[1:2], axis=-1, keepdims=True))
               - jnp.exp(jnp.sum(lv[2:3] * lv[3:4], axis=-1, keepdims=True)) + lambda_init)
        blocks = []
        for (r,) in res:
            a = tiles(r)
            o = _halves(a[0] - lam * a[1], a[2] - lam * a[3], a[0].shape)
            blocks.append(o * _head_rms(o, _lane(o.shape) < HALF) * g_ref[...] * (1.0 - lambda_init))
    for ob, blk in enumerate(blocks):
        o_ref[:, ob * LANES:(ob + 1) * LANES] = blk.astype(BF16)


def _qkv_specs(lay, latent, s_all, tq):
    (q0, qn), (k0, kn), (v0, vn) = lay
    assert q0 % qn == 0 and k0 % kn == 0 and v0 % vn == 0
    ctx_blk = s_all // CTX_TILE - 1
    if latent:
        q_map = lambda i, t: (i, t, q0 // qn)
        k_map = lambda i, t: (i, 0, k0 // kn)
        v_map = lambda i, t: (i, 0, v0 // vn)
    else:
        q_map = lambda i, t: (i, ctx_blk, q0 // qn)
        k_map = lambda i, t: (i, ctx_blk, k0 // kn)
        v_map = lambda i, t: (i, ctx_blk, v0 // vn)
    kv_rows = s_all if latent else CTX_TILE
    return [
        pl.BlockSpec((None, tq, qn * LANES), q_map),
        pl.BlockSpec((None, kv_rows, kn * LANES), k_map),
        pl.BlockSpec((None, kv_rows, vn * LANES), v_map),
    ]


def dense_attn_call(kind, qkv, latent, l, extra=None, lambda_init=None):
    b, s_all, _ = qkv.shape
    tq = ATT_TILE if latent else CTX_TILE
    sq = s_all - CTX_TILE if latent else CTX_TILE
    lay = {"diff": ((G_QD, 2), (G_KD, 2), (G_VD, 4)),
           "mla": ((G_QM, 8), (G_KM, 2), (G_VM, 2)),
           "gqa": ((G_QG, 4), (G_KG, 1), (G_VG, 2))}[kind]
    specs = _qkv_specs(lay, latent, s_all, tq)
    args = [qkv, qkv, qkv]
    if kind == "diff":
        specs = [_layer_spec(l, (4, DIFF_QK_DIM)), _layer_spec(l, (1, LANES))] + specs
        args = list(extra) + args
    elif kind == "mla":
        specs = [_layer_spec(l, (2, 2 * LANES, LANES))] + specs
        args = list(extra) + args
    body = functools.partial(dense_kernel, kind=kind, n_chunks=KV_CHUNKS if latent else 1,
                             lambda_init=lambda_init)
    return pl.pallas_call(
        body,
        name=kind + ("_attn" if latent else "_attn_ctx"),
        grid=(b, sq // tq),
        in_specs=specs,
        out_specs=pl.BlockSpec((None, tq, MIX_COLS), lambda i, t: (i, t, 0)),
        out_shape=jax.ShapeDtypeStruct((b, sq, MIX_COLS), BF16),
        compiler_params=pltpu.CompilerParams(
            dimension_semantics=("arbitrary", "arbitrary"), vmem_limit_bytes=VMEM_LIMIT),
    )(*args)


def swa_kernel(sink_ref, q_ref, k_ref, v_ref, o_ref, *, s_lat, layer):
    tq = q_ref.shape[0]
    heads = q_ref.shape[1] // LANES
    rows = heads * tq
    t = pl.program_id(1)
    q = jnp.concatenate([_group(q_ref, slice(None), g) for g in range(heads)], axis=0)
    row = lax.broadcasted_iota(jnp.int32, (rows, 1), 0)
    sink = jnp.zeros((rows, 1), F32)
    for h in range(heads):
        sink = jnp.where(row // tq == h, sink_ref[layer, h] * LOG2E, sink)

    spans = [(slice(s_lat, s_lat + CTX_TILE), None)]
    if s_lat:
        width = tq + 2 * WINDOW
        start = jnp.clip(t * tq - WINDOW, 0, s_lat - width)
        rel = (start - t * tq + lax.broadcasted_iota(jnp.int32, (tq, width), 1)
               - lax.broadcasted_iota(jnp.int32, (tq, width), 0))
        valid = jnp.concatenate([jnp.abs(rel) <= WINDOW] * heads, axis=0)
        spans.append((pl.ds(pl.multiple_of(start, WINDOW), width), valid))

    ss = []
    for span, valid in spans:
        s = _dot_nt(q, _group(k_ref, span, 0))
        ss.append(s if valid is None else jnp.where(valid, s, NEG_INF))
    m = functools.reduce(jnp.maximum, [jnp.max(s, axis=-1, keepdims=True) for s in ss] + [sink])
    ps = [jnp.exp2(s - m).astype(BF16) for s in ss]
    half = rows // 2
    acc = jnp.concatenate(
        [sum(_dot(p[vb * half:(vb + 1) * half], _group(v_ref, span, vb))
             for p, (span, _) in zip(ps, spans)) for vb in range(2)], axis=0)
    res = acc / (pltpu.roll(acc, HALF, 1) + jnp.exp2(sink - m))
    h = [res[i * tq:(i + 1) * tq] for i in range(heads)]
    o_ref[:, :LANES] = _halves(h[0], pltpu.roll(h[1], HALF, 1), h[0].shape).astype(BF16)
    o_ref[:, LANES:] = _halves(pltpu.roll(h[2], HALF, 1), h[3], h[0].shape).astype(BF16)


def swa_call(sink, qkv, latent, l):
    b, s_all, _ = qkv.shape
    tq = TOK_TILE
    s_lat = s_all - CTX_TILE if latent else 0
    sq = s_lat if latent else CTX_TILE
    specs = [pl.BlockSpec(memory_space=pltpu.SMEM)]
    specs += _qkv_specs(((G_QS, 4), (G_KS, 1), (G_VS, 2)), latent, s_all, tq)
    return pl.pallas_call(
        functools.partial(swa_kernel, s_lat=s_lat, layer=l),
        name="swa_attn" if latent else "swa_attn_ctx",
        grid=(b, sq // tq),
        in_specs=specs,
        out_specs=pl.BlockSpec((None, tq, MIX_COLS), lambda i, t: (i, t, 0)),
        out_shape=jax.ShapeDtypeStruct((b, sq, MIX_COLS), BF16),
        compiler_params=pltpu.CompilerParams(
            dimension_semantics=("arbitrary", "arbitrary"), vmem_limit_bytes=VMEM_LIMIT),
    )(sink, qkv, qkv, qkv)


def mixout_kernel(x_ref, ya_ref, yb_ref, ym_ref, yd_ref, w_ref, mod_ref, gpost_ref,
                  gfpre_ref, x1_ref, h2_ref, za_ref, zb_ref):
    g = pl.program_id(0)

    @pl.when(g == 0)
    def _():
        zb_ref[...] = jnp.zeros_like(zb_ref)

    def project(dst_ref):
        y = jnp.concatenate([ya_ref[...], yb_ref[...], ym_ref[...], yd_ref[...]], axis=1)
        dst_ref[...] = _dot(y, w_ref[...])

    def finish(z_ref):
        z = z_ref[...]
        x1 = x_ref[...] + mod_ref[2:3, :] * (z * _rms(z, D_MODEL) * gpost_ref[...])
        x1_ref[...] = x1
        h2 = x1 * _rms(x1, D_MODEL) * gfpre_ref[...]
        h2_ref[...] = (h2 * (1.0 + mod_ref[4:5, :]) + mod_ref[3:4, :]).astype(BF16)

    @pl.when(g % 2 == 0)
    def _():
        project(za_ref)
        finish(zb_ref)

    @pl.when(g % 2 == 1)
    def _():
        project(zb_ref)
        finish(za_ref)


def mixout_call(x, ys, mod, mod_row, l, lw):
    b, s, d = x.shape
    tq = min(MIX_TILE, s)
    per = s // tq
    n = b * per

    def proj_tile(g):
        ga = jnp.minimum(g, n - 1)
        return ga // per, ga % per, 0

    def done_tile(g):
        gb = jnp.maximum(g - 1, 0)
        return gb // per, gb % per, 0

    if mod_row is None:
        mod_map = lambda g: (l, done_tile(g)[0], 0, 0)
    else:
        mod_map = lambda g: (l, mod_row, 0, 0)
    const2 = lambda g: (0, 0)
    y_spec = pl.BlockSpec((None, tq, MIX_COLS), proj_tile)
    return pl.pallas_call(
        mixout_kernel,
        name="mixout",
        grid=(n + 1,),
        in_specs=[pl.BlockSpec((None, tq, d), done_tile), y_spec, y_spec, y_spec, y_spec,
                  pl.BlockSpec((4 * MIX_COLS, d), const2),
                  pl.BlockSpec((None, None, N_MOD, d), mod_map),
                  _layer_spec(l, (1, d)), _layer_spec(l, (1, d))],
        out_specs=[pl.BlockSpec((None, tq, d), done_tile), pl.BlockSpec((None, tq, d), done_tile)],
        out_shape=[jax.ShapeDtypeStruct((b, s, d), F32), jax.ShapeDtypeStruct((b, s, d), BF16)],
        scratch_shapes=[pltpu.VMEM((tq, d), F32), pltpu.VMEM((tq, d), F32)],
        compiler_params=pltpu.CompilerParams(
            dimension_semantics=("arbitrary",), vmem_limit_bytes=VMEM_LIMIT),
    )(x, *ys, lw["w_out"][l], mod, lw["g_mix_post"], lw["g_ffn_pre"])


def ffn_kernel(x1_ref, hc_ref, hp_ref, hn_ref, wup_ref, cw_ref, cb_ref, wdn_ref, mod_ref,
               gpost_ref, o_ref, g_ref):
    t = pl.program_id(1)
    tq = hc_ref.shape[0]
    halo = hp_ref.shape[0]
    hp = jnp.where(t > 0, hp_ref[...], jnp.zeros_like(hp_ref))
    hn = jnp.where(t < pl.num_programs(1) - 1, hn_ref[...], jnp.zeros_like(hn_ref))
    hext = jnp.concatenate([hp, hc_ref[...], hn], axis=0)
    rows = hext.shape[0]

    def conv_cols(cols):
        u = _dot(hext, wup_ref[:, cols])
        w = cw_ref[:, cols]
        return (pltpu.roll(u, 1, 0)[halo:halo + tq] * w[0:1]
                + u[halo:halo + tq] * w[1:2]
                + pltpu.roll(u, rows - 1, 0)[halo:halo + tq] * w[2:3] + cb_ref[:, cols])

    for j in range(FFN_DIM // FFN_CHUNK):
        a = conv_cols(slice(j * FFN_CHUNK, (j + 1) * FFN_CHUNK))
        v = conv_cols(slice(FFN_DIM + j * FFN_CHUNK, FFN_DIM + (j + 1) * FFN_CHUNK))
        g_ref[:, j * FFN_CHUNK:(j + 1) * FFN_CHUNK] = (a * (1.0 / (1.0 + jnp.exp(-a))) * v).astype(BF16)

    f = _dot(g_ref[...], wdn_ref[...])
    o_ref[...] = x1_ref[...] + mod_ref[5:6, :] * (f * _rms(f, D_MODEL) * gpost_ref[...])


def ffn_call(x1, h2, mod, mod_row, l, lw):
    b, s, d = x1.shape
    tq = min(FFN_TILE, s)
    halo = BF16_SUBLANES
    per_tile = tq // halo
    last = s // halo - 1
    if mod_row is None:
        mod_map = lambda i, t: (l, i, 0, 0)
    else:
        mod_map = lambda i, t: (l, mod_row, 0, 0)
    const2 = lambda i, t: (0, 0)
    tok = lambda i, t: (i, t, 0)
    return pl.pallas_call(
        ffn_kernel,
        name="conv_ffn",
        grid=(b, s // tq),
        in_specs=[
            pl.BlockSpec((None, tq, d), tok),
            pl.BlockSpec((None, tq, d), tok),
            pl.BlockSpec((None, halo, d), lambda i, t: (i, jnp.maximum(t * per_tile - 1, 0), 0)),
            pl.BlockSpec((None, halo, d), lambda i, t: (i, jnp.minimum((t + 1) * per_tile, last), 0)),
            pl.BlockSpec((d, 2 * FFN_DIM), const2, pipeline_mode=pl.Buffered(1)),
            _layer_spec(l, (3, 2 * FFN_DIM)),
            _layer_spec(l, (1, 2 * FFN_DIM)),
            pl.BlockSpec((FFN_DIM, d), const2, pipeline_mode=pl.Buffered(1)),
            pl.BlockSpec((None, None, N_MOD, d), mod_map),
            _layer_spec(l, (1, d)),
        ],
        out_specs=pl.BlockSpec((None, tq, d), tok),
        out_shape=jax.ShapeDtypeStruct((b, s, d), F32),
        scratch_shapes=[pltpu.VMEM((tq, FFN_DIM), BF16)],
        compiler_params=pltpu.CompilerParams(
            dimension_semantics=("arbitrary", "arbitrary"), vmem_limit_bytes=VMEM_LIMIT),
    )(x1, h2, h2, h2, lw["w_up"][l], lw["conv_w"], lw["conv_b"], lw["w_down"][l], mod,
      lw["g_ffn_post"])


def _pair_layout(rot_dim, n_units):
    n = rot_dim // 4
    lane = np.arange(LANES)
    half, mu = lane // HALF, lane % HALF
    unit, r = mu // (2 * n), mu % (2 * n)
    axis, f = r // n, r % n
    used = unit < n_units
    orig = np.where(used, unit * rot_dim + axis * 2 * n + half * n + f, -1)
    return orig, axis, f, used


def _rope_table(n_ctx, s, rot_dim, n_units):
    n = rot_dim // 4
    _, axis, f, used = _pair_layout(rot_dim, n_units)
    tok = np.arange(s)
    pos = np.stack([tok // GRID_W, tok % GRID_W], axis=1).astype(np.float32)
    inv_freq = ROPE_THETA ** (-jnp.arange(n, dtype=F32) / n)
    ang = jnp.asarray(pos)[:, axis] * inv_freq[f][None, :]
    sign = np.where(np.arange(LANES) < HALF, -1.0, 1.0).astype(np.float32)
    cos = jnp.where(used[None], jnp.cos(ang), 1.0)
    sin = jnp.where(used[None], jnp.sin(ang) * sign[None], 0.0)
    tab = jnp.stack([cos, sin]).astype(F32)
    ident = jnp.stack([jnp.ones((n_ctx, LANES), F32), jnp.zeros((n_ctx, LANES), F32)])
    return jnp.concatenate([tab, ident], axis=1)


def _take_cols(a, idx):
    idx = np.asarray(idx)
    return jnp.where(idx >= 0, jnp.take(a, np.maximum(idx, 0), axis=-1), 0.0)


def _prepared_weights(w):
    w_in = w["w_in"]
    depth = w_in.shape[0]
    mla0 = 768 + 512
    cq_hi = mla0 + MLA_Q_RANK
    kr0 = cq_hi + MLA_KV_RANK
    src = np.concatenate([
        np.arange(cq_hi), np.arange(kr0, kr0 + MLA_ROPE_DIM),
        np.full(LANES - HALF - MLA_ROPE_DIM, -1),
        np.arange(cq_hi, kr0), np.arange(kr0 + MLA_ROPE_DIM, w_in.shape[-1])])
    assert src.shape == (P_COLS,)
    o32, _, _, _ = _pair_layout(DIFF_QK_DIM, LANES // DIFF_QK_DIM)
    o64, _, _, _ = _pair_layout(GQA_HEAD_DIM, LANES // GQA_HEAD_DIM)
    oml, _, _, rope_lane = _pair_layout(MLA_ROPE_DIM, 1)
    cols = np.arange(P_COLS)
    for g, o in ((0, o32), (1, o32), (2, o32), (3, o32), (6, o64), (7, o64), (8, o64),
                 (13, o64), (14, o64), (15, o64)):
        cols[g * LANES:(g + 1) * LANES] = g * LANES + o
    cq_lanes = [i for i in range(LANES) if not rope_lane[i]][:MLA_Q_RANK - LANES]
    g11 = np.full(LANES, -1)
    g11[rope_lane] = HALF + oml[rope_lane]
    g11[cq_lanes] = np.arange(len(cq_lanes))
    cols[11 * LANES:12 * LANES] = np.where(g11 >= 0, 11 * LANES + g11, -1)
    w_in_idx = np.where(cols >= 0, src[np.maximum(cols, 0)], -1)
    cq_src = np.full(2 * LANES, -1)
    cq_src[:LANES] = np.arange(LANES)
    cq_src[LANES + np.asarray(cq_lanes)] = LANES + np.arange(len(cq_lanes))

    dqk = MLA_NOPE_DIM + MLA_ROPE_DIM
    assert MLA_KV_RANK == LANES
    wuq = w["mla_w_uq"].reshape(depth, MLA_Q_RANK, MLA_HEADS, dqk)
    wukv = w["mla_w_ukv"].reshape(depth, MLA_KV_RANK, MLA_HEADS, MLA_NOPE_DIM + MLA_V_DIM)
    row_pad = 2 * LANES - MLA_Q_RANK
    wuq_n = jnp.pad(jnp.transpose(wuq[..., :MLA_NOPE_DIM], (0, 2, 1, 3)),
                    ((0, 0), (0, 0), (0, row_pad), (0, 0)))
    wuk_t = jnp.transpose(wukv[..., :MLA_NOPE_DIM], (0, 2, 3, 1))
    w_abs = absorb_call(wuq_n.reshape((-1,) + wuq_n.shape[2:]),
                        wuk_t.reshape((-1,) + wuk_t.shape[2:]))
    w_abs = w_abs.reshape(depth, MLA_HEADS, 2 * LANES, LANES)
    wq_r = jnp.pad(_take_cols(wuq[..., MLA_NOPE_DIM:], np.where(rope_lane, oml, -1)),
                   ((0, 0), (0, row_pad), (0, 0), (0, 0)))
    wq = jnp.concatenate([jnp.transpose(w_abs, (0, 2, 1, 3)).reshape(depth, 2 * LANES, -1),
                          wq_r.reshape(depth, 2 * LANES, -1)], axis=2)
    wq = jnp.swapaxes(_take_cols(jnp.swapaxes(wq, 1, 2), cq_src), 1, 2)
    wuv = jnp.transpose(wukv[..., MLA_NOPE_DIM:], (0, 2, 1, 3))
    own_half = jnp.eye(2, dtype=F32)[None, None, :, None, :, None]
    wv = (wuv.reshape(depth, 2, 2, LANES, 1, MLA_V_DIM) * own_half).reshape(
        depth, 2, 2 * LANES, LANES)
    gq = _take_cols(jnp.pad(w["mla_g_q"], ((0, 0), (0, row_pad))), cq_src)
    return {
        "g_mix_pre": w["g_mix_pre"][:, None], "g_mix_post": w["g_mix_post"][:, None],
        "g_ffn_pre": w["g_ffn_pre"][:, None], "g_ffn_post": w["g_ffn_post"][:, None],
        "w_in": [_take_cols(w_in[l], w_in_idx).astype(BF16) for l in range(depth)],
        "wq": wq.astype(BF16), "wv": wv.astype(BF16),
        "gq": gq[:, None], "gkv": w["mla_g_kv"][:, None],
        "gnq": w["gqa_g_q"][:, o64 % GQA_HEAD_DIM][:, None],
        "gnk": w["gqa_g_k"][:, o64 % GQA_HEAD_DIM][:, None],
        "lam": jnp.stack([w["diff_lam_q1"], w["diff_lam_k1"],
                          w["diff_lam_q2"], w["diff_lam_k2"]], axis=1),
        "g_sub": jnp.tile(w["diff_g_sub"], (1, 2))[:, None],
        "sink": w["swa_sink"],
        "w_out": [w["w_out"][l].astype(BF16) for l in range(depth)],
        "w_up": [w["ffn_w_up"][l].astype(BF16) for l in range(depth)],
        "conv_w": w["ffn_conv_w"],
        "conv_b": w["ffn_conv_b"][:, None],
        "w_down": [w["ffn_w_down"][l].astype(BF16) for l in range(depth)],
    }


def kernel(x, c, ctx, c_ctx, w_mod, b_mod, g_mix_pre, g_mix_post, g_ffn_pre, g_ffn_post, w_in, diff_lam_q1, diff_lam_k1, diff_lam_q2, diff_lam_k2, diff_g_sub, swa_sink, mla_g_q, mla_g_kv, mla_w_uq, mla_w_ukv, gqa_g_q, gqa_g_k, w_out, ffn_w_up, ffn_conv_w, ffn_conv_b, ffn_w_down):
    w = dict(g_mix_pre=g_mix_pre, g_mix_post=g_mix_post, g_ffn_pre=g_ffn_pre,
             g_ffn_post=g_ffn_post, w_in=w_in, diff_lam_q1=diff_lam_q1, diff_lam_k1=diff_lam_k1,
             diff_lam_q2=diff_lam_q2, diff_lam_k2=diff_lam_k2, diff_g_sub=diff_g_sub,
             swa_sink=swa_sink, mla_g_q=mla_g_q, mla_g_kv=mla_g_kv, mla_w_uq=mla_w_uq,
             mla_w_ukv=mla_w_ukv, gqa_g_q=gqa_g_q, gqa_g_k=gqa_g_k, w_out=w_out,
             ffn_w_up=ffn_w_up, ffn_conv_w=ffn_conv_w, ffn_conv_b=ffn_conv_b,
             ffn_w_down=ffn_w_down)
    b, s, d = x.shape
    depth = w_mod.shape[0]
    ctx_row = b
    mod_rows = 16
    cc = jnp.zeros((mod_rows, d), F32).at[:b].set(c).at[ctx_row].set(c_ctx)
    mod_all = mod_call(cc, w_mod, b_mod).reshape(depth, mod_rows, N_MOD, d)

    n_ctx = ctx.shape[1]
    tabs = (_rope_table(n_ctx, s, DIFF_QK_DIM, LANES // DIFF_QK_DIM),
            _rope_table(n_ctx, s, GQA_HEAD_DIM, LANES // GQA_HEAD_DIM),
            _rope_table(n_ctx, s, MLA_ROPE_DIM, 1))

    lw = _prepared_weights(w)
    mod = mod_all
    diff_extra = (lw["lam"], lw["g_sub"])
    mla_extra = (lw["wv"],)
    xc = ctx
    for l in range(depth):
        need_ctx = l < depth - 1
        lambda_init = 0.8 - 0.6 * math.exp(-0.3 * l)
        qkv = proj_call(xc, x, mod, ctx_row, l, lw, tabs)
        ys = (dense_attn_call("diff", qkv, True, l, diff_extra, lambda_init),
              swa_call(lw["sink"], qkv, True, l),
              dense_attn_call("mla", qkv, True, l, mla_extra),
              dense_attn_call("gqa", qkv, True, l))
        x1, h2 = mixout_call(x, ys, mod, None, l, lw)
        x = ffn_call(x1, h2, mod, None, l, lw)
        if need_ctx:
            ysc = (dense_attn_call("diff", qkv, False, l, diff_extra, lambda_init),
                   swa_call(lw["sink"], qkv, False, l),
                   dense_attn_call("mla", qkv, False, l, mla_extra),
                   dense_attn_call("gqa", qkv, False, l))
            xc1, hc2 = mixout_call(xc, ysc, mod, ctx_row, l, lw)
            xc = ffn_call(xc1, hc2, mod, ctx_row, l, lw)
    return x
```

```python
import functools
import math

import jax
import jax.numpy as jnp
import numpy as np
from jax import lax
from jax.experimental import pallas as pl
from jax.experimental.pallas import tpu as pltpu

F32 = jnp.float32
BF16 = jnp.bfloat16

D_MODEL = 1024
GRID_W = 64
ROPE_THETA = 10000.0
NORM_EPS = 1e-6
NEG_INF = -1e30
N_MOD = 6

DIFF_QK_DIM = 32
SWA_HEAD_DIM, WINDOW = 64, 128
MLA_Q_RANK, MLA_KV_RANK = 192, 128
MLA_NOPE_DIM, MLA_ROPE_DIM, MLA_V_DIM, MLA_HEADS = 64, 32, 64, 4
GQA_HEAD_DIM = 64
FFN_DIM = 2816
MIX_COLS = 256

LANES = 128
BF16_SUBLANES = 16
HALF = LANES // 2
VMEM_LIMIT = 56 * 1024 * 1024

P_COLS = 17 * LANES
G_QM, G_VD, G_QG, G_QS = 0, 8, 12, 16
G_QD, G_KD, G_VG, G_KM, G_VM, G_VS, G_KG, G_KS = 20, 22, 24, 26, 28, 30, 32, 33
QKV_COLS = 34 * LANES
LOG2E = math.log2(math.e)

TOK_TILE = 256
CTX_TILE = 256
ATT_TILE = 512
KV_CHUNKS = 2
MIX_TILE = 1024
FFN_TILE = 1024
FFN_CHUNK = 256
MOD_TILE = 2048


def _rms(x, n):
    return lax.rsqrt(jnp.sum(x * x, axis=-1, keepdims=True) * (1.0 / n) + NORM_EPS)


def _lane(shape):
    return lax.broadcasted_iota(jnp.int32, shape, len(shape) - 1)


def _dot(a, b):
    return jnp.dot(a, b, preferred_element_type=F32)


def _dot_nt(a, b):
    return lax.dot_general(a, b, (((1,), (1,)), ((), ())), preferred_element_type=F32)


def mod_kernel(c_ref, w_ref, b_ref, o_ref):
    c = c_ref[...]
    s = c * (1.0 / (1.0 + jnp.exp(-c)))
    o_ref[...] = jnp.dot(s, w_ref[...], preferred_element_type=F32,
                         precision=lax.Precision.HIGHEST) + b_ref[...]


def mod_call(cc, w_mod, b_mod):
    depth, d, n = w_mod.shape
    rows = cc.shape[0]
    tn = MOD_TILE
    return pl.pallas_call(
        mod_kernel,
        name="mod_vectors",
        grid=(depth, n // tn),
        in_specs=[
            pl.BlockSpec((rows, d), lambda l, j: (0, 0)),
            pl.BlockSpec((None, d, tn), lambda l, j: (l, 0, j)),
            pl.BlockSpec((None, 1, tn), lambda l, j: (l, 0, j)),
        ],
        out_specs=pl.BlockSpec((None, rows, tn), lambda l, j: (l, 0, j)),
        out_shape=jax.ShapeDtypeStruct((depth, rows, n), F32),
        compiler_params=pltpu.CompilerParams(
            dimension_semantics=("arbitrary", "arbitrary"), vmem_limit_bytes=VMEM_LIMIT),
    )(cc, w_mod, b_mod.reshape(depth, 1, n))


def absorb_kernel(a_ref, b_ref, o_ref):
    o_ref[...] = jnp.dot(a_ref[...], b_ref[...], preferred_element_type=F32,
                         precision=lax.Precision.HIGHEST)


def absorb_call(a, b):
    heads, m, k = a.shape
    n = b.shape[2]
    return pl.pallas_call(
        absorb_kernel,
        name="mla_absorb",
        grid=(heads,),
        in_specs=[pl.BlockSpec((None, m, k), lambda h: (h, 0, 0)),
                  pl.BlockSpec((None, k, n), lambda h: (h, 0, 0))],
        out_specs=pl.BlockSpec((None, m, n), lambda h: (h, 0, 0)),
        out_shape=jax.ShapeDtypeStruct((heads, m, n), F32),
        compiler_params=pltpu.CompilerParams(dimension_semantics=("arbitrary",)),
    )(a, b)


def _rope(x, tab_ref):
    return x * tab_ref[0] + pltpu.roll(x, HALF, 1) * tab_ref[1]


def _halves(lo, hi, shape):
    return jnp.where(_lane(shape) < HALF, lo, hi)


def _head_rms(x, first):
    sq = x * x
    ss_a = jnp.sum(jnp.where(first, sq, 0.0), axis=-1, keepdims=True)
    ss_b = jnp.sum(jnp.where(first, 0.0, sq), axis=-1, keepdims=True)
    r_a = lax.rsqrt(ss_a * (1.0 / HALF) + NORM_EPS)
    r_b = lax.rsqrt(ss_b * (1.0 / HALF) + NORM_EPS)
    return jnp.where(first, r_a, r_b)


def proj_kernel(xc_ref, x_ref, mod_ref, gpre_ref, win_ref, wq_ref, gq_ref, gkv_ref,
                gnq_ref, gnk_ref, t32_ref, t64_ref, tml_ref, o_ref, pa_ref, pb_ref, *,
                tiles_per_sample, n_tiles):
    g = pl.program_id(0)
    is_ctx = jnp.minimum(g, n_tiles - 1) % tiles_per_sample == tiles_per_sample - 1

    @pl.when(g == 0)
    def _():
        pb_ref[...] = jnp.zeros_like(pb_ref)

    def project(dst_ref):
        x = jnp.where(is_ctx, xc_ref[...], x_ref[...])
        h = x * _rms(x, D_MODEL) * gpre_ref[...]
        h = h * (1.0 + mod_ref[1:2, :]) + mod_ref[0:1, :]
        dst_ref[...] = _dot(h.astype(BF16), win_ref[...])

    finish = functools.partial(_proj_finish, wq_ref, gq_ref, gkv_ref, gnq_ref, gnk_ref,
                               t32_ref, t64_ref, tml_ref, o_ref)

    @pl.when(g % 2 == 0)
    def _():
        project(pa_ref)
        finish(pb_ref)

    @pl.when(g % 2 == 1)
    def _():
        project(pb_ref)
        finish(pa_ref)


def _proj_finish(wq_ref, gq_ref, gkv_ref, gnq_ref, gnk_ref, t32_ref, t64_ref, tml_ref,
                 o_ref, p_ref):
    def grp(g):
        return p_ref[:, g * LANES:(g + 1) * LANES]

    def put(g, v):
        o_ref[:, g * LANES:(g + 1) * LANES] = v.astype(BF16)

    def rope32(v):
        return _rope(v, t32_ref)

    def rope64(v):
        return _rope(v, t64_ref)

    def ropeml(v):
        return _rope(v, tml_ref)

    lane = _lane((p_ref.shape[0], LANES))
    low = lane < HALF
    head_a = lane % HALF < HALF // 2

    def with_ones(v):
        return jnp.where(low, v, 1.0), jnp.where(low, 1.0, v)

    for g in range(2):
        put(G_QD + g, rope32(grp(g)) * (DIFF_QK_DIM ** -0.5 * LOG2E))
        put(G_KD + g, rope32(grp(2 + g)))
        v_lo, v_hi = with_ones(grp(4 + g))
        put(G_VD + 2 * g, v_lo)
        put(G_VD + 2 * g + 1, v_hi)

    def put_gqa_queries(g0, q01, q23):
        put(g0, jnp.where(head_a, q01, 0.0))
        put(g0 + 1, jnp.where(head_a, pltpu.roll(q01, LANES - HALF // 2, 1), 0.0))
        put(g0 + 2, jnp.where(head_a, 0.0, pltpu.roll(q23, HALF // 2, 1)))
        put(g0 + 3, jnp.where(head_a, 0.0, q23))

    scale_s = SWA_HEAD_DIM ** -0.5 * LOG2E
    put_gqa_queries(G_QS, rope64(grp(6)) * scale_s, rope64(grp(7)) * scale_s)
    put(G_KS, rope64(grp(8)))
    v_lo, v_hi = with_ones(grp(9))
    put(G_VS, v_lo)
    put(G_VS + 1, v_hi)

    cq0, cq1, ckv = grp(10), grp(11), grp(12)
    rope_lanes = lane % HALF < MLA_ROPE_DIM // 2
    cq_lanes = ~rope_lanes & (lane < HALF + MLA_ROPE_DIM)
    ss = (jnp.sum(cq0 * cq0, axis=-1, keepdims=True)
          + jnp.sum(jnp.where(cq_lanes, cq1 * cq1, 0.0), axis=-1, keepdims=True))
    rq = lax.rsqrt(ss * (1.0 / MLA_Q_RANK) + NORM_EPS)
    cqn = jnp.concatenate([cq0 * rq * gq_ref[:, :LANES], cq1 * rq * gq_ref[:, LANES:]], axis=1)
    qm = _dot(cqn.astype(BF16), wq_ref[...])
    scale_m = (MLA_NOPE_DIM + MLA_ROPE_DIM) ** -0.5 * LOG2E
    for hd in range(MLA_HEADS):
        put(G_QM + 2 * hd, qm[:, hd * LANES:(hd + 1) * LANES] * scale_m)
        qr = qm[:, (MLA_HEADS + hd) * LANES:(MLA_HEADS + hd + 1) * LANES]
        put(G_QM + 2 * hd + 1, ropeml(qr) * scale_m)
    ckvn = ckv * _rms(ckv, MLA_KV_RANK) * gkv_ref[...]
    put(G_KM, ckvn)
    put(G_KM + 1, jnp.where(rope_lanes, ropeml(cq1), 0.0))
    put(G_VM, ckvn)
    put(G_VM + 1, jnp.ones_like(ckvn))

    scale_g = GQA_HEAD_DIM ** -0.5 * LOG2E
    q01 = grp(13)
    q01 = rope64(q01 * _head_rms(q01, head_a) * gnq_ref[...]) * scale_g
    q23 = grp(14)
    q23 = rope64(q23 * _head_rms(q23, head_a) * gnq_ref[...]) * scale_g
    put_gqa_queries(G_QG, q01, q23)
    kg = grp(15)
    put(G_KG, rope64(kg * _head_rms(kg, head_a) * gnk_ref[...]))
    v_lo, v_hi = with_ones(grp(16))
    put(G_VG, v_lo)
    put(G_VG + 1, v_hi)


def proj_call(xc, x, mod, ctx_row, lw, tabs):
    b, s, d = x.shape
    tq = TOK_TILE
    assert xc.shape[1] == CTX_TILE == tq
    s_all = CTX_TILE + s
    per = s_all // tq
    n = b * per
    const2 = lambda g: (0, 0)

    def proj_tile(g):
        ga = jnp.minimum(g, n - 1)
        return ga // per, ga % per

    def done_tile(g):
        gb = jnp.maximum(g - 1, 0)
        return gb // per, gb % per

    def x_map(g):
        i, t = proj_tile(g)
        return i, jnp.minimum(t, per - 2), 0

    def mod_map(g):
        i, t = proj_tile(g)
        return jnp.where(t == per - 1, ctx_row, i), 0, 0

    tab_spec = pl.BlockSpec((2, tq, LANES), lambda g: (0, done_tile(g)[1], 0))
    return pl.pallas_call(
        functools.partial(proj_kernel, tiles_per_sample=per, n_tiles=n),
        name="proj",
        grid=(n + 1,),
        in_specs=[
            pl.BlockSpec((None, tq, d), lambda g: (proj_tile(g)[0], 0, 0)),
            pl.BlockSpec((None, tq, d), x_map),
            pl.BlockSpec((None, N_MOD, d), mod_map),
            pl.BlockSpec((1, d), const2),
            pl.BlockSpec((d, P_COLS), const2),
            pl.BlockSpec((2 * LANES, 2 * MLA_HEADS * LANES), const2),
            pl.BlockSpec((1, 2 * LANES), const2),
            pl.BlockSpec((1, LANES), const2),
            pl.BlockSpec((1, LANES), const2),
            pl.BlockSpec((1, LANES), const2),
            tab_spec, tab_spec, tab_spec,
        ],
        out_specs=pl.BlockSpec((None, tq, QKV_COLS), lambda g: done_tile(g) + (0,)),
        out_shape=jax.ShapeDtypeStruct((b, s_all, QKV_COLS), BF16),
        scratch_shapes=[pltpu.VMEM((tq, P_COLS), F32), pltpu.VMEM((tq, P_COLS), F32)],
        compiler_params=pltpu.CompilerParams(
            dimension_semantics=("arbitrary",), vmem_limit_bytes=VMEM_LIMIT),
    )(xc, x, mod, lw["g_mix_pre"], lw["w_in"], lw["wq"], lw["gq"], lw["gkv"],
      lw["gnq"], lw["gnk"], tabs[0], tabs[1], tabs[2])


def _group(ref, rows, g, width=1):
    return ref[rows, g * width * LANES:(g + 1) * width * LANES]


def _attend(q_ref, segs, k_ref, v_ref, n_chunks, width):
    chunk = k_ref.shape[0] // n_chunks

    def query(g):
        if isinstance(g, int):
            return _group(q_ref, slice(None), g, width)
        g, unit, n_units = g
        q = _group(q_ref, slice(None), g, width)
        per = HALF // n_units
        keep = _lane(q.shape) % HALF // per == unit
        return jnp.where(keep, q, jnp.zeros_like(q))

    qs = [jnp.concatenate([query(g) for g in qg], axis=0) for qg, _, _ in segs]

    def body(i, carry):
        rows = pl.ds(pl.multiple_of(i * chunk, LANES), chunk)
        out = []
        for j, (q, (_, kb, parts)) in enumerate(zip(qs, segs)):
            m, acc = carry[2 * j], carry[2 * j + 1]
            s = _dot_nt(q, _group(k_ref, rows, kb, width))
            m_new = jnp.maximum(m, jnp.max(s, axis=-1, keepdims=True))
            p = jnp.exp2(s - m_new).astype(BF16)
            tq = q.shape[0] // sum(n for n, _ in parts)
            pv, r0 = [], 0
            for n, vb in parts:
                pv.append(_dot(p[r0:r0 + n * tq], _group(v_ref, rows, vb, width)))
                r0 += n * tq
            pv = pv[0] if len(pv) == 1 else jnp.concatenate(pv, axis=0)
            out += [m_new, acc * jnp.exp2(m - m_new) + pv]
        return tuple(out)

    init = []
    for q in qs:
        init += [jnp.full((q.shape[0], 1), NEG_INF, F32),
                 jnp.zeros((q.shape[0], width * LANES), F32)]
    carry = lax.fori_loop(0, n_chunks, body, tuple(init), unroll=True)
    half = width * HALF
    return [carry[2 * j + 1] / pltpu.roll(carry[2 * j + 1], half, 1) for j in range(len(segs))]


_PLANS = {
    "gqa": (1, ((((0, 1, 2, 3), 0, ((2, 0), (2, 1))),),)),
    "mla": (2, ((((0, 1, 2, 3), 0, ((2, 0), (2, 0))),),)),
    "diff": (1, tuple(
        ((tuple((ob, i, 4) for i in range(4)), ob,
          ((2, 2 * ob), (2, 2 * ob + 1))),) for ob in range(2))),
}


def dense_kernel(*refs, kind, n_chunks, lambda_init):
    refs = list(refs)
    if kind == "diff":
        lam_ref, g_ref = refs[:2]
        refs = refs[2:]
    elif kind == "mla":
        wv_ref = refs[0]
        refs = refs[1:]
    q_ref, k_ref, v_ref, o_ref = refs
    tq = q_ref.shape[0]
    width, loops = _PLANS[kind]
    res = [_attend(q_ref, segs, k_ref, v_ref, n_chunks, width) for segs in loops]

    def tiles(r):
        return [r[i * tq:(i + 1) * tq] for i in range(r.shape[0] // tq)]

    if kind == "gqa":
        h = [t for r in res[0] for t in tiles(r)]
        blocks = [_halves(h[0], pltpu.roll(h[1], HALF, 1), h[0].shape),
                  _halves(pltpu.roll(h[2], HALF, 1), h[3], h[0].shape)]
    elif kind == "mla":
        h = [r[:, :LANES].astype(BF16) for r in tiles(res[0][0])]
        blocks = [_dot(jnp.concatenate(h[2 * ob:2 * ob + 2], axis=1), wv_ref[ob]) for ob in range(2)]
    else:
        lv = lam_ref[...]
        lam = (jnp.exp(jnp.sum(lv[0:1] * lv[1:2], axis=-1, keepdims=True))
               - jnp.exp(jnp.sum(lv[2:3] * lv[3:4], axis=-1, keepdims=True)) + lambda_init)
        blocks = []
        for (r,) in res:
            a = tiles(r)
            o = _halves(a[0] - lam * a[1], a[2] - lam * a[3], a[0].shape)
            blocks.append(o * _head_rms(o, _lane(o.shape) < HALF) * g_ref[...] * (1.0 - lambda_init))
    for ob, blk in enumerate(blocks):
        o_ref[:, ob * LANES:(ob + 1) * LANES] = blk.astype(BF16)


def _qkv_specs(lay, latent, s_all, tq):
    (q0, qn), (k0, kn), (v0, vn) = lay
    assert q0 % qn == 0 and k0 % kn == 0 and v0 % vn == 0
    ctx_blk = s_all // CTX_TILE - 1
    if latent:
        q_map = lambda i, t: (i, t, q0 // qn)
        k_map = lambda i, t: (i, 0, k0 // kn)
        v_map = lambda i, t: (i, 0, v0 // vn)
    else:
        q_map = lambda i, t: (i, ctx_blk, q0 // qn)
        k_map = lambda i, t: (i, ctx_blk, k0 // kn)
        v_map = lambda i, t: (i, ctx_blk, v0 // vn)
    kv_rows = s_all if latent else CTX_TILE
    return [
        pl.BlockSpec((None, tq, qn * LANES), q_map),
        pl.BlockSpec((None, kv_rows, kn * LANES), k_map),
        pl.BlockSpec((None, kv_rows, vn * LANES), v_map),
    ]


def dense_attn_call(kind, qkv, latent, extra=None, lambda_init=None):
    b, s_all, _ = qkv.shape
    tq = ATT_TILE if latent else CTX_TILE
    sq = s_all - CTX_TILE if latent else CTX_TILE
    lay = {"diff": ((G_QD, 2), (G_KD, 2), (G_VD, 4)),
           "mla": ((G_QM, 8), (G_KM, 2), (G_VM, 2)),
           "gqa": ((G_QG, 4), (G_KG, 1), (G_VG, 2))}[kind]
    specs = _qkv_specs(lay, latent, s_all, tq)
    args = [qkv, qkv, qkv]
    if kind == "diff":
        specs = [pl.BlockSpec((4, DIFF_QK_DIM), lambda i, t: (0, 0)),
                 pl.BlockSpec((1, LANES), lambda i, t: (0, 0))] + specs
        args = list(extra) + args
    elif kind == "mla":
        specs = [pl.BlockSpec((2, 2 * LANES, LANES), lambda i, t: (0, 0, 0))] + specs
        args = list(extra) + args
    body = functools.partial(dense_kernel, kind=kind, n_chunks=KV_CHUNKS if latent else 1,
                             lambda_init=lambda_init)
    return pl.pallas_call(
        body,
        name=kind + ("_attn" if latent else "_attn_ctx"),
        grid=(b, sq // tq),
        in_specs=specs,
        out_specs=pl.BlockSpec((None, tq, MIX_COLS), lambda i, t: (i, t, 0)),
        out_shape=jax.ShapeDtypeStruct((b, sq, MIX_COLS), BF16),
        compiler_params=pltpu.CompilerParams(
            dimension_semantics=("arbitrary", "arbitrary"), vmem_limit_bytes=VMEM_LIMIT),
    )(*args)


def swa_kernel(sink_ref, q_ref, k_ref, v_ref, o_ref, *, s_lat):
    tq = q_ref.shape[0]
    heads = q_ref.shape[1] // LANES
    rows = heads * tq
    t = pl.program_id(1)
    q = jnp.concatenate([_group(q_ref, slice(None), g) for g in range(heads)], axis=0)
    row = lax.broadcasted_iota(jnp.int32, (rows, 1), 0)
    sink = jnp.zeros((rows, 1), F32)
    for h in range(heads):
        sink = jnp.where(row // tq == h, sink_ref[h] * LOG2E, sink)

    spans = [(slice(s_lat, s_lat + CTX_TILE), None)]
    if s_lat:
        width = tq + 2 * WINDOW
        start = jnp.clip(t * tq - WINDOW, 0, s_lat - width)
        rel = (start - t * tq + lax.broadcasted_iota(jnp.int32, (tq, width), 1)
               - lax.broadcasted_iota(jnp.int32, (tq, width), 0))
        valid = jnp.concatenate([jnp.abs(rel) <= WINDOW] * heads, axis=0)
        spans.append((pl.ds(pl.multiple_of(start, WINDOW), width), valid))

    ss = []
    for span, valid in spans:
        s = _dot_nt(q, _group(k_ref, span, 0))
        ss.append(s if valid is None else jnp.where(valid, s, NEG_INF))
    m = functools.reduce(jnp.maximum, [jnp.max(s, axis=-1, keepdims=True) for s in ss] + [sink])
    ps = [jnp.exp2(s - m).astype(BF16) for s in ss]
    half = rows // 2
    acc = jnp.concatenate(
        [sum(_dot(p[vb * half:(vb + 1) * half], _group(v_ref, span, vb))
             for p, (span, _) in zip(ps, spans)) for vb in range(2)], axis=0)
    res = acc / (pltpu.roll(acc, HALF, 1) + jnp.exp2(sink - m))
    h = [res[i * tq:(i + 1) * tq] for i in range(heads)]
    o_ref[:, :LANES] = _halves(h[0], pltpu.roll(h[1], HALF, 1), h[0].shape).astype(BF16)
    o_ref[:, LANES:] = _halves(pltpu.roll(h[2], HALF, 1), h[3], h[0].shape).astype(BF16)


def swa_call(sink, qkv, latent):
    b, s_all, _ = qkv.shape
    tq = TOK_TILE
    s_lat = s_all - CTX_TILE if latent else 0
    sq = s_lat if latent else CTX_TILE
    specs = [pl.BlockSpec(memory_space=pltpu.SMEM)]
    specs += _qkv_specs(((G_QS, 4), (G_KS, 1), (G_VS, 2)), latent, s_all, tq)
    return pl.pallas_call(
        functools.partial(swa_kernel, s_lat=s_lat),
        name="swa_attn" if latent else "swa_attn_ctx",
        grid=(b, sq // tq),
        in_specs=specs,
        out_specs=pl.BlockSpec((None, tq, MIX_COLS), lambda i, t: (i, t, 0)),
        out_shape=jax.ShapeDtypeStruct((b, sq, MIX_COLS), BF16),
        compiler_params=pltpu.CompilerParams(
            dimension_semantics=("arbitrary", "arbitrary"), vmem_limit_bytes=VMEM_LIMIT),
    )(sink, qkv, qkv, qkv)


def mixout_kernel(x_ref, ya_ref, yb_ref, ym_ref, yd_ref, w_ref, mod_ref, gpost_ref,
                  gfpre_ref, x1_ref, h2_ref, za_ref, zb_ref):
    g = pl.program_id(0)

    @pl.when(g == 0)
    def _():
        zb_ref[...] = jnp.zeros_like(zb_ref)

    def project(dst_ref):
        y = jnp.concatenate([ya_ref[...], yb_ref[...], ym_ref[...], yd_ref[...]], axis=1)
        dst_ref[...] = _dot(y, w_ref[...])

    def finish(z_ref):
        z = z_ref[...]
        x1 = x_ref[...] + mod_ref[2:3, :] * (z * _rms(z, D_MODEL) * gpost_ref[...])
        x1_ref[...] = x1
        h2 = x1 * _rms(x1, D_MODEL) * gfpre_ref[...]
        h2_ref[...] = (h2 * (1.0 + mod_ref[4:5, :]) + mod_ref[3:4, :]).astype(BF16)

    @pl.when(g % 2 == 0)
    def _():
        project(za_ref)
        finish(zb_ref)

    @pl.when(g % 2 == 1)
    def _():
        project(zb_ref)
        finish(za_ref)


def mixout_call(x, ys, mod, mod_row, lw):
    b, s, d = x.shape
    tq = min(MIX_TILE, s)
    per = s // tq
    n = b * per

    def proj_tile(g):
        ga = jnp.minimum(g, n - 1)
        return ga // per, ga % per, 0

    def done_tile(g):
        gb = jnp.maximum(g - 1, 0)
        return gb // per, gb % per, 0

    if mod_row is None:
        mod_map = lambda g: (done_tile(g)[0], 0, 0)
    else:
        mod_map = lambda g: (mod_row, 0, 0)
    const2 = lambda g: (0, 0)
    y_spec = pl.BlockSpec((None, tq, MIX_COLS), proj_tile)
    return pl.pallas_call(
        mixout_kernel,
        name="mixout",
        grid=(n + 1,),
        in_specs=[pl.BlockSpec((None, tq, d), done_tile), y_spec, y_spec, y_spec, y_spec,
                  pl.BlockSpec((4 * MIX_COLS, d), const2),
                  pl.BlockSpec((None, N_MOD, d), mod_map),
                  pl.BlockSpec((1, d), const2), pl.BlockSpec((1, d), const2)],
        out_specs=[pl.BlockSpec((None, tq, d), done_tile), pl.BlockSpec((None, tq, d), done_tile)],
        out_shape=[jax.ShapeDtypeStruct((b, s, d), F32), jax.ShapeDtypeStruct((b, s, d), BF16)],
        scratch_shapes=[pltpu.VMEM((tq, d), F32), pltpu.VMEM((tq, d), F32)],
        compiler_params=pltpu.CompilerParams(
            dimension_semantics=("arbitrary",), vmem_limit_bytes=VMEM_LIMIT,
            allow_input_fusion=[i == 5 for i in range(9)]),
    )(x, *ys, lw["w_out"], mod, lw["g_mix_post"], lw["g_ffn_pre"])


def ffn_kernel(x1_ref, hc_ref, hp_ref, hn_ref, wup_ref, cw_ref, cb_ref, wdn_ref, mod_ref,
               gpost_ref, o_ref, g_ref):
    t = pl.program_id(1)
    tq = hc_ref.shape[0]
    halo = hp_ref.shape[0]
    hp = jnp.where(t > 0, hp_ref[...], jnp.zeros_like(hp_ref))
    hn = jnp.where(t < pl.num_programs(1) - 1, hn_ref[...], jnp.zeros_like(hn_ref))
    hext = jnp.concatenate([hp, hc_ref[...], hn], axis=0)
    rows = hext.shape[0]

    def conv_cols(cols):
        u = _dot(hext, wup_ref[:, cols])
        w = cw_ref[:, cols]
        return (pltpu.roll(u, 1, 0)[halo:halo + tq] * w[0:1]
                + u[halo:halo + tq] * w[1:2]
                + pltpu.roll(u, rows - 1, 0)[halo:halo + tq] * w[2:3] + cb_ref[:, cols])

    for j in range(FFN_DIM // FFN_CHUNK):
        a = conv_cols(slice(j * FFN_CHUNK, (j + 1) * FFN_CHUNK))
        v = conv_cols(slice(FFN_DIM + j * FFN_CHUNK, FFN_DIM + (j + 1) * FFN_CHUNK))
        g_ref[:, j * FFN_CHUNK:(j + 1) * FFN_CHUNK] = (a * (1.0 / (1.0 + jnp.exp(-a))) * v).astype(BF16)

    f = _dot(g_ref[...], wdn_ref[...])
    o_ref[...] = x1_ref[...] + mod_ref[5:6, :] * (f * _rms(f, D_MODEL) * gpost_ref[...])


def ffn_call(x1, h2, mod, mod_row, lw):
    b, s, d = x1.shape
    tq = min(FFN_TILE, s)
    halo = BF16_SUBLANES
    per_tile = tq // halo
    last = s // halo - 1
    if mod_row is None:
        mod_map = lambda i, t: (i, 0, 0)
    else:
        mod_map = lambda i, t: (mod_row, 0, 0)
    const2 = lambda i, t: (0, 0)
    tok = lambda i, t: (i, t, 0)
    return pl.pallas_call(
        ffn_kernel,
        name="conv_ffn",
        grid=(b, s // tq),
        in_specs=[
            pl.BlockSpec((None, tq, d), tok),
            pl.BlockSpec((None, tq, d), tok),
            pl.BlockSpec((None, halo, d), lambda i, t: (i, jnp.maximum(t * per_tile - 1, 0), 0)),
            pl.BlockSpec((None, halo, d), lambda i, t: (i, jnp.minimum((t + 1) * per_tile, last), 0)),
            pl.BlockSpec((d, 2 * FFN_DIM), const2, pipeline_mode=pl.Buffered(1)),
            pl.BlockSpec((3, 2 * FFN_DIM), const2),
            pl.BlockSpec((1, 2 * FFN_DIM), const2),
            pl.BlockSpec((FFN_DIM, d), const2, pipeline_mode=pl.Buffered(1)),
            pl.BlockSpec((None, N_MOD, d), mod_map),
            pl.BlockSpec((1, d), const2),
        ],
        out_specs=pl.BlockSpec((None, tq, d), tok),
        out_shape=jax.ShapeDtypeStruct((b, s, d), F32),
        scratch_shapes=[pltpu.VMEM((tq, FFN_DIM), BF16)],
        compiler_params=pltpu.CompilerParams(
            dimension_semantics=("arbitrary", "arbitrary"), vmem_limit_bytes=VMEM_LIMIT,
            allow_input_fusion=[i in (4, 7) for i in range(10)]),
    )(x1, h2, h2, h2, lw["w_up"], lw["conv_w"], lw["conv_b"], lw["w_down"], mod,
      lw["g_ffn_post"])


def _pair_layout(rot_dim, n_units):
    n = rot_dim // 4
    lane = np.arange(LANES)
    half, mu = lane // HALF, lane % HALF
    unit, r = mu // (2 * n), mu % (2 * n)
    axis, f = r // n, r % n
    used = unit < n_units
    orig = np.where(used, unit * rot_dim + axis * 2 * n + half * n + f, -1)
    return orig, axis, f, used


def _rope_table(n_ctx, s, rot_dim, n_units):
    n = rot_dim // 4
    _, axis, f, used = _pair_layout(rot_dim, n_units)
    tok = np.arange(s)
    pos = np.stack([tok // GRID_W, tok % GRID_W], axis=1).astype(np.float32)
    inv_freq = ROPE_THETA ** (-jnp.arange(n, dtype=F32) / n)
    ang = jnp.asarray(pos)[:, axis] * inv_freq[f][None, :]
    sign = np.where(np.arange(LANES) < HALF, -1.0, 1.0).astype(np.float32)
    cos = jnp.where(used[None], jnp.cos(ang), 1.0)
    sin = jnp.where(used[None], jnp.sin(ang) * sign[None], 0.0)
    tab = jnp.stack([cos, sin]).astype(F32)
    ident = jnp.stack([jnp.ones((n_ctx, LANES), F32), jnp.zeros((n_ctx, LANES), F32)])
    return jnp.concatenate([tab, ident], axis=1)


def _take_cols(a, idx):
    idx = np.asarray(idx)
    return jnp.where(idx >= 0, jnp.take(a, np.maximum(idx, 0), axis=-1), 0.0)


def _layer_weights(l, w):
    w_in = w["w_in"][l]
    mla0 = 768 + 512
    cq_hi = mla0 + MLA_Q_RANK
    kr0 = cq_hi + MLA_KV_RANK
    src = np.concatenate([
        np.arange(cq_hi), np.arange(kr0, kr0 + MLA_ROPE_DIM),
        np.full(LANES - HALF - MLA_ROPE_DIM, -1),
        np.arange(cq_hi, kr0), np.arange(kr0 + MLA_ROPE_DIM, w_in.shape[1])])
    assert src.shape == (P_COLS,)
    o32, _, _, _ = _pair_layout(DIFF_QK_DIM, LANES // DIFF_QK_DIM)
    o64, _, _, _ = _pair_layout(GQA_HEAD_DIM, LANES // GQA_HEAD_DIM)
    oml, _, _, rope_lane = _pair_layout(MLA_ROPE_DIM, 1)
    cols = np.arange(P_COLS)
    for g, o in ((0, o32), (1, o32), (2, o32), (3, o32), (6, o64), (7, o64), (8, o64),
                 (13, o64), (14, o64), (15, o64)):
        cols[g * LANES:(g + 1) * LANES] = g * LANES + o
    cq_lanes = [i for i in range(LANES) if not rope_lane[i]][:MLA_Q_RANK - LANES]
    g11 = np.full(LANES, -1)
    g11[rope_lane] = HALF + oml[rope_lane]
    g11[cq_lanes] = np.arange(len(cq_lanes))
    cols[11 * LANES:12 * LANES] = np.where(g11 >= 0, 11 * LANES + g11, -1)
    w_in_p = _take_cols(w_in, np.where(cols >= 0, src[np.maximum(cols, 0)], -1))
    cq_src = np.full(2 * LANES, -1)
    cq_src[:LANES] = np.arange(LANES)
    cq_src[LANES + np.asarray(cq_lanes)] = LANES + np.arange(len(cq_lanes))

    dqk = MLA_NOPE_DIM + MLA_ROPE_DIM
    assert MLA_KV_RANK == LANES
    wuq = w["mla_w_uq"][l].reshape(MLA_Q_RANK, MLA_HEADS, dqk)
    wukv = w["mla_w_ukv"][l].reshape(MLA_KV_RANK, MLA_HEADS, MLA_NOPE_DIM + MLA_V_DIM)
    row_pad = 2 * LANES - MLA_Q_RANK
    wuq_n = jnp.pad(jnp.transpose(wuq[:, :, :MLA_NOPE_DIM], (1, 0, 2)), ((0, 0), (0, row_pad), (0, 0)))
    w_abs = absorb_call(wuq_n, jnp.transpose(wukv[:, :, :MLA_NOPE_DIM], (1, 2, 0)))
    wq_r = jnp.pad(_take_cols(wuq[:, :, MLA_NOPE_DIM:], np.where(rope_lane, oml, -1)),
                   ((0, row_pad), (0, 0), (0, 0)))
    wq = jnp.concatenate([jnp.transpose(w_abs, (1, 0, 2)).reshape(2 * LANES, -1),
                          wq_r.reshape(2 * LANES, -1)], axis=1)
    wq = _take_cols(wq.T, cq_src).T
    wuv = jnp.transpose(wukv[:, :, MLA_NOPE_DIM:], (1, 0, 2))
    wv = jnp.zeros((2, 2, LANES, 2, MLA_V_DIM), F32)
    for hd in range(MLA_HEADS):
        wv = wv.at[hd // 2, hd % 2, :, hd % 2, :].set(wuv[hd])
    wv = wv.reshape(2, 2 * LANES, LANES)
    gq = _take_cols(jnp.pad(w["mla_g_q"][l], (0, row_pad)), cq_src)[None]
    return {
        "g_mix_pre": w["g_mix_pre"][l][None], "g_mix_post": w["g_mix_post"][l][None],
        "g_ffn_pre": w["g_ffn_pre"][l][None], "g_ffn_post": w["g_ffn_post"][l][None],
        "w_in": w_in_p.astype(BF16),
        "wq": wq.astype(BF16), "wv": wv.astype(BF16),
        "gq": gq, "gkv": w["mla_g_kv"][l][None],
        "gnq": w["gqa_g_q"][l][o64 % GQA_HEAD_DIM][None],
        "gnk": w["gqa_g_k"][l][o64 % GQA_HEAD_DIM][None],
        "lam": jnp.stack([w["diff_lam_q1"][l], w["diff_lam_k1"][l],
                          w["diff_lam_q2"][l], w["diff_lam_k2"][l]]),
        "g_sub": jnp.tile(w["diff_g_sub"][l], 2)[None],
        "sink": w["swa_sink"][l],
        "w_out": w["w_out"][l].astype(BF16),
        "w_up": w["ffn_w_up"][l].astype(BF16),
        "conv_w": w["ffn_conv_w"][l],
        "conv_b": w["ffn_conv_b"][l][None],
        "w_down": w["ffn_w_down"][l].astype(BF16),
    }


def kernel(x, c, ctx, c_ctx, w_mod, b_mod, g_mix_pre, g_mix_post, g_ffn_pre, g_ffn_post, w_in, diff_lam_q1, diff_lam_k1, diff_lam_q2, diff_lam_k2, diff_g_sub, swa_sink, mla_g_q, mla_g_kv, mla_w_uq, mla_w_ukv, gqa_g_q, gqa_g_k, w_out, ffn_w_up, ffn_conv_w, ffn_conv_b, ffn_w_down):
    w = dict(g_mix_pre=g_mix_pre, g_mix_post=g_mix_post, g_ffn_pre=g_ffn_pre,
             g_ffn_post=g_ffn_post, w_in=w_in, diff_lam_q1=diff_lam_q1, diff_lam_k1=diff_lam_k1,
             diff_lam_q2=diff_lam_q2, diff_lam_k2=diff_lam_k2, diff_g_sub=diff_g_sub,
             swa_sink=swa_sink, mla_g_q=mla_g_q, mla_g_kv=mla_g_kv, mla_w_uq=mla_w_uq,
             mla_w_ukv=mla_w_ukv, gqa_g_q=gqa_g_q, gqa_g_k=gqa_g_k, w_out=w_out,
             ffn_w_up=ffn_w_up, ffn_conv_w=ffn_conv_w, ffn_conv_b=ffn_conv_b,
             ffn_w_down=ffn_w_down)
    b, s, d = x.shape
    depth = w_mod.shape[0]
    ctx_row = b
    mod_rows = 16
    cc = jnp.zeros((mod_rows, d), F32).at[:b].set(c).at[ctx_row].set(c_ctx)
    mod_all = mod_call(cc, w_mod, b_mod).reshape(depth, mod_rows, N_MOD, d)

    n_ctx = ctx.shape[1]
    tabs = (_rope_table(n_ctx, s, DIFF_QK_DIM, LANES // DIFF_QK_DIM),
            _rope_table(n_ctx, s, GQA_HEAD_DIM, LANES // GQA_HEAD_DIM),
            _rope_table(n_ctx, s, MLA_ROPE_DIM, 1))

    xc = ctx
    for l in range(depth):
        need_ctx = l < depth - 1
        lambda_init = 0.8 - 0.6 * math.exp(-0.3 * l)
        lw = _layer_weights(l, w)
        mod = mod_all[l]
        qkv = proj_call(xc, x, mod, ctx_row, lw, tabs)
        diff_extra = (lw["lam"], lw["g_sub"])
        mla_extra = (lw["wv"],)
        ys = (dense_attn_call("diff", qkv, True, diff_extra, lambda_init),
              swa_call(lw["sink"], qkv, True),
              dense_attn_call("mla", qkv, True, mla_extra),
              dense_attn_call("gqa", qkv, True))
        x1, h2 = mixout_call(x, ys, mod, None, lw)
        x = ffn_call(x1, h2, mod, None, lw)
        if need_ctx:
            ysc = (dense_attn_call("diff", qkv, False, diff_extra, lambda_init),
                   swa_call(lw["sink"], qkv, False),
                   dense_attn_call("mla", qkv, False, mla_extra),
                   dense_attn_call("gqa", qkv, False))
            xc1, hc2 = mixout_call(xc, ysc, mod, ctx_row, lw)
            xc = ffn_call(xc1, hc2, mod, ctx_row, lw)
    return x
```

```python
import functools
import math

import jax
import jax.numpy as jnp
import numpy as np
from jax import lax
from jax.experimental import pallas as pl
from jax.experimental.pallas import tpu as pltpu

F32 = jnp.float32
BF16 = jnp.bfloat16

D_MODEL = 1024
GRID_W = 64
ROPE_THETA = 10000.0
NORM_EPS = 1e-6
NEG_INF = -1e30
N_MOD = 6

DIFF_QK_DIM = 32
SWA_HEAD_DIM, WINDOW = 64, 128
MLA_Q_RANK, MLA_KV_RANK = 192, 128
MLA_NOPE_DIM, MLA_ROPE_DIM, MLA_V_DIM, MLA_HEADS = 64, 32, 64, 4
GQA_HEAD_DIM = 64
FFN_DIM = 2816
MIX_COLS = 256

LANES = 128
BF16_SUBLANES = 16
HALF = LANES // 2
VMEM_LIMIT = 56 * 1024 * 1024

P_COLS = 17 * LANES
G_QM, G_VD, G_QG, G_QS = 0, 8, 12, 16
G_QD, G_KD, G_VG, G_KM, G_VM, G_VS, G_KG, G_KS = 20, 22, 24, 26, 28, 30, 32, 33
QKV_COLS = 34 * LANES
LOG2E = math.log2(math.e)

TOK_TILE = 256
SWA_TILE = 128
CTX_TILE = 256
ATT_TILE = 512
KV_CHUNKS = 2
MIX_TILE = 1024
FFN_TILE = 1024
FFN_CHUNK = 256
MOD_TILE = 2048


def _rms(x, n):
    return lax.rsqrt(jnp.sum(x * x, axis=-1, keepdims=True) * (1.0 / n) + NORM_EPS)


def _lane(shape):
    return lax.broadcasted_iota(jnp.int32, shape, len(shape) - 1)


def _dot(a, b):
    return jnp.dot(a, b, preferred_element_type=F32)


def _dot_nt(a, b):
    return lax.dot_general(a, b, (((1,), (1,)), ((), ())), preferred_element_type=F32)


def mod_kernel(c_ref, w_ref, b_ref, o_ref):
    c = c_ref[...]
    s = c * (1.0 / (1.0 + jnp.exp(-c)))
    o_ref[...] = jnp.dot(s, w_ref[...], preferred_element_type=F32,
                         precision=lax.Precision.HIGHEST) + b_ref[...]


def mod_call(cc, w_mod, b_mod):
    depth, d, n = w_mod.shape
    rows = cc.shape[0]
    tn = MOD_TILE
    return pl.pallas_call(
        mod_kernel,
        name="mod_vectors",
        grid=(depth, n // tn),
        in_specs=[
            pl.BlockSpec((rows, d), lambda l, j: (0, 0)),
            pl.BlockSpec((None, d, tn), lambda l, j: (l, 0, j)),
            pl.BlockSpec((None, 1, tn), lambda l, j: (l, 0, j)),
        ],
        out_specs=pl.BlockSpec((None, rows, tn), lambda l, j: (l, 0, j)),
        out_shape=jax.ShapeDtypeStruct((depth, rows, n), F32),
        compiler_params=pltpu.CompilerParams(
            dimension_semantics=("arbitrary", "arbitrary"), vmem_limit_bytes=VMEM_LIMIT),
    )(cc, w_mod, b_mod.reshape(depth, 1, n))


def absorb_kernel(a_ref, b_ref, o_ref):
    o_ref[...] = jnp.dot(a_ref[...], b_ref[...], preferred_element_type=F32,
                         precision=lax.Precision.HIGHEST)


def absorb_call(a, b):
    heads, m, k = a.shape
    n = b.shape[2]
    return pl.pallas_call(
        absorb_kernel,
        name="mla_absorb",
        grid=(heads,),
        in_specs=[pl.BlockSpec((None, m, k), lambda h: (h, 0, 0)),
                  pl.BlockSpec((None, k, n), lambda h: (h, 0, 0))],
        out_specs=pl.BlockSpec((None, m, n), lambda h: (h, 0, 0)),
        out_shape=jax.ShapeDtypeStruct((heads, m, n), F32),
        compiler_params=pltpu.CompilerParams(dimension_semantics=("arbitrary",)),
    )(a, b)


def _rope(x, tab_ref):
    return x * tab_ref[0] + pltpu.roll(x, HALF, 1) * tab_ref[1]


def _halves(lo, hi, shape):
    return jnp.where(_lane(shape) < HALF, lo, hi)


def _head_rms(x, first):
    sq = x * x
    ss_a = jnp.sum(jnp.where(first, sq, 0.0), axis=-1, keepdims=True)
    ss_b = jnp.sum(jnp.where(first, 0.0, sq), axis=-1, keepdims=True)
    r_a = lax.rsqrt(ss_a * (1.0 / HALF) + NORM_EPS)
    r_b = lax.rsqrt(ss_b * (1.0 / HALF) + NORM_EPS)
    return jnp.where(first, r_a, r_b)


def proj_kernel(xc_ref, x_ref, mod_ref, gpre_ref, win_ref, wq_ref, gq_ref, gkv_ref,
                gnq_ref, gnk_ref, t32_ref, t64_ref, tml_ref, o_ref, pa_ref, pb_ref, *,
                tiles_per_sample, n_tiles):
    g = pl.program_id(0)
    is_ctx = jnp.minimum(g, n_tiles - 1) % tiles_per_sample == tiles_per_sample - 1

    @pl.when(g == 0)
    def _():
        pb_ref[...] = jnp.zeros_like(pb_ref)

    def project(dst_ref):
        x = jnp.where(is_ctx, xc_ref[...], x_ref[...])
        h = x * _rms(x, D_MODEL) * gpre_ref[...]
        h = h * (1.0 + mod_ref[1:2, :]) + mod_ref[0:1, :]
        dst_ref[...] = _dot(h.astype(BF16), win_ref[...])

    finish = functools.partial(_proj_finish, wq_ref, gq_ref, gkv_ref, gnq_ref, gnk_ref,
                               t32_ref, t64_ref, tml_ref, o_ref)

    @pl.when(g % 2 == 0)
    def _():
        project(pa_ref)
        finish(pb_ref)

    @pl.when(g % 2 == 1)
    def _():
        project(pb_ref)
        finish(pa_ref)


def _proj_finish(wq_ref, gq_ref, gkv_ref, gnq_ref, gnk_ref, t32_ref, t64_ref, tml_ref,
                 o_ref, p_ref):
    def grp(g):
        return p_ref[:, g * LANES:(g + 1) * LANES]

    def put(g, v):
        o_ref[:, g * LANES:(g + 1) * LANES] = v.astype(BF16)

    def rope32(v):
        return _rope(v, t32_ref)

    def rope64(v):
        return _rope(v, t64_ref)

    def ropeml(v):
        return _rope(v, tml_ref)

    lane = _lane((p_ref.shape[0], LANES))
    low = lane < HALF
    head_a = lane % HALF < HALF // 2

    def with_ones(v):
        return jnp.where(low, v, 1.0), jnp.where(low, 1.0, v)

    for g in range(2):
        put(G_QD + g, rope32(grp(g)) * (DIFF_QK_DIM ** -0.5 * LOG2E))
        put(G_KD + g, rope32(grp(2 + g)))
        v_lo, v_hi = with_ones(grp(4 + g))
        put(G_VD + 2 * g, v_lo)
        put(G_VD + 2 * g + 1, v_hi)

    def put_gqa_queries(g0, q01, q23):
        put(g0, jnp.where(head_a, q01, 0.0))
        put(g0 + 1, jnp.where(head_a, pltpu.roll(q01, LANES - HALF // 2, 1), 0.0))
        put(g0 + 2, jnp.where(head_a, 0.0, pltpu.roll(q23, HALF // 2, 1)))
        put(g0 + 3, jnp.where(head_a, 0.0, q23))

    scale_s = SWA_HEAD_DIM ** -0.5 * LOG2E
    put_gqa_queries(G_QS, rope64(grp(6)) * scale_s, rope64(grp(7)) * scale_s)
    put(G_KS, rope64(grp(8)))
    v_lo, v_hi = with_ones(grp(9))
    put(G_VS, v_lo)
    put(G_VS + 1, v_hi)

    cq0, cq1, ckv = grp(10), grp(11), grp(12)
    rope_lanes = lane % HALF < MLA_ROPE_DIM // 2
    cq_lanes = ~rope_lanes & (lane < HALF + MLA_ROPE_DIM)
    ss = (jnp.sum(cq0 * cq0, axis=-1, keepdims=True)
          + jnp.sum(jnp.where(cq_lanes, cq1 * cq1, 0.0), axis=-1, keepdims=True))
    rq = lax.rsqrt(ss * (1.0 / MLA_Q_RANK) + NORM_EPS)
    cqn = jnp.concatenate([cq0 * rq * gq_ref[:, :LANES], cq1 * rq * gq_ref[:, LANES:]], axis=1)
    qm = _dot(cqn.astype(BF16), wq_ref[...])
    scale_m = (MLA_NOPE_DIM + MLA_ROPE_DIM) ** -0.5 * LOG2E
    for hd in range(MLA_HEADS):
        put(G_QM + 2 * hd, qm[:, hd * LANES:(hd + 1) * LANES] * scale_m)
        qr = qm[:, (MLA_HEADS + hd) * LANES:(MLA_HEADS + hd + 1) * LANES]
        put(G_QM + 2 * hd + 1, ropeml(qr) * scale_m)
    ckvn = ckv * _rms(ckv, MLA_KV_RANK) * gkv_ref[...]
    put(G_KM, ckvn)
    put(G_KM + 1, jnp.where(rope_lanes, ropeml(cq1), 0.0))
    put(G_VM, ckvn)
    put(G_VM + 1, jnp.ones_like(ckvn))

    scale_g = GQA_HEAD_DIM ** -0.5 * LOG2E
    q01 = grp(13)
    q01 = rope64(q01 * _head_rms(q01, head_a) * gnq_ref[...]) * scale_g
    q23 = grp(14)
    q23 = rope64(q23 * _head_rms(q23, head_a) * gnq_ref[...]) * scale_g
    put_gqa_queries(G_QG, q01, q23)
    kg = grp(15)
    put(G_KG, rope64(kg * _head_rms(kg, head_a) * gnk_ref[...]))
    v_lo, v_hi = with_ones(grp(16))
    put(G_VG, v_lo)
    put(G_VG + 1, v_hi)


def proj_call(xc, x, mod, ctx_row, lw, tabs):
    b, s, d = x.shape
    tq = TOK_TILE
    assert xc.shape[1] == CTX_TILE == tq
    s_all = CTX_TILE + s
    per = s_all // tq
    n = b * per
    const2 = lambda g: (0, 0)

    def proj_tile(g):
        ga = jnp.minimum(g, n - 1)
        return ga // per, ga % per

    def done_tile(g):
        gb = jnp.maximum(g - 1, 0)
        return gb // per, gb % per

    def x_map(g):
        i, t = proj_tile(g)
        return i, jnp.minimum(t, per - 2), 0

    def mod_map(g):
        i, t = proj_tile(g)
        return jnp.where(t == per - 1, ctx_row, i), 0, 0

    tab_spec = pl.BlockSpec((2, tq, LANES), lambda g: (0, done_tile(g)[1], 0))
    return pl.pallas_call(
        functools.partial(proj_kernel, tiles_per_sample=per, n_tiles=n),
        name="proj",
        grid=(n + 1,),
        in_specs=[
            pl.BlockSpec((None, tq, d), lambda g: (proj_tile(g)[0], 0, 0)),
            pl.BlockSpec((None, tq, d), x_map),
            pl.BlockSpec((None, N_MOD, d), mod_map),
            pl.BlockSpec((1, d), const2),
            pl.BlockSpec((d, P_COLS), const2),
            pl.BlockSpec((2 * LANES, 2 * MLA_HEADS * LANES), const2),
            pl.BlockSpec((1, 2 * LANES), const2),
            pl.BlockSpec((1, LANES), const2),
            pl.BlockSpec((1, LANES), const2),
            pl.BlockSpec((1, LANES), const2),
            tab_spec, tab_spec, tab_spec,
        ],
        out_specs=pl.BlockSpec((None, tq, QKV_COLS), lambda g: done_tile(g) + (0,)),
        out_shape=jax.ShapeDtypeStruct((b, s_all, QKV_COLS), BF16),
        scratch_shapes=[pltpu.VMEM((tq, P_COLS), F32), pltpu.VMEM((tq, P_COLS), F32)],
        compiler_params=pltpu.CompilerParams(
            dimension_semantics=("arbitrary",), vmem_limit_bytes=VMEM_LIMIT),
    )(xc, x, mod, lw["g_mix_pre"], lw["w_in"], lw["wq"], lw["gq"], lw["gkv"],
      lw["gnq"], lw["gnk"], tabs[0], tabs[1], tabs[2])


def _group(ref, rows, g, width=1):
    return ref[rows, g * width * LANES:(g + 1) * width * LANES]


def _attend(q_ref, segs, k_ref, v_ref, n_chunks, width):
    chunk = k_ref.shape[0] // n_chunks

    def query(g):
        if isinstance(g, int):
            return _group(q_ref, slice(None), g, width)
        g, unit, n_units = g
        q = _group(q_ref, slice(None), g, width)
        per = HALF // n_units
        keep = _lane(q.shape) % HALF // per == unit
        return jnp.where(keep, q, jnp.zeros_like(q))

    qs = [jnp.concatenate([query(g) for g in qg], axis=0) for qg, _, _ in segs]

    def body(i, carry):
        rows = pl.ds(pl.multiple_of(i * chunk, LANES), chunk)
        out = []
        for j, (q, (_, kb, parts)) in enumerate(zip(qs, segs)):
            m, acc = carry[2 * j], carry[2 * j + 1]
            s = _dot_nt(q, _group(k_ref, rows, kb, width))
            m_new = jnp.maximum(m, jnp.max(s, axis=-1, keepdims=True))
            p = jnp.exp2(s - m_new).astype(BF16)
            tq = q.shape[0] // sum(n for n, _ in parts)
            pv, r0 = [], 0
            for n, vb in parts:
                pv.append(_dot(p[r0:r0 + n * tq], _group(v_ref, rows, vb, width)))
                r0 += n * tq
            pv = pv[0] if len(pv) == 1 else jnp.concatenate(pv, axis=0)
            out += [m_new, acc * jnp.exp2(m - m_new) + pv]
        return tuple(out)

    init = []
    for q in qs:
        init += [jnp.full((q.shape[0], 1), NEG_INF, F32),
                 jnp.zeros((q.shape[0], width * LANES), F32)]
    carry = lax.fori_loop(0, n_chunks, body, tuple(init), unroll=True)
    half = width * HALF
    return [carry[2 * j + 1] / pltpu.roll(carry[2 * j + 1], half, 1) for j in range(len(segs))]


_PLANS = {
    "gqa": (1, ((((0, 1, 2, 3), 0, ((2, 0), (2, 1))),),)),
    "mla": (2, ((((0, 1, 2, 3), 0, ((2, 0), (2, 0))),),)),
    "diff": (1, tuple(
        ((tuple((ob, i, 4) for i in range(4)), ob,
          ((2, 2 * ob), (2, 2 * ob + 1))),) for ob in range(2))),
}


def dense_kernel(*refs, kind, n_chunks, lambda_init):
    refs = list(refs)
    if kind == "diff":
        lam_ref, g_ref = refs[:2]
        refs = refs[2:]
    elif kind == "mla":
        wv_ref = refs[0]
        refs = refs[1:]
    q_ref, k_ref, v_ref, o_ref = refs
    tq = q_ref.shape[0]
    width, loops = _PLANS[kind]
    res = [_attend(q_ref, segs, k_ref, v_ref, n_chunks, width) for segs in loops]

    def tiles(r):
        return [r[i * tq:(i + 1) * tq] for i in range(r.shape[0] // tq)]

    if kind == "gqa":
        h = [t for r in res[0] for t in tiles(r)]
        blocks = [_halves(h[0], pltpu.roll(h[1], HALF, 1), h[0].shape),
                  _halves(pltpu.roll(h[2], HALF, 1), h[3], h[0].shape)]
    elif kind == "mla":
        h = [r[:, :LANES].astype(BF16) for r in tiles(res[0][0])]
        blocks = [_dot(jnp.concatenate(h[2 * ob:2 * ob + 2], axis=1), wv_ref[ob]) for ob in range(2)]
    else:
        lv = lam_ref[...]
        lam = (jnp.exp(jnp.sum(lv[0:1] * lv[1:2], axis=-1, keepdims=True))
               - jnp.exp(jnp.sum(lv[2:3] * lv[3:4], axis=-1, keepdims=True)) + lambda_init)
        blocks = []
        for (r,) in res:
            a = tiles(r)
            o = _halves(a[0] - lam * a[1], a[2] - lam * a[3], a[0].shape)
            blocks.append(o * _head_rms(o, _lane(o.shape) < HALF) * g_ref[...] * (1.0 - lambda_init))
    for ob, blk in enumerate(blocks):
        o_ref[:, ob * LANES:(ob + 1) * LANES] = blk.astype(BF16)


def _qkv_specs(lay, latent, s_all, tq):
    (q0, qn), (k0, kn), (v0, vn) = lay
    assert q0 % qn == 0 and k0 % kn == 0 and v0 % vn == 0
    ctx_blk = s_all // CTX_TILE - 1
    if latent:
        q_map = lambda i, t: (i, t, q0 // qn)
        k_map = lambda i, t: (i, 0, k0 // kn)
        v_map = lambda i, t: (i, 0, v0 // vn)
    else:
        q_map = lambda i, t: (i, ctx_blk, q0 // qn)
        k_map = lambda i, t: (i, ctx_blk, k0 // kn)
        v_map = lambda i, t: (i, ctx_blk, v0 // vn)
    kv_rows = s_all if latent else CTX_TILE
    return [
        pl.BlockSpec((None, tq, qn * LANES), q_map),
        pl.BlockSpec((None, kv_rows, kn * LANES), k_map),
        pl.BlockSpec((None, kv_rows, vn * LANES), v_map),
    ]


def dense_attn_call(kind, qkv, latent, extra=None, lambda_init=None):
    b, s_all, _ = qkv.shape
    tq = ATT_TILE if latent else CTX_TILE
    sq = s_all - CTX_TILE if latent else CTX_TILE
    lay = {"diff": ((G_QD, 2), (G_KD, 2), (G_VD, 4)),
           "mla": ((G_QM, 8), (G_KM, 2), (G_VM, 2)),
           "gqa": ((G_QG, 4), (G_KG, 1), (G_VG, 2))}[kind]
    specs = _qkv_specs(lay, latent, s_all, tq)
    args = [qkv, qkv, qkv]
    if kind == "diff":
        specs = [pl.BlockSpec((4, DIFF_QK_DIM), lambda i, t: (0, 0)),
                 pl.BlockSpec((1, LANES), lambda i, t: (0, 0))] + specs
        args = list(extra) + args
    elif kind == "mla":
        specs = [pl.BlockSpec((2, 2 * LANES, LANES), lambda i, t: (0, 0, 0))] + specs
        args = list(extra) + args
    body = functools.partial(dense_kernel, kind=kind, n_chunks=KV_CHUNKS if latent else 1,
                             lambda_init=lambda_init)
    return pl.pallas_call(
        body,
        name=kind + ("_attn" if latent else "_attn_ctx"),
        grid=(b, sq // tq),
        in_specs=specs,
        out_specs=pl.BlockSpec((None, tq, MIX_COLS), lambda i, t: (i, t, 0)),
        out_shape=jax.ShapeDtypeStruct((b, sq, MIX_COLS), BF16),
        compiler_params=pltpu.CompilerParams(
            dimension_semantics=("arbitrary", "arbitrary"), vmem_limit_bytes=VMEM_LIMIT),
    )(*args)


def swa_kernel(sink_ref, q_ref, k_ref, v_ref, o_ref, *, s_lat):
    tq = q_ref.shape[0]
    heads = q_ref.shape[1] // LANES
    rows = heads * tq
    t = pl.program_id(1)
    q = jnp.concatenate([_group(q_ref, slice(None), g) for g in range(heads)], axis=0)
    row = lax.broadcasted_iota(jnp.int32, (rows, 1), 0)
    sink = jnp.zeros((rows, 1), F32)
    for h in range(heads):
        sink = jnp.where(row // tq == h, sink_ref[h] * LOG2E, sink)

    spans = [(slice(s_lat, s_lat + CTX_TILE), None)]
    if s_lat:
        width = tq + 2 * WINDOW
        start = jnp.clip(t * tq - WINDOW, 0, s_lat - width)
        rel = (start - t * tq + lax.broadcasted_iota(jnp.int32, (tq, width), 1)
               - lax.broadcasted_iota(jnp.int32, (tq, width), 0))
        valid = jnp.concatenate([jnp.abs(rel) <= WINDOW] * heads, axis=0)
        spans.append((pl.ds(pl.multiple_of(start, WINDOW), width), valid))

    ss = []
    for span, valid in spans:
        s = _dot_nt(q, _group(k_ref, span, 0))
        ss.append(s if valid is None else jnp.where(valid, s, NEG_INF))
    m = functools.reduce(jnp.maximum, [jnp.max(s, axis=-1, keepdims=True) for s in ss] + [sink])
    ps = [jnp.exp2(s - m).astype(BF16) for s in ss]
    half = rows // 2
    acc = jnp.concatenate(
        [sum(_dot(p[vb * half:(vb + 1) * half], _group(v_ref, span, vb))
             for p, (span, _) in zip(ps, spans)) for vb in range(2)], axis=0)
    res = acc / (pltpu.roll(acc, HALF, 1) + jnp.exp2(sink - m))
    h = [res[i * tq:(i + 1) * tq] for i in range(heads)]
    o_ref[:, :LANES] = _halves(h[0], pltpu.roll(h[1], HALF, 1), h[0].shape).astype(BF16)
    o_ref[:, LANES:] = _halves(pltpu.roll(h[2], HALF, 1), h[3], h[0].shape).astype(BF16)


def swa_call(sink, qkv, latent):
    b, s_all, _ = qkv.shape
    tq = SWA_TILE if latent else CTX_TILE
    s_lat = s_all - CTX_TILE if latent else 0
    sq = s_lat if latent else CTX_TILE
    specs = [pl.BlockSpec(memory_space=pltpu.SMEM)]
    specs += _qkv_specs(((G_QS, 4), (G_KS, 1), (G_VS, 2)), latent, s_all, tq)
    return pl.pallas_call(
        functools.partial(swa_kernel, s_lat=s_lat),
        name="swa_attn" if latent else "swa_attn_ctx",
        grid=(b, sq // tq),
        in_specs=specs,
        out_specs=pl.BlockSpec((None, tq, MIX_COLS), lambda i, t: (i, t, 0)),
        out_shape=jax.ShapeDtypeStruct((b, sq, MIX_COLS), BF16),
        compiler_params=pltpu.CompilerParams(
            dimension_semantics=("arbitrary", "arbitrary"), vmem_limit_bytes=VMEM_LIMIT),
    )(sink, qkv, qkv, qkv)


def mixout_kernel(x_ref, ya_ref, yb_ref, ym_ref, yd_ref, w_ref, mod_ref, gpost_ref,
                  gfpre_ref, x1_ref, h2_ref, za_ref, zb_ref):
    g = pl.program_id(0)

    @pl.when(g == 0)
    def _():
        zb_ref[...] = jnp.zeros_like(zb_ref)

    def project(dst_ref):
        y = jnp.concatenate([ya_ref[...], yb_ref[...], ym_ref[...], yd_ref[...]], axis=1)
        dst_ref[...] = _dot(y, w_ref[...])

    def finish(z_ref):
        z = z_ref[...]
        x1 = x_ref[...] + mod_ref[2:3, :] * (z * _rms(z, D_MODEL) * gpost_ref[...])
        x1_ref[...] = x1
        h2 = x1 * _rms(x1, D_MODEL) * gfpre_ref[...]
        h2_ref[...] = (h2 * (1.0 + mod_ref[4:5, :]) + mod_ref[3:4, :]).astype(BF16)

    @pl.when(g % 2 == 0)
    def _():
        project(za_ref)
        finish(zb_ref)

    @pl.when(g % 2 == 1)
    def _():
        project(zb_ref)
        finish(za_ref)


def mixout_call(x, ys, mod, mod_row, lw):
    b, s, d = x.shape
    tq = min(MIX_TILE, s)
    per = s // tq
    n = b * per

    def proj_tile(g):
        ga = jnp.minimum(g, n - 1)
        return ga // per, ga % per, 0

    def done_tile(g):
        gb = jnp.maximum(g - 1, 0)
        return gb // per, gb % per, 0

    if mod_row is None:
        mod_map = lambda g: (done_tile(g)[0], 0, 0)
    else:
        mod_map = lambda g: (mod_row, 0, 0)
    const2 = lambda g: (0, 0)
    y_spec = pl.BlockSpec((None, tq, MIX_COLS), proj_tile)
    return pl.pallas_call(
        mixout_kernel,
        name="mixout",
        grid=(n + 1,),
        in_specs=[pl.BlockSpec((None, tq, d), done_tile), y_spec, y_spec, y_spec, y_spec,
                  pl.BlockSpec((4 * MIX_COLS, d), const2),
                  pl.BlockSpec((None, N_MOD, d), mod_map),
                  pl.BlockSpec((1, d), const2), pl.BlockSpec((1, d), const2)],
        out_specs=[pl.BlockSpec((None, tq, d), done_tile), pl.BlockSpec((None, tq, d), done_tile)],
        out_shape=[jax.ShapeDtypeStruct((b, s, d), F32), jax.ShapeDtypeStruct((b, s, d), BF16)],
        scratch_shapes=[pltpu.VMEM((tq, d), F32), pltpu.VMEM((tq, d), F32)],
        compiler_params=pltpu.CompilerParams(
            dimension_semantics=("arbitrary",), vmem_limit_bytes=VMEM_LIMIT),
    )(x, *ys, lw["w_out"], mod, lw["g_mix_post"], lw["g_ffn_pre"])


def ffn_kernel(x1_ref, hc_ref, hp_ref, hn_ref, wup_ref, cw_ref, cb_ref, wdn_ref, mod_ref,
               gpost_ref, o_ref, g_ref):
    t = pl.program_id(1)
    tq = hc_ref.shape[0]
    halo = hp_ref.shape[0]
    hp = jnp.where(t > 0, hp_ref[...], jnp.zeros_like(hp_ref))
    hn = jnp.where(t < pl.num_programs(1) - 1, hn_ref[...], jnp.zeros_like(hn_ref))
    hext = jnp.concatenate([hp, hc_ref[...], hn], axis=0)
    rows = hext.shape[0]

    def conv_cols(cols):
        u = _dot(hext, wup_ref[:, cols])
        w = cw_ref[:, cols]
        return (pltpu.roll(u, 1, 0)[halo:halo + tq] * w[0:1]
                + u[halo:halo + tq] * w[1:2]
                + pltpu.roll(u, rows - 1, 0)[halo:halo + tq] * w[2:3] + cb_ref[:, cols])

    for j in range(FFN_DIM // FFN_CHUNK):
        a = conv_cols(slice(j * FFN_CHUNK, (j + 1) * FFN_CHUNK))
        v = conv_cols(slice(FFN_DIM + j * FFN_CHUNK, FFN_DIM + (j + 1) * FFN_CHUNK))
        g_ref[:, j * FFN_CHUNK:(j + 1) * FFN_CHUNK] = (a * (1.0 / (1.0 + jnp.exp(-a))) * v).astype(BF16)

    f = _dot(g_ref[...], wdn_ref[...])
    o_ref[...] = x1_ref[...] + mod_ref[5:6, :] * (f * _rms(f, D_MODEL) * gpost_ref[...])


def ffn_call(x1, h2, mod, mod_row, lw):
    b, s, d = x1.shape
    tq = min(FFN_TILE, s)
    halo = BF16_SUBLANES
    per_tile = tq // halo
    last = s // halo - 1
    if mod_row is None:
        mod_map = lambda i, t: (i, 0, 0)
    else:
        mod_map = lambda i, t: (mod_row, 0, 0)
    const2 = lambda i, t: (0, 0)
    tok = lambda i, t: (i, t, 0)
    return pl.pallas_call(
        ffn_kernel,
        name="conv_ffn",
        grid=(b, s // tq),
        in_specs=[
            pl.BlockSpec((None, tq, d), tok),
            pl.BlockSpec((None, tq, d), tok),
            pl.BlockSpec((None, halo, d), lambda i, t: (i, jnp.maximum(t * per_tile - 1, 0), 0)),
            pl.BlockSpec((None, halo, d), lambda i, t: (i, jnp.minimum((t + 1) * per_tile, last), 0)),
            pl.BlockSpec((d, 2 * FFN_DIM), const2, pipeline_mode=pl.Buffered(1)),
            pl.BlockSpec((3, 2 * FFN_DIM), const2),
            pl.BlockSpec((1, 2 * FFN_DIM), const2),
            pl.BlockSpec((FFN_DIM, d), const2, pipeline_mode=pl.Buffered(1)),
            pl.BlockSpec((None, N_MOD, d), mod_map),
            pl.BlockSpec((1, d), const2),
        ],
        out_specs=pl.BlockSpec((None, tq, d), tok),
        out_shape=jax.ShapeDtypeStruct((b, s, d), F32),
        scratch_shapes=[pltpu.VMEM((tq, FFN_DIM), BF16)],
        compiler_params=pltpu.CompilerParams(
            dimension_semantics=("arbitrary", "arbitrary"), vmem_limit_bytes=VMEM_LIMIT),
    )(x1, h2, h2, h2, lw["w_up"], lw["conv_w"], lw["conv_b"], lw["w_down"], mod,
      lw["g_ffn_post"])


def _pair_layout(rot_dim, n_units):
    n = rot_dim // 4
    lane = np.arange(LANES)
    half, mu = lane // HALF, lane % HALF
    unit, r = mu // (2 * n), mu % (2 * n)
    axis, f = r // n, r % n
    used = unit < n_units
    orig = np.where(used, unit * rot_dim + axis * 2 * n + half * n + f, -1)
    return orig, axis, f, used


def _rope_table(n_ctx, s, rot_dim, n_units):
    n = rot_dim // 4
    _, axis, f, used = _pair_layout(rot_dim, n_units)
    tok = np.arange(s)
    pos = np.stack([tok // GRID_W, tok % GRID_W], axis=1).astype(np.float32)
    inv_freq = ROPE_THETA ** (-jnp.arange(n, dtype=F32) / n)
    ang = jnp.asarray(pos)[:, axis] * inv_freq[f][None, :]
    sign = np.where(np.arange(LANES) < HALF, -1.0, 1.0).astype(np.float32)
    cos = jnp.where(used[None], jnp.cos(ang), 1.0)
    sin = jnp.where(used[None], jnp.sin(ang) * sign[None], 0.0)
    tab = jnp.stack([cos, sin]).astype(F32)
    ident = jnp.stack([jnp.ones((n_ctx, LANES), F32), jnp.zeros((n_ctx, LANES), F32)])
    return jnp.concatenate([tab, ident], axis=1)


def _take_cols(a, idx):
    idx = np.asarray(idx)
    return jnp.where(idx >= 0, jnp.take(a, np.maximum(idx, 0), axis=-1), 0.0)


def _layer_weights(l, w):
    w_in = w["w_in"][l]
    mla0 = 768 + 512
    cq_hi = mla0 + MLA_Q_RANK
    kr0 = cq_hi + MLA_KV_RANK
    src = np.concatenate([
        np.arange(cq_hi), np.arange(kr0, kr0 + MLA_ROPE_DIM),
        np.full(LANES - HALF - MLA_ROPE_DIM, -1),
        np.arange(cq_hi, kr0), np.arange(kr0 + MLA_ROPE_DIM, w_in.shape[1])])
    assert src.shape == (P_COLS,)
    o32, _, _, _ = _pair_layout(DIFF_QK_DIM, LANES // DIFF_QK_DIM)
    o64, _, _, _ = _pair_layout(GQA_HEAD_DIM, LANES // GQA_HEAD_DIM)
    oml, _, _, rope_lane = _pair_layout(MLA_ROPE_DIM, 1)
    cols = np.arange(P_COLS)
    for g, o in ((0, o32), (1, o32), (2, o32), (3, o32), (6, o64), (7, o64), (8, o64),
                 (13, o64), (14, o64), (15, o64)):
        cols[g * LANES:(g + 1) * LANES] = g * LANES + o
    cq_lanes = [i for i in range(LANES) if not rope_lane[i]][:MLA_Q_RANK - LANES]
    g11 = np.full(LANES, -1)
    g11[rope_lane] = HALF + oml[rope_lane]
    g11[cq_lanes] = np.arange(len(cq_lanes))
    cols[11 * LANES:12 * LANES] = np.where(g11 >= 0, 11 * LANES + g11, -1)
    w_in_p = _take_cols(w_in, np.where(cols >= 0, src[np.maximum(cols, 0)], -1))
    cq_src = np.full(2 * LANES, -1)
    cq_src[:LANES] = np.arange(LANES)
    cq_src[LANES + np.asarray(cq_lanes)] = LANES + np.arange(len(cq_lanes))

    dqk = MLA_NOPE_DIM + MLA_ROPE_DIM
    assert MLA_KV_RANK == LANES
    wuq = w["mla_w_uq"][l].reshape(MLA_Q_RANK, MLA_HEADS, dqk)
    wukv = w["mla_w_ukv"][l].reshape(MLA_KV_RANK, MLA_HEADS, MLA_NOPE_DIM + MLA_V_DIM)
    row_pad = 2 * LANES - MLA_Q_RANK
    wuq_n = jnp.pad(jnp.transpose(wuq[:, :, :MLA_NOPE_DIM], (1, 0, 2)), ((0, 0), (0, row_pad), (0, 0)))
    w_abs = absorb_call(wuq_n, jnp.transpose(wukv[:, :, :MLA_NOPE_DIM], (1, 2, 0)))
    wq_r = jnp.pad(_take_cols(wuq[:, :, MLA_NOPE_DIM:], np.where(rope_lane, oml, -1)),
                   ((0, row_pad), (0, 0), (0, 0)))
    wq = jnp.concatenate([jnp.transpose(w_abs, (1, 0, 2)).reshape(2 * LANES, -1),
                          wq_r.reshape(2 * LANES, -1)], axis=1)
    wq = _take_cols(wq.T, cq_src).T
    wuv = jnp.transpose(wukv[:, :, MLA_NOPE_DIM:], (1, 0, 2))
    wv = jnp.zeros((2, 2, LANES, 2, MLA_V_DIM), F32)
    for hd in range(MLA_HEADS):
        wv = wv.at[hd // 2, hd % 2, :, hd % 2, :].set(wuv[hd])
    wv = wv.reshape(2, 2 * LANES, LANES)
    gq = _take_cols(jnp.pad(w["mla_g_q"][l], (0, row_pad)), cq_src)[None]
    return {
        "g_mix_pre": w["g_mix_pre"][l][None], "g_mix_post": w["g_mix_post"][l][None],
        "g_ffn_pre": w["g_ffn_pre"][l][None], "g_ffn_post": w["g_ffn_post"][l][None],
        "w_in": w_in_p.astype(BF16),
        "wq": wq.astype(BF16), "wv": wv.astype(BF16),
        "gq": gq, "gkv": w["mla_g_kv"][l][None],
        "gnq": w["gqa_g_q"][l][o64 % GQA_HEAD_DIM][None],
        "gnk": w["gqa_g_k"][l][o64 % GQA_HEAD_DIM][None],
        "lam": jnp.stack([w["diff_lam_q1"][l], w["diff_lam_k1"][l],
                          w["diff_lam_q2"][l], w["diff_lam_k2"][l]]),
        "g_sub": jnp.tile(w["diff_g_sub"][l], 2)[None],
        "sink": w["swa_sink"][l],
        "w_out": w["w_out"][l].astype(BF16),
        "w_up": w["ffn_w_up"][l].astype(BF16),
        "conv_w": w["ffn_conv_w"][l],
        "conv_b": w["ffn_conv_b"][l][None],
        "w_down": w["ffn_w_down"][l].astype(BF16),
    }


def kernel(x, c, ctx, c_ctx, w_mod, b_mod, g_mix_pre, g_mix_post, g_ffn_pre, g_ffn_post, w_in, diff_lam_q1, diff_lam_k1, diff_lam_q2, diff_lam_k2, diff_g_sub, swa_sink, mla_g_q, mla_g_kv, mla_w_uq, mla_w_ukv, gqa_g_q, gqa_g_k, w_out, ffn_w_up, ffn_conv_w, ffn_conv_b, ffn_w_down):
    w = dict(g_mix_pre=g_mix_pre, g_mix_post=g_mix_post, g_ffn_pre=g_ffn_pre,
             g_ffn_post=g_ffn_post, w_in=w_in, diff_lam_q1=diff_lam_q1, diff_lam_k1=diff_lam_k1,
             diff_lam_q2=diff_lam_q2, diff_lam_k2=diff_lam_k2, diff_g_sub=diff_g_sub,
             swa_sink=swa_sink, mla_g_q=mla_g_q, mla_g_kv=mla_g_kv, mla_w_uq=mla_w_uq,
             mla_w_ukv=mla_w_ukv, gqa_g_q=gqa_g_q, gqa_g_k=gqa_g_k, w_out=w_out,
             ffn_w_up=ffn_w_up, ffn_conv_w=ffn_conv_w, ffn_conv_b=ffn_conv_b,
             ffn_w_down=ffn_w_down)
    b, s, d = x.shape
    depth = w_mod.shape[0]
    ctx_row = b
    mod_rows = 16
    cc = jnp.zeros((mod_rows, d), F32).at[:b].set(c).at[ctx_row].set(c_ctx)
    mod_all = mod_call(cc, w_mod, b_mod).reshape(depth, mod_rows, N_MOD, d)

    n_ctx = ctx.shape[1]
    tabs = (_rope_table(n_ctx, s, DIFF_QK_DIM, LANES // DIFF_QK_DIM),
            _rope_table(n_ctx, s, GQA_HEAD_DIM, LANES // GQA_HEAD_DIM),
            _rope_table(n_ctx, s, MLA_ROPE_DIM, 1))

    xc = ctx
    for l in range(depth):
        need_ctx = l < depth - 1
        lambda_init = 0.8 - 0.6 * math.exp(-0.3 * l)
        lw = _layer_weights(l, w)
        mod = mod_all[l]
        qkv = proj_call(xc, x, mod, ctx_row, lw, tabs)
        diff_extra = (lw["lam"], lw["g_sub"])
        mla_extra = (lw["wv"],)
        ys = (dense_attn_call("diff", qkv, True, diff_extra, lambda_init),
              swa_call(lw["sink"], qkv, True),
              dense_attn_call("mla", qkv, True, mla_extra),
              dense_attn_call("gqa", qkv, True))
        x1, h2 = mixout_call(x, ys, mod, None, lw)
        x = ffn_call(x1, h2, mod, None, lw)
        if need_ctx:
            ysc = (dense_attn_call("diff", qkv, False, diff_extra, lambda_init),
                   swa_call(lw["sink"], qkv, False),
                   dense_attn_call("mla", qkv, False, mla_extra),
                   dense_attn_call("gqa", qkv, False))
            xc1, hc2 = mixout_call(xc, ysc, mod, ctx_row, lw)
            xc = ffn_call(xc1, hc2, mod, ctx_row, lw)
    return x
```
